```python
import math
import jax
import jax.numpy as jnp
from jax import lax
import numpy as np

D_MODEL = 4096
BATCH = 16
SEQ = 2048
DEPTH = 1

HEAD_DIM = 128
N_SB_HEADS = 16
N_DIL_HEADS = 16
SB_WIDTH = N_SB_HEADS * HEAD_DIM
DIL_WIDTH = N_DIL_HEADS * HEAD_DIM
MIX_WIDTH = SB_WIDTH + DIL_WIDTH
SPLIT_SIZES = (SB_WIDTH, SB_WIDTH, SB_WIDTH, SB_WIDTH, DIL_WIDTH, DIL_WIDTH, DIL_WIDTH, DIL_WIDTH)
IN_COLS = sum(SPLIT_SIZES)
SPLIT_POINTS = tuple(sum(SPLIT_SIZES[:i + 1]) for i in range(len(SPLIT_SIZES) - 1))
Q_BLOCK = 128
DIL_PAIRS = ((128, 1), (512, 4), (2048, 16))
ALIBI_MAX_BIAS = 8.0
EPS = 1e-6

kernel_name = 'hybrid_stickbreak_dilated_block'


def rmsnorm(x, g):
    xf = x.astype(jnp.float32)
    y = xf * lax.rsqrt(jnp.mean(xf * xf, axis=-1, keepdims=True) + EPS)
    return (y * g.astype(jnp.float32)).astype(x.dtype)


def split_heads(t, n_heads):
    b, s, _ = t.shape
    return t.reshape(b, s, n_heads, HEAD_DIM).transpose(0, 2, 1, 3)


def head_rmsnorm_merge(y, g):
    b, h, s, d = y.shape
    yf = y.astype(jnp.float32)
    yf = yf * lax.rsqrt(jnp.mean(yf * yf, axis=-1, keepdims=True) + EPS)
    return yf.transpose(0, 2, 1, 3).reshape(b, s, h * d) * g.astype(jnp.float32)


def alibi_slopes(n_heads):
    return jnp.exp2(-ALIBI_MAX_BIAS * jnp.arange(1, n_heads + 1, dtype=jnp.float32) / n_heads)


def stick_breaking_attention(q, k, v):
    b, h, s, d = q.shape
    nb = s // Q_BLOCK
    inv_sqrt_d = 1.0 / math.sqrt(d)
    kf = k.astype(jnp.float32)
    vf = v.astype(jnp.float32)
    qb = q.astype(jnp.float32).reshape(b, h, nb, Q_BLOCK, d).transpose(2, 0, 1, 3, 4)
    k_pos = jnp.arange(s)

    def one_block(args):
        qi, blk = args
        z = jnp.einsum('bhqd,bhkd->bhqk', qi, kf) * inv_sqrt_d
        q_pos = blk * Q_BLOCK + jnp.arange(Q_BLOCK)
        mask = k_pos[None, :] < q_pos[:, None]
        log_beta = jax.nn.log_sigmoid(z)
        log_one_minus = jnp.where(mask, jax.nn.log_sigmoid(-z), 0.0)
        suffix = lax.cumsum(log_one_minus, axis=3, reverse=True) - log_one_minus
        a = jnp.where(mask, jnp.exp(log_beta + suffix), 0.0)
        return jnp.einsum('bhqk,bhkd->bhqd', a, vf)

    out = lax.map(one_block, (qb, jnp.arange(nb)))
    return out.transpose(1, 2, 0, 3, 4).reshape(b, h, s, d)


def dilated_branch(q, k, v, slopes, window, dilation):
    b, h, s, d = q.shape
    n = window // dilation
    L = s // dilation
    Lp = -(-L // n) * n
    nb = Lp // n
    inv_sqrt_d = 1.0 / math.sqrt(d)

    def to_local(t):
        t = t.astype(jnp.float32).reshape(b, h, L, dilation, d).transpose(0, 1, 3, 2, 4)
        t = jnp.pad(t, ((0, 0), (0, 0), (0, 0), (0, Lp - L), (0, 0)))
        return t.reshape(b, h, dilation, nb, n, d)

    def with_prev(t):
        prev = jnp.pad(t, ((0, 0), (0, 0), (0, 0), (1, 0), (0, 0), (0, 0)))[:, :, :, :-1]
        return jnp.concatenate([prev, t], axis=4)

    ql = to_local(q)
    kw = with_prev(to_local(k))
    vw = with_prev(to_local(v))
    qi = jnp.arange(n)[:, None]
    ki = jnp.arange(2 * n)[None, :] - n
    steps = qi - ki
    blk = jnp.arange(nb)[:, None, None]
    valid = (steps >= 0)[None] & (steps <= n)[None] & ((blk * n + ki[None]) >= 0)
    dist = (steps * dilation).astype(jnp.float32)
    sc = jnp.einsum('bhrnqd,bhrnkd->bhrnqk', ql, kw) * inv_sqrt_d
    sc = sc - slopes[None, :, None, None, None, None] * dist
    sc = jnp.where(valid, sc, -jnp.inf)
    m = jnp.max(sc, axis=-1, keepdims=True)
    p = jnp.exp(sc - m)
    den = jnp.sum(p, axis=-1, keepdims=True)
    num = jnp.einsum('bhrnqk,bhrnkd->bhrnqd', p, vw)

    def from_local(t):
        x_dim = t.shape[-1]
        t = t.reshape(b, h, dilation, Lp, x_dim)[:, :, :, :L]
        return t.transpose(0, 1, 3, 2, 4).reshape(b, h, s, x_dim)

    return from_local(num), from_local(den), from_local(m)


def dilated_attention(q, k, v):
    slopes = alibi_slopes(q.shape[1])
    parts = [dilated_branch(q, k, v, slopes, w, r) for (w, r) in DIL_PAIRS]
    m_all = parts[0][2]
    for part in parts[1:]:
        m_all = jnp.maximum(m_all, part[2])
    num = sum(pn * jnp.exp(pm - m_all) for (pn, _, pm) in parts)
    den = sum(pd * jnp.exp(pm - m_all) for (_, pd, pm) in parts)
    return num / den


def _fwd_setup_inputs(seed: int = 0) -> dict:
    key = jax.random.key(seed)
    ks = jax.random.split(key, 10)
    x = jax.random.normal(ks[0], (BATCH, SEQ, D_MODEL), jnp.float32)
    c = jax.random.normal(ks[1], (BATCH, D_MODEL), jnp.float32)
    w_ada = jax.random.normal(ks[2], (DEPTH, D_MODEL, 3 * D_MODEL), jnp.float32) * D_MODEL ** -0.5
    b_ada = 0.01 * jax.random.normal(ks[3], (DEPTH, 3 * D_MODEL), jnp.float32)
    g_norm = 1.0 + 0.1 * jax.random.normal(ks[4], (DEPTH, D_MODEL), jnp.float32)
    w_in = jax.random.normal(ks[5], (DEPTH, D_MODEL, IN_COLS), jnp.float32) * D_MODEL ** -0.5
    g_sb = 1.0 + 0.1 * jax.random.normal(ks[6], (DEPTH, SB_WIDTH), jnp.float32)
    g_dil = 1.0 + 0.1 * jax.random.normal(ks[7], (DEPTH, DIL_WIDTH), jnp.float32)
    w_out = jax.random.normal(ks[8], (DEPTH, MIX_WIDTH, D_MODEL), jnp.float32) * MIX_WIDTH ** -0.5
    g_final = 1.0 + 0.1 * jax.random.normal(ks[9], (D_MODEL,), jnp.float32)
    return {'x': x, 'c': c, 'w_ada': w_ada, 'b_ada': b_ada, 'g_norm': g_norm,
            'w_in': w_in, 'g_sb': g_sb, 'g_dil': g_dil, 'w_out': w_out, 'g_final': g_final}


def _fwd_reference(x, c, w_ada, b_ada, g_norm, w_in, g_sb, g_dil, w_out, g_final):
    cs = jax.nn.silu(c.astype(jnp.float32))
    for layer in range(DEPTH):
        mod = cs @ w_ada[layer].astype(jnp.float32) + b_ada[layer].astype(jnp.float32)
        shift, scale, gate = jnp.split(mod, 3, axis=-1)
        h = rmsnorm(x, g_norm[layer]).astype(jnp.float32) * (1.0 + scale[:, None, :]) + shift[:, None, :]
        proj = jnp.einsum('bsd,de->bse', h.astype(x.dtype), w_in[layer])
        sb_q, sb_k, sb_v, sb_z, dl_q, dl_k, dl_v, dl_z = jnp.split(proj, SPLIT_POINTS, axis=-1)
        y_sb = stick_breaking_attention(split_heads(sb_q, N_SB_HEADS), split_heads(sb_k, N_SB_HEADS),
                                        split_heads(sb_v, N_SB_HEADS))
        y_sb = head_rmsnorm_merge(y_sb, g_sb[layer]) * jax.nn.silu(sb_z.astype(jnp.float32))
        y_dl = dilated_attention(split_heads(dl_q, N_DIL_HEADS), split_heads(dl_k, N_DIL_HEADS),
                                 split_heads(dl_v, N_DIL_HEADS))
        y_dl = head_rmsnorm_merge(y_dl, g_dil[layer]) * jax.nn.silu(dl_z.astype(jnp.float32))
        y = jnp.concatenate([y_sb, y_dl], axis=-1).astype(x.dtype)
        out = jnp.einsum('bse,ed->bsd', y, w_out[layer])
        x = x + (gate[:, None, :] * out.astype(jnp.float32)).astype(x.dtype)
    return rmsnorm(x, g_final)


import jax as _jax
import jax.numpy as _jnp

TWIN_FORMAT = 'train_step'
FWD_PARAMS = ['x', 'c', 'w_ada', 'b_ada', 'g_norm', 'w_in', 'g_sb', 'g_dil', 'w_out', 'g_final']
TWIN_WEIGHTS = ['w_ada', 'b_ada', 'g_norm', 'w_in', 'g_sb', 'g_dil', 'w_out', 'g_final']
TWIN_DIFF_INPUT = 'x'
TWIN_INPUTS = ['x', 'c', 'w_ada', 'b_ada', 'g_norm', 'w_in', 'g_sb', 'g_dil', 'w_out', 'g_final', 'loss_target', 'm_w_ada', 'm_b_ada', 'm_g_norm', 'm_w_in', 'm_g_sb', 'm_g_dil', 'm_w_out', 'm_g_final', 'v_w_ada', 'v_b_ada', 'v_g_norm', 'v_w_in', 'v_g_sb', 'v_g_dil', 'v_w_out', 'v_g_final']
TWIN_OUTPUTS = ['loss', 'grad_x', 'grad_w_ada', 'grad_b_ada', 'grad_g_norm', 'grad_w_in', 'grad_g_sb', 'grad_g_dil', 'grad_w_out', 'grad_g_final', 'delta_w_ada', 'delta_b_ada', 'delta_g_norm', 'delta_w_in', 'delta_g_sb', 'delta_g_dil', 'delta_w_out', 'delta_g_final', 'new_m_w_ada', 'new_m_b_ada', 'new_m_g_norm', 'new_m_w_in', 'new_m_g_sb', 'new_m_g_dil', 'new_m_w_out', 'new_m_g_final', 'new_v_w_ada', 'new_v_b_ada', 'new_v_g_norm', 'new_v_w_in', 'new_v_g_sb', 'new_v_g_dil', 'new_v_w_out', 'new_v_g_final']
TWIN_LEAF_KINDS = {'loss': 'loss', 'grad_x': 'grad_x', 'grad_w_ada': 'grad_w', 'grad_b_ada': 'grad_w', 'grad_g_norm': 'grad_w', 'grad_w_in': 'grad_w', 'grad_g_sb': 'grad_w', 'grad_g_dil': 'grad_w', 'grad_w_out': 'grad_w', 'grad_g_final': 'grad_w', 'delta_w_ada': 'delta_w', 'delta_b_ada': 'delta_w', 'delta_g_norm': 'delta_w', 'delta_w_in': 'delta_w', 'delta_g_sb': 'delta_w', 'delta_g_dil': 'delta_w', 'delta_w_out': 'delta_w', 'delta_g_final': 'delta_w', 'new_m_w_ada': 'new_m', 'new_m_b_ada': 'new_m', 'new_m_g_norm': 'new_m', 'new_m_w_in': 'new_m', 'new_m_g_sb': 'new_m', 'new_m_g_dil': 'new_m', 'new_m_w_out': 'new_m', 'new_m_g_final': 'new_m', 'new_v_w_ada': 'new_v', 'new_v_b_ada': 'new_v', 'new_v_g_norm': 'new_v', 'new_v_w_in': 'new_v', 'new_v_g_sb': 'new_v', 'new_v_g_dil': 'new_v', 'new_v_w_out': 'new_v', 'new_v_g_final': 'new_v'}


def _forward(args):
    return _fwd_reference(*[args[k] for k in FWD_PARAMS])


def _output_shape():
    def fwd():
        inp = _fwd_setup_inputs(0)
        return _fwd_reference(*[inp[k] for k in FWD_PARAMS])
    out = _jax.eval_shape(fwd)
    return out.shape, out.dtype

N_MICROBATCH = 1
ADAM_LR = 0.001
ADAM_B1 = 0.9
ADAM_B2 = 0.999
ADAM_EPS = 1e-08
ADAM_WD = 0.01
ADAM_STEP = 10
PER_EXAMPLE_BATCH_AXIS = {'x': 0, 'c': 0, 'loss_target': 0}
SHARED_INPUTS = []
_WEIGHT_DTYPES = {'w_ada': _jnp.float32, 'b_ada': _jnp.float32, 'g_norm': _jnp.float32, 'w_in': _jnp.float32, 'g_sb': _jnp.float32, 'g_dil': _jnp.float32, 'w_out': _jnp.float32, 'g_final': _jnp.float32}
MOMENT_SCALE = {'w_ada': 6.446261e-02, 'b_ada': 1.416194e-01, 'g_norm': 2.832798e-02, 'w_in': 2.085264e-02, 'g_sb': 2.725300e-02, 'g_dil': 3.198275e-02, 'w_out': 2.922338e-02, 'g_final': 8.078834e+00}


def _to_microbatches(a, axis):
    t = _jnp.moveaxis(a, axis, 0)
    t = t.reshape((N_MICROBATCH, t.shape[0] // N_MICROBATCH) + t.shape[1:])
    return _jnp.moveaxis(t, 1, axis + 1)


def setup_inputs(seed: int = 0) -> dict:
    inp = _fwd_setup_inputs(seed)
    key = _jax.random.fold_in(_jax.random.key(seed), 7919)
    shape, _ = _output_shape()
    out = dict(inp)
    out["loss_target"] = _jax.random.normal(_jax.random.fold_in(key, 0), shape, _jnp.float32)
    for i, name in enumerate(TWIN_WEIGHTS):
        w = inp[name].astype(_jnp.float32)
        if MOMENT_SCALE is None:
            s = _jnp.sqrt(_jnp.mean(_jnp.square(w)) + 1e-30)
        else:
            s = MOMENT_SCALE[name]
        km, kv = _jax.random.split(_jax.random.fold_in(key, i + 1))
        out[name] = w
        out["m_" + name] = s * _jax.random.normal(km, w.shape, _jnp.float32)
        out["v_" + name] = (s * s) * _jax.random.uniform(kv, w.shape, _jnp.float32, 0.5, 1.5)
    if N_MICROBATCH > 1:
        for name, axis in PER_EXAMPLE_BATCH_AXIS.items():
            out[name] = _to_microbatches(out[name], axis)
    return {'x': out['x'], 'c': out['c'], 'w_ada': out['w_ada'], 'b_ada': out['b_ada'], 'g_norm': out['g_norm'], 'w_in': out['w_in'], 'g_sb': out['g_sb'], 'g_dil': out['g_dil'], 'w_out': out['w_out'], 'g_final': out['g_final'], 'loss_target': out['loss_target'], 'm_w_ada': out['m_w_ada'], 'm_b_ada': out['m_b_ada'], 'm_g_norm': out['m_g_norm'], 'm_w_in': out['m_w_in'], 'm_g_sb': out['m_g_sb'], 'm_g_dil': out['m_g_dil'], 'm_w_out': out['m_w_out'], 'm_g_final': out['m_g_final'], 'v_w_ada': out['v_w_ada'], 'v_b_ada': out['v_b_ada'], 'v_g_norm': out['v_g_norm'], 'v_w_in': out['v_w_in'], 'v_g_sb': out['v_g_sb'], 'v_g_dil': out['v_g_dil'], 'v_w_out': out['v_w_out'], 'v_g_final': out['v_g_final']}


def _loss(weights, diff, rest, loss_target):
    with _jax.named_scope("forward"):
        args = {**rest, TWIN_DIFF_INPUT: diff, **{k: w.astype(_WEIGHT_DTYPES[k]) for k, w in weights.items()}}
        y = _forward(args)
    with _jax.named_scope("loss_head"):
        err = _jnp.square(y.astype(_jnp.float32) - loss_target)
        return 0.5 * _jnp.sum(_jnp.mean(err, axis=-1)) if err.ndim else 0.5 * err


def _adamw(w, g, m, v):
    m = ADAM_B1 * m + (1.0 - ADAM_B1) * g
    v = ADAM_B2 * v + (1.0 - ADAM_B2) * _jnp.square(g)
    m_hat = m / (1.0 - ADAM_B1 ** ADAM_STEP)
    v_hat = v / (1.0 - ADAM_B2 ** ADAM_STEP)
    delta = -ADAM_LR * (m_hat / (_jnp.sqrt(v_hat) + ADAM_EPS) + ADAM_WD * w)
    return delta, m, v


def reference(x, c, w_ada, b_ada, g_norm, w_in, g_sb, g_dil, w_out, g_final, loss_target, m_w_ada, m_b_ada, m_g_norm, m_w_in, m_g_sb, m_g_dil, m_w_out, m_g_final, v_w_ada, v_b_ada, v_g_norm, v_w_in, v_g_sb, v_g_dil, v_w_out, v_g_final):
    given = dict(x=x, c=c, w_ada=w_ada, b_ada=b_ada, g_norm=g_norm, w_in=w_in, g_sb=g_sb, g_dil=g_dil, w_out=w_out, g_final=g_final, loss_target=loss_target, m_w_ada=m_w_ada, m_b_ada=m_b_ada, m_g_norm=m_g_norm, m_w_in=m_w_in, m_g_sb=m_g_sb, m_g_dil=m_g_dil, m_w_out=m_w_out, m_g_final=m_g_final, v_w_ada=v_w_ada, v_b_ada=v_b_ada, v_g_norm=v_g_norm, v_w_in=v_w_in, v_g_sb=v_g_sb, v_g_dil=v_g_dil, v_w_out=v_w_out, v_g_final=v_g_final)
    weights = {n: given[n] for n in TWIN_WEIGHTS}
    shared = {n: given[n] for n in SHARED_INPUTS}
    per_example = {n: given[n] for n in ['x', 'c']}
    grad_fn = _jax.value_and_grad(_loss, argnums=(0, 1))

    def one_microbatch(ex, loss_target):
        ex = dict(ex)
        diff = ex.pop(TWIN_DIFF_INPUT)
        return grad_fn(weights, diff, {**shared, **ex}, loss_target)

    if N_MICROBATCH == 1:
        loss, (grad_w, grad_x) = one_microbatch(per_example, given["loss_target"])
    else:
        def body(carry, xs):
            loss_sum, grad_sum = carry
            l_k, (gw_k, gx_k) = one_microbatch(xs[0], xs[1])
            with _jax.named_scope("update"):
                return (loss_sum + l_k, _jax.tree.map(_jnp.add, grad_sum, gw_k)), gx_k

        init = (_jnp.zeros((), _jnp.float32), _jax.tree.map(_jnp.zeros_like, weights))
        (loss, grad_w), grad_x = _jax.lax.scan(body, init, (per_example, given["loss_target"]))
    with _jax.named_scope("update"):
        delta_w, new_m, new_v = {}, {}, {}
        for n in TWIN_WEIGHTS:
            delta_w[n], new_m[n], new_v[n] = _adamw(weights[n], grad_w[n], given["m_" + n], given["v_" + n])
    return (loss, grad_x, *[grad_w[n] for n in TWIN_WEIGHTS], *[delta_w[n] for n in TWIN_WEIGHTS],
            *[new_m[n] for n in TWIN_WEIGHTS], *[new_v[n] for n in TWIN_WEIGHTS])
```

```python
import functools
import math

import jax
import jax.numpy as jnp
from jax import lax
from jax.experimental import pallas as pl
from jax.experimental.pallas import tpu as pltpu

F32 = jnp.float32
BF16 = jnp.bfloat16
MESH = pl.DeviceIdType.MESH

N_DEV = 8
HEAD_DIM = 128
EPS = 1e-6
ALIBI_MAX_BIAS = 8.0
DIL_PAIRS = ((128, 1), (512, 4), (2048, 16))
DIL_STEPS = 128
NEG = -1e30

ADAM_LR = 0.001
ADAM_B1 = 0.9
ADAM_B2 = 0.999
ADAM_EPS = 1e-08
ADAM_WD = 0.01
ADAM_STEP = 10

VMEM_LIMIT_BYTES = 56 * 1024 * 1024
SMALL_ROWS = 8

NT_DIMS = (((1,), (1,)), ((), ()))
TN_DIMS = (((0,), (0,)), ((), ()))


def _params(*semantics):
    return pltpu.CompilerParams(dimension_semantics=semantics, vmem_limit_bytes=VMEM_LIMIT_BYTES)


def _tile(n, want):
    t = min(n, want)
    assert n % t == 0, (n, want)
    return t


def _mesh_pos():
    return lax.axis_index("x"), lax.axis_index("y"), lax.axis_index("c")


def _allgather_rows(x_shard, name):
    m_per, n = x_shard.shape

    def body(x_ref, out_ref, send_sems, recv_sems, local_sem):
        x, y, c = _mesh_pos()
        me, sibling = (x, y, c), (x, y, 1 - c)
        chips = [(1 - x, y), (x, 1 - y), (1 - x, 1 - y)]

        def rows(px, py, pc):
            return out_ref.at[pl.ds((4 * px + 2 * py + pc) * m_per, m_per), :]

        def copy(k, block, to, src=None):
            return pltpu.make_async_remote_copy(
                src_ref=rows(*block) if src is None else src, dst_ref=rows(*block),
                send_sem=send_sems.at[k], recv_sem=recv_sems.at[k], device_id=to, device_id_type=MESH)

        mine = pltpu.make_async_copy(x_ref, rows(*me), local_sem)
        mine.start()
        first = [copy(0, me, sibling, src=x_ref)]
        first += [copy(1 + j, me, (*chip, c), src=x_ref) for j, chip in enumerate(chips)]
        for cp in first:
            cp.start()
        passed = [copy(4 + j, (*chip, c), sibling) for j, chip in enumerate(chips)]
        for j, chip in enumerate(chips):
            copy(1 + j, (*chip, c), me).wait_recv()
            passed[j].start()
        copy(0, sibling, me).wait_recv()
        for j, chip in enumerate(chips):
            copy(4 + j, (*chip, 1 - c), me).wait_recv()
        for cp in first + passed:
            cp.wait_send()
        mine.wait()

    return pl.pallas_call(
        body, name=name,
        out_shape=jax.ShapeDtypeStruct((N_DEV * m_per, n), x_shard.dtype),
        in_specs=[pl.BlockSpec(memory_space=pltpu.VMEM)],
        out_specs=pl.BlockSpec(memory_space=pltpu.VMEM),
        scratch_shapes=[pltpu.SemaphoreType.DMA((7,)), pltpu.SemaphoreType.DMA((7,)), pltpu.SemaphoreType.DMA],
    )(x_shard)


def _allgather_weights(wa, wb):
    def body(a_ref, b_ref, oa_ref, ob_ref, send_sems, recv_sems, local_sems):
        x, y, c = _mesh_pos()
        me, sibling = (x, y, c), (x, y, 1 - c)
        chips = [(1 - x, y), (x, 1 - y), (1 - x, 1 - y)]
        arrays = ((a_ref, oa_ref), (b_ref, ob_ref))

        def slot(out_ref, px, py, pc):
            return out_ref.at[4 * px + 2 * py + pc]

        def copy(t, k, block, to, src=None):
            dst = slot(arrays[t][1], *block)
            return pltpu.make_async_remote_copy(
                src_ref=dst if src is None else src, dst_ref=dst,
                send_sem=send_sems.at[7 * t + k], recv_sem=recv_sems.at[7 * t + k],
                device_id=to, device_id_type=MESH)

        mine, first, passed = [], [], []
        for t, (src_ref, out_ref) in enumerate(arrays):
            mine.append(pltpu.make_async_copy(src_ref, slot(out_ref, *me), local_sems.at[t]))
            mine[-1].start()
            first.append(copy(t, 0, me, sibling, src=src_ref))
            first += [copy(t, 1 + j, me, (*chip, c), src=src_ref) for j, chip in enumerate(chips)]
        for cp in first:
            cp.start()
        for t in range(2):
            for j, chip in enumerate(chips):
                copy(t, 1 + j, (*chip, c), me).wait_recv()
                passed.append(copy(t, 4 + j, (*chip, c), sibling))
                passed[-1].start()
        for t in range(2):
            copy(t, 0, sibling, me).wait_recv()
            for j, chip in enumerate(chips):
                copy(t, 4 + j, (*chip, 1 - c), me).wait_recv()
        for cp in first + passed:
            cp.wait_send()
        for cp in mine:
            cp.wait()

    any_spec = pl.BlockSpec(memory_space=pl.ANY)
    return pl.pallas_call(
        body, name="ag_weights",
        out_shape=(jax.ShapeDtypeStruct((N_DEV,) + wa.shape, wa.dtype),
                   jax.ShapeDtypeStruct((N_DEV,) + wb.shape, wb.dtype)),
        in_specs=[any_spec, any_spec], out_specs=(any_spec, any_spec),
        scratch_shapes=[pltpu.SemaphoreType.DMA((14,)), pltpu.SemaphoreType.DMA((14,)),
                        pltpu.SemaphoreType.DMA((2,))],
    )(wa, wb)


def _exchange_partials(pa, pb):
    def body(a_ref, b_ref, ra_ref, rb_ref, send_sems, recv_sems, local_sems):
        x, y, c = _mesh_pos()
        my = 4 * x + 2 * y + c
        copies = []
        for t, (src, dst) in enumerate(((a_ref, ra_ref), (b_ref, rb_ref))):
            local = pltpu.make_async_copy(src.at[my], dst.at[my], local_sems.at[t])
            local.start()
            copies.append(local)
            for d in range(1, N_DEV):
                px = 1 - x if d & 4 else x
                py = 1 - y if d & 2 else y
                pc = 1 - c if d & 1 else c
                k = 7 * t + d - 1
                cp = pltpu.make_async_remote_copy(
                    src_ref=src.at[4 * px + 2 * py + pc], dst_ref=dst.at[my],
                    send_sem=send_sems.at[k], recv_sem=recv_sems.at[k],
                    device_id=(px, py, pc), device_id_type=MESH)
                cp.start()
                copies.append(cp)
        for cp in copies:
            cp.wait()

    any_spec = pl.BlockSpec(memory_space=pl.ANY)
    return pl.pallas_call(
        body, name="exchange_partials",
        out_shape=(jax.ShapeDtypeStruct(pa.shape, pa.dtype), jax.ShapeDtypeStruct(pb.shape, pb.dtype)),
        in_specs=[any_spec, any_spec], out_specs=(any_spec, any_spec),
        scratch_shapes=[pltpu.SemaphoreType.DMA((14,)), pltpu.SemaphoreType.DMA((14,)),
                        pltpu.SemaphoreType.DMA((2,))],
    )(pa, pb)


def _mm_call(a, b, dims, nk, grid, a_spec, b_spec, o_spec, out_shape, acc_shape, name):
    def body(a_ref, b_ref, o_ref, acc_ref):
        k = pl.program_id(2)

        @pl.when(k == 0)
        def _():
            acc_ref[...] = jnp.zeros_like(acc_ref)

        acc_ref[...] += lax.dot_general(a_ref[...], b_ref[...], dims, preferred_element_type=F32)

        @pl.when(k == nk - 1)
        def _():
            o_ref[...] = acc_ref[...].astype(o_ref.dtype)

    return pl.pallas_call(
        body, name=name, grid=grid, in_specs=[a_spec, b_spec], out_specs=o_spec, out_shape=out_shape,
        scratch_shapes=[pltpu.VMEM(acc_shape, F32)],
        compiler_params=_params("parallel", "parallel", "arbitrary"),
    )(a, b)


MM_TM, MM_TN, MM_TK = 1024, 2048, 1024


def _mm_nn(a, b3, out_dtype, name):
    m, kk = a.shape
    g, _, nb = b3.shape
    tm, tn, tk = _tile(m, MM_TM), _tile(nb, MM_TN), _tile(kk, MM_TK)
    npb = nb // tn
    return _mm_call(
        a, b3, (((1,), (0,)), ((), ())), kk // tk, (m // tm, g * npb, kk // tk),
        pl.BlockSpec((tm, tk), lambda i, j, k: (i, k)),
        pl.BlockSpec((None, tk, tn), lambda i, j, k: (j // npb, k, j % npb)),
        pl.BlockSpec((tm, tn), lambda i, j, k: (i, j)),
        jax.ShapeDtypeStruct((m, g * nb), out_dtype), (tm, tn), name)


def _mm_nt(a, b3, out_dtype, name):
    m, kk = a.shape
    g, n, kb = b3.shape
    tm, tn, tk = _tile(m, MM_TM), _tile(n, MM_TN), _tile(kb, MM_TK)
    kpb = kb // tk
    return _mm_call(
        a, b3, NT_DIMS, kk // tk, (m // tm, n // tn, kk // tk),
        pl.BlockSpec((tm, tk), lambda i, j, k: (i, k)),
        pl.BlockSpec((None, tn, tk), lambda i, j, k: (k // kpb, j, k % kpb)),
        pl.BlockSpec((tm, tn), lambda i, j, k: (i, j)),
        jax.ShapeDtypeStruct((m, n), out_dtype), (tm, tn), name)


def _mm_tn(a, b, g, out_dtype, name):
    t, m = a.shape
    nb = b.shape[1] // g
    tm, tn, tk = _tile(m, MM_TM), _tile(nb, MM_TN), _tile(t, MM_TK)
    npb = nb // tn
    return _mm_call(
        a, b, TN_DIMS, t // tk, (m // tm, g * npb, t // tk),
        pl.BlockSpec((tk, tm), lambda i, j, k: (k, i)),
        pl.BlockSpec((tk, tn), lambda i, j, k: (k, j)),
        pl.BlockSpec((None, tm, tn), lambda i, j, k: (j // npb, i, j % npb)),
        jax.ShapeDtypeStruct((g, m, nb), out_dtype), (tm, tn), name)


def _silu(z):
    return z * jax.nn.sigmoid(z)


def _ada_fwd(c_all, w_shard, b_own):
    r, d = c_all.shape
    na = w_shard.shape[1]
    tk = _tile(d, 512)
    nk = d // tk

    def body(c_ref, w_ref, b_ref, o_ref):
        k = pl.program_id(0)

        @pl.when(k == 0)
        def _():
            o_ref[...] = jnp.zeros_like(o_ref) + b_ref[...]

        cs = _silu(c_ref[...]).astype(BF16)
        o_ref[...] += jnp.dot(cs, w_ref[...].astype(BF16), preferred_element_type=F32)

    return pl.pallas_call(
        body, name="ada_fwd", grid=(nk,),
        in_specs=[pl.BlockSpec((r, tk), lambda k: (0, k)), pl.BlockSpec((tk, na), lambda k: (k, 0)),
                  pl.BlockSpec((1, na), lambda k: (0, 0))],
        out_specs=pl.BlockSpec((r, na), lambda k: (0, 0)),
        out_shape=jax.ShapeDtypeStruct((r, na), F32),
        compiler_params=_params("arbitrary"),
    )(c_all, w_shard, b_own)


def _adam(w, g, m, v):
    nm = ADAM_B1 * m + (1.0 - ADAM_B1) * g
    nv = ADAM_B2 * v + (1.0 - ADAM_B2) * (g * g)
    m_hat = nm / (1.0 - ADAM_B1 ** ADAM_STEP)
    v_hat = nv / (1.0 - ADAM_B2 ** ADAM_STEP)
    delta = -ADAM_LR * (m_hat / (jnp.sqrt(v_hat) + ADAM_EPS) + ADAM_WD * w)
    return delta, nm, nv


def _ada_bwd_adam(c_rows, dmod_cols, w, m, v):
    bg, d = c_rows.shape
    na = w.shape[1]
    tr = _tile(d, 256)

    def body(c_ref, dm_ref, w_ref, m_ref, v_ref, g_ref, d_ref, nm_ref, nv_ref):
        cs = _silu(c_ref[...]).astype(BF16)
        g = lax.dot_general(cs, dm_ref[...].astype(BF16), TN_DIMS, preferred_element_type=F32)
        delta, nm, nv = _adam(w_ref[...], g, m_ref[...], v_ref[...])
        g_ref[...] = g
        d_ref[...] = delta
        nm_ref[...] = nm
        nv_ref[...] = nv

    blk = pl.BlockSpec((tr, na), lambda i: (i, 0))
    shp = jax.ShapeDtypeStruct((d, na), F32)
    return pl.pallas_call(
        body, name="ada_bwd_adam", grid=(d // tr,),
        in_specs=[pl.BlockSpec((bg, tr), lambda i: (0, i)), pl.BlockSpec((bg, na), lambda i: (0, 0)), blk, blk, blk],
        out_specs=(blk, blk, blk, blk), out_shape=(shp, shp, shp, shp),
        compiler_params=_params("parallel"),
    )(c_rows, dmod_cols, w, m, v)


def _small_adam(pkg_all, n_batch_rows, w, m, v):
    d = w.shape[1]

    def body(p_ref, w_ref, m_ref, v_ref, g_ref, d_ref, nm_ref, nv_ref):
        for part in range(3):
            acc = jnp.zeros((1, d), F32)
            for dev in range(N_DEV):
                for b in range(n_batch_rows // 3):
                    acc = acc + p_ref[dev, 3 * b + part:3 * b + part + 1, :]
            g_ref[part:part + 1, :] = acc
        for rrow in range(3):
            acc = jnp.zeros((1, d), F32)
            for dev in range(N_DEV):
                acc = acc + p_ref[dev, n_batch_rows + rrow:n_batch_rows + rrow + 1, :]
            g_ref[3 + rrow:4 + rrow, :] = acc
        g_ref[6:8, :] = jnp.zeros((2, d), F32)
        g = g_ref[...]
        delta, nm, nv = _adam(w_ref[...], g, m_ref[...], v_ref[...])
        d_ref[...] = delta
        nm_ref[...] = nm
        nv_ref[...] = nv

    vm = pl.BlockSpec(memory_space=pltpu.VMEM)
    shp = jax.ShapeDtypeStruct((SMALL_ROWS, d), F32)
    return pl.pallas_call(
        body, name="small_adam", in_specs=[vm, vm, vm, vm], out_specs=(vm, vm, vm, vm),
        out_shape=(shp, shp, shp, shp),
    )(pkg_all, w, m, v)


def _adam_from_partials(recv, w, m, v, name):
    _, r, c = recv.shape
    tr = _tile(r, 128)

    def body(p_ref, w_ref, m_ref, v_ref, g_ref, d_ref, nm_ref, nv_ref):
        g = p_ref[0].astype(F32)
        for dev in range(1, N_DEV):
            g = g + p_ref[dev].astype(F32)
        delta, nm, nv = _adam(w_ref[...], g, m_ref[...], v_ref[...])
        g_ref[...] = g
        d_ref[...] = delta
        nm_ref[...] = nm
        nv_ref[...] = nv

    blk = pl.BlockSpec((tr, c), lambda i: (i, 0))
    shp = jax.ShapeDtypeStruct((r, c), F32)
    return pl.pallas_call(
        body, name=name, grid=(r // tr,),
        in_specs=[pl.BlockSpec((N_DEV, tr, c), lambda i: (0, i, 0)), blk, blk, blk],
        out_specs=(blk, blk, blk, blk), out_shape=(shp, shp, shp, shp),
        compiler_params=_params("parallel"),
    )(recv, w, m, v)


def _norm_mod(x, g_norm, scale, shift):
    b, s, d = x.shape
    ts = _tile(s, 256)

    def body(x_ref, g_ref, sc_ref, sh_ref, h_ref):
        xv = x_ref[...]
        r = lax.rsqrt(jnp.mean(xv * xv, axis=-1, keepdims=True) + EPS)
        xn = (xv * r) * g_ref[...]
        h_ref[...] = (xn * (1.0 + sc_ref[...]) + sh_ref[...]).astype(BF16)

    tok = pl.BlockSpec((None, ts, d), lambda i, j: (i, j, 0))
    per_b = pl.BlockSpec((None, 1, d), lambda i, j: (i, 0, 0))
    return pl.pallas_call(
        body, name="norm_mod", grid=(b, s // ts),
        in_specs=[tok, pl.BlockSpec((1, d), lambda i, j: (0, 0)), per_b, per_b],
        out_specs=tok, out_shape=jax.ShapeDtypeStruct((b, s, d), BF16),
        compiler_params=_params("parallel", "parallel"),
    )(x, g_norm, scale, shift)


def _final_fwd_bwd(x, out, gate, g_final, target):
    b, s, d = x.shape
    ts = _tile(s, 256)

    def body(x_ref, o_ref, gt_ref, g_ref, t_ref, loss_ref, dx2_ref, dout_ref, dgate_ref, gg_ref):
        i, j = pl.program_id(0), pl.program_id(1)

        @pl.when((i == 0) & (j == 0))
        def _():
            loss_ref[...] = jnp.zeros_like(loss_ref)
            gg_ref[...] = jnp.zeros_like(gg_ref)

        @pl.when(j == 0)
        def _():
            dgate_ref[...] = jnp.zeros_like(dgate_ref)

        ov = o_ref[...]
        gt = gt_ref[...]
        x2 = x_ref[...] + gt * ov
        r = lax.rsqrt(jnp.mean(x2 * x2, axis=-1, keepdims=True) + EPS)
        xh = x2 * r
        err = xh * g_ref[...] - t_ref[...]
        loss_ref[...] += 0.5 * jnp.sum(jnp.mean(err * err, axis=-1, keepdims=True), axis=0, keepdims=True)
        dfin = err * (1.0 / d)
        gg_ref[...] += jnp.sum(dfin * xh, axis=0, keepdims=True)
        dxh = dfin * g_ref[...]
        dx2 = r * (dxh - xh * jnp.mean(dxh * xh, axis=-1, keepdims=True))
        dx2_ref[...] = dx2
        dout_ref[...] = (gt * dx2).astype(BF16)
        dgate_ref[...] += jnp.sum(dx2 * ov, axis=0, keepdims=True)

    tok = pl.BlockSpec((None, ts, d), lambda i, j: (i, j, 0))
    per_b = pl.BlockSpec((None, 1, d), lambda i, j: (i, 0, 0))
    vec = pl.BlockSpec((1, d), lambda i, j: (0, 0))
    return pl.pallas_call(
        body, name="final_fwd_bwd", grid=(b, s // ts),
        in_specs=[tok, tok, per_b, vec, tok],
        out_specs=(pl.BlockSpec((8, 128), lambda i, j: (0, 0)), tok, tok, per_b, vec),
        out_shape=(jax.ShapeDtypeStruct((8, 128), F32), jax.ShapeDtypeStruct((b, s, d), F32),
                   jax.ShapeDtypeStruct((b, s, d), BF16), jax.ShapeDtypeStruct((b, 1, d), F32),
                   jax.ShapeDtypeStruct((1, d), F32)),
        compiler_params=_params("arbitrary", "arbitrary"),
    )(x, out, gate, g_final, target)


def _norm_bwd(x, dh, dx2, scale, g_norm):
    b, s, d = x.shape
    ts = _tile(s, 256)

    def body(x_ref, dh_ref, dx2_ref, sc_ref, g_ref, gx_ref, dsh_ref, dsc_ref, gg_ref):
        i, j = pl.program_id(0), pl.program_id(1)

        @pl.when((i == 0) & (j == 0))
        def _():
            gg_ref[...] = jnp.zeros_like(gg_ref)

        @pl.when(j == 0)
        def _():
            dsh_ref[...] = jnp.zeros_like(dsh_ref)
            dsc_ref[...] = jnp.zeros_like(dsc_ref)

        xv = x_ref[...]
        dhv = dh_ref[...]
        r = lax.rsqrt(jnp.mean(xv * xv, axis=-1, keepdims=True) + EPS)
        xh = xv * r
        xn = xh * g_ref[...]
        dsh_ref[...] += jnp.sum(dhv, axis=0, keepdims=True)
        dsc_ref[...] += jnp.sum(dhv * xn, axis=0, keepdims=True)
        dxn = dhv * (1.0 + sc_ref[...])
        gg_ref[...] += jnp.sum(dxn * xh, axis=0, keepdims=True)
        dxh = dxn * g_ref[...]
        gx_ref[...] = dx2_ref[...] + r * (dxh - xh * jnp.mean(dxh * xh, axis=-1, keepdims=True))

    tok = pl.BlockSpec((None, ts, d), lambda i, j: (i, j, 0))
    per_b = pl.BlockSpec((None, 1, d), lambda i, j: (i, 0, 0))
    vec = pl.BlockSpec((1, d), lambda i, j: (0, 0))
    return pl.pallas_call(
        body, name="norm_bwd", grid=(b, s // ts),
        in_specs=[tok, tok, tok, per_b, vec],
        out_specs=(tok, per_b, per_b, vec),
        out_shape=(jax.ShapeDtypeStruct((b, s, d), F32), jax.ShapeDtypeStruct((b, 1, d), F32),
                   jax.ShapeDtypeStruct((b, 1, d), F32), jax.ShapeDtypeStruct((1, d), F32)),
        compiler_params=_params("arbitrary", "arbitrary"),
    )(x, dh, dx2, scale, g_norm)


def _gate_fwd(y_sb, y_dl, proj, g_sb, g_dl):
    t, e = y_sb.shape
    n_heads = e // HEAD_DIM
    tt = _tile(t, 256)

    def body(ys_ref, yd_ref, zs_ref, zd_ref, gs_ref, gd_ref, o_ref):
        for grp, (y_ref, z_ref, g_ref) in enumerate(((ys_ref, zs_ref, gs_ref), (yd_ref, zd_ref, gd_ref))):
            for h in range(n_heads):
                sl = slice(h * HEAD_DIM, (h + 1) * HEAD_DIM)
                y = y_ref[:, sl]
                r = lax.rsqrt(jnp.mean(y * y, axis=-1, keepdims=True) + EPS)
                yn = (y * r) * g_ref[:, sl]
                z = z_ref[:, sl].astype(F32)
                o_ref[:, grp * e + h * HEAD_DIM:grp * e + (h + 1) * HEAD_DIM] = (yn * _silu(z)).astype(BF16)

    yblk = pl.BlockSpec((tt, e), lambda i: (i, 0))
    gblk = pl.BlockSpec((1, e), lambda i: (0, 0))
    return pl.pallas_call(
        body, name="gate_fwd", grid=(t // tt,),
        in_specs=[yblk, yblk, pl.BlockSpec((tt, e), lambda i: (i, 3)), pl.BlockSpec((tt, e), lambda i: (i, 7)),
                  gblk, gblk],
        out_specs=pl.BlockSpec((tt, 2 * e), lambda i: (i, 0)),
        out_shape=jax.ShapeDtypeStruct((t, 2 * e), BF16),
        compiler_params=_params("parallel"),
    )(y_sb, y_dl, proj, proj, g_sb, g_dl)


def _gate_bwd(dyg, y_sb, y_dl, proj, g_sb, g_dl):
    t, e = y_sb.shape
    n_heads = e // HEAD_DIM
    tt = _tile(t, 256)

    def body(dg_ref, ys_ref, yd_ref, zs_ref, zd_ref, gs_ref, gd_ref,
             dys_ref, dyd_ref, dzs_ref, dzd_ref, ggs_ref, ggd_ref):
        @pl.when(pl.program_id(0) == 0)
        def _():
            ggs_ref[...] = jnp.zeros_like(ggs_ref)
            ggd_ref[...] = jnp.zeros_like(ggd_ref)

        groups = ((ys_ref, zs_ref, gs_ref, dys_ref, dzs_ref, ggs_ref), (yd_ref, zd_ref, gd_ref, dyd_ref, dzd_ref, ggd_ref))
        for grp, (y_ref, z_ref, g_ref, dy_ref, dz_ref, gg_ref) in enumerate(groups):
            for h in range(n_heads):
                sl = slice(h * HEAD_DIM, (h + 1) * HEAD_DIM)
                dg = dg_ref[:, grp * e + h * HEAD_DIM:grp * e + (h + 1) * HEAD_DIM].astype(F32)
                y = y_ref[:, sl]
                z = z_ref[:, sl].astype(F32)
                g = g_ref[:, sl]
                r = lax.rsqrt(jnp.mean(y * y, axis=-1, keepdims=True) + EPS)
                yh = y * r
                sig = jax.nn.sigmoid(z)
                dyn = dg * (z * sig)
                dz_ref[:, sl] = (dg * (yh * g) * (sig * (1.0 + z * (1.0 - sig)))).astype(BF16)
                gg_ref[:, sl] += jnp.sum(dyn * yh, axis=0, keepdims=True)
                dyh = dyn * g
                dy_ref[:, sl] = (r * (dyh - yh * jnp.mean(dyh * yh, axis=-1, keepdims=True))).astype(BF16)

    yblk = pl.BlockSpec((tt, e), lambda i: (i, 0))
    gblk = pl.BlockSpec((1, e), lambda i: (0, 0))
    act = jax.ShapeDtypeStruct((t, e), BF16)
    vec = jax.ShapeDtypeStruct((1, e), F32)
    return pl.pallas_call(
        body, name="gate_bwd", grid=(t // tt,),
        in_specs=[pl.BlockSpec((tt, 2 * e), lambda i: (i, 0)), yblk, yblk,
                  pl.BlockSpec((tt, e), lambda i: (i, 3)), pl.BlockSpec((tt, e), lambda i: (i, 7)), gblk, gblk],
        out_specs=(yblk, yblk, yblk, yblk, gblk, gblk),
        out_shape=(act, act, act, act, vec, vec),
        compiler_params=_params("arbitrary"),
    )(dyg, y_sb, y_dl, proj, proj, g_sb, g_dl)


ATT_TQ = 256


def _split3_dot(x, u):
    hi = x.astype(BF16)
    r1 = x - hi.astype(F32)
    mid = r1.astype(BF16)
    lo = (r1 - mid.astype(F32)).astype(BF16)
    dot = functools.partial(jnp.dot, preferred_element_type=F32)
    return dot(hi, u) + dot(mid, u) + dot(lo, u)


def _att_specs(b, s, e, tq, col0):
    n_heads = e // HEAD_DIM
    q_spec = pl.BlockSpec((None, tq, HEAD_DIM), lambda i, h, j: (i, j, col0 + h))
    k_spec = pl.BlockSpec((None, s, HEAD_DIM), lambda i, h, j: (i, 0, col0 + n_heads + h))
    v_spec = pl.BlockSpec((None, s, HEAD_DIM), lambda i, h, j: (i, 0, col0 + 2 * n_heads + h))
    return q_spec, k_spec, v_spec


def _sb_logs(q, k, row, col, offset, inv):
    z = lax.dot_general(q, k, NT_DIMS, preferred_element_type=F32) * inv
    mask = (offset + row - col) > 0
    e = jnp.exp(-jnp.abs(z))
    t = jnp.log1p(e)
    log_beta = jnp.minimum(z, 0.0) - t
    lom = jnp.where(mask, -jnp.maximum(z, 0.0) - t, 0.0)
    return z, e, log_beta, lom, mask


def _sb_fwd(proj3, e):
    b, s, _ = proj3.shape
    n_heads = e // HEAD_DIM
    tq = _tile(s, ATT_TQ)
    nq = s // tq
    inv = 1.0 / math.sqrt(HEAD_DIM)

    def body(q_ref, k_ref, v_ref, y_ref, tot_ref, acc_ref, car_ref):
        i = pl.program_id(2)
        q = q_ref[...]
        row = lax.broadcasted_iota(jnp.int32, (tq, tq), 0)
        col = lax.broadcasted_iota(jnp.int32, (tq, tq), 1)
        u_after = (row > col).astype(BF16)
        acc_ref[...] = jnp.zeros_like(acc_ref)
        car_ref[...] = jnp.zeros_like(car_ref)

        def step(it, carry):
            j = i - it
            start = pl.multiple_of(j * tq, tq)
            k = k_ref[pl.ds(start, tq), :]
            v = v_ref[pl.ds(start, tq), :]
            _, _, log_beta, lom, mask = _sb_logs(q, k, row, col, it * tq, inv)
            suffix = _split3_dot(lom, u_after) + car_ref[...]
            a = jnp.where(mask, jnp.exp(log_beta + suffix), 0.0)
            acc_ref[...] += jnp.dot(a.astype(BF16), v, preferred_element_type=F32)
            car_ref[...] += jnp.sum(lom, axis=1, keepdims=True)
            return carry

        lax.fori_loop(0, i + 1, step, 0)
        y_ref[...] = acc_ref[...]
        tot_ref[...] = jnp.broadcast_to(car_ref[...], (tq, HEAD_DIM))

    q_spec, k_spec, v_spec = _att_specs(b, s, e, tq, 0)
    blk_q = pl.BlockSpec((None, tq, HEAD_DIM), lambda i, h, j: (i, j, h))
    shp = jax.ShapeDtypeStruct((b, s, e), F32)
    return pl.pallas_call(
        body, name="sb_fwd", grid=(b, n_heads, nq),
        in_specs=[q_spec, k_spec, v_spec],
        out_specs=(blk_q, blk_q), out_shape=(shp, shp),
        scratch_shapes=[pltpu.VMEM((tq, HEAD_DIM), F32), pltpu.VMEM((tq, 1), F32)],
        compiler_params=_params("parallel", "parallel", "arbitrary"),
    )(proj3, proj3, proj3)


def _sb_bwd(proj3, lom_total, dy):
    b, s, e = dy.shape
    n_heads = e // HEAD_DIM
    tq = _tile(s, ATT_TQ)
    nq = s // tq
    inv = 1.0 / math.sqrt(HEAD_DIM)

    def body(q_ref, k_ref, v_ref, tot_ref, dy_ref, dq_ref, dk_ref, dv_ref, dqa, dka, dva, car, car2):
        i = pl.program_id(2)

        @pl.when(i == 0)
        def _():
            dka[...] = jnp.zeros_like(dka)
            dva[...] = jnp.zeros_like(dva)

        q = q_ref[...]
        dy_b = dy_ref[...]
        total = tot_ref[:, 0:1]
        row = lax.broadcasted_iota(jnp.int32, (tq, tq), 0)
        col = lax.broadcasted_iota(jnp.int32, (tq, tq), 1)
        u_upto = (row <= col).astype(BF16)
        u_before = (row < col).astype(BF16)
        dqa[...] = jnp.zeros_like(dqa)
        car[...] = jnp.zeros_like(car)
        car2[...] = jnp.zeros_like(car2)

        def step(j, carry):
            start = pl.multiple_of(j * tq, tq)
            k = k_ref[pl.ds(start, tq), :]
            v = v_ref[pl.ds(start, tq), :]
            z, ez, log_beta, lom, mask = _sb_logs(q, k, row, col, (i - j) * tq, inv)
            suffix = total - (_split3_dot(lom, u_upto) + car[...])
            a = jnp.where(mask, jnp.exp(log_beta + suffix), 0.0)
            rcp = 1.0 / (1.0 + ez)
            pos = z >= 0.0
            beta = jnp.where(pos, rcp, ez * rcp)
            one_minus_beta = jnp.where(pos, ez * rcp, rcp)
            da = lax.dot_general(dy_b, v, NT_DIMS, preferred_element_type=F32)
            dl = a * da
            prefix = jnp.where(mask, _split3_dot(dl, u_before) + car2[...], 0.0)
            dz = ((dl * one_minus_beta - prefix * beta) * inv).astype(BF16)
            dqa[...] += jnp.dot(dz, k, preferred_element_type=F32)
            dka[pl.ds(start, tq), :] += lax.dot_general(dz, q, TN_DIMS, preferred_element_type=F32)
            dva[pl.ds(start, tq), :] += lax.dot_general(a.astype(BF16), dy_b, TN_DIMS, preferred_element_type=F32)
            car[...] += jnp.sum(lom, axis=1, keepdims=True)
            car2[...] += jnp.sum(dl, axis=1, keepdims=True)
            return carry

        lax.fori_loop(0, i + 1, step, 0)
        dq_ref[...] = dqa[...].astype(BF16)

        @pl.when(i == nq - 1)
        def _():
            dk_ref[...] = dka[...].astype(BF16)
            dv_ref[...] = dva[...].astype(BF16)

    q_spec, k_spec, v_spec = _att_specs(b, s, e, tq, 0)
    blk_q = pl.BlockSpec((None, tq, HEAD_DIM), lambda i, h, j: (i, j, h))
    blk_kv = pl.BlockSpec((None, s, HEAD_DIM), lambda i, h, j: (i, 0, h))
    shp = jax.ShapeDtypeStruct((b, s, e), BF16)
    return pl.pallas_call(
        body, name="sb_bwd", grid=(b, n_heads, nq),
        in_specs=[q_spec, k_spec, v_spec, blk_q, blk_q],
        out_specs=(blk_q, blk_kv, blk_kv), out_shape=(shp, shp, shp),
        scratch_shapes=[pltpu.VMEM((tq, HEAD_DIM), F32), pltpu.VMEM((s, HEAD_DIM), F32), pltpu.VMEM((s, HEAD_DIM), F32),
                        pltpu.VMEM((tq, 1), F32), pltpu.VMEM((tq, 1), F32)],
        compiler_params=_params("parallel", "parallel", "arbitrary"),
    )(proj3, proj3, proj3, lom_total, dy)


def _dil_scores(q, k, row, col, offset, inv, slope):
    sc = lax.dot_general(q, k, NT_DIMS, preferred_element_type=F32) * inv
    dist = offset + row - col
    cnt = jnp.zeros(dist.shape, jnp.int32)
    for window, dilation in DIL_PAIRS:
        cnt = cnt + (((dist & (dilation - 1)) == 0) & (dist <= window)).astype(jnp.int32)
    bias = jnp.where(cnt == 3, math.log(3.0), jnp.where(cnt == 2, math.log(2.0), 0.0))
    valid = (dist >= 0) & (cnt > 0)
    return jnp.where(valid, sc - slope * dist.astype(F32) + bias, NEG)


def _dil_fwd(proj3, e, slopes):
    b, s, _ = proj3.shape
    n_heads = e // HEAD_DIM
    tq = _tile(s, ATT_TQ)
    nq = s // tq
    inv = 1.0 / math.sqrt(HEAD_DIM)

    def body(q_ref, k_ref, v_ref, sl_ref, y_ref, lse_ref, acc_ref, m_ref, l_ref):
        i = pl.program_id(2)
        q = q_ref[...]
        slope = sl_ref[0:1, 0:1]
        row = lax.broadcasted_iota(jnp.int32, (tq, tq), 0)
        col = lax.broadcasted_iota(jnp.int32, (tq, tq), 1)
        acc_ref[...] = jnp.zeros_like(acc_ref)
        m_ref[...] = jnp.full_like(m_ref, NEG)
        l_ref[...] = jnp.zeros_like(l_ref)

        def step(it, carry):
            j = i - it
            start = pl.multiple_of(j * tq, tq)
            k = k_ref[pl.ds(start, tq), :]
            v = v_ref[pl.ds(start, tq), :]
            sc = _dil_scores(q, k, row, col, it * tq, inv, slope)
            m_old = m_ref[...]
            m_new = jnp.maximum(m_old, jnp.max(sc, axis=1, keepdims=True))
            p = jnp.exp(sc - m_new)
            alpha = jnp.exp(m_old - m_new)
            l_ref[...] = alpha * l_ref[...] + jnp.sum(p, axis=1, keepdims=True)
            acc_ref[...] = alpha * acc_ref[...] + jnp.dot(p.astype(BF16), v, preferred_element_type=F32)
            m_ref[...] = m_new
            return carry

        lax.fori_loop(0, i + 1, step, 0)
        y_ref[...] = acc_ref[...] / l_ref[...]
        lse_ref[...] = jnp.broadcast_to(m_ref[...] + jnp.log(l_ref[...]), (tq, HEAD_DIM))

    q_spec, k_spec, v_spec = _att_specs(b, s, e, tq, 4 * n_heads)
    blk_q = pl.BlockSpec((None, tq, HEAD_DIM), lambda i, h, j: (i, j, h))
    shp = jax.ShapeDtypeStruct((b, s, e), F32)
    return pl.pallas_call(
        body, name="dil_fwd", grid=(b, n_heads, nq),
        in_specs=[q_spec, k_spec, v_spec, pl.BlockSpec((None, 8, HEAD_DIM), lambda i, h, j: (h, 0, 0))],
        out_specs=(blk_q, blk_q), out_shape=(shp, shp),
        scratch_shapes=[pltpu.VMEM((tq, HEAD_DIM), F32), pltpu.VMEM((tq, 1), F32), pltpu.VMEM((tq, 1), F32)],
        compiler_params=_params("parallel", "parallel", "arbitrary"),
    )(proj3, proj3, proj3, slopes)


def _dil_bwd(proj3, y, lse, dy, slopes):
    b, s, e = y.shape
    n_heads = e // HEAD_DIM
    tq = _tile(s, ATT_TQ)
    nq = s // tq
    inv = 1.0 / math.sqrt(HEAD_DIM)

    def body(q_ref, k_ref, v_ref, sl_ref, y_ref, lse_ref, dy_ref, dq_ref, dk_ref, dv_ref, dqa, dka, dva):
        i = pl.program_id(2)

        @pl.when(i == 0)
        def _():
            dka[...] = jnp.zeros_like(dka)
            dva[...] = jnp.zeros_like(dva)

        q = q_ref[...]
        dy_b = dy_ref[...]
        slope = sl_ref[0:1, 0:1]
        lse_col = lse_ref[:, 0:1]
        delta = jnp.sum(dy_b.astype(F32) * y_ref[...], axis=1, keepdims=True)
        row = lax.broadcasted_iota(jnp.int32, (tq, tq), 0)
        col = lax.broadcasted_iota(jnp.int32, (tq, tq), 1)
        dqa[...] = jnp.zeros_like(dqa)

        def step(it, carry):
            j = i - it
            start = pl.multiple_of(j * tq, tq)
            k = k_ref[pl.ds(start, tq), :]
            v = v_ref[pl.ds(start, tq), :]
            sc = _dil_scores(q, k, row, col, it * tq, inv, slope)
            p = jnp.exp(sc - lse_col)
            dp = lax.dot_general(dy_b, v, NT_DIMS, preferred_element_type=F32)
            ds = ((p * (dp - delta)) * inv).astype(BF16)
            dqa[...] += jnp.dot(ds, k, preferred_element_type=F32)
            dka[pl.ds(start, tq), :] += lax.dot_general(ds, q, TN_DIMS, preferred_element_type=F32)
            dva[pl.ds(start, tq), :] += lax.dot_general(p.astype(BF16), dy_b, TN_DIMS, preferred_element_type=F32)
            return carry

        lax.fori_loop(0, i + 1, step, 0)
        dq_ref[...] = dqa[...].astype(BF16)

        @pl.when(i == nq - 1)
        def _():
            dk_ref[...] = dka[...].astype(BF16)
            dv_ref[...] = dva[...].astype(BF16)

    q_spec, k_spec, v_spec = _att_specs(b, s, e, tq, 4 * n_heads)
    blk_q = pl.BlockSpec((None, tq, HEAD_DIM), lambda i, h, j: (i, j, h))
    blk_kv = pl.BlockSpec((None, s, HEAD_DIM), lambda i, h, j: (i, 0, h))
    shp = jax.ShapeDtypeStruct((b, s, e), BF16)
    return pl.pallas_call(
        body, name="dil_bwd", grid=(b, n_heads, nq),
        in_specs=[q_spec, k_spec, v_spec, pl.BlockSpec((None, 8, HEAD_DIM), lambda i, h, j: (h, 0, 0)),
                  blk_q, blk_q, blk_q],
        out_specs=(blk_q, blk_kv, blk_kv), out_shape=(shp, shp, shp),
        scratch_shapes=[pltpu.VMEM((tq, HEAD_DIM), F32), pltpu.VMEM((s, HEAD_DIM), F32), pltpu.VMEM((s, HEAD_DIM), F32)],
        compiler_params=_params("parallel", "parallel", "arbitrary"),
    )(proj3, proj3, proj3, slopes, y, lse, dy)


def kernel(x, c, w_ada, b_ada, g_norm, w_in, g_sb, g_dil, w_out, g_final, loss_target, m_w_ada, m_b_ada, m_g_norm, m_w_in, m_g_sb, m_g_dil, m_w_out, m_g_final, v_w_ada, v_b_ada, v_g_norm, v_w_in, v_g_sb, v_g_dil, v_w_out, v_g_final):
    b, s, d = x.shape
    t = b * s
    e = w_in.shape[2]
    n_heads = e // HEAD_DIM
    na = w_ada.shape[2]
    r_out = w_out.shape[1]
    assert g_sb.shape[1] == e and g_dil.shape[1] == e and N_DEV * r_out == 2 * e and N_DEV * na == 3 * d
    assert b <= SMALL_ROWS and 3 * b + 3 <= 2 * SMALL_ROWS
    ix, iy, ic = _mesh_pos()
    me = 4 * ix + 2 * iy + ic

    c_all = _allgather_rows(jnp.pad(c, ((0, SMALL_ROWS - b), (0, 0))), "ag_c")
    b_own = lax.dynamic_slice(b_ada, (0, me * na), (1, na))
    mod_cols = _ada_fwd(c_all, w_ada[0], b_own)
    mod_all = _allgather_rows(mod_cols, "ag_mod").reshape(N_DEV, N_DEV, SMALL_ROWS, na)
    mod_own = lax.dynamic_slice(mod_all, (0, me, 0, 0), (N_DEV, 1, b, na))[:, 0]
    mod = mod_own.transpose(1, 0, 2).reshape(b, 1, 3 * d)
    shift, scale, gate = mod[:, :, :d], mod[:, :, d:2 * d], mod[:, :, 2 * d:]

    w_in3, w_out3 = _allgather_weights(w_in[0].astype(BF16), w_out[0].astype(BF16))
    w_out1 = w_out3.reshape(1, N_DEV * r_out, d)

    h = _norm_mod(x, g_norm, scale, shift).reshape(t, d)
    proj = _mm_nn(h, w_in3, BF16, "mm_proj")
    proj3 = proj.reshape(b, s, N_DEV * e)
    slopes = jnp.exp2(-ALIBI_MAX_BIAS * jnp.arange(1, n_heads + 1, dtype=F32) / n_heads)
    slopes = jnp.broadcast_to(slopes[:, None, None], (n_heads, 8, HEAD_DIM))
    y_sb, lom_total = _sb_fwd(proj3, e)
    y_dl, lse = _dil_fwd(proj3, e, slopes)
    yg = _gate_fwd(y_sb.reshape(t, e), y_dl.reshape(t, e), proj, g_sb, g_dil)
    out = _mm_nn(yg, w_out1, F32, "mm_out").reshape(b, s, d)
    loss_p, dx2, d_out, dgate, gg_final = _final_fwd_bwd(x, out, gate, g_final.reshape(1, d), loss_target)

    d_out2 = d_out.reshape(t, d)
    dyg = _mm_nt(d_out2, w_out1, BF16, "mm_dy")
    gw_out_p = _mm_tn(yg, d_out2, 1, BF16, "mm_gw_out").reshape(N_DEV, r_out, d)
    dy_sb, dy_dl, dz_sb, dz_dl, gg_sb, gg_dl = _gate_bwd(dyg, y_sb.reshape(t, e), y_dl.reshape(t, e), proj, g_sb, g_dil)
    dq_sb, dk_sb, dv_sb = _sb_bwd(proj3, lom_total, dy_sb.reshape(b, s, e))
    dq_dl, dk_dl, dv_dl = _dil_bwd(proj3, y_dl, lse, dy_dl.reshape(b, s, e), slopes)
    dproj = jnp.concatenate(
        [a.reshape(t, e) for a in (dq_sb, dk_sb, dv_sb, dz_sb, dq_dl, dk_dl, dv_dl, dz_dl)], axis=1)
    dh = _mm_nt(dproj, w_in3, F32, "mm_dh").reshape(b, s, d)
    gw_in_p = _mm_tn(h, dproj, N_DEV, BF16, "mm_gw_in")
    grad_x, dshift, dscale, gg_norm = _norm_bwd(x, dh, dx2, scale, g_norm)

    dmod = jnp.concatenate([dshift, dscale, dgate], axis=1).reshape(3 * b, d)
    pkg = jnp.concatenate([dmod, gg_norm, gg_final, jnp.concatenate([gg_sb, gg_dl], axis=1),
                           jnp.zeros((2 * SMALL_ROWS - 3 * b - 3, d), F32)], axis=0)
    pkg_all = _allgather_rows(pkg, "ag_small_grads").reshape(N_DEV, 2 * SMALL_ROWS, d)
    dmod_all = pkg_all[:, :3 * b].reshape(N_DEV * b, 3 * d)
    dmod_cols = lax.dynamic_slice(dmod_all, (0, me * na), (N_DEV * b, na))
    c_rows = c_all.reshape(N_DEV, SMALL_ROWS, d)[:, :b].reshape(N_DEV * b, d)
    g_w_ada, d_w_ada, nm_w_ada, nv_w_ada = _ada_bwd_adam(c_rows, dmod_cols, w_ada[0], m_w_ada[0], v_w_ada[0])

    def pack(b_ada_like, g_norm_like, g_sb_like, g_dil_like, g_final_like):
        return jnp.concatenate([b_ada_like.reshape(3, d), g_norm_like.reshape(1, d), g_final_like.reshape(1, d),
                                jnp.concatenate([g_sb_like, g_dil_like], axis=1).reshape(1, d),
                                jnp.zeros((2, d), F32)], axis=0)

    small = _small_adam(pkg_all, 3 * b, pack(b_ada, g_norm, g_sb, g_dil, g_final),
                        pack(m_b_ada, m_g_norm, m_g_sb, m_g_dil, m_g_final),
                        pack(v_b_ada, v_g_norm, v_g_sb, v_g_dil, v_g_final))

    def unpack(p):
        return (p[0:3].reshape(1, 3 * d), p[3:4], p[5:6, :e], p[5:6, e:], p[4])

    sm_g, sm_d, sm_m, sm_v = (unpack(p) for p in small)

    recv_in, recv_out = _exchange_partials(gw_in_p, gw_out_p)
    g_w_in, d_w_in, nm_w_in, nv_w_in = _adam_from_partials(recv_in, w_in[0], m_w_in[0], v_w_in[0], "adam_w_in")
    g_w_out, d_w_out, nm_w_out, nv_w_out = _adam_from_partials(recv_out, w_out[0], m_w_out[0], v_w_out[0], "adam_w_out")

    loss = lax.psum(loss_p[0, 0], ("x", "y", "c"))

    def weights(ada, small_parts, w_in_part, w_out_part):
        b_ada_p, g_norm_p, g_sb_p, g_dil_p, g_final_p = small_parts
        return (ada[None], b_ada_p, g_norm_p, w_in_part[None], g_sb_p, g_dil_p, w_out_part[None], g_final_p)

    return (loss, grad_x,
            *weights(g_w_ada, sm_g, g_w_in, g_w_out),
            *weights(d_w_ada, sm_d, d_w_in, d_w_out),
            *weights(nm_w_ada, sm_m, nm_w_in, nm_w_out),
            *weights(nv_w_ada, sm_v, nv_w_in, nv_w_out))
```

```python
import math

import jax
import jax.numpy as jnp
from jax import lax
from jax.experimental import pallas as pl
from jax.experimental.pallas import tpu as pltpu

F32 = jnp.float32
BF16 = jnp.bfloat16
MESH = pl.DeviceIdType.MESH

N_DEV = 8
HEAD_DIM = 128
EPS = 1e-6
ALIBI_MAX_BIAS = 8.0
DIL_PAIRS = ((128, 1), (512, 4), (2048, 16))
DIL_STEPS = 128
NEG = -1e30

ADAM_LR = 0.001
ADAM_B1 = 0.9
ADAM_B2 = 0.999
ADAM_EPS = 1e-08
ADAM_WD = 0.01
ADAM_STEP = 10

VMEM_LIMIT_BYTES = 56 * 1024 * 1024
SMALL_ROWS = 8

NT_DIMS = (((1,), (1,)), ((), ()))
TN_DIMS = (((0,), (0,)), ((), ()))


def _params(*semantics):
    return pltpu.CompilerParams(dimension_semantics=semantics, vmem_limit_bytes=VMEM_LIMIT_BYTES)


def _tile(n, want):
    t = min(n, want)
    assert n % t == 0, (n, want)
    return t


def _mesh_pos():
    return lax.axis_index("x"), lax.axis_index("y"), lax.axis_index("c")


def _allgather_rows(x_shard, name):
    m_per, n = x_shard.shape

    def body(x_ref, out_ref, send_sems, recv_sems, local_sem):
        x, y, c = _mesh_pos()
        me, sibling = (x, y, c), (x, y, 1 - c)
        chips = [(1 - x, y), (x, 1 - y), (1 - x, 1 - y)]

        def rows(px, py, pc):
            return out_ref.at[pl.ds((4 * px + 2 * py + pc) * m_per, m_per), :]

        def copy(k, block, to, src=None):
            return pltpu.make_async_remote_copy(
                src_ref=rows(*block) if src is None else src, dst_ref=rows(*block),
                send_sem=send_sems.at[k], recv_sem=recv_sems.at[k], device_id=to, device_id_type=MESH)

        mine = pltpu.make_async_copy(x_ref, rows(*me), local_sem)
        mine.start()
        first = [copy(0, me, sibling, src=x_ref)]
        first += [copy(1 + j, me, (*chip, c), src=x_ref) for j, chip in enumerate(chips)]
        for cp in first:
            cp.start()
        passed = [copy(4 + j, (*chip, c), sibling) for j, chip in enumerate(chips)]
        for j, chip in enumerate(chips):
            copy(1 + j, (*chip, c), me).wait_recv()
            passed[j].start()
        copy(0, sibling, me).wait_recv()
        for j, chip in enumerate(chips):
            copy(4 + j, (*chip, 1 - c), me).wait_recv()
        for cp in first + passed:
            cp.wait_send()
        mine.wait()

    return pl.pallas_call(
        body, name=name,
        out_shape=jax.ShapeDtypeStruct((N_DEV * m_per, n), x_shard.dtype),
        in_specs=[pl.BlockSpec(memory_space=pltpu.VMEM)],
        out_specs=pl.BlockSpec(memory_space=pltpu.VMEM),
        scratch_shapes=[pltpu.SemaphoreType.DMA((7,)), pltpu.SemaphoreType.DMA((7,)), pltpu.SemaphoreType.DMA],
    )(x_shard)


def _allgather_weights(wa, wb):
    def body(a_ref, b_ref, oa_ref, ob_ref, send_sems, recv_sems, local_sems):
        x, y, c = _mesh_pos()
        me, sibling = (x, y, c), (x, y, 1 - c)
        chips = [(1 - x, y), (x, 1 - y), (1 - x, 1 - y)]
        arrays = ((a_ref, oa_ref), (b_ref, ob_ref))

        def slot(out_ref, px, py, pc):
            return out_ref.at[4 * px + 2 * py + pc]

        def copy(t, k, block, to, src=None):
            dst = slot(arrays[t][1], *block)
            return pltpu.make_async_remote_copy(
                src_ref=dst if src is None else src, dst_ref=dst,
                send_sem=send_sems.at[7 * t + k], recv_sem=recv_sems.at[7 * t + k],
                device_id=to, device_id_type=MESH)

        mine, first, passed = [], [], []
        for t, (src_ref, out_ref) in enumerate(arrays):
            mine.append(pltpu.make_async_copy(src_ref, slot(out_ref, *me), local_sems.at[t]))
            mine[-1].start()
            first.append(copy(t, 0, me, sibling, src=src_ref))
            first += [copy(t, 1 + j, me, (*chip, c), src=src_ref) for j, chip in enumerate(chips)]
        for cp in first:
            cp.start()
        for t in range(2):
            for j, chip in enumerate(chips):
                copy(t, 1 + j, (*chip, c), me).wait_recv()
                passed.append(copy(t, 4 + j, (*chip, c), sibling))
                passed[-1].start()
        for t in range(2):
            copy(t, 0, sibling, me).wait_recv()
            for j, chip in enumerate(chips):
                copy(t, 4 + j, (*chip, 1 - c), me).wait_recv()
        for cp in first + passed:
            cp.wait_send()
        for cp in mine:
            cp.wait()

    any_spec = pl.BlockSpec(memory_space=pl.ANY)
    return pl.pallas_call(
        body, name="ag_weights",
        out_shape=(jax.ShapeDtypeStruct((N_DEV,) + wa.shape, wa.dtype),
                   jax.ShapeDtypeStruct((N_DEV,) + wb.shape, wb.dtype)),
        in_specs=[any_spec, any_spec], out_specs=(any_spec, any_spec),
        scratch_shapes=[pltpu.SemaphoreType.DMA((14,)), pltpu.SemaphoreType.DMA((14,)),
                        pltpu.SemaphoreType.DMA((2,))],
    )(wa, wb)


def _exchange_partials(pa, pb):
    def body(a_ref, b_ref, ra_ref, rb_ref, send_sems, recv_sems, local_sems):
        x, y, c = _mesh_pos()
        my = 4 * x + 2 * y + c
        copies = []
        for t, (src, dst) in enumerate(((a_ref, ra_ref), (b_ref, rb_ref))):
            local = pltpu.make_async_copy(src.at[my], dst.at[my], local_sems.at[t])
            local.start()
            copies.append(local)
            for d in range(1, N_DEV):
                px = 1 - x if d & 4 else x
                py = 1 - y if d & 2 else y
                pc = 1 - c if d & 1 else c
                k = 7 * t + d - 1
                cp = pltpu.make_async_remote_copy(
                    src_ref=src.at[4 * px + 2 * py + pc], dst_ref=dst.at[my],
                    send_sem=send_sems.at[k], recv_sem=recv_sems.at[k],
                    device_id=(px, py, pc), device_id_type=MESH)
                cp.start()
                copies.append(cp)
        for cp in copies:
            cp.wait()

    any_spec = pl.BlockSpec(memory_space=pl.ANY)
    return pl.pallas_call(
        body, name="exchange_partials",
        out_shape=(jax.ShapeDtypeStruct(pa.shape, pa.dtype), jax.ShapeDtypeStruct(pb.shape, pb.dtype)),
        in_specs=[any_spec, any_spec], out_specs=(any_spec, any_spec),
        scratch_shapes=[pltpu.SemaphoreType.DMA((14,)), pltpu.SemaphoreType.DMA((14,)),
                        pltpu.SemaphoreType.DMA((2,))],
    )(pa, pb)


def _mm_call(a, b, dims, nk, grid, a_spec, b_spec, o_spec, out_shape, acc_shape, name):
    def body(a_ref, b_ref, o_ref, acc_ref):
        k = pl.program_id(2)

        @pl.when(k == 0)
        def _():
            acc_ref[...] = jnp.zeros_like(acc_ref)

        acc_ref[...] += lax.dot_general(a_ref[...], b_ref[...], dims, preferred_element_type=F32)

        @pl.when(k == nk - 1)
        def _():
            o_ref[...] = acc_ref[...].astype(o_ref.dtype)

    return pl.pallas_call(
        body, name=name, grid=grid, in_specs=[a_spec, b_spec], out_specs=o_spec, out_shape=out_shape,
        scratch_shapes=[pltpu.VMEM(acc_shape, F32)],
        compiler_params=_params("parallel", "parallel", "arbitrary"),
    )(a, b)


MM_TM, MM_TN, MM_TK = 1024, 2048, 1024


def _mm_nn(a, b3, out_dtype, name):
    m, kk = a.shape
    g, _, nb = b3.shape
    tm, tn, tk = _tile(m, MM_TM), _tile(nb, MM_TN), _tile(kk, MM_TK)
    npb = nb // tn
    return _mm_call(
        a, b3, (((1,), (0,)), ((), ())), kk // tk, (m // tm, g * npb, kk // tk),
        pl.BlockSpec((tm, tk), lambda i, j, k: (i, k)),
        pl.BlockSpec((None, tk, tn), lambda i, j, k: (j // npb, k, j % npb)),
        pl.BlockSpec((tm, tn), lambda i, j, k: (i, j)),
        jax.ShapeDtypeStruct((m, g * nb), out_dtype), (tm, tn), name)


def _mm_nt(a, b3, out_dtype, name):
    m, kk = a.shape
    g, n, kb = b3.shape
    tm, tn, tk = _tile(m, MM_TM), _tile(n, MM_TN), _tile(kb, MM_TK)
    kpb = kb // tk
    return _mm_call(
        a, b3, NT_DIMS, kk // tk, (m // tm, n // tn, kk // tk),
        pl.BlockSpec((tm, tk), lambda i, j, k: (i, k)),
        pl.BlockSpec((None, tn, tk), lambda i, j, k: (k // kpb, j, k % kpb)),
        pl.BlockSpec((tm, tn), lambda i, j, k: (i, j)),
        jax.ShapeDtypeStruct((m, n), out_dtype), (tm, tn), name)


def _mm_tn(a, b, g, out_dtype, name):
    t, m = a.shape
    nb = b.shape[1] // g
    tm, tn, tk = _tile(m, MM_TM), _tile(nb, MM_TN), _tile(t, MM_TK)
    npb = nb // tn
    return _mm_call(
        a, b, TN_DIMS, t // tk, (m // tm, g * npb, t // tk),
        pl.BlockSpec((tk, tm), lambda i, j, k: (k, i)),
        pl.BlockSpec((tk, tn), lambda i, j, k: (k, j)),
        pl.BlockSpec((None, tm, tn), lambda i, j, k: (j // npb, i, j % npb)),
        jax.ShapeDtypeStruct((g, m, nb), out_dtype), (tm, tn), name)


def _silu(z):
    return z * jax.nn.sigmoid(z)


def _ada_fwd(c_all, w_shard, b_own):
    r, d = c_all.shape
    na = w_shard.shape[1]
    tk = _tile(d, 512)
    nk = d // tk

    def body(c_ref, w_ref, b_ref, o_ref):
        k = pl.program_id(0)

        @pl.when(k == 0)
        def _():
            o_ref[...] = jnp.zeros_like(o_ref) + b_ref[...]

        cs = _silu(c_ref[...]).astype(BF16)
        o_ref[...] += jnp.dot(cs, w_ref[...].astype(BF16), preferred_element_type=F32)

    return pl.pallas_call(
        body, name="ada_fwd", grid=(nk,),
        in_specs=[pl.BlockSpec((r, tk), lambda k: (0, k)), pl.BlockSpec((tk, na), lambda k: (k, 0)),
                  pl.BlockSpec((1, na), lambda k: (0, 0))],
        out_specs=pl.BlockSpec((r, na), lambda k: (0, 0)),
        out_shape=jax.ShapeDtypeStruct((r, na), F32),
        compiler_params=_params("arbitrary"),
    )(c_all, w_shard, b_own)


def _adam(w, g, m, v):
    nm = ADAM_B1 * m + (1.0 - ADAM_B1) * g
    nv = ADAM_B2 * v + (1.0 - ADAM_B2) * (g * g)
    m_hat = nm / (1.0 - ADAM_B1 ** ADAM_STEP)
    v_hat = nv / (1.0 - ADAM_B2 ** ADAM_STEP)
    delta = -ADAM_LR * (m_hat / (jnp.sqrt(v_hat) + ADAM_EPS) + ADAM_WD * w)
    return delta, nm, nv


def _ada_bwd_adam(c_rows, dmod_cols, w, m, v):
    bg, d = c_rows.shape
    na = w.shape[1]
    tr = _tile(d, 256)

    def body(c_ref, dm_ref, w_ref, m_ref, v_ref, g_ref, d_ref, nm_ref, nv_ref):
        cs = _silu(c_ref[...]).astype(BF16)
        g = lax.dot_general(cs, dm_ref[...].astype(BF16), TN_DIMS, preferred_element_type=F32)
        delta, nm, nv = _adam(w_ref[...], g, m_ref[...], v_ref[...])
        g_ref[...] = g
        d_ref[...] = delta
        nm_ref[...] = nm
        nv_ref[...] = nv

    blk = pl.BlockSpec((tr, na), lambda i: (i, 0))
    shp = jax.ShapeDtypeStruct((d, na), F32)
    return pl.pallas_call(
        body, name="ada_bwd_adam", grid=(d // tr,),
        in_specs=[pl.BlockSpec((bg, tr), lambda i: (0, i)), pl.BlockSpec((bg, na), lambda i: (0, 0)), blk, blk, blk],
        out_specs=(blk, blk, blk, blk), out_shape=(shp, shp, shp, shp),
        compiler_params=_params("parallel"),
    )(c_rows, dmod_cols, w, m, v)


def _small_adam(pkg_all, n_batch_rows, w, m, v):
    d = w.shape[1]

    def body(p_ref, w_ref, m_ref, v_ref, g_ref, d_ref, nm_ref, nv_ref):
        for part in range(3):
            acc = jnp.zeros((1, d), F32)
            for dev in range(N_DEV):
                for b in range(n_batch_rows // 3):
                    acc = acc + p_ref[dev, 3 * b + part:3 * b + part + 1, :]
            g_ref[part:part + 1, :] = acc
        for rrow in range(3):
            acc = jnp.zeros((1, d), F32)
            for dev in range(N_DEV):
                acc = acc + p_ref[dev, n_batch_rows + rrow:n_batch_rows + rrow + 1, :]
            g_ref[3 + rrow:4 + rrow, :] = acc
        g_ref[6:8, :] = jnp.zeros((2, d), F32)
        g = g_ref[...]
        delta, nm, nv = _adam(w_ref[...], g, m_ref[...], v_ref[...])
        d_ref[...] = delta
        nm_ref[...] = nm
        nv_ref[...] = nv

    vm = pl.BlockSpec(memory_space=pltpu.VMEM)
    shp = jax.ShapeDtypeStruct((SMALL_ROWS, d), F32)
    return pl.pallas_call(
        body, name="small_adam", in_specs=[vm, vm, vm, vm], out_specs=(vm, vm, vm, vm),
        out_shape=(shp, shp, shp, shp),
    )(pkg_all, w, m, v)


def _adam_from_partials(recv, w, m, v, name):
    _, r, c = recv.shape
    tr = _tile(r, 128)

    def body(p_ref, w_ref, m_ref, v_ref, g_ref, d_ref, nm_ref, nv_ref):
        g = p_ref[0].astype(F32)
        for dev in range(1, N_DEV):
            g = g + p_ref[dev].astype(F32)
        delta, nm, nv = _adam(w_ref[...], g, m_ref[...], v_ref[...])
        g_ref[...] = g
        d_ref[...] = delta
        nm_ref[...] = nm
        nv_ref[...] = nv

    blk = pl.BlockSpec((tr, c), lambda i: (i, 0))
    shp = jax.ShapeDtypeStruct((r, c), F32)
    return pl.pallas_call(
        body, name=name, grid=(r // tr,),
        in_specs=[pl.BlockSpec((N_DEV, tr, c), lambda i: (0, i, 0)), blk, blk, blk],
        out_specs=(blk, blk, blk, blk), out_shape=(shp, shp, shp, shp),
        compiler_params=_params("parallel"),
    )(recv, w, m, v)


def _norm_mod(x, g_norm, scale, shift):
    b, s, d = x.shape
    ts = _tile(s, 256)

    def body(x_ref, g_ref, sc_ref, sh_ref, h_ref):
        xv = x_ref[...]
        r = lax.rsqrt(jnp.mean(xv * xv, axis=-1, keepdims=True) + EPS)
        xn = (xv * r) * g_ref[...]
        h_ref[...] = (xn * (1.0 + sc_ref[...]) + sh_ref[...]).astype(BF16)

    tok = pl.BlockSpec((None, ts, d), lambda i, j: (i, j, 0))
    per_b = pl.BlockSpec((None, 1, d), lambda i, j: (i, 0, 0))
    return pl.pallas_call(
        body, name="norm_mod", grid=(b, s // ts),
        in_specs=[tok, pl.BlockSpec((1, d), lambda i, j: (0, 0)), per_b, per_b],
        out_specs=tok, out_shape=jax.ShapeDtypeStruct((b, s, d), BF16),
        compiler_params=_params("parallel", "parallel"),
    )(x, g_norm, scale, shift)


def _final_fwd_bwd(x, out, gate, g_final, target):
    b, s, d = x.shape
    ts = _tile(s, 256)

    def body(x_ref, o_ref, gt_ref, g_ref, t_ref, loss_ref, dx2_ref, dout_ref, dgate_ref, gg_ref):
        i, j = pl.program_id(0), pl.program_id(1)

        @pl.when((i == 0) & (j == 0))
        def _():
            loss_ref[...] = jnp.zeros_like(loss_ref)
            gg_ref[...] = jnp.zeros_like(gg_ref)

        @pl.when(j == 0)
        def _():
            dgate_ref[...] = jnp.zeros_like(dgate_ref)

        ov = o_ref[...]
        gt = gt_ref[...]
        x2 = x_ref[...] + gt * ov
        r = lax.rsqrt(jnp.mean(x2 * x2, axis=-1, keepdims=True) + EPS)
        xh = x2 * r
        err = xh * g_ref[...] - t_ref[...]
        loss_ref[...] += 0.5 * jnp.sum(jnp.mean(err * err, axis=-1, keepdims=True), axis=0, keepdims=True)
        dfin = err * (1.0 / d)
        gg_ref[...] += jnp.sum(dfin * xh, axis=0, keepdims=True)
        dxh = dfin * g_ref[...]
        dx2 = r * (dxh - xh * jnp.mean(dxh * xh, axis=-1, keepdims=True))
        dx2_ref[...] = dx2
        dout_ref[...] = (gt * dx2).astype(BF16)
        dgate_ref[...] += jnp.sum(dx2 * ov, axis=0, keepdims=True)

    tok = pl.BlockSpec((None, ts, d), lambda i, j: (i, j, 0))
    per_b = pl.BlockSpec((None, 1, d), lambda i, j: (i, 0, 0))
    vec = pl.BlockSpec((1, d), lambda i, j: (0, 0))
    return pl.pallas_call(
        body, name="final_fwd_bwd", grid=(b, s // ts),
        in_specs=[tok, tok, per_b, vec, tok],
        out_specs=(pl.BlockSpec((8, 128), lambda i, j: (0, 0)), tok, tok, per_b, vec),
        out_shape=(jax.ShapeDtypeStruct((8, 128), F32), jax.ShapeDtypeStruct((b, s, d), F32),
                   jax.ShapeDtypeStruct((b, s, d), BF16), jax.ShapeDtypeStruct((b, 1, d), F32),
                   jax.ShapeDtypeStruct((1, d), F32)),
        compiler_params=_params("arbitrary", "arbitrary"),
    )(x, out, gate, g_final, target)


def _norm_bwd(x, dh, dx2, scale, g_norm):
    b, s, d = x.shape
    ts = _tile(s, 256)

    def body(x_ref, dh_ref, dx2_ref, sc_ref, g_ref, gx_ref, dsh_ref, dsc_ref, gg_ref):
        i, j = pl.program_id(0), pl.program_id(1)

        @pl.when((i == 0) & (j == 0))
        def _():
            gg_ref[...] = jnp.zeros_like(gg_ref)

        @pl.when(j == 0)
        def _():
            dsh_ref[...] = jnp.zeros_like(dsh_ref)
            dsc_ref[...] = jnp.zeros_like(dsc_ref)

        xv = x_ref[...]
        dhv = dh_ref[...]
        r = lax.rsqrt(jnp.mean(xv * xv, axis=-1, keepdims=True) + EPS)
        xh = xv * r
        xn = xh * g_ref[...]
        dsh_ref[...] += jnp.sum(dhv, axis=0, keepdims=True)
        dsc_ref[...] += jnp.sum(dhv * xn, axis=0, keepdims=True)
        dxn = dhv * (1.0 + sc_ref[...])
        gg_ref[...] += jnp.sum(dxn * xh, axis=0, keepdims=True)
        dxh = dxn * g_ref[...]
        gx_ref[...] = dx2_ref[...] + r * (dxh - xh * jnp.mean(dxh * xh, axis=-1, keepdims=True))

    tok = pl.BlockSpec((None, ts, d), lambda i, j: (i, j, 0))
    per_b = pl.BlockSpec((None, 1, d), lambda i, j: (i, 0, 0))
    vec = pl.BlockSpec((1, d), lambda i, j: (0, 0))
    return pl.pallas_call(
        body, name="norm_bwd", grid=(b, s // ts),
        in_specs=[tok, tok, tok, per_b, vec],
        out_specs=(tok, per_b, per_b, vec),
        out_shape=(jax.ShapeDtypeStruct((b, s, d), F32), jax.ShapeDtypeStruct((b, 1, d), F32),
                   jax.ShapeDtypeStruct((b, 1, d), F32), jax.ShapeDtypeStruct((1, d), F32)),
        compiler_params=_params("arbitrary", "arbitrary"),
    )(x, dh, dx2, scale, g_norm)


def _gate_fwd(y_sb, y_dl, proj, g_sb, g_dl):
    t, e = y_sb.shape
    n_heads = e // HEAD_DIM
    tt = _tile(t, 256)

    def body(ys_ref, yd_ref, zs_ref, zd_ref, gs_ref, gd_ref, o_ref):
        for grp, (y_ref, z_ref, g_ref) in enumerate(((ys_ref, zs_ref, gs_ref), (yd_ref, zd_ref, gd_ref))):
            for h in range(n_heads):
                sl = slice(h * HEAD_DIM, (h + 1) * HEAD_DIM)
                y = y_ref[:, sl]
                r = lax.rsqrt(jnp.mean(y * y, axis=-1, keepdims=True) + EPS)
                yn = (y * r) * g_ref[:, sl]
                z = z_ref[:, sl].astype(F32)
                o_ref[:, grp * e + h * HEAD_DIM:grp * e + (h + 1) * HEAD_DIM] = (yn * _silu(z)).astype(BF16)

    yblk = pl.BlockSpec((tt, e), lambda i: (i, 0))
    gblk = pl.BlockSpec((1, e), lambda i: (0, 0))
    return pl.pallas_call(
        body, name="gate_fwd", grid=(t // tt,),
        in_specs=[yblk, yblk, pl.BlockSpec((tt, e), lambda i: (i, 3)), pl.BlockSpec((tt, e), lambda i: (i, 7)),
                  gblk, gblk],
        out_specs=pl.BlockSpec((tt, 2 * e), lambda i: (i, 0)),
        out_shape=jax.ShapeDtypeStruct((t, 2 * e), BF16),
        compiler_params=_params("parallel"),
    )(y_sb, y_dl, proj, proj, g_sb, g_dl)


def _gate_bwd(dyg, y_sb, y_dl, proj, g_sb, g_dl):
    t, e = y_sb.shape
    n_heads = e // HEAD_DIM
    tt = _tile(t, 256)

    def body(dg_ref, ys_ref, yd_ref, zs_ref, zd_ref, gs_ref, gd_ref,
             dys_ref, dyd_ref, dzs_ref, dzd_ref, ggs_ref, ggd_ref):
        @pl.when(pl.program_id(0) == 0)
        def _():
            ggs_ref[...] = jnp.zeros_like(ggs_ref)
            ggd_ref[...] = jnp.zeros_like(ggd_ref)

        groups = ((ys_ref, zs_ref, gs_ref, dys_ref, dzs_ref, ggs_ref), (yd_ref, zd_ref, gd_ref, dyd_ref, dzd_ref, ggd_ref))
        for grp, (y_ref, z_ref, g_ref, dy_ref, dz_ref, gg_ref) in enumerate(groups):
            for h in range(n_heads):
                sl = slice(h * HEAD_DIM, (h + 1) * HEAD_DIM)
                dg = dg_ref[:, grp * e + h * HEAD_DIM:grp * e + (h + 1) * HEAD_DIM].astype(F32)
                y = y_ref[:, sl]
                z = z_ref[:, sl].astype(F32)
                g = g_ref[:, sl]
                r = lax.rsqrt(jnp.mean(y * y, axis=-1, keepdims=True) + EPS)
                yh = y * r
                sig = jax.nn.sigmoid(z)
                dyn = dg * (z * sig)
                dz_ref[:, sl] = (dg * (yh * g) * (sig * (1.0 + z * (1.0 - sig)))).astype(BF16)
                gg_ref[:, sl] += jnp.sum(dyn * yh, axis=0, keepdims=True)
                dyh = dyn * g
                dy_ref[:, sl] = (r * (dyh - yh * jnp.mean(dyh * yh, axis=-1, keepdims=True))).astype(BF16)

    yblk = pl.BlockSpec((tt, e), lambda i: (i, 0))
    gblk = pl.BlockSpec((1, e), lambda i: (0, 0))
    act = jax.ShapeDtypeStruct((t, e), BF16)
    vec = jax.ShapeDtypeStruct((1, e), F32)
    return pl.pallas_call(
        body, name="gate_bwd", grid=(t // tt,),
        in_specs=[pl.BlockSpec((tt, 2 * e), lambda i: (i, 0)), yblk, yblk,
                  pl.BlockSpec((tt, e), lambda i: (i, 3)), pl.BlockSpec((tt, e), lambda i: (i, 7)), gblk, gblk],
        out_specs=(yblk, yblk, yblk, yblk, gblk, gblk),
        out_shape=(act, act, act, act, vec, vec),
        compiler_params=_params("arbitrary"),
    )(dyg, y_sb, y_dl, proj, proj, g_sb, g_dl)


ATT_TQ = 256
HEADS_PER_STEP = 2
SOFTPLUS_CLAMP = 30.0


def _split2_dot(x, u):
    hi = x.astype(BF16)
    lo = (x - hi.astype(F32)).astype(BF16)
    n = x.shape[0]
    both = jnp.dot(jnp.concatenate([hi, lo], axis=0), u, preferred_element_type=F32)
    return both[:n] + both[n:]


def _iota2(n):
    return lax.broadcasted_iota(jnp.int32, (n, n), 0), lax.broadcasted_iota(jnp.int32, (n, n), 1)


def _head_slices():
    return [slice(hh * HEAD_DIM, (hh + 1) * HEAD_DIM) for hh in range(HEADS_PER_STEP)]


def _att_specs(s, e, tq, col0):
    n_heads = e // HEAD_DIM
    hp = HEADS_PER_STEP
    assert n_heads % hp == 0 and col0 % hp == 0
    w = hp * HEAD_DIM
    q_spec = pl.BlockSpec((None, tq, w), lambda i, h, j: (i, j, col0 // hp + h))
    k_spec = pl.BlockSpec((None, s, w), lambda i, h, j: (i, 0, (col0 + n_heads) // hp + h))
    v_spec = pl.BlockSpec((None, s, w), lambda i, h, j: (i, 0, (col0 + 2 * n_heads) // hp + h))
    return q_spec, k_spec, v_spec


def _sb_fwd(proj3, e):
    b, s, _ = proj3.shape
    n_heads = e // HEAD_DIM
    hp = HEADS_PER_STEP
    tq = _tile(s, ATT_TQ)
    nq = s // tq
    inv = 1.0 / math.sqrt(HEAD_DIM)

    def body(q_ref, k_ref, v_ref, y_ref, tot_ref, acc_ref, car_ref):
        i = pl.program_id(2)
        row, col = _iota2(tq)
        before = row > col
        u_after = before.astype(BF16)
        acc_ref[...] = jnp.zeros_like(acc_ref)
        car_ref[...] = jnp.zeros_like(car_ref)

        def block(j, diagonal):
            keys = pl.ds(pl.multiple_of(j * tq, tq), tq)
            for hh, hs in enumerate(_head_slices()):
                z = lax.dot_general(q_ref[:, hs], k_ref[keys, hs], NT_DIMS, preferred_element_type=F32) * inv
                sp = jnp.maximum(jnp.log(1.0 + jnp.exp(jnp.minimum(z, SOFTPLUS_CLAMP))), z)
                lom = jnp.where(before, -sp, 0.0) if diagonal else -sp
                suffix = _split2_dot(lom, u_after) + car_ref[hh]
                a = jnp.exp((z - sp) + suffix)
                if diagonal:
                    a = jnp.where(before, a, 0.0)
                acc_ref[hh] += jnp.dot(a.astype(BF16), v_ref[keys, hs], preferred_element_type=F32)
                car_ref[hh] += jnp.sum(lom, axis=1, keepdims=True)

        block(i, True)

        def step(it, carry):
            block(i - it, False)
            return carry

        lax.fori_loop(1, i + 1, step, 0)
        for hh, hs in enumerate(_head_slices()):
            y_ref[:, hs] = acc_ref[hh]
            tot_ref[:, hs] = jnp.broadcast_to(car_ref[hh], (tq, HEAD_DIM))

    q_spec, k_spec, v_spec = _att_specs(s, e, tq, 0)
    blk_q = pl.BlockSpec((None, tq, hp * HEAD_DIM), lambda i, h, j: (i, j, h))
    shp = jax.ShapeDtypeStruct((b, s, e), F32)
    return pl.pallas_call(
        body, name="sb_fwd", grid=(b, n_heads // hp, nq),
        in_specs=[q_spec, k_spec, v_spec],
        out_specs=(blk_q, blk_q), out_shape=(shp, shp),
        scratch_shapes=[pltpu.VMEM((hp, tq, HEAD_DIM), F32), pltpu.VMEM((hp, tq, 1), F32)],
        compiler_params=_params("parallel", "parallel", "arbitrary"),
    )(proj3, proj3, proj3)


def _sb_bwd(proj3, lom_total, dy):
    b, s, e = dy.shape
    n_heads = e // HEAD_DIM
    hp = HEADS_PER_STEP
    tq = _tile(s, ATT_TQ)
    nq = s // tq
    inv = 1.0 / math.sqrt(HEAD_DIM)

    def body(q_ref, k_ref, v_ref, tot_ref, dy_ref, dq_ref, dk_ref, dv_ref, dqa, dka, dva, car, car2):
        i = pl.program_id(2)

        @pl.when(i == 0)
        def _():
            dka[...] = jnp.zeros_like(dka)
            dva[...] = jnp.zeros_like(dva)

        row, col = _iota2(tq)
        before = row > col
        u_upto = (row <= col).astype(BF16)
        u_before = (row < col).astype(BF16)
        dqa[...] = jnp.zeros_like(dqa)
        car[...] = jnp.zeros_like(car)
        car2[...] = jnp.zeros_like(car2)

        def block(j, diagonal):
            keys = pl.ds(pl.multiple_of(j * tq, tq), tq)
            for hh, hs in enumerate(_head_slices()):
                q, k, v, dy_b = q_ref[:, hs], k_ref[keys, hs], v_ref[keys, hs], dy_ref[:, hs]
                z = lax.dot_general(q, k, NT_DIMS, preferred_element_type=F32) * inv
                ez = jnp.exp(jnp.minimum(z, SOFTPLUS_CLAMP))
                sp = jnp.maximum(jnp.log(1.0 + ez), z)
                lom = jnp.where(before, -sp, 0.0) if diagonal else -sp
                suffix = tot_ref[:, hh * HEAD_DIM:hh * HEAD_DIM + 1] - (_split2_dot(lom, u_upto) + car[hh])
                a = jnp.exp((z - sp) + suffix)
                if diagonal:
                    a = jnp.where(before, a, 0.0)
                one_minus_beta = 1.0 / (1.0 + ez)
                beta = ez * one_minus_beta
                dl = a * lax.dot_general(dy_b, v, NT_DIMS, preferred_element_type=F32)
                prefix = _split2_dot(dl, u_before) + car2[hh]
                dz = (dl * one_minus_beta - prefix * beta) * inv
                if diagonal:
                    dz = jnp.where(before, dz, 0.0)
                dz = dz.astype(BF16)
                dqa[hh] += jnp.dot(dz, k, preferred_element_type=F32)
                dka[keys, hs] += lax.dot_general(dz, q, TN_DIMS, preferred_element_type=F32)
                dva[keys, hs] += lax.dot_general(a.astype(BF16), dy_b, TN_DIMS, preferred_element_type=F32)
                car[hh] += jnp.sum(lom, axis=1, keepdims=True)
                car2[hh] += jnp.sum(dl, axis=1, keepdims=True)

        def step(j, carry):
            block(j, False)
            return carry

        lax.fori_loop(0, i, step, 0)
        block(i, True)
        for hh, hs in enumerate(_head_slices()):
            dq_ref[:, hs] = dqa[hh].astype(BF16)

        @pl.when(i == nq - 1)
        def _():
            dk_ref[...] = dka[...].astype(BF16)
            dv_ref[...] = dva[...].astype(BF16)

    q_spec, k_spec, v_spec = _att_specs(s, e, tq, 0)
    w = hp * HEAD_DIM
    blk_q = pl.BlockSpec((None, tq, w), lambda i, h, j: (i, j, h))
    blk_kv = pl.BlockSpec((None, s, w), lambda i, h, j: (i, 0, h))
    shp = jax.ShapeDtypeStruct((b, s, e), BF16)
    return pl.pallas_call(
        body, name="sb_bwd", grid=(b, n_heads // hp, nq),
        in_specs=[q_spec, k_spec, v_spec, blk_q, blk_q],
        out_specs=(blk_q, blk_kv, blk_kv), out_shape=(shp, shp, shp),
        scratch_shapes=[pltpu.VMEM((hp, tq, HEAD_DIM), F32), pltpu.VMEM((s, w), F32), pltpu.VMEM((s, w), F32),
                        pltpu.VMEM((hp, tq, 1), F32), pltpu.VMEM((hp, tq, 1), F32)],
        compiler_params=_params("parallel", "parallel", "arbitrary"),
    )(proj3, proj3, proj3, lom_total, dy)


def _dil_near_tiles(tq):
    return (DIL_PAIRS[1][0] + tq - 1) // tq + 1


def _dil_fill_bias(bias_ref, sl_ref, tq):
    row, col = _iota2(tq)
    for hh in range(HEADS_PER_STEP):
        slope = sl_ref[hh, 0:1, 0:1]
        for d in range(_dil_near_tiles(tq) + 1):
            dist = d * tq + row - col
            cnt = jnp.zeros(dist.shape, jnp.int32)
            for window, dilation in DIL_PAIRS:
                cnt = cnt + (((dist & (dilation - 1)) == 0) & (dist <= window)).astype(jnp.int32)
            bias = jnp.where(cnt == 3, math.log(3.0), jnp.where(cnt == 2, math.log(2.0), 0.0))
            bias_ref[hh, d] = jnp.where((dist >= 0) & (cnt > 0), bias - slope * dist.astype(F32), NEG)


def _dil_scores(q, k, bias_ref, hh, slope, it, tq, inv):
    near = _dil_near_tiles(tq)
    beyond = jnp.maximum(it - near, 0).astype(F32) * float(tq)
    sc = lax.dot_general(q, k, NT_DIMS, preferred_element_type=F32) * inv
    return (sc + bias_ref[hh, jnp.minimum(it, near)]) - slope * beyond


def _dil_fwd(proj3, e, slopes):
    b, s, _ = proj3.shape
    n_heads = e // HEAD_DIM
    hp = HEADS_PER_STEP
    tq = _tile(s, ATT_TQ)
    nq = s // tq
    inv = 1.0 / math.sqrt(HEAD_DIM)
    assert s <= DIL_PAIRS[2][0]

    def body(q_ref, k_ref, v_ref, sl_ref, y_ref, lse_ref, acc_ref, m_ref, l_ref, bias_ref):
        i = pl.program_id(2)

        @pl.when(i == 0)
        def _():
            _dil_fill_bias(bias_ref, sl_ref, tq)

        acc_ref[...] = jnp.zeros_like(acc_ref)
        m_ref[...] = jnp.full_like(m_ref, NEG)
        l_ref[...] = jnp.zeros_like(l_ref)

        def step(it, carry):
            keys = pl.ds(pl.multiple_of((i - it) * tq, tq), tq)
            for hh, hs in enumerate(_head_slices()):
                sc = _dil_scores(q_ref[:, hs], k_ref[keys, hs], bias_ref, hh, sl_ref[hh, 0:1, 0:1], it, tq, inv)
                m_old = m_ref[hh]
                m_new = jnp.maximum(m_old, jnp.max(sc, axis=1, keepdims=True))
                p = jnp.exp(sc - m_new)
                alpha = jnp.exp(m_old - m_new)
                l_ref[hh] = alpha * l_ref[hh] + jnp.sum(p, axis=1, keepdims=True)
                acc_ref[hh] = alpha * acc_ref[hh] + jnp.dot(p.astype(BF16), v_ref[keys, hs], preferred_element_type=F32)
                m_ref[hh] = m_new
            return carry

        lax.fori_loop(0, i + 1, step, 0)
        for hh, hs in enumerate(_head_slices()):
            y_ref[:, hs] = acc_ref[hh] / l_ref[hh]
            lse_ref[:, hs] = jnp.broadcast_to(m_ref[hh] + jnp.log(l_ref[hh]), (tq, HEAD_DIM))

    q_spec, k_spec, v_spec = _att_specs(s, e, tq, 4 * n_heads)
    blk_q = pl.BlockSpec((None, tq, hp * HEAD_DIM), lambda i, h, j: (i, j, h))
    shp = jax.ShapeDtypeStruct((b, s, e), F32)
    return pl.pallas_call(
        body, name="dil_fwd", grid=(b, n_heads // hp, nq),
        in_specs=[q_spec, k_spec, v_spec, pl.BlockSpec((hp, 8, HEAD_DIM), lambda i, h, j: (h, 0, 0))],
        out_specs=(blk_q, blk_q), out_shape=(shp, shp),
        scratch_shapes=[pltpu.VMEM((hp, tq, HEAD_DIM), F32), pltpu.VMEM((hp, tq, 1), F32), pltpu.VMEM((hp, tq, 1), F32),
                        pltpu.VMEM((hp, _dil_near_tiles(tq) + 1, tq, tq), F32)],
        compiler_params=_params("parallel", "parallel", "arbitrary"),
    )(proj3, proj3, proj3, slopes)


def _dil_bwd(proj3, y, lse, dy, slopes):
    b, s, e = y.shape
    n_heads = e // HEAD_DIM
    hp = HEADS_PER_STEP
    tq = _tile(s, ATT_TQ)
    nq = s // tq
    inv = 1.0 / math.sqrt(HEAD_DIM)

    def body(q_ref, k_ref, v_ref, sl_ref, y_ref, lse_ref, dy_ref, dq_ref, dk_ref, dv_ref, dqa, dka, dva, bias_ref):
        i = pl.program_id(2)

        @pl.when(i == 0)
        def _():
            dka[...] = jnp.zeros_like(dka)
            dva[...] = jnp.zeros_like(dva)
            _dil_fill_bias(bias_ref, sl_ref, tq)

        delta = [jnp.sum(dy_ref[:, hs].astype(F32) * y_ref[:, hs], axis=1, keepdims=True) for hs in _head_slices()]
        dqa[...] = jnp.zeros_like(dqa)

        def step(it, carry):
            keys = pl.ds(pl.multiple_of((i - it) * tq, tq), tq)
            for hh, hs in enumerate(_head_slices()):
                q, k, v, dy_b = q_ref[:, hs], k_ref[keys, hs], v_ref[keys, hs], dy_ref[:, hs]
                sc = _dil_scores(q, k, bias_ref, hh, sl_ref[hh, 0:1, 0:1], it, tq, inv)
                p = jnp.exp(sc - lse_ref[:, hh * HEAD_DIM:hh * HEAD_DIM + 1])
                dp = lax.dot_general(dy_b, v, NT_DIMS, preferred_element_type=F32)
                ds = ((p * (dp - delta[hh])) * inv).astype(BF16)
                dqa[hh] += jnp.dot(ds, k, preferred_element_type=F32)
                dka[keys, hs] += lax.dot_general(ds, q, TN_DIMS, preferred_element_type=F32)
                dva[keys, hs] += lax.dot_general(p.astype(BF16), dy_b, TN_DIMS, preferred_element_type=F32)
            return carry

        lax.fori_loop(0, i + 1, step, 0)
        for hh, hs in enumerate(_head_slices()):
            dq_ref[:, hs] = dqa[hh].astype(BF16)

        @pl.when(i == nq - 1)
        def _():
            dk_ref[...] = dka[...].astype(BF16)
            dv_ref[...] = dva[...].astype(BF16)

    q_spec, k_spec, v_spec = _att_specs(s, e, tq, 4 * n_heads)
    w = hp * HEAD_DIM
    blk_q = pl.BlockSpec((None, tq, w), lambda i, h, j: (i, j, h))
    blk_kv = pl.BlockSpec((None, s, w), lambda i, h, j: (i, 0, h))
    shp = jax.ShapeDtypeStruct((b, s, e), BF16)
    return pl.pallas_call(
        body, name="dil_bwd", grid=(b, n_heads // hp, nq),
        in_specs=[q_spec, k_spec, v_spec, pl.BlockSpec((hp, 8, HEAD_DIM), lambda i, h, j: (h, 0, 0)),
                  blk_q, blk_q, blk_q],
        out_specs=(blk_q, blk_kv, blk_kv), out_shape=(shp, shp, shp),
        scratch_shapes=[pltpu.VMEM((hp, tq, HEAD_DIM), F32), pltpu.VMEM((s, w), F32), pltpu.VMEM((s, w), F32),
                        pltpu.VMEM((hp, _dil_near_tiles(tq) + 1, tq, tq), F32)],
        compiler_params=_params("parallel", "parallel", "arbitrary"),
    )(proj3, proj3, proj3, slopes, y, lse, dy)


def kernel(x, c, w_ada, b_ada, g_norm, w_in, g_sb, g_dil, w_out, g_final, loss_target, m_w_ada, m_b_ada, m_g_norm, m_w_in, m_g_sb, m_g_dil, m_w_out, m_g_final, v_w_ada, v_b_ada, v_g_norm, v_w_in, v_g_sb, v_g_dil, v_w_out, v_g_final):
    b, s, d = x.shape
    t = b * s
    e = w_in.shape[2]
    n_heads = e // HEAD_DIM
    na = w_ada.shape[2]
    r_out = w_out.shape[1]
    assert g_sb.shape[1] == e and g_dil.shape[1] == e and N_DEV * r_out == 2 * e and N_DEV * na == 3 * d
    assert b <= SMALL_ROWS and 3 * b + 3 <= 2 * SMALL_ROWS
    ix, iy, ic = _mesh_pos()
    me = 4 * ix + 2 * iy + ic

    c_all = _allgather_rows(jnp.pad(c, ((0, SMALL_ROWS - b), (0, 0))), "ag_c")
    b_own = lax.dynamic_slice(b_ada, (0, me * na), (1, na))
    mod_cols = _ada_fwd(c_all, w_ada[0], b_own)
    mod_all = _allgather_rows(mod_cols, "ag_mod").reshape(N_DEV, N_DEV, SMALL_ROWS, na)
    mod_own = lax.dynamic_slice(mod_all, (0, me, 0, 0), (N_DEV, 1, b, na))[:, 0]
    mod = mod_own.transpose(1, 0, 2).reshape(b, 1, 3 * d)
    shift, scale, gate = mod[:, :, :d], mod[:, :, d:2 * d], mod[:, :, 2 * d:]

    w_in3, w_out3 = _allgather_weights(w_in[0].astype(BF16), w_out[0].astype(BF16))
    w_out1 = w_out3.reshape(1, N_DEV * r_out, d)

    h = _norm_mod(x, g_norm, scale, shift).reshape(t, d)
    proj = _mm_nn(h, w_in3, BF16, "mm_proj")
    proj3 = proj.reshape(b, s, N_DEV * e)
    slopes = jnp.exp2(-ALIBI_MAX_BIAS * jnp.arange(1, n_heads + 1, dtype=F32) / n_heads)
    slopes = jnp.broadcast_to(slopes[:, None, None], (n_heads, 8, HEAD_DIM))
    y_sb, lom_total = _sb_fwd(proj3, e)
    y_dl, lse = _dil_fwd(proj3, e, slopes)
    yg = _gate_fwd(y_sb.reshape(t, e), y_dl.reshape(t, e), proj, g_sb, g_dil)
    out = _mm_nn(yg, w_out1, F32, "mm_out").reshape(b, s, d)
    loss_p, dx2, d_out, dgate, gg_final = _final_fwd_bwd(x, out, gate, g_final.reshape(1, d), loss_target)

    d_out2 = d_out.reshape(t, d)
    dyg = _mm_nt(d_out2, w_out1, BF16, "mm_dy")
    gw_out_p = _mm_tn(yg, d_out2, 1, BF16, "mm_gw_out").reshape(N_DEV, r_out, d)
    dy_sb, dy_dl, dz_sb, dz_dl, gg_sb, gg_dl = _gate_bwd(dyg, y_sb.reshape(t, e), y_dl.reshape(t, e), proj, g_sb, g_dil)
    dq_sb, dk_sb, dv_sb = _sb_bwd(proj3, lom_total, dy_sb.reshape(b, s, e))
    dq_dl, dk_dl, dv_dl = _dil_bwd(proj3, y_dl, lse, dy_dl.reshape(b, s, e), slopes)
    dproj = jnp.concatenate(
        [a.reshape(t, e) for a in (dq_sb, dk_sb, dv_sb, dz_sb, dq_dl, dk_dl, dv_dl, dz_dl)], axis=1)
    dh = _mm_nt(dproj, w_in3, F32, "mm_dh").reshape(b, s, d)
    gw_in_p = _mm_tn(h, dproj, N_DEV, BF16, "mm_gw_in")
    grad_x, dshift, dscale, gg_norm = _norm_bwd(x, dh, dx2, scale, g_norm)

    dmod = jnp.concatenate([dshift, dscale, dgate], axis=1).reshape(3 * b, d)
    pkg = jnp.concatenate([dmod, gg_norm, gg_final, jnp.concatenate([gg_sb, gg_dl], axis=1),
                           jnp.zeros((2 * SMALL_ROWS - 3 * b - 3, d), F32)], axis=0)
    pkg_all = _allgather_rows(pkg, "ag_small_grads").reshape(N_DEV, 2 * SMALL_ROWS, d)
    dmod_all = pkg_all[:, :3 * b].reshape(N_DEV * b, 3 * d)
    dmod_cols = lax.dynamic_slice(dmod_all, (0, me * na), (N_DEV * b, na))
    c_rows = c_all.reshape(N_DEV, SMALL_ROWS, d)[:, :b].reshape(N_DEV * b, d)
    g_w_ada, d_w_ada, nm_w_ada, nv_w_ada = _ada_bwd_adam(c_rows, dmod_cols, w_ada[0], m_w_ada[0], v_w_ada[0])

    def pack(b_ada_like, g_norm_like, g_sb_like, g_dil_like, g_final_like):
        return jnp.concatenate([b_ada_like.reshape(3, d), g_norm_like.reshape(1, d), g_final_like.reshape(1, d),
                                jnp.concatenate([g_sb_like, g_dil_like], axis=1).reshape(1, d),
                                jnp.zeros((2, d), F32)], axis=0)

    small = _small_adam(pkg_all, 3 * b, pack(b_ada, g_norm, g_sb, g_dil, g_final),
                        pack(m_b_ada, m_g_norm, m_g_sb, m_g_dil, m_g_final),
                        pack(v_b_ada, v_g_norm, v_g_sb, v_g_dil, v_g_final))

    def unpack(p):
        return (p[0:3].reshape(1, 3 * d), p[3:4], p[5:6, :e], p[5:6, e:], p[4])

    sm_g, sm_d, sm_m, sm_v = (unpack(p) for p in small)

    recv_in, recv_out = _exchange_partials(gw_in_p, gw_out_p)
    g_w_in, d_w_in, nm_w_in, nv_w_in = _adam_from_partials(recv_in, w_in[0], m_w_in[0], v_w_in[0], "adam_w_in")
    g_w_out, d_w_out, nm_w_out, nv_w_out = _adam_from_partials(recv_out, w_out[0], m_w_out[0], v_w_out[0], "adam_w_out")

    loss = lax.psum(loss_p[0, 0], ("x", "y", "c"))

    def weights(ada, small_parts, w_in_part, w_out_part):
        b_ada_p, g_norm_p, g_sb_p, g_dil_p, g_final_p = small_parts
        return (ada[None], b_ada_p, g_norm_p, w_in_part[None], g_sb_p, g_dil_p, w_out_part[None], g_final_p)

    return (loss, grad_x,
            *weights(g_w_ada, sm_g, g_w_in, g_w_out),
            *weights(d_w_ada, sm_d, d_w_in, d_w_out),
            *weights(nm_w_ada, sm_m, nm_w_in, nm_w_out),
            *weights(nv_w_ada, sm_v, nv_w_in, nv_w_out))
```

```python
import functools
import math

import jax
import jax.numpy as jnp
from jax import lax
from jax.experimental import pallas as pl
from jax.experimental.pallas import tpu as pltpu

F32 = jnp.float32
BF16 = jnp.bfloat16
MESH = pl.DeviceIdType.MESH

N_DEV = 8
HEAD_DIM = 128
EPS = 1e-6
ALIBI_MAX_BIAS = 8.0
DIL_PAIRS = ((128, 1), (512, 4), (2048, 16))
DIL_STEPS = 128
NEG = -1e30

ADAM_LR = 0.001
ADAM_B1 = 0.9
ADAM_B2 = 0.999
ADAM_EPS = 1e-08
ADAM_WD = 0.01
ADAM_STEP = 10

VMEM_LIMIT_BYTES = 56 * 1024 * 1024
SMALL_ROWS = 8

NT_DIMS = (((1,), (1,)), ((), ()))
TN_DIMS = (((0,), (0,)), ((), ()))


def _params(*semantics):
    return pltpu.CompilerParams(dimension_semantics=semantics, vmem_limit_bytes=VMEM_LIMIT_BYTES)


def _tile(n, want):
    t = min(n, want)
    assert n % t == 0, (n, want)
    return t


def _mesh_pos():
    return lax.axis_index("x"), lax.axis_index("y"), lax.axis_index("c")


def _allgather_rows(x_shard, name):
    m_per, n = x_shard.shape

    def body(x_ref, out_ref, send_sems, recv_sems, local_sem):
        x, y, c = _mesh_pos()
        me, sibling = (x, y, c), (x, y, 1 - c)
        chips = [(1 - x, y), (x, 1 - y), (1 - x, 1 - y)]

        def rows(px, py, pc):
            return out_ref.at[pl.ds((4 * px + 2 * py + pc) * m_per, m_per), :]

        def copy(k, block, to, src=None):
            return pltpu.make_async_remote_copy(
                src_ref=rows(*block) if src is None else src, dst_ref=rows(*block),
                send_sem=send_sems.at[k], recv_sem=recv_sems.at[k], device_id=to, device_id_type=MESH)

        mine = pltpu.make_async_copy(x_ref, rows(*me), local_sem)
        mine.start()
        first = [copy(0, me, sibling, src=x_ref)]
        first += [copy(1 + j, me, (*chip, c), src=x_ref) for j, chip in enumerate(chips)]
        for cp in first:
            cp.start()
        passed = [copy(4 + j, (*chip, c), sibling) for j, chip in enumerate(chips)]
        for j, chip in enumerate(chips):
            copy(1 + j, (*chip, c), me).wait_recv()
            passed[j].start()
        copy(0, sibling, me).wait_recv()
        for j, chip in enumerate(chips):
            copy(4 + j, (*chip, 1 - c), me).wait_recv()
        for cp in first + passed:
            cp.wait_send()
        mine.wait()

    return pl.pallas_call(
        body, name=name,
        out_shape=jax.ShapeDtypeStruct((N_DEV * m_per, n), x_shard.dtype),
        in_specs=[pl.BlockSpec(memory_space=pltpu.VMEM)],
        out_specs=pl.BlockSpec(memory_space=pltpu.VMEM),
        scratch_shapes=[pltpu.SemaphoreType.DMA((7,)), pltpu.SemaphoreType.DMA((7,)), pltpu.SemaphoreType.DMA],
    )(x_shard)


def _allgather_weights(wa, wb):
    def body(a_ref, b_ref, oa_ref, ob_ref, send_sems, recv_sems, local_sems):
        x, y, c = _mesh_pos()
        me, sibling = (x, y, c), (x, y, 1 - c)
        chips = [(1 - x, y), (x, 1 - y), (1 - x, 1 - y)]
        arrays = ((a_ref, oa_ref), (b_ref, ob_ref))

        def slot(out_ref, px, py, pc):
            return out_ref.at[4 * px + 2 * py + pc]

        def copy(t, k, block, to, src=None):
            dst = slot(arrays[t][1], *block)
            return pltpu.make_async_remote_copy(
                src_ref=dst if src is None else src, dst_ref=dst,
                send_sem=send_sems.at[7 * t + k], recv_sem=recv_sems.at[7 * t + k],
                device_id=to, device_id_type=MESH)

        mine, first, passed = [], [], []
        for t, (src_ref, out_ref) in enumerate(arrays):
            mine.append(pltpu.make_async_copy(src_ref, slot(out_ref, *me), local_sems.at[t]))
            mine[-1].start()
            first.append(copy(t, 0, me, sibling, src=src_ref))
            first += [copy(t, 1 + j, me, (*chip, c), src=src_ref) for j, chip in enumerate(chips)]
        for cp in first:
            cp.start()
        for t in range(2):
            for j, chip in enumerate(chips):
                copy(t, 1 + j, (*chip, c), me).wait_recv()
                passed.append(copy(t, 4 + j, (*chip, c), sibling))
                passed[-1].start()
        for t in range(2):
            copy(t, 0, sibling, me).wait_recv()
            for j, chip in enumerate(chips):
                copy(t, 4 + j, (*chip, 1 - c), me).wait_recv()
        for cp in first + passed:
            cp.wait_send()
        for cp in mine:
            cp.wait()

    any_spec = pl.BlockSpec(memory_space=pl.ANY)
    return pl.pallas_call(
        body, name="ag_weights",
        out_shape=(jax.ShapeDtypeStruct((N_DEV,) + wa.shape, wa.dtype),
                   jax.ShapeDtypeStruct((N_DEV,) + wb.shape, wb.dtype)),
        in_specs=[any_spec, any_spec], out_specs=(any_spec, any_spec),
        scratch_shapes=[pltpu.SemaphoreType.DMA((14,)), pltpu.SemaphoreType.DMA((14,)),
                        pltpu.SemaphoreType.DMA((2,))],
    )(wa, wb)


ANY_SPEC = pl.BlockSpec(memory_space=pl.ANY)


def _grid_first_last(grid):
    ids = [pl.program_id(a) for a in range(len(grid))]
    first = functools.reduce(lambda p, q: p & q, [i == 0 for i in ids])
    last = functools.reduce(lambda p, q: p & q, [i == n - 1 for i, n in zip(ids, grid)])
    return first, last


def _all_to_all_copies(src, dst, send_sems, recv_sems, local_sem):
    x, y, c = _mesh_pos()
    my = 4 * x + 2 * y + c
    copies = [pltpu.make_async_copy(src.at[my], dst.at[my], local_sem)]
    for d in range(1, N_DEV):
        px = 1 - x if d & 4 else x
        py = 1 - y if d & 2 else y
        pc = 1 - c if d & 1 else c
        copies.append(pltpu.make_async_remote_copy(
            src_ref=src.at[4 * px + 2 * py + pc], dst_ref=dst.at[my],
            send_sem=send_sems.at[d - 1], recv_sem=recv_sems.at[d - 1],
            device_id=(px, py, pc), device_id_type=MESH))
    return copies


def _other_chips():
    x, y, _ = _mesh_pos()
    return [(1 - x, y), (x, 1 - y), (1 - x, 1 - y)]


def _same_core_copies(src, dst, send_sems, recv_sems):
    c = lax.axis_index("c")
    return [pltpu.make_async_remote_copy(
        src_ref=src.at[j], dst_ref=dst.at[j], send_sem=send_sems.at[j], recv_sem=recv_sems.at[j],
        device_id=(*chip, c), device_id_type=MESH) for j, chip in enumerate(_other_chips())]


def _swap_with_sibling(partials):
    _, r, cdim = partials.shape

    def body(p_ref, o_ref, send_sems, recv_sems):
        x, y, c = _mesh_pos()
        dests = [(x, y)] + _other_chips()
        copies = [pltpu.make_async_remote_copy(
            src_ref=p_ref.at[4 * px + 2 * py + (1 - c)], dst_ref=o_ref.at[j],
            send_sem=send_sems.at[j], recv_sem=recv_sems.at[j],
            device_id=(x, y, 1 - c), device_id_type=MESH) for j, (px, py) in enumerate(dests)]
        for cp in copies:
            cp.start()
        for cp in copies:
            cp.wait()

    return pl.pallas_call(
        body, name="swap_with_sibling",
        out_shape=jax.ShapeDtypeStruct((4, r, cdim), partials.dtype),
        in_specs=[ANY_SPEC], out_specs=ANY_SPEC,
        scratch_shapes=[pltpu.SemaphoreType.DMA((4,)), pltpu.SemaphoreType.DMA((4,))],
    )(partials)


def _chip_presum(blocks, partials, from_sibling):
    _, r, cdim = partials.shape
    tr = _tile(r, 256)

    def body(blocks_ref, p_ref, s_ref, o_ref):
        o_ref[...] = (p_ref[...].astype(F32) + s_ref[...].astype(F32)).astype(o_ref.dtype)

    return pl.pallas_call(
        body, name="chip_presum",
        grid_spec=pltpu.PrefetchScalarGridSpec(
            num_scalar_prefetch=1, grid=(3, r // tr),
            in_specs=[pl.BlockSpec((None, tr, cdim), lambda j, i, blk: (blk[j], i, 0)),
                      pl.BlockSpec((None, tr, cdim), lambda j, i, blk: (1 + j, i, 0))],
            out_specs=pl.BlockSpec((None, tr, cdim), lambda j, i, blk: (j, i, 0))),
        out_shape=jax.ShapeDtypeStruct((3, r, cdim), partials.dtype),
        compiler_params=_params("parallel", "parallel"),
    )(blocks, partials, from_sibling)


def _mm_call(a, b, dims, nk, grid, a_spec, b_spec, o_spec, out_shape, acc_shape, name):
    def body(a_ref, b_ref, o_ref, acc_ref):
        k = pl.program_id(2)

        @pl.when(k == 0)
        def _():
            acc_ref[...] = jnp.zeros_like(acc_ref)

        acc_ref[...] += lax.dot_general(a_ref[...], b_ref[...], dims, preferred_element_type=F32)

        @pl.when(k == nk - 1)
        def _():
            o_ref[...] = acc_ref[...].astype(o_ref.dtype)

    return pl.pallas_call(
        body, name=name, grid=grid, in_specs=[a_spec, b_spec], out_specs=o_spec, out_shape=out_shape,
        scratch_shapes=[pltpu.VMEM(acc_shape, F32)],
        compiler_params=_params("parallel", "parallel", "arbitrary"),
    )(a, b)


MM_TM, MM_TN, MM_TK = 1024, 2048, 1024


def _mm_nn(a, b3, out_dtype, name):
    m, kk = a.shape
    g, _, nb = b3.shape
    tm, tn, tk = _tile(m, MM_TM), _tile(nb, MM_TN), _tile(kk, MM_TK)
    npb = nb // tn
    return _mm_call(
        a, b3, (((1,), (0,)), ((), ())), kk // tk, (m // tm, g * npb, kk // tk),
        pl.BlockSpec((tm, tk), lambda i, j, k: (i, k)),
        pl.BlockSpec((None, tk, tn), lambda i, j, k: (j // npb, k, j % npb)),
        pl.BlockSpec((tm, tn), lambda i, j, k: (i, j)),
        jax.ShapeDtypeStruct((m, g * nb), out_dtype), (tm, tn), name)


def _mm_nt(a, b3, out_dtype, name):
    m, kk = a.shape
    g, n, kb = b3.shape
    tm, tn, tk = _tile(m, MM_TM), _tile(n, MM_TN), _tile(kb, MM_TK)
    kpb = kb // tk
    return _mm_call(
        a, b3, NT_DIMS, kk // tk, (m // tm, n // tn, kk // tk),
        pl.BlockSpec((tm, tk), lambda i, j, k: (i, k)),
        pl.BlockSpec((None, tn, tk), lambda i, j, k: (k // kpb, j, k % kpb)),
        pl.BlockSpec((tm, tn), lambda i, j, k: (i, j)),
        jax.ShapeDtypeStruct((m, n), out_dtype), (tm, tn), name)


def _mm_nt_with_chip_exchange(a, b3, out_dtype, send3, name):
    m, kk = a.shape
    g, n, kb = b3.shape
    tm, tn, tk = _tile(m, MM_TM), _tile(n, MM_TN), _tile(kb, MM_TK)
    kpb = kb // tk
    grid = (m // tm, n // tn, kk // tk)

    def body(a_ref, b_ref, s_ref, o_ref, r_ref, acc_ref, send_sems, recv_sems):
        first, last = _grid_first_last(grid)
        k = pl.program_id(2)

        @pl.when(first)
        def _():
            for cp in _same_core_copies(s_ref, r_ref, send_sems, recv_sems):
                cp.start()

        @pl.when(k == 0)
        def _():
            acc_ref[...] = jnp.zeros_like(acc_ref)

        acc_ref[...] += lax.dot_general(a_ref[...], b_ref[...], NT_DIMS, preferred_element_type=F32)

        @pl.when(k == grid[2] - 1)
        def _():
            o_ref[...] = acc_ref[...].astype(o_ref.dtype)

        @pl.when(last)
        def _():
            for cp in _same_core_copies(s_ref, r_ref, send_sems, recv_sems):
                cp.wait()

    return pl.pallas_call(
        body, name=name, grid=grid,
        in_specs=[pl.BlockSpec((tm, tk), lambda i, j, k: (i, k)),
                  pl.BlockSpec((None, tn, tk), lambda i, j, k: (k // kpb, j, k % kpb)), ANY_SPEC],
        out_specs=(pl.BlockSpec((tm, tn), lambda i, j, k: (i, j)), ANY_SPEC),
        out_shape=(jax.ShapeDtypeStruct((m, n), out_dtype), jax.ShapeDtypeStruct(send3.shape, send3.dtype)),
        scratch_shapes=[pltpu.VMEM((tm, tn), F32), pltpu.SemaphoreType.DMA((3,)), pltpu.SemaphoreType.DMA((3,))],
        compiler_params=_params("arbitrary", "arbitrary", "arbitrary"),
    )(a, b3, send3)


def _mm_tn(a, b, g, out_dtype, name):
    t, m = a.shape
    nb = b.shape[1] // g
    tm, tn, tk = _tile(m, MM_TM), _tile(nb, MM_TN), _tile(t, MM_TK)
    npb = nb // tn
    return _mm_call(
        a, b, TN_DIMS, t // tk, (m // tm, g * npb, t // tk),
        pl.BlockSpec((tk, tm), lambda i, j, k: (k, i)),
        pl.BlockSpec((tk, tn), lambda i, j, k: (k, j)),
        pl.BlockSpec((None, tm, tn), lambda i, j, k: (j // npb, i, j % npb)),
        jax.ShapeDtypeStruct((g, m, nb), out_dtype), (tm, tn), name)


def _silu(z):
    return z * jax.nn.sigmoid(z)


def _ada_fwd(c_all, w_shard, b_own):
    r, d = c_all.shape
    na = w_shard.shape[1]
    tk = _tile(d, 512)
    nk = d // tk

    def body(c_ref, w_ref, b_ref, o_ref):
        k = pl.program_id(0)

        @pl.when(k == 0)
        def _():
            o_ref[...] = jnp.zeros_like(o_ref) + b_ref[...]

        cs = _silu(c_ref[...]).astype(BF16)
        o_ref[...] += jnp.dot(cs, w_ref[...].astype(BF16), preferred_element_type=F32)

    return pl.pallas_call(
        body, name="ada_fwd", grid=(nk,),
        in_specs=[pl.BlockSpec((r, tk), lambda k: (0, k)), pl.BlockSpec((tk, na), lambda k: (k, 0)),
                  pl.BlockSpec((1, na), lambda k: (0, 0))],
        out_specs=pl.BlockSpec((r, na), lambda k: (0, 0)),
        out_shape=jax.ShapeDtypeStruct((r, na), F32),
        compiler_params=_params("arbitrary"),
    )(c_all, w_shard, b_own)


def _adam(w, g, m, v):
    nm = ADAM_B1 * m + (1.0 - ADAM_B1) * g
    nv = ADAM_B2 * v + (1.0 - ADAM_B2) * (g * g)
    m_hat = nm / (1.0 - ADAM_B1 ** ADAM_STEP)
    v_hat = nv / (1.0 - ADAM_B2 ** ADAM_STEP)
    delta = -ADAM_LR * (m_hat / (jnp.sqrt(v_hat) + ADAM_EPS) + ADAM_WD * w)
    return delta, nm, nv


def _ada_bwd_adam(c_rows, dmod_cols, w, m, v):
    bg, d = c_rows.shape
    na = w.shape[1]
    tr = _tile(d, 256)

    def body(c_ref, dm_ref, w_ref, m_ref, v_ref, g_ref, d_ref, nm_ref, nv_ref):
        cs = _silu(c_ref[...]).astype(BF16)
        g = lax.dot_general(cs, dm_ref[...].astype(BF16), TN_DIMS, preferred_element_type=F32)
        delta, nm, nv = _adam(w_ref[...], g, m_ref[...], v_ref[...])
        g_ref[...] = g
        d_ref[...] = delta
        nm_ref[...] = nm
        nv_ref[...] = nv

    blk = pl.BlockSpec((tr, na), lambda i: (i, 0))
    shp = jax.ShapeDtypeStruct((d, na), F32)
    return pl.pallas_call(
        body, name="ada_bwd_adam", grid=(d // tr,),
        in_specs=[pl.BlockSpec((bg, tr), lambda i: (0, i)), pl.BlockSpec((bg, na), lambda i: (0, 0)), blk, blk, blk],
        out_specs=(blk, blk, blk, blk), out_shape=(shp, shp, shp, shp),
        compiler_params=_params("parallel"),
    )(c_rows, dmod_cols, w, m, v)


def _small_adam(pkg_all, n_batch_rows, w, m, v):
    d = w.shape[1]

    def body(p_ref, w_ref, m_ref, v_ref, g_ref, d_ref, nm_ref, nv_ref):
        for part in range(3):
            acc = jnp.zeros((1, d), F32)
            for dev in range(N_DEV):
                for b in range(n_batch_rows // 3):
                    acc = acc + p_ref[dev, 3 * b + part:3 * b + part + 1, :]
            g_ref[part:part + 1, :] = acc
        for rrow in range(3):
            acc = jnp.zeros((1, d), F32)
            for dev in range(N_DEV):
                acc = acc + p_ref[dev, n_batch_rows + rrow:n_batch_rows + rrow + 1, :]
            g_ref[3 + rrow:4 + rrow, :] = acc
        g_ref[6:8, :] = jnp.zeros((2, d), F32)
        g = g_ref[...]
        delta, nm, nv = _adam(w_ref[...], g, m_ref[...], v_ref[...])
        d_ref[...] = delta
        nm_ref[...] = nm
        nv_ref[...] = nv

    vm = pl.BlockSpec(memory_space=pltpu.VMEM)
    shp = jax.ShapeDtypeStruct((SMALL_ROWS, d), F32)
    return pl.pallas_call(
        body, name="small_adam", in_specs=[vm, vm, vm, vm], out_specs=(vm, vm, vm, vm),
        out_shape=(shp, shp, shp, shp),
    )(pkg_all, w, m, v)


def _adam_from_chip_sums(own_slot, partials, from_sibling, from_chips, w, m, v, name):
    _, r, c = partials.shape
    tr = _tile(r, 128)

    def body(slot_ref, p_ref, s_ref, f_ref, w_ref, m_ref, v_ref, g_ref, d_ref, nm_ref, nv_ref):
        g = p_ref[...].astype(F32) + s_ref[...].astype(F32)
        for j in range(3):
            g = g + f_ref[j].astype(F32)
        delta, nm, nv = _adam(w_ref[...], g, m_ref[...], v_ref[...])
        g_ref[...] = g
        d_ref[...] = delta
        nm_ref[...] = nm
        nv_ref[...] = nv

    blk = pl.BlockSpec((tr, c), lambda i, slot: (i, 0))
    shp = jax.ShapeDtypeStruct((r, c), F32)
    return pl.pallas_call(
        body, name=name,
        grid_spec=pltpu.PrefetchScalarGridSpec(
            num_scalar_prefetch=1, grid=(r // tr,),
            in_specs=[pl.BlockSpec((None, tr, c), lambda i, slot: (slot[0], i, 0)),
                      pl.BlockSpec((None, tr, c), lambda i, slot: (0, i, 0)),
                      pl.BlockSpec((3, tr, c), lambda i, slot: (0, i, 0)), blk, blk, blk],
            out_specs=(blk, blk, blk, blk)),
        out_shape=(shp, shp, shp, shp),
        compiler_params=_params("parallel"),
    )(own_slot, partials, from_sibling, from_chips, w, m, v)


def _adam_from_partials(recv, w, m, v, name):
    _, r, c = recv.shape
    tr = _tile(r, 128)

    def body(p_ref, w_ref, m_ref, v_ref, g_ref, d_ref, nm_ref, nv_ref):
        g = p_ref[0].astype(F32)
        for dev in range(1, N_DEV):
            g = g + p_ref[dev].astype(F32)
        delta, nm, nv = _adam(w_ref[...], g, m_ref[...], v_ref[...])
        g_ref[...] = g
        d_ref[...] = delta
        nm_ref[...] = nm
        nv_ref[...] = nv

    blk = pl.BlockSpec((tr, c), lambda i: (i, 0))
    shp = jax.ShapeDtypeStruct((r, c), F32)
    return pl.pallas_call(
        body, name=name, grid=(r // tr,),
        in_specs=[pl.BlockSpec((N_DEV, tr, c), lambda i: (0, i, 0)), blk, blk, blk],
        out_specs=(blk, blk, blk, blk), out_shape=(shp, shp, shp, shp),
        compiler_params=_params("parallel"),
    )(recv, w, m, v)


def _norm_mod(x, g_norm, scale, shift):
    b, s, d = x.shape
    ts = _tile(s, 256)

    def body(x_ref, g_ref, sc_ref, sh_ref, h_ref):
        xv = x_ref[...]
        r = lax.rsqrt(jnp.mean(xv * xv, axis=-1, keepdims=True) + EPS)
        xn = (xv * r) * g_ref[...]
        h_ref[...] = (xn * (1.0 + sc_ref[...]) + sh_ref[...]).astype(BF16)

    tok = pl.BlockSpec((None, ts, d), lambda i, j: (i, j, 0))
    per_b = pl.BlockSpec((None, 1, d), lambda i, j: (i, 0, 0))
    return pl.pallas_call(
        body, name="norm_mod", grid=(b, s // ts),
        in_specs=[tok, pl.BlockSpec((1, d), lambda i, j: (0, 0)), per_b, per_b],
        out_specs=tok, out_shape=jax.ShapeDtypeStruct((b, s, d), BF16),
        compiler_params=_params("parallel", "parallel"),
    )(x, g_norm, scale, shift)


def _final_fwd_bwd(x, out, gate, g_final, target):
    b, s, d = x.shape
    ts = _tile(s, 256)

    def body(x_ref, o_ref, gt_ref, g_ref, t_ref, loss_ref, dx2_ref, dout_ref, dgate_ref, gg_ref):
        i, j = pl.program_id(0), pl.program_id(1)

        @pl.when((i == 0) & (j == 0))
        def _():
            loss_ref[...] = jnp.zeros_like(loss_ref)
            gg_ref[...] = jnp.zeros_like(gg_ref)

        @pl.when(j == 0)
        def _():
            dgate_ref[...] = jnp.zeros_like(dgate_ref)

        ov = o_ref[...]
        gt = gt_ref[...]
        x2 = x_ref[...] + gt * ov
        r = lax.rsqrt(jnp.mean(x2 * x2, axis=-1, keepdims=True) + EPS)
        xh = x2 * r
        err = xh * g_ref[...] - t_ref[...]
        loss_ref[...] += 0.5 * jnp.sum(jnp.mean(err * err, axis=-1, keepdims=True), axis=0, keepdims=True)
        dfin = err * (1.0 / d)
        gg_ref[...] += jnp.sum(dfin * xh, axis=0, keepdims=True)
        dxh = dfin * g_ref[...]
        dx2 = r * (dxh - xh * jnp.mean(dxh * xh, axis=-1, keepdims=True))
        dx2_ref[...] = dx2
        dout_ref[...] = (gt * dx2).astype(BF16)
        dgate_ref[...] += jnp.sum(dx2 * ov, axis=0, keepdims=True)

    tok = pl.BlockSpec((None, ts, d), lambda i, j: (i, j, 0))
    per_b = pl.BlockSpec((None, 1, d), lambda i, j: (i, 0, 0))
    vec = pl.BlockSpec((1, d), lambda i, j: (0, 0))
    return pl.pallas_call(
        body, name="final_fwd_bwd", grid=(b, s // ts),
        in_specs=[tok, tok, per_b, vec, tok],
        out_specs=(pl.BlockSpec((8, 128), lambda i, j: (0, 0)), tok, tok, per_b, vec),
        out_shape=(jax.ShapeDtypeStruct((8, 128), F32), jax.ShapeDtypeStruct((b, s, d), F32),
                   jax.ShapeDtypeStruct((b, s, d), BF16), jax.ShapeDtypeStruct((b, 1, d), F32),
                   jax.ShapeDtypeStruct((1, d), F32)),
        compiler_params=_params("arbitrary", "arbitrary"),
    )(x, out, gate, g_final, target)


def _norm_bwd(x, dh, dx2, scale, g_norm):
    b, s, d = x.shape
    ts = _tile(s, 256)

    def body(x_ref, dh_ref, dx2_ref, sc_ref, g_ref, gx_ref, dsh_ref, dsc_ref, gg_ref):
        i, j = pl.program_id(0), pl.program_id(1)

        @pl.when((i == 0) & (j == 0))
        def _():
            gg_ref[...] = jnp.zeros_like(gg_ref)

        @pl.when(j == 0)
        def _():
            dsh_ref[...] = jnp.zeros_like(dsh_ref)
            dsc_ref[...] = jnp.zeros_like(dsc_ref)

        xv = x_ref[...]
        dhv = dh_ref[...]
        r = lax.rsqrt(jnp.mean(xv * xv, axis=-1, keepdims=True) + EPS)
        xh = xv * r
        xn = xh * g_ref[...]
        dsh_ref[...] += jnp.sum(dhv, axis=0, keepdims=True)
        dsc_ref[...] += jnp.sum(dhv * xn, axis=0, keepdims=True)
        dxn = dhv * (1.0 + sc_ref[...])
        gg_ref[...] += jnp.sum(dxn * xh, axis=0, keepdims=True)
        dxh = dxn * g_ref[...]
        gx_ref[...] = dx2_ref[...] + r * (dxh - xh * jnp.mean(dxh * xh, axis=-1, keepdims=True))

    tok = pl.BlockSpec((None, ts, d), lambda i, j: (i, j, 0))
    per_b = pl.BlockSpec((None, 1, d), lambda i, j: (i, 0, 0))
    vec = pl.BlockSpec((1, d), lambda i, j: (0, 0))
    return pl.pallas_call(
        body, name="norm_bwd", grid=(b, s // ts),
        in_specs=[tok, tok, tok, per_b, vec],
        out_specs=(tok, per_b, per_b, vec),
        out_shape=(jax.ShapeDtypeStruct((b, s, d), F32), jax.ShapeDtypeStruct((b, 1, d), F32),
                   jax.ShapeDtypeStruct((b, 1, d), F32), jax.ShapeDtypeStruct((1, d), F32)),
        compiler_params=_params("arbitrary", "arbitrary"),
    )(x, dh, dx2, scale, g_norm)


def _gate_fwd(y_sb, y_dl, proj, g_sb, g_dl):
    t, e = y_sb.shape
    n_heads = e // HEAD_DIM
    tt = _tile(t, 256)

    def body(ys_ref, yd_ref, zs_ref, zd_ref, gs_ref, gd_ref, o_ref):
        for grp, (y_ref, z_ref, g_ref) in enumerate(((ys_ref, zs_ref, gs_ref), (yd_ref, zd_ref, gd_ref))):
            for h in range(n_heads):
                sl = slice(h * HEAD_DIM, (h + 1) * HEAD_DIM)
                y = y_ref[:, sl]
                r = lax.rsqrt(jnp.mean(y * y, axis=-1, keepdims=True) + EPS)
                yn = (y * r) * g_ref[:, sl]
                z = z_ref[:, sl].astype(F32)
                o_ref[:, grp * e + h * HEAD_DIM:grp * e + (h + 1) * HEAD_DIM] = (yn * _silu(z)).astype(BF16)

    yblk = pl.BlockSpec((tt, e), lambda i: (i, 0))
    gblk = pl.BlockSpec((1, e), lambda i: (0, 0))
    return pl.pallas_call(
        body, name="gate_fwd", grid=(t // tt,),
        in_specs=[yblk, yblk, pl.BlockSpec((tt, e), lambda i: (i, 3)), pl.BlockSpec((tt, e), lambda i: (i, 7)),
                  gblk, gblk],
        out_specs=pl.BlockSpec((tt, 2 * e), lambda i: (i, 0)),
        out_shape=jax.ShapeDtypeStruct((t, 2 * e), BF16),
        compiler_params=_params("parallel"),
    )(y_sb, y_dl, proj, proj, g_sb, g_dl)


def _gate_bwd(dyg, y_sb, y_dl, proj, g_sb, g_dl):
    t, e = y_sb.shape
    n_heads = e // HEAD_DIM
    tt = _tile(t, 256)

    def body(dg_ref, ys_ref, yd_ref, zs_ref, zd_ref, gs_ref, gd_ref,
             dys_ref, dyd_ref, dzs_ref, dzd_ref, ggs_ref, ggd_ref):
        @pl.when(pl.program_id(0) == 0)
        def _():
            ggs_ref[...] = jnp.zeros_like(ggs_ref)
            ggd_ref[...] = jnp.zeros_like(ggd_ref)

        groups = ((ys_ref, zs_ref, gs_ref, dys_ref, dzs_ref, ggs_ref), (yd_ref, zd_ref, gd_ref, dyd_ref, dzd_ref, ggd_ref))
        for grp, (y_ref, z_ref, g_ref, dy_ref, dz_ref, gg_ref) in enumerate(groups):
            for h in range(n_heads):
                sl = slice(h * HEAD_DIM, (h + 1) * HEAD_DIM)
                dg = dg_ref[:, grp * e + h * HEAD_DIM:grp * e + (h + 1) * HEAD_DIM].astype(F32)
                y = y_ref[:, sl]
                z = z_ref[:, sl].astype(F32)
                g = g_ref[:, sl]
                r = lax.rsqrt(jnp.mean(y * y, axis=-1, keepdims=True) + EPS)
                yh = y * r
                sig = jax.nn.sigmoid(z)
                dyn = dg * (z * sig)
                dz_ref[:, sl] = (dg * (yh * g) * (sig * (1.0 + z * (1.0 - sig)))).astype(BF16)
                gg_ref[:, sl] += jnp.sum(dyn * yh, axis=0, keepdims=True)
                dyh = dyn * g
                dy_ref[:, sl] = (r * (dyh - yh * jnp.mean(dyh * yh, axis=-1, keepdims=True))).astype(BF16)

    yblk = pl.BlockSpec((tt, e), lambda i: (i, 0))
    gblk = pl.BlockSpec((1, e), lambda i: (0, 0))
    act = jax.ShapeDtypeStruct((t, e), BF16)
    vec = jax.ShapeDtypeStruct((1, e), F32)
    return pl.pallas_call(
        body, name="gate_bwd", grid=(t // tt,),
        in_specs=[pl.BlockSpec((tt, 2 * e), lambda i: (i, 0)), yblk, yblk,
                  pl.BlockSpec((tt, e), lambda i: (i, 3)), pl.BlockSpec((tt, e), lambda i: (i, 7)), gblk, gblk],
        out_specs=(yblk, yblk, yblk, yblk, gblk, gblk),
        out_shape=(act, act, act, act, vec, vec),
        compiler_params=_params("arbitrary"),
    )(dyg, y_sb, y_dl, proj, proj, g_sb, g_dl)


ATT_TQ = 256
HEADS_PER_STEP = 2
SOFTPLUS_CLAMP = 30.0


def _split2_dot(x, u):
    hi = x.astype(BF16)
    lo = (x - hi.astype(F32)).astype(BF16)
    n = x.shape[0]
    both = jnp.dot(jnp.concatenate([hi, lo], axis=0), u, preferred_element_type=F32)
    return both[:n] + both[n:]


def _iota2(n):
    return lax.broadcasted_iota(jnp.int32, (n, n), 0), lax.broadcasted_iota(jnp.int32, (n, n), 1)


def _head_slices():
    return [slice(hh * HEAD_DIM, (hh + 1) * HEAD_DIM) for hh in range(HEADS_PER_STEP)]


def _att_specs(s, e, tq, col0):
    n_heads = e // HEAD_DIM
    hp = HEADS_PER_STEP
    assert n_heads % hp == 0 and col0 % hp == 0
    w = hp * HEAD_DIM
    q_spec = pl.BlockSpec((None, tq, w), lambda i, h, j: (i, j, col0 // hp + h))
    k_spec = pl.BlockSpec((None, s, w), lambda i, h, j: (i, 0, (col0 + n_heads) // hp + h))
    v_spec = pl.BlockSpec((None, s, w), lambda i, h, j: (i, 0, (col0 + 2 * n_heads) // hp + h))
    return q_spec, k_spec, v_spec


def _sb_fwd(proj3, e):
    b, s, _ = proj3.shape
    n_heads = e // HEAD_DIM
    hp = HEADS_PER_STEP
    tq = _tile(s, ATT_TQ)
    nq = s // tq
    inv = 1.0 / math.sqrt(HEAD_DIM)

    def body(q_ref, k_ref, v_ref, y_ref, tot_ref, acc_ref, car_ref):
        i = pl.program_id(2)
        row, col = _iota2(tq)
        before = row > col
        u_after = before.astype(BF16)
        acc_ref[...] = jnp.zeros_like(acc_ref)
        car_ref[...] = jnp.zeros_like(car_ref)

        def block(j, diagonal):
            keys = pl.ds(pl.multiple_of(j * tq, tq), tq)
            for hh, hs in enumerate(_head_slices()):
                z = lax.dot_general(q_ref[:, hs], k_ref[keys, hs], NT_DIMS, preferred_element_type=F32) * inv
                sp = jnp.maximum(jnp.log(1.0 + jnp.exp(jnp.minimum(z, SOFTPLUS_CLAMP))), z)
                lom = jnp.where(before, -sp, 0.0) if diagonal else -sp
                suffix = _split2_dot(lom, u_after) + car_ref[hh]
                a = jnp.exp((z - sp) + suffix)
                if diagonal:
                    a = jnp.where(before, a, 0.0)
                acc_ref[hh] += jnp.dot(a.astype(BF16), v_ref[keys, hs], preferred_element_type=F32)
                car_ref[hh] += jnp.sum(lom, axis=1, keepdims=True)

        block(i, True)

        def step(it, carry):
            block(i - it, False)
            return carry

        lax.fori_loop(1, i + 1, step, 0)
        for hh, hs in enumerate(_head_slices()):
            y_ref[:, hs] = acc_ref[hh]
            tot_ref[:, hs] = jnp.broadcast_to(car_ref[hh], (tq, HEAD_DIM))

    q_spec, k_spec, v_spec = _att_specs(s, e, tq, 0)
    blk_q = pl.BlockSpec((None, tq, hp * HEAD_DIM), lambda i, h, j: (i, j, h))
    shp = jax.ShapeDtypeStruct((b, s, e), F32)
    return pl.pallas_call(
        body, name="sb_fwd", grid=(b, n_heads // hp, nq),
        in_specs=[q_spec, k_spec, v_spec],
        out_specs=(blk_q, blk_q), out_shape=(shp, shp),
        scratch_shapes=[pltpu.VMEM((hp, tq, HEAD_DIM), F32), pltpu.VMEM((hp, tq, 1), F32)],
        compiler_params=_params("parallel", "parallel", "arbitrary"),
    )(proj3, proj3, proj3)


def _sb_bwd(proj3, lom_total, dy, partials):
    b, s, e = dy.shape
    n_heads = e // HEAD_DIM
    hp = HEADS_PER_STEP
    tq = _tile(s, ATT_TQ)
    nq = s // tq
    inv = 1.0 / math.sqrt(HEAD_DIM)
    grid = (b, n_heads // hp, nq)

    def body(q_ref, k_ref, v_ref, tot_ref, dy_ref, p_ref, dq_ref, dk_ref, dv_ref, r_ref,
             dqa, dka, dva, car, car2, send_sems, recv_sems, local_sem):
        i = pl.program_id(2)
        first, last = _grid_first_last(grid)

        @pl.when(first)
        def _():
            for cp in _all_to_all_copies(p_ref, r_ref, send_sems, recv_sems, local_sem):
                cp.start()

        @pl.when(i == 0)
        def _():
            dka[...] = jnp.zeros_like(dka)
            dva[...] = jnp.zeros_like(dva)

        row, col = _iota2(tq)
        before = row > col
        u_upto = (row <= col).astype(BF16)
        u_before = (row < col).astype(BF16)
        dqa[...] = jnp.zeros_like(dqa)
        car[...] = jnp.zeros_like(car)
        car2[...] = jnp.zeros_like(car2)

        def block(j, diagonal):
            keys = pl.ds(pl.multiple_of(j * tq, tq), tq)
            for hh, hs in enumerate(_head_slices()):
                q, k, v, dy_b = q_ref[:, hs], k_ref[keys, hs], v_ref[keys, hs], dy_ref[:, hs]
                z = lax.dot_general(q, k, NT_DIMS, preferred_element_type=F32) * inv
                ez = jnp.exp(jnp.minimum(z, SOFTPLUS_CLAMP))
                sp = jnp.maximum(jnp.log(1.0 + ez), z)
                lom = jnp.where(before, -sp, 0.0) if diagonal else -sp
                suffix = tot_ref[:, hh * HEAD_DIM:hh * HEAD_DIM + 1] - (_split2_dot(lom, u_upto) + car[hh])
                a = jnp.exp((z - sp) + suffix)
                if diagonal:
                    a = jnp.where(before, a, 0.0)
                one_minus_beta = 1.0 / (1.0 + ez)
                beta = ez * one_minus_beta
                dl = a * lax.dot_general(dy_b, v, NT_DIMS, preferred_element_type=F32)
                prefix = _split2_dot(dl, u_before) + car2[hh]
                dz = (dl * one_minus_beta - prefix * beta) * inv
                if diagonal:
                    dz = jnp.where(before, dz, 0.0)
                dz = dz.astype(BF16)
                dqa[hh] += jnp.dot(dz, k, preferred_element_type=F32)
                dka[keys, hs] += lax.dot_general(dz, q, TN_DIMS, preferred_element_type=F32)
                dva[keys, hs] += lax.dot_general(a.astype(BF16), dy_b, TN_DIMS, preferred_element_type=F32)
                car[hh] += jnp.sum(lom, axis=1, keepdims=True)
                car2[hh] += jnp.sum(dl, axis=1, keepdims=True)

        def step(j, carry):
            block(j, False)
            return carry

        lax.fori_loop(0, i, step, 0)
        block(i, True)
        for hh, hs in enumerate(_head_slices()):
            dq_ref[:, hs] = dqa[hh].astype(BF16)

        @pl.when(i == nq - 1)
        def _():
            dk_ref[...] = dka[...].astype(BF16)
            dv_ref[...] = dva[...].astype(BF16)

        @pl.when(last)
        def _():
            for cp in _all_to_all_copies(p_ref, r_ref, send_sems, recv_sems, local_sem):
                cp.wait()

    q_spec, k_spec, v_spec = _att_specs(s, e, tq, 0)
    w = hp * HEAD_DIM
    blk_q = pl.BlockSpec((None, tq, w), lambda i, h, j: (i, j, h))
    blk_kv = pl.BlockSpec((None, s, w), lambda i, h, j: (i, 0, h))
    shp = jax.ShapeDtypeStruct((b, s, e), BF16)
    return pl.pallas_call(
        body, name="sb_bwd", grid=grid,
        in_specs=[q_spec, k_spec, v_spec, blk_q, blk_q, ANY_SPEC],
        out_specs=(blk_q, blk_kv, blk_kv, ANY_SPEC),
        out_shape=(shp, shp, shp, jax.ShapeDtypeStruct(partials.shape, partials.dtype)),
        scratch_shapes=[pltpu.VMEM((hp, tq, HEAD_DIM), F32), pltpu.VMEM((s, w), F32), pltpu.VMEM((s, w), F32),
                        pltpu.VMEM((hp, tq, 1), F32), pltpu.VMEM((hp, tq, 1), F32),
                        pltpu.SemaphoreType.DMA((N_DEV - 1,)), pltpu.SemaphoreType.DMA((N_DEV - 1,)),
                        pltpu.SemaphoreType.DMA],
        compiler_params=_params("arbitrary", "arbitrary", "arbitrary"),
    )(proj3, proj3, proj3, lom_total, dy, partials)


def _dil_near_tiles(tq):
    return (DIL_PAIRS[1][0] + tq - 1) // tq + 1


def _dil_fill_bias(bias_ref, sl_ref, tq):
    row, col = _iota2(tq)
    for hh in range(HEADS_PER_STEP):
        slope = sl_ref[hh, 0:1, 0:1]
        for d in range(_dil_near_tiles(tq) + 1):
            dist = d * tq + row - col
            cnt = jnp.zeros(dist.shape, jnp.int32)
            for window, dilation in DIL_PAIRS:
                cnt = cnt + (((dist & (dilation - 1)) == 0) & (dist <= window)).astype(jnp.int32)
            bias = jnp.where(cnt == 3, math.log(3.0), jnp.where(cnt == 2, math.log(2.0), 0.0))
            bias_ref[hh, d] = jnp.where((dist >= 0) & (cnt > 0), bias - slope * dist.astype(F32), NEG)


def _dil_scores(q, k, bias_ref, hh, slope, it, tq, inv):
    near = _dil_near_tiles(tq)
    beyond = jnp.maximum(it - near, 0).astype(F32) * float(tq)
    sc = lax.dot_general(q, k, NT_DIMS, preferred_element_type=F32) * inv
    return (sc + bias_ref[hh, jnp.minimum(it, near)]) - slope * beyond


def _dil_fwd(proj3, e, slopes):
    b, s, _ = proj3.shape
    n_heads = e // HEAD_DIM
    hp = HEADS_PER_STEP
    tq = _tile(s, ATT_TQ)
    nq = s // tq
    inv = 1.0 / math.sqrt(HEAD_DIM)
    assert s <= DIL_PAIRS[2][0]

    def body(q_ref, k_ref, v_ref, sl_ref, y_ref, lse_ref, acc_ref, m_ref, l_ref, bias_ref):
        i = pl.program_id(2)

        @pl.when(i == 0)
        def _():
            _dil_fill_bias(bias_ref, sl_ref, tq)

        acc_ref[...] = jnp.zeros_like(acc_ref)
        m_ref[...] = jnp.full_like(m_ref, NEG)
        l_ref[...] = jnp.zeros_like(l_ref)

        def step(it, carry):
            keys = pl.ds(pl.multiple_of((i - it) * tq, tq), tq)
            for hh, hs in enumerate(_head_slices()):
                sc = _dil_scores(q_ref[:, hs], k_ref[keys, hs], bias_ref, hh, sl_ref[hh, 0:1, 0:1], it, tq, inv)
                m_old = m_ref[hh]
                m_new = jnp.maximum(m_old, jnp.max(sc, axis=1, keepdims=True))
                p = jnp.exp(sc - m_new)
                alpha = jnp.exp(m_old - m_new)
                l_ref[hh] = alpha * l_ref[hh] + jnp.sum(p, axis=1, keepdims=True)
                acc_ref[hh] = alpha * acc_ref[hh] + jnp.dot(p.astype(BF16), v_ref[keys, hs], preferred_element_type=F32)
                m_ref[hh] = m_new
            return carry

        lax.fori_loop(0, i + 1, step, 0)
        for hh, hs in enumerate(_head_slices()):
            y_ref[:, hs] = acc_ref[hh] / l_ref[hh]
            lse_ref[:, hs] = jnp.broadcast_to(m_ref[hh] + jnp.log(l_ref[hh]), (tq, HEAD_DIM))

    q_spec, k_spec, v_spec = _att_specs(s, e, tq, 4 * n_heads)
    blk_q = pl.BlockSpec((None, tq, hp * HEAD_DIM), lambda i, h, j: (i, j, h))
    shp = jax.ShapeDtypeStruct((b, s, e), F32)
    return pl.pallas_call(
        body, name="dil_fwd", grid=(b, n_heads // hp, nq),
        in_specs=[q_spec, k_spec, v_spec, pl.BlockSpec((hp, 8, HEAD_DIM), lambda i, h, j: (h, 0, 0))],
        out_specs=(blk_q, blk_q), out_shape=(shp, shp),
        scratch_shapes=[pltpu.VMEM((hp, tq, HEAD_DIM), F32), pltpu.VMEM((hp, tq, 1), F32), pltpu.VMEM((hp, tq, 1), F32),
                        pltpu.VMEM((hp, _dil_near_tiles(tq) + 1, tq, tq), F32)],
        compiler_params=_params("parallel", "parallel", "arbitrary"),
    )(proj3, proj3, proj3, slopes)


def _dil_bwd(proj3, y, lse, dy, slopes):
    b, s, e = y.shape
    n_heads = e // HEAD_DIM
    hp = HEADS_PER_STEP
    tq = _tile(s, ATT_TQ)
    nq = s // tq
    inv = 1.0 / math.sqrt(HEAD_DIM)

    def body(q_ref, k_ref, v_ref, sl_ref, y_ref, lse_ref, dy_ref, dq_ref, dk_ref, dv_ref, dqa, dka, dva, bias_ref):
        i = pl.program_id(2)

        @pl.when(i == 0)
        def _():
            dka[...] = jnp.zeros_like(dka)
            dva[...] = jnp.zeros_like(dva)
            _dil_fill_bias(bias_ref, sl_ref, tq)

        delta = [jnp.sum(dy_ref[:, hs].astype(F32) * y_ref[:, hs], axis=1, keepdims=True) for hs in _head_slices()]
        dqa[...] = jnp.zeros_like(dqa)

        def step(it, carry):
            keys = pl.ds(pl.multiple_of((i - it) * tq, tq), tq)
            for hh, hs in enumerate(_head_slices()):
                q, k, v, dy_b = q_ref[:, hs], k_ref[keys, hs], v_ref[keys, hs], dy_ref[:, hs]
                sc = _dil_scores(q, k, bias_ref, hh, sl_ref[hh, 0:1, 0:1], it, tq, inv)
                p = jnp.exp(sc - lse_ref[:, hh * HEAD_DIM:hh * HEAD_DIM + 1])
                dp = lax.dot_general(dy_b, v, NT_DIMS, preferred_element_type=F32)
                ds = ((p * (dp - delta[hh])) * inv).astype(BF16)
                dqa[hh] += jnp.dot(ds, k, preferred_element_type=F32)
                dka[keys, hs] += lax.dot_general(ds, q, TN_DIMS, preferred_element_type=F32)
                dva[keys, hs] += lax.dot_general(p.astype(BF16), dy_b, TN_DIMS, preferred_element_type=F32)
            return carry

        lax.fori_loop(0, i + 1, step, 0)
        for hh, hs in enumerate(_head_slices()):
            dq_ref[:, hs] = dqa[hh].astype(BF16)

        @pl.when(i == nq - 1)
        def _():
            dk_ref[...] = dka[...].astype(BF16)
            dv_ref[...] = dva[...].astype(BF16)

    q_spec, k_spec, v_spec = _att_specs(s, e, tq, 4 * n_heads)
    w = hp * HEAD_DIM
    blk_q = pl.BlockSpec((None, tq, w), lambda i, h, j: (i, j, h))
    blk_kv = pl.BlockSpec((None, s, w), lambda i, h, j: (i, 0, h))
    shp = jax.ShapeDtypeStruct((b, s, e), BF16)
    return pl.pallas_call(
        body, name="dil_bwd", grid=(b, n_heads // hp, nq),
        in_specs=[q_spec, k_spec, v_spec, pl.BlockSpec((hp, 8, HEAD_DIM), lambda i, h, j: (h, 0, 0)),
                  blk_q, blk_q, blk_q],
        out_specs=(blk_q, blk_kv, blk_kv), out_shape=(shp, shp, shp),
        scratch_shapes=[pltpu.VMEM((hp, tq, HEAD_DIM), F32), pltpu.VMEM((s, w), F32), pltpu.VMEM((s, w), F32),
                        pltpu.VMEM((hp, _dil_near_tiles(tq) + 1, tq, tq), F32)],
        compiler_params=_params("parallel", "parallel", "arbitrary"),
    )(proj3, proj3, proj3, slopes, y, lse, dy)


def kernel(x, c, w_ada, b_ada, g_norm, w_in, g_sb, g_dil, w_out, g_final, loss_target, m_w_ada, m_b_ada, m_g_norm, m_w_in, m_g_sb, m_g_dil, m_w_out, m_g_final, v_w_ada, v_b_ada, v_g_norm, v_w_in, v_g_sb, v_g_dil, v_w_out, v_g_final):
    b, s, d = x.shape
    t = b * s
    e = w_in.shape[2]
    n_heads = e // HEAD_DIM
    na = w_ada.shape[2]
    r_out = w_out.shape[1]
    assert g_sb.shape[1] == e and g_dil.shape[1] == e and N_DEV * r_out == 2 * e and N_DEV * na == 3 * d
    assert b <= SMALL_ROWS and 3 * b + 3 <= 2 * SMALL_ROWS
    ix, iy, ic = _mesh_pos()
    me = 4 * ix + 2 * iy + ic

    c_all = _allgather_rows(jnp.pad(c, ((0, SMALL_ROWS - b), (0, 0))), "ag_c")
    b_own = lax.dynamic_slice(b_ada, (0, me * na), (1, na))
    mod_cols = _ada_fwd(c_all, w_ada[0], b_own)
    mod_all = _allgather_rows(mod_cols, "ag_mod").reshape(N_DEV, N_DEV, SMALL_ROWS, na)
    mod_own = lax.dynamic_slice(mod_all, (0, me, 0, 0), (N_DEV, 1, b, na))[:, 0]
    mod = mod_own.transpose(1, 0, 2).reshape(b, 1, 3 * d)
    shift, scale, gate = mod[:, :, :d], mod[:, :, d:2 * d], mod[:, :, 2 * d:]

    w_in3, w_out3 = _allgather_weights(w_in[0].astype(BF16), w_out[0].astype(BF16))
    w_out1 = w_out3.reshape(1, N_DEV * r_out, d)

    h = _norm_mod(x, g_norm, scale, shift).reshape(t, d)
    proj = _mm_nn(h, w_in3, BF16, "mm_proj")
    proj3 = proj.reshape(b, s, N_DEV * e)
    slopes = jnp.exp2(-ALIBI_MAX_BIAS * jnp.arange(1, n_heads + 1, dtype=F32) / n_heads)
    slopes = jnp.broadcast_to(slopes[:, None, None], (n_heads, 8, HEAD_DIM))
    y_sb, lom_total = _sb_fwd(proj3, e)
    y_dl, lse = _dil_fwd(proj3, e, slopes)
    yg = _gate_fwd(y_sb.reshape(t, e), y_dl.reshape(t, e), proj, g_sb, g_dil)
    out = _mm_nn(yg, w_out1, F32, "mm_out").reshape(b, s, d)
    loss_p, dx2, d_out, dgate, gg_final = _final_fwd_bwd(x, out, gate, g_final.reshape(1, d), loss_target)

    d_out2 = d_out.reshape(t, d)
    dyg = _mm_nt(d_out2, w_out1, BF16, "mm_dy")
    gw_out_p = _mm_tn(yg, d_out2, 1, BF16, "mm_gw_out").reshape(N_DEV, r_out, d)
    dy_sb, dy_dl, dz_sb, dz_dl, gg_sb, gg_dl = _gate_bwd(dyg, y_sb.reshape(t, e), y_dl.reshape(t, e), proj, g_sb, g_dil)
    dq_sb, dk_sb, dv_sb, recv_out = _sb_bwd(proj3, lom_total, dy_sb.reshape(b, s, e), gw_out_p)
    dq_dl, dk_dl, dv_dl = _dil_bwd(proj3, y_dl, lse, dy_dl.reshape(b, s, e), slopes)
    dproj = jnp.concatenate(
        [a.reshape(t, e) for a in (dq_sb, dk_sb, dv_sb, dz_sb, dq_dl, dk_dl, dv_dl, dz_dl)], axis=1)
    gw_in_p = _mm_tn(h, dproj, N_DEV, BF16, "mm_gw_in")
    gw_in_sib = _swap_with_sibling(gw_in_p)
    chip_blocks = jnp.stack([4 * px + 2 * py + ic for px, py in _other_chips()]).astype(jnp.int32)
    gw_in_send = _chip_presum(chip_blocks, gw_in_p, gw_in_sib)
    dh, gw_in_recv = _mm_nt_with_chip_exchange(dproj, w_in3, F32, gw_in_send, "mm_dh")
    dh = dh.reshape(b, s, d)
    grad_x, dshift, dscale, gg_norm = _norm_bwd(x, dh, dx2, scale, g_norm)

    dmod = jnp.concatenate([dshift, dscale, dgate], axis=1).reshape(3 * b, d)
    pkg = jnp.concatenate([dmod, gg_norm, gg_final, jnp.concatenate([gg_sb, gg_dl], axis=1),
                           jnp.zeros((2 * SMALL_ROWS - 3 * b - 3, d), F32)], axis=0)
    pkg_all = _allgather_rows(pkg, "ag_small_grads").reshape(N_DEV, 2 * SMALL_ROWS, d)
    dmod_all = pkg_all[:, :3 * b].reshape(N_DEV * b, 3 * d)
    dmod_cols = lax.dynamic_slice(dmod_all, (0, me * na), (N_DEV * b, na))
    c_rows = c_all.reshape(N_DEV, SMALL_ROWS, d)[:, :b].reshape(N_DEV * b, d)
    g_w_ada, d_w_ada, nm_w_ada, nv_w_ada = _ada_bwd_adam(c_rows, dmod_cols, w_ada[0], m_w_ada[0], v_w_ada[0])

    def pack(b_ada_like, g_norm_like, g_sb_like, g_dil_like, g_final_like):
        return jnp.concatenate([b_ada_like.reshape(3, d), g_norm_like.reshape(1, d), g_final_like.reshape(1, d),
                                jnp.concatenate([g_sb_like, g_dil_like], axis=1).reshape(1, d),
                                jnp.zeros((2, d), F32)], axis=0)

    small = _small_adam(pkg_all, 3 * b, pack(b_ada, g_norm, g_sb, g_dil, g_final),
                        pack(m_b_ada, m_g_norm, m_g_sb, m_g_dil, m_g_final),
                        pack(v_b_ada, v_g_norm, v_g_sb, v_g_dil, v_g_final))

    def unpack(p):
        return (p[0:3].reshape(1, 3 * d), p[3:4], p[5:6, :e], p[5:6, e:], p[4])

    sm_g, sm_d, sm_m, sm_v = (unpack(p) for p in small)

    g_w_in, d_w_in, nm_w_in, nv_w_in = _adam_from_chip_sums(
        jnp.reshape(me, (1,)).astype(jnp.int32), gw_in_p, gw_in_sib, gw_in_recv, w_in[0], m_w_in[0], v_w_in[0], "adam_w_in")
    g_w_out, d_w_out, nm_w_out, nv_w_out = _adam_from_partials(recv_out, w_out[0], m_w_out[0], v_w_out[0], "adam_w_out")

    loss = lax.psum(loss_p[0, 0], ("x", "y", "c"))

    def weights(ada, small_parts, w_in_part, w_out_part):
        b_ada_p, g_norm_p, g_sb_p, g_dil_p, g_final_p = small_parts
        return (ada[None], b_ada_p, g_norm_p, w_in_part[None], g_sb_p, g_dil_p, w_out_part[None], g_final_p)

    return (loss, grad_x,
            *weights(g_w_ada, sm_g, g_w_in, g_w_out),
            *weights(d_w_ada, sm_d, d_w_in, d_w_out),
            *weights(nm_w_ada, sm_m, nm_w_in, nm_w_out),
            *weights(nv_w_ada, sm_v, nv_w_in, nv_w_out))
```

```python
import functools
import math

import jax
import jax.numpy as jnp
from jax import lax
from jax.experimental import pallas as pl
from jax.experimental.pallas import tpu as pltpu

F32 = jnp.float32
BF16 = jnp.bfloat16
MESH = pl.DeviceIdType.MESH

N_DEV = 8
HEAD_DIM = 128
EPS = 1e-6
ALIBI_MAX_BIAS = 8.0
DIL_PAIRS = ((128, 1), (512, 4), (2048, 16))
DIL_STEPS = 128
NEG = -1e30

ADAM_LR = 0.001
ADAM_B1 = 0.9
ADAM_B2 = 0.999
ADAM_EPS = 1e-08
ADAM_WD = 0.01
ADAM_STEP = 10

VMEM_LIMIT_BYTES = 56 * 1024 * 1024
SMALL_ROWS = 8

NT_DIMS = (((1,), (1,)), ((), ()))
TN_DIMS = (((0,), (0,)), ((), ()))


def _params(*semantics):
    return pltpu.CompilerParams(dimension_semantics=semantics, vmem_limit_bytes=VMEM_LIMIT_BYTES)


def _tile(n, want):
    t = min(n, want)
    assert n % t == 0, (n, want)
    return t


def _mesh_pos():
    return lax.axis_index("x"), lax.axis_index("y"), lax.axis_index("c")


def _allgather_rows(x_shard, name):
    m_per, n = x_shard.shape

    def body(x_ref, out_ref, send_sems, recv_sems, local_sem):
        x, y, c = _mesh_pos()
        me, sibling = (x, y, c), (x, y, 1 - c)
        chips = [(1 - x, y), (x, 1 - y), (1 - x, 1 - y)]

        def rows(px, py, pc):
            return out_ref.at[pl.ds((4 * px + 2 * py + pc) * m_per, m_per), :]

        def copy(k, block, to, src=None):
            return pltpu.make_async_remote_copy(
                src_ref=rows(*block) if src is None else src, dst_ref=rows(*block),
                send_sem=send_sems.at[k], recv_sem=recv_sems.at[k], device_id=to, device_id_type=MESH)

        mine = pltpu.make_async_copy(x_ref, rows(*me), local_sem)
        mine.start()
        first = [copy(0, me, sibling, src=x_ref)]
        first += [copy(1 + j, me, (*chip, c), src=x_ref) for j, chip in enumerate(chips)]
        for cp in first:
            cp.start()
        passed = [copy(4 + j, (*chip, c), sibling) for j, chip in enumerate(chips)]
        for j, chip in enumerate(chips):
            copy(1 + j, (*chip, c), me).wait_recv()
            passed[j].start()
        copy(0, sibling, me).wait_recv()
        for j, chip in enumerate(chips):
            copy(4 + j, (*chip, 1 - c), me).wait_recv()
        for cp in first + passed:
            cp.wait_send()
        mine.wait()

    return pl.pallas_call(
        body, name=name,
        out_shape=jax.ShapeDtypeStruct((N_DEV * m_per, n), x_shard.dtype),
        in_specs=[pl.BlockSpec(memory_space=pltpu.VMEM)],
        out_specs=pl.BlockSpec(memory_space=pltpu.VMEM),
        scratch_shapes=[pltpu.SemaphoreType.DMA((7,)), pltpu.SemaphoreType.DMA((7,)), pltpu.SemaphoreType.DMA],
    )(x_shard)


def _allgather_weights(wa, wb):
    def body(a_ref, b_ref, oa_ref, ob_ref, send_sems, recv_sems, local_sems):
        x, y, c = _mesh_pos()
        me, sibling = (x, y, c), (x, y, 1 - c)
        chips = [(1 - x, y), (x, 1 - y), (1 - x, 1 - y)]
        arrays = ((a_ref, oa_ref), (b_ref, ob_ref))

        def slot(out_ref, px, py, pc):
            return out_ref.at[4 * px + 2 * py + pc]

        def copy(t, k, block, to, src=None):
            dst = slot(arrays[t][1], *block)
            return pltpu.make_async_remote_copy(
                src_ref=dst if src is None else src, dst_ref=dst,
                send_sem=send_sems.at[7 * t + k], recv_sem=recv_sems.at[7 * t + k],
                device_id=to, device_id_type=MESH)

        mine, first, passed = [], [], []
        for t, (src_ref, out_ref) in enumerate(arrays):
            mine.append(pltpu.make_async_copy(src_ref, slot(out_ref, *me), local_sems.at[t]))
            mine[-1].start()
            first.append(copy(t, 0, me, sibling, src=src_ref))
            first += [copy(t, 1 + j, me, (*chip, c), src=src_ref) for j, chip in enumerate(chips)]
        for cp in first:
            cp.start()
        for t in range(2):
            for j, chip in enumerate(chips):
                copy(t, 1 + j, (*chip, c), me).wait_recv()
                passed.append(copy(t, 4 + j, (*chip, c), sibling))
                passed[-1].start()
        for t in range(2):
            copy(t, 0, sibling, me).wait_recv()
            for j, chip in enumerate(chips):
                copy(t, 4 + j, (*chip, 1 - c), me).wait_recv()
        for cp in first + passed:
            cp.wait_send()
        for cp in mine:
            cp.wait()

    any_spec = pl.BlockSpec(memory_space=pl.ANY)
    return pl.pallas_call(
        body, name="ag_weights",
        out_shape=(jax.ShapeDtypeStruct((N_DEV,) + wa.shape, wa.dtype),
                   jax.ShapeDtypeStruct((N_DEV,) + wb.shape, wb.dtype)),
        in_specs=[any_spec, any_spec], out_specs=(any_spec, any_spec),
        scratch_shapes=[pltpu.SemaphoreType.DMA((14,)), pltpu.SemaphoreType.DMA((14,)),
                        pltpu.SemaphoreType.DMA((2,))],
    )(wa, wb)


ANY_SPEC = pl.BlockSpec(memory_space=pl.ANY)


def _grid_first_last(grid):
    ids = [pl.program_id(a) for a in range(len(grid))]
    first = functools.reduce(lambda p, q: p & q, [i == 0 for i in ids])
    last = functools.reduce(lambda p, q: p & q, [i == n - 1 for i, n in zip(ids, grid)])
    return first, last


def _all_to_all_copies(src, dst, send_sems, recv_sems, local_sem):
    x, y, c = _mesh_pos()
    my = 4 * x + 2 * y + c
    copies = [pltpu.make_async_copy(src.at[my], dst.at[my], local_sem)]
    for d in range(1, N_DEV):
        px = 1 - x if d & 4 else x
        py = 1 - y if d & 2 else y
        pc = 1 - c if d & 1 else c
        copies.append(pltpu.make_async_remote_copy(
            src_ref=src.at[4 * px + 2 * py + pc], dst_ref=dst.at[my],
            send_sem=send_sems.at[d - 1], recv_sem=recv_sems.at[d - 1],
            device_id=(px, py, pc), device_id_type=MESH))
    return copies


def _other_chips():
    x, y, _ = _mesh_pos()
    return [(1 - x, y), (x, 1 - y), (1 - x, 1 - y)]


def _same_core_copies(src, dst, send_sems, recv_sems):
    c = lax.axis_index("c")
    return [pltpu.make_async_remote_copy(
        src_ref=src.at[j], dst_ref=dst.at[j], send_sem=send_sems.at[j], recv_sem=recv_sems.at[j],
        device_id=(*chip, c), device_id_type=MESH) for j, chip in enumerate(_other_chips())]


def _swap_with_sibling(partials):
    _, r, cdim = partials.shape

    def body(p_ref, o_ref, send_sems, recv_sems):
        x, y, c = _mesh_pos()
        dests = [(x, y)] + _other_chips()
        copies = [pltpu.make_async_remote_copy(
            src_ref=p_ref.at[4 * px + 2 * py + (1 - c)], dst_ref=o_ref.at[j],
            send_sem=send_sems.at[j], recv_sem=recv_sems.at[j],
            device_id=(x, y, 1 - c), device_id_type=MESH) for j, (px, py) in enumerate(dests)]
        for cp in copies:
            cp.start()
        for cp in copies:
            cp.wait()

    return pl.pallas_call(
        body, name="swap_with_sibling",
        out_shape=jax.ShapeDtypeStruct((4, r, cdim), partials.dtype),
        in_specs=[ANY_SPEC], out_specs=ANY_SPEC,
        scratch_shapes=[pltpu.SemaphoreType.DMA((4,)), pltpu.SemaphoreType.DMA((4,))],
    )(partials)


def _chip_presum(blocks, partials, from_sibling):
    _, r, cdim = partials.shape
    tr = _tile(r, 256)

    def body(blocks_ref, p_ref, s_ref, o_ref):
        o_ref[...] = (p_ref[...].astype(F32) + s_ref[...].astype(F32)).astype(o_ref.dtype)

    return pl.pallas_call(
        body, name="chip_presum",
        grid_spec=pltpu.PrefetchScalarGridSpec(
            num_scalar_prefetch=1, grid=(3, r // tr),
            in_specs=[pl.BlockSpec((None, tr, cdim), lambda j, i, blk: (blk[j], i, 0)),
                      pl.BlockSpec((None, tr, cdim), lambda j, i, blk: (1 + j, i, 0))],
            out_specs=pl.BlockSpec((None, tr, cdim), lambda j, i, blk: (j, i, 0))),
        out_shape=jax.ShapeDtypeStruct((3, r, cdim), partials.dtype),
        compiler_params=_params("parallel", "parallel"),
    )(blocks, partials, from_sibling)


def _mm_call(a, b, dims, nk, grid, a_spec, b_spec, o_spec, out_shape, acc_shape, name):
    def body(a_ref, b_ref, o_ref, acc_ref):
        k = pl.program_id(2)

        @pl.when(k == 0)
        def _():
            acc_ref[...] = jnp.zeros_like(acc_ref)

        acc_ref[...] += lax.dot_general(a_ref[...], b_ref[...], dims, preferred_element_type=F32)

        @pl.when(k == nk - 1)
        def _():
            o_ref[...] = acc_ref[...].astype(o_ref.dtype)

    return pl.pallas_call(
        body, name=name, grid=grid, in_specs=[a_spec, b_spec], out_specs=o_spec, out_shape=out_shape,
        scratch_shapes=[pltpu.VMEM(acc_shape, F32)],
        compiler_params=_params("parallel", "parallel", "arbitrary"),
    )(a, b)


MM_TM, MM_TN, MM_TK = 1024, 2048, 1024


def _mm_nn(a, b3, out_dtype, name):
    m, kk = a.shape
    g, _, nb = b3.shape
    tm, tn, tk = _tile(m, MM_TM), _tile(nb, MM_TN), _tile(kk, MM_TK)
    npb = nb // tn
    return _mm_call(
        a, b3, (((1,), (0,)), ((), ())), kk // tk, (m // tm, g * npb, kk // tk),
        pl.BlockSpec((tm, tk), lambda i, j, k: (i, k)),
        pl.BlockSpec((None, tk, tn), lambda i, j, k: (j // npb, k, j % npb)),
        pl.BlockSpec((tm, tn), lambda i, j, k: (i, j)),
        jax.ShapeDtypeStruct((m, g * nb), out_dtype), (tm, tn), name)


def _mm_nt(a, b3, out_dtype, name):
    m, kk = a.shape
    g, n, kb = b3.shape
    tm, tn, tk = _tile(m, MM_TM), _tile(n, MM_TN), _tile(kb, MM_TK)
    kpb = kb // tk
    return _mm_call(
        a, b3, NT_DIMS, kk // tk, (m // tm, n // tn, kk // tk),
        pl.BlockSpec((tm, tk), lambda i, j, k: (i, k)),
        pl.BlockSpec((None, tn, tk), lambda i, j, k: (k // kpb, j, k % kpb)),
        pl.BlockSpec((tm, tn), lambda i, j, k: (i, j)),
        jax.ShapeDtypeStruct((m, n), out_dtype), (tm, tn), name)


def _mm_nt_with_chip_exchange(a, b3, out_dtype, send3, name):
    m, kk = a.shape
    g, n, kb = b3.shape
    tm, tn, tk = _tile(m, MM_TM), _tile(n, MM_TN), _tile(kb, MM_TK)
    kpb = kb // tk
    grid = (m // tm, n // tn, kk // tk)

    def body(a_ref, b_ref, s_ref, o_ref, r_ref, acc_ref, send_sems, recv_sems):
        first, last = _grid_first_last(grid)
        k = pl.program_id(2)

        @pl.when(first)
        def _():
            for cp in _same_core_copies(s_ref, r_ref, send_sems, recv_sems):
                cp.start()

        @pl.when(k == 0)
        def _():
            acc_ref[...] = jnp.zeros_like(acc_ref)

        acc_ref[...] += lax.dot_general(a_ref[...], b_ref[...], NT_DIMS, preferred_element_type=F32)

        @pl.when(k == grid[2] - 1)
        def _():
            o_ref[...] = acc_ref[...].astype(o_ref.dtype)

        @pl.when(last)
        def _():
            for cp in _same_core_copies(s_ref, r_ref, send_sems, recv_sems):
                cp.wait()

    return pl.pallas_call(
        body, name=name, grid=grid,
        in_specs=[pl.BlockSpec((tm, tk), lambda i, j, k: (i, k)),
                  pl.BlockSpec((None, tn, tk), lambda i, j, k: (k // kpb, j, k % kpb)), ANY_SPEC],
        out_specs=(pl.BlockSpec((tm, tn), lambda i, j, k: (i, j)), ANY_SPEC),
        out_shape=(jax.ShapeDtypeStruct((m, n), out_dtype), jax.ShapeDtypeStruct(send3.shape, send3.dtype)),
        scratch_shapes=[pltpu.VMEM((tm, tn), F32), pltpu.SemaphoreType.DMA((3,)), pltpu.SemaphoreType.DMA((3,))],
        compiler_params=_params("arbitrary", "arbitrary", "arbitrary"),
    )(a, b3, send3)


def _mm_tn(a, b, g, out_dtype, name):
    t, m = a.shape
    nb = b.shape[1] // g
    tm, tn, tk = _tile(m, MM_TM), _tile(nb, MM_TN), _tile(t, MM_TK)
    npb = nb // tn
    return _mm_call(
        a, b, TN_DIMS, t // tk, (m // tm, g * npb, t // tk),
        pl.BlockSpec((tk, tm), lambda i, j, k: (k, i)),
        pl.BlockSpec((tk, tn), lambda i, j, k: (k, j)),
        pl.BlockSpec((None, tm, tn), lambda i, j, k: (j // npb, i, j % npb)),
        jax.ShapeDtypeStruct((g, m, nb), out_dtype), (tm, tn), name)


def _silu(z):
    return z * jax.nn.sigmoid(z)


def _ada_fwd(c_all, w_shard, b_own):
    r, d = c_all.shape
    na = w_shard.shape[1]
    tk = _tile(d, 512)
    nk = d // tk

    def body(c_ref, w_ref, b_ref, o_ref):
        k = pl.program_id(0)

        @pl.when(k == 0)
        def _():
            o_ref[...] = jnp.zeros_like(o_ref) + b_ref[...]

        cs = _silu(c_ref[...]).astype(BF16)
        o_ref[...] += jnp.dot(cs, w_ref[...].astype(BF16), preferred_element_type=F32)

    return pl.pallas_call(
        body, name="ada_fwd", grid=(nk,),
        in_specs=[pl.BlockSpec((r, tk), lambda k: (0, k)), pl.BlockSpec((tk, na), lambda k: (k, 0)),
                  pl.BlockSpec((1, na), lambda k: (0, 0))],
        out_specs=pl.BlockSpec((r, na), lambda k: (0, 0)),
        out_shape=jax.ShapeDtypeStruct((r, na), F32),
        compiler_params=_params("arbitrary"),
    )(c_all, w_shard, b_own)


def _adam(w, g, m, v):
    nm = ADAM_B1 * m + (1.0 - ADAM_B1) * g
    nv = ADAM_B2 * v + (1.0 - ADAM_B2) * (g * g)
    m_hat = nm / (1.0 - ADAM_B1 ** ADAM_STEP)
    v_hat = nv / (1.0 - ADAM_B2 ** ADAM_STEP)
    delta = -ADAM_LR * (m_hat / (jnp.sqrt(v_hat) + ADAM_EPS) + ADAM_WD * w)
    return delta, nm, nv


def _ada_bwd_adam(c_rows, dmod_cols, w, m, v):
    bg, d = c_rows.shape
    na = w.shape[1]
    tr = _tile(d, 256)

    def body(c_ref, dm_ref, w_ref, m_ref, v_ref, g_ref, d_ref, nm_ref, nv_ref):
        cs = _silu(c_ref[...]).astype(BF16)
        g = lax.dot_general(cs, dm_ref[...].astype(BF16), TN_DIMS, preferred_element_type=F32)
        delta, nm, nv = _adam(w_ref[...], g, m_ref[...], v_ref[...])
        g_ref[...] = g
        d_ref[...] = delta
        nm_ref[...] = nm
        nv_ref[...] = nv

    blk = pl.BlockSpec((tr, na), lambda i: (i, 0))
    shp = jax.ShapeDtypeStruct((d, na), F32)
    return pl.pallas_call(
        body, name="ada_bwd_adam", grid=(d // tr,),
        in_specs=[pl.BlockSpec((bg, tr), lambda i: (0, i)), pl.BlockSpec((bg, na), lambda i: (0, 0)), blk, blk, blk],
        out_specs=(blk, blk, blk, blk), out_shape=(shp, shp, shp, shp),
        compiler_params=_params("parallel"),
    )(c_rows, dmod_cols, w, m, v)


def _small_adam(pkg_all, n_batch_rows, w, m, v):
    d = w.shape[1]

    def body(p_ref, w_ref, m_ref, v_ref, g_ref, d_ref, nm_ref, nv_ref):
        for part in range(3):
            acc = jnp.zeros((1, d), F32)
            for dev in range(N_DEV):
                for b in range(n_batch_rows // 3):
                    acc = acc + p_ref[dev, 3 * b + part:3 * b + part + 1, :]
            g_ref[part:part + 1, :] = acc
        for rrow in range(3):
            acc = jnp.zeros((1, d), F32)
            for dev in range(N_DEV):
                acc = acc + p_ref[dev, n_batch_rows + rrow:n_batch_rows + rrow + 1, :]
            g_ref[3 + rrow:4 + rrow, :] = acc
        g_ref[6:8, :] = jnp.zeros((2, d), F32)
        g = g_ref[...]
        delta, nm, nv = _adam(w_ref[...], g, m_ref[...], v_ref[...])
        d_ref[...] = delta
        nm_ref[...] = nm
        nv_ref[...] = nv

    vm = pl.BlockSpec(memory_space=pltpu.VMEM)
    shp = jax.ShapeDtypeStruct((SMALL_ROWS, d), F32)
    return pl.pallas_call(
        body, name="small_adam", in_specs=[vm, vm, vm, vm], out_specs=(vm, vm, vm, vm),
        out_shape=(shp, shp, shp, shp),
    )(pkg_all, w, m, v)


def _adam_from_chip_sums(own_slot, partials, from_sibling, from_chips, w, m, v, name):
    _, r, c = partials.shape
    tr = _tile(r, 128)

    def body(slot_ref, p_ref, s_ref, f_ref, w_ref, m_ref, v_ref, g_ref, d_ref, nm_ref, nv_ref):
        g = p_ref[...].astype(F32) + s_ref[...].astype(F32)
        for j in range(3):
            g = g + f_ref[j].astype(F32)
        delta, nm, nv = _adam(w_ref[...], g, m_ref[...], v_ref[...])
        g_ref[...] = g
        d_ref[...] = delta
        nm_ref[...] = nm
        nv_ref[...] = nv

    blk = pl.BlockSpec((tr, c), lambda i, slot: (i, 0))
    shp = jax.ShapeDtypeStruct((r, c), F32)
    return pl.pallas_call(
        body, name=name,
        grid_spec=pltpu.PrefetchScalarGridSpec(
            num_scalar_prefetch=1, grid=(r // tr,),
            in_specs=[pl.BlockSpec((None, tr, c), lambda i, slot: (slot[0], i, 0)),
                      pl.BlockSpec((None, tr, c), lambda i, slot: (0, i, 0)),
                      pl.BlockSpec((3, tr, c), lambda i, slot: (0, i, 0)), blk, blk, blk],
            out_specs=(blk, blk, blk, blk)),
        out_shape=(shp, shp, shp, shp),
        compiler_params=_params("parallel"),
    )(own_slot, partials, from_sibling, from_chips, w, m, v)


def _adam_from_partials(recv, w, m, v, name):
    _, r, c = recv.shape
    tr = _tile(r, 128)

    def body(p_ref, w_ref, m_ref, v_ref, g_ref, d_ref, nm_ref, nv_ref):
        g = p_ref[0].astype(F32)
        for dev in range(1, N_DEV):
            g = g + p_ref[dev].astype(F32)
        delta, nm, nv = _adam(w_ref[...], g, m_ref[...], v_ref[...])
        g_ref[...] = g
        d_ref[...] = delta
        nm_ref[...] = nm
        nv_ref[...] = nv

    blk = pl.BlockSpec((tr, c), lambda i: (i, 0))
    shp = jax.ShapeDtypeStruct((r, c), F32)
    return pl.pallas_call(
        body, name=name, grid=(r // tr,),
        in_specs=[pl.BlockSpec((N_DEV, tr, c), lambda i: (0, i, 0)), blk, blk, blk],
        out_specs=(blk, blk, blk, blk), out_shape=(shp, shp, shp, shp),
        compiler_params=_params("parallel"),
    )(recv, w, m, v)


def _norm_mod(x, g_norm, scale, shift):
    b, s, d = x.shape
    ts = _tile(s, 256)

    def body(x_ref, g_ref, sc_ref, sh_ref, h_ref):
        xv = x_ref[...]
        r = lax.rsqrt(jnp.mean(xv * xv, axis=-1, keepdims=True) + EPS)
        xn = (xv * r) * g_ref[...]
        h_ref[...] = (xn * (1.0 + sc_ref[...]) + sh_ref[...]).astype(BF16)

    tok = pl.BlockSpec((None, ts, d), lambda i, j: (i, j, 0))
    per_b = pl.BlockSpec((None, 1, d), lambda i, j: (i, 0, 0))
    return pl.pallas_call(
        body, name="norm_mod", grid=(b, s // ts),
        in_specs=[tok, pl.BlockSpec((1, d), lambda i, j: (0, 0)), per_b, per_b],
        out_specs=tok, out_shape=jax.ShapeDtypeStruct((b, s, d), BF16),
        compiler_params=_params("parallel", "parallel"),
    )(x, g_norm, scale, shift)


def _final_fwd_bwd(x, out, gate, g_final, target):
    b, s, d = x.shape
    ts = _tile(s, 256)

    def body(x_ref, o_ref, gt_ref, g_ref, t_ref, loss_ref, dx2_ref, dout_ref, dgate_ref, gg_ref):
        i, j = pl.program_id(0), pl.program_id(1)

        @pl.when((i == 0) & (j == 0))
        def _():
            loss_ref[...] = jnp.zeros_like(loss_ref)
            gg_ref[...] = jnp.zeros_like(gg_ref)

        @pl.when(j == 0)
        def _():
            dgate_ref[...] = jnp.zeros_like(dgate_ref)

        ov = o_ref[...]
        gt = gt_ref[...]
        x2 = x_ref[...] + gt * ov
        r = lax.rsqrt(jnp.mean(x2 * x2, axis=-1, keepdims=True) + EPS)
        xh = x2 * r
        err = xh * g_ref[...] - t_ref[...]
        loss_ref[...] += 0.5 * jnp.sum(jnp.mean(err * err, axis=-1, keepdims=True), axis=0, keepdims=True)
        dfin = err * (1.0 / d)
        gg_ref[...] += jnp.sum(dfin * xh, axis=0, keepdims=True)
        dxh = dfin * g_ref[...]
        dx2 = r * (dxh - xh * jnp.mean(dxh * xh, axis=-1, keepdims=True))
        dx2_ref[...] = dx2
        dout_ref[...] = (gt * dx2).astype(BF16)
        dgate_ref[...] += jnp.sum(dx2 * ov, axis=0, keepdims=True)

    tok = pl.BlockSpec((None, ts, d), lambda i, j: (i, j, 0))
    per_b = pl.BlockSpec((None, 1, d), lambda i, j: (i, 0, 0))
    vec = pl.BlockSpec((1, d), lambda i, j: (0, 0))
    return pl.pallas_call(
        body, name="final_fwd_bwd", grid=(b, s // ts),
        in_specs=[tok, tok, per_b, vec, tok],
        out_specs=(pl.BlockSpec((8, 128), lambda i, j: (0, 0)), tok, tok, per_b, vec),
        out_shape=(jax.ShapeDtypeStruct((8, 128), F32), jax.ShapeDtypeStruct((b, s, d), F32),
                   jax.ShapeDtypeStruct((b, s, d), BF16), jax.ShapeDtypeStruct((b, 1, d), F32),
                   jax.ShapeDtypeStruct((1, d), F32)),
        compiler_params=_params("arbitrary", "arbitrary"),
    )(x, out, gate, g_final, target)


def _norm_bwd(x, dh, dx2, scale, g_norm):
    b, s, d = x.shape
    ts = _tile(s, 256)

    def body(x_ref, dh_ref, dx2_ref, sc_ref, g_ref, gx_ref, dsh_ref, dsc_ref, gg_ref):
        i, j = pl.program_id(0), pl.program_id(1)

        @pl.when((i == 0) & (j == 0))
        def _():
            gg_ref[...] = jnp.zeros_like(gg_ref)

        @pl.when(j == 0)
        def _():
            dsh_ref[...] = jnp.zeros_like(dsh_ref)
            dsc_ref[...] = jnp.zeros_like(dsc_ref)

        xv = x_ref[...]
        dhv = dh_ref[...]
        r = lax.rsqrt(jnp.mean(xv * xv, axis=-1, keepdims=True) + EPS)
        xh = xv * r
        xn = xh * g_ref[...]
        dsh_ref[...] += jnp.sum(dhv, axis=0, keepdims=True)
        dsc_ref[...] += jnp.sum(dhv * xn, axis=0, keepdims=True)
        dxn = dhv * (1.0 + sc_ref[...])
        gg_ref[...] += jnp.sum(dxn * xh, axis=0, keepdims=True)
        dxh = dxn * g_ref[...]
        gx_ref[...] = dx2_ref[...] + r * (dxh - xh * jnp.mean(dxh * xh, axis=-1, keepdims=True))

    tok = pl.BlockSpec((None, ts, d), lambda i, j: (i, j, 0))
    per_b = pl.BlockSpec((None, 1, d), lambda i, j: (i, 0, 0))
    vec = pl.BlockSpec((1, d), lambda i, j: (0, 0))
    return pl.pallas_call(
        body, name="norm_bwd", grid=(b, s // ts),
        in_specs=[tok, tok, tok, per_b, vec],
        out_specs=(tok, per_b, per_b, vec),
        out_shape=(jax.ShapeDtypeStruct((b, s, d), F32), jax.ShapeDtypeStruct((b, 1, d), F32),
                   jax.ShapeDtypeStruct((b, 1, d), F32), jax.ShapeDtypeStruct((1, d), F32)),
        compiler_params=_params("arbitrary", "arbitrary"),
    )(x, dh, dx2, scale, g_norm)


def _gate_fwd(y_sb, y_dl, proj, g_sb, g_dl):
    t, e = y_sb.shape
    n_heads = e // HEAD_DIM
    tt = _tile(t, 256)

    def body(ys_ref, yd_ref, zs_ref, zd_ref, gs_ref, gd_ref, o_ref):
        for grp, (y_ref, z_ref, g_ref) in enumerate(((ys_ref, zs_ref, gs_ref), (yd_ref, zd_ref, gd_ref))):
            for h in range(n_heads):
                sl = slice(h * HEAD_DIM, (h + 1) * HEAD_DIM)
                y = y_ref[:, sl]
                r = lax.rsqrt(jnp.mean(y * y, axis=-1, keepdims=True) + EPS)
                yn = (y * r) * g_ref[:, sl]
                z = z_ref[:, sl].astype(F32)
                o_ref[:, grp * e + h * HEAD_DIM:grp * e + (h + 1) * HEAD_DIM] = (yn * _silu(z)).astype(BF16)

    yblk = pl.BlockSpec((tt, e), lambda i: (i, 0))
    gblk = pl.BlockSpec((1, e), lambda i: (0, 0))
    return pl.pallas_call(
        body, name="gate_fwd", grid=(t // tt,),
        in_specs=[yblk, yblk, pl.BlockSpec((tt, e), lambda i: (i, 3)), pl.BlockSpec((tt, e), lambda i: (i, 7)),
                  gblk, gblk],
        out_specs=pl.BlockSpec((tt, 2 * e), lambda i: (i, 0)),
        out_shape=jax.ShapeDtypeStruct((t, 2 * e), BF16),
        compiler_params=_params("parallel"),
    )(y_sb, y_dl, proj, proj, g_sb, g_dl)


def _gate_bwd(dyg, y_sb, y_dl, proj, g_sb, g_dl):
    t, e = y_sb.shape
    n_heads = e // HEAD_DIM
    tt = _tile(t, 256)

    def body(dg_ref, ys_ref, yd_ref, zs_ref, zd_ref, gs_ref, gd_ref,
             dys_ref, dyd_ref, dzs_ref, dzd_ref, ggs_ref, ggd_ref):
        @pl.when(pl.program_id(0) == 0)
        def _():
            ggs_ref[...] = jnp.zeros_like(ggs_ref)
            ggd_ref[...] = jnp.zeros_like(ggd_ref)

        groups = ((ys_ref, zs_ref, gs_ref, dys_ref, dzs_ref, ggs_ref), (yd_ref, zd_ref, gd_ref, dyd_ref, dzd_ref, ggd_ref))
        for grp, (y_ref, z_ref, g_ref, dy_ref, dz_ref, gg_ref) in enumerate(groups):
            for h in range(n_heads):
                sl = slice(h * HEAD_DIM, (h + 1) * HEAD_DIM)
                dg = dg_ref[:, grp * e + h * HEAD_DIM:grp * e + (h + 1) * HEAD_DIM].astype(F32)
                y = y_ref[:, sl]
                z = z_ref[:, sl].astype(F32)
                g = g_ref[:, sl]
                r = lax.rsqrt(jnp.mean(y * y, axis=-1, keepdims=True) + EPS)
                yh = y * r
                sig = jax.nn.sigmoid(z)
                dyn = dg * (z * sig)
                dz_ref[:, sl] = (dg * (yh * g) * (sig * (1.0 + z * (1.0 - sig)))).astype(BF16)
                gg_ref[:, sl] += jnp.sum(dyn * yh, axis=0, keepdims=True)
                dyh = dyn * g
                dy_ref[:, sl] = (r * (dyh - yh * jnp.mean(dyh * yh, axis=-1, keepdims=True))).astype(BF16)

    yblk = pl.BlockSpec((tt, e), lambda i: (i, 0))
    gblk = pl.BlockSpec((1, e), lambda i: (0, 0))
    act = jax.ShapeDtypeStruct((t, e), BF16)
    vec = jax.ShapeDtypeStruct((1, e), F32)
    return pl.pallas_call(
        body, name="gate_bwd", grid=(t // tt,),
        in_specs=[pl.BlockSpec((tt, 2 * e), lambda i: (i, 0)), yblk, yblk,
                  pl.BlockSpec((tt, e), lambda i: (i, 3)), pl.BlockSpec((tt, e), lambda i: (i, 7)), gblk, gblk],
        out_specs=(yblk, yblk, yblk, yblk, gblk, gblk),
        out_shape=(act, act, act, act, vec, vec),
        compiler_params=_params("arbitrary"),
    )(dyg, y_sb, y_dl, proj, proj, g_sb, g_dl)


ATT_TQ = 256
HEADS_PER_STEP = 4
SOFTPLUS_CLAMP = 30.0


def _split2_dot(x, u):
    hi = x.astype(BF16)
    lo = (x - hi.astype(F32)).astype(BF16)
    n = x.shape[0]
    both = jnp.dot(jnp.concatenate([hi, lo], axis=0), u, preferred_element_type=F32)
    return both[:n] + both[n:]


def _iota2(n):
    return lax.broadcasted_iota(jnp.int32, (n, n), 0), lax.broadcasted_iota(jnp.int32, (n, n), 1)


def _head_slices():
    return [slice(hh * HEAD_DIM, (hh + 1) * HEAD_DIM) for hh in range(HEADS_PER_STEP)]


def _att_specs(s, e, tq, col0):
    n_heads = e // HEAD_DIM
    hp = HEADS_PER_STEP
    assert n_heads % hp == 0 and col0 % hp == 0
    w = hp * HEAD_DIM
    q_spec = pl.BlockSpec((None, tq, w), lambda i, h, j: (i, j, col0 // hp + h))
    k_spec = pl.BlockSpec((None, s, w), lambda i, h, j: (i, 0, (col0 + n_heads) // hp + h))
    v_spec = pl.BlockSpec((None, s, w), lambda i, h, j: (i, 0, (col0 + 2 * n_heads) // hp + h))
    return q_spec, k_spec, v_spec


def _sb_fwd(proj3, e):
    b, s, _ = proj3.shape
    n_heads = e // HEAD_DIM
    hp = HEADS_PER_STEP
    tq = _tile(s, ATT_TQ)
    nq = s // tq
    inv = 1.0 / math.sqrt(HEAD_DIM)

    def body(q_ref, k_ref, v_ref, y_ref, tot_ref, acc_ref, car_ref):
        i = pl.program_id(2)
        row, col = _iota2(tq)
        before = row > col
        u_after = before.astype(BF16)
        acc_ref[...] = jnp.zeros_like(acc_ref)
        car_ref[...] = jnp.zeros_like(car_ref)

        def block(j, diagonal):
            keys = pl.ds(pl.multiple_of(j * tq, tq), tq)
            heads = list(enumerate(_head_slices()))
            zs = [lax.dot_general(q_ref[:, hs], k_ref[keys, hs], NT_DIMS, preferred_element_type=F32) * inv
                  for _, hs in heads]
            sps = [jnp.maximum(jnp.log(1.0 + jnp.exp(jnp.minimum(z, SOFTPLUS_CLAMP))), z) for z in zs]
            loms = [jnp.where(before, -sp, 0.0) if diagonal else -sp for sp in sps]
            sufs = [_split2_dot(loms[hh], u_after) + car_ref[hh] for hh, _ in heads]
            avs = [jnp.exp((zs[hh] - sps[hh]) + sufs[hh]) for hh, _ in heads]
            if diagonal:
                avs = [jnp.where(before, a, 0.0) for a in avs]
            pvs = [jnp.dot(avs[hh].astype(BF16), v_ref[keys, hs], preferred_element_type=F32) for hh, hs in heads]
            for hh, _ in heads:
                acc_ref[hh] += pvs[hh]
                car_ref[hh] += jnp.sum(loms[hh], axis=1, keepdims=True)

        block(i, True)

        def step(it, carry):
            block(i - it, False)
            return carry

        lax.fori_loop(1, i + 1, step, 0)
        for hh, hs in enumerate(_head_slices()):
            y_ref[:, hs] = acc_ref[hh]
            tot_ref[:, hs] = jnp.broadcast_to(car_ref[hh], (tq, HEAD_DIM))

    q_spec, k_spec, v_spec = _att_specs(s, e, tq, 0)
    blk_q = pl.BlockSpec((None, tq, hp * HEAD_DIM), lambda i, h, j: (i, j, h))
    shp = jax.ShapeDtypeStruct((b, s, e), F32)
    return pl.pallas_call(
        body, name="sb_fwd", grid=(b, n_heads // hp, nq),
        in_specs=[q_spec, k_spec, v_spec],
        out_specs=(blk_q, blk_q), out_shape=(shp, shp),
        scratch_shapes=[pltpu.VMEM((hp, tq, HEAD_DIM), F32), pltpu.VMEM((hp, tq, 1), F32)],
        compiler_params=_params("parallel", "parallel", "arbitrary"),
    )(proj3, proj3, proj3)


def _sb_bwd(proj3, lom_total, dy, partials):
    b, s, e = dy.shape
    n_heads = e // HEAD_DIM
    hp = HEADS_PER_STEP
    tq = _tile(s, ATT_TQ)
    nq = s // tq
    inv = 1.0 / math.sqrt(HEAD_DIM)
    grid = (b, n_heads // hp, nq)

    def body(q_ref, k_ref, v_ref, tot_ref, dy_ref, p_ref, dq_ref, dk_ref, dv_ref, r_ref,
             dqa, dka, dva, car, car2, send_sems, recv_sems, local_sem):
        i = pl.program_id(2)
        first, last = _grid_first_last(grid)

        @pl.when(first)
        def _():
            for cp in _all_to_all_copies(p_ref, r_ref, send_sems, recv_sems, local_sem):
                cp.start()

        @pl.when(i == 0)
        def _():
            dka[...] = jnp.zeros_like(dka)
            dva[...] = jnp.zeros_like(dva)

        row, col = _iota2(tq)
        before = row > col
        u_upto = (row <= col).astype(BF16)
        u_before = (row < col).astype(BF16)
        dqa[...] = jnp.zeros_like(dqa)
        car[...] = jnp.zeros_like(car)
        car2[...] = jnp.zeros_like(car2)

        def block(j, diagonal):
            keys = pl.ds(pl.multiple_of(j * tq, tq), tq)
            heads = list(enumerate(_head_slices()))
            zs = [lax.dot_general(q_ref[:, hs], k_ref[keys, hs], NT_DIMS, preferred_element_type=F32) * inv
                  for _, hs in heads]
            das = [lax.dot_general(dy_ref[:, hs], v_ref[keys, hs], NT_DIMS, preferred_element_type=F32) for _, hs in heads]
            ezs = [jnp.exp(jnp.minimum(z, SOFTPLUS_CLAMP)) for z in zs]
            sps = [jnp.maximum(jnp.log(1.0 + ezs[hh]), zs[hh]) for hh, _ in heads]
            loms = [jnp.where(before, -sp, 0.0) if diagonal else -sp for sp in sps]
            sufs = [tot_ref[:, hh * HEAD_DIM:hh * HEAD_DIM + 1] - (_split2_dot(loms[hh], u_upto) + car[hh])
                    for hh, _ in heads]
            avs = [jnp.exp((zs[hh] - sps[hh]) + sufs[hh]) for hh, _ in heads]
            if diagonal:
                avs = [jnp.where(before, a, 0.0) for a in avs]
            dls = [avs[hh] * das[hh] for hh, _ in heads]
            prefixes = [_split2_dot(dls[hh], u_before) + car2[hh] for hh, _ in heads]
            dzs = []
            for hh, _ in heads:
                one_minus_beta = 1.0 / (1.0 + ezs[hh])
                dz = (dls[hh] * one_minus_beta - prefixes[hh] * (ezs[hh] * one_minus_beta)) * inv
                if diagonal:
                    dz = jnp.where(before, dz, 0.0)
                dzs.append(dz.astype(BF16))
            dqs = [jnp.dot(dzs[hh], k_ref[keys, hs], preferred_element_type=F32) for hh, hs in heads]
            dks = [lax.dot_general(dzs[hh], q_ref[:, hs], TN_DIMS, preferred_element_type=F32) for hh, hs in heads]
            dvs = [lax.dot_general(avs[hh].astype(BF16), dy_ref[:, hs], TN_DIMS, preferred_element_type=F32)
                   for hh, hs in heads]
            for hh, hs in heads:
                dqa[hh] += dqs[hh]
                dka[keys, hs] += dks[hh]
                dva[keys, hs] += dvs[hh]
                car[hh] += jnp.sum(loms[hh], axis=1, keepdims=True)
                car2[hh] += jnp.sum(dls[hh], axis=1, keepdims=True)

        def step(j, carry):
            block(j, False)
            return carry

        lax.fori_loop(0, i, step, 0)
        block(i, True)
        for hh, hs in enumerate(_head_slices()):
            dq_ref[:, hs] = dqa[hh].astype(BF16)

        @pl.when(i == nq - 1)
        def _():
            dk_ref[...] = dka[...].astype(BF16)
            dv_ref[...] = dva[...].astype(BF16)

        @pl.when(last)
        def _():
            for cp in _all_to_all_copies(p_ref, r_ref, send_sems, recv_sems, local_sem):
                cp.wait()

    q_spec, k_spec, v_spec = _att_specs(s, e, tq, 0)
    w = hp * HEAD_DIM
    blk_q = pl.BlockSpec((None, tq, w), lambda i, h, j: (i, j, h))
    blk_kv = pl.BlockSpec((None, s, w), lambda i, h, j: (i, 0, h))
    shp = jax.ShapeDtypeStruct((b, s, e), BF16)
    return pl.pallas_call(
        body, name="sb_bwd", grid=grid,
        in_specs=[q_spec, k_spec, v_spec, blk_q, blk_q, ANY_SPEC],
        out_specs=(blk_q, blk_kv, blk_kv, ANY_SPEC),
        out_shape=(shp, shp, shp, jax.ShapeDtypeStruct(partials.shape, partials.dtype)),
        scratch_shapes=[pltpu.VMEM((hp, tq, HEAD_DIM), F32), pltpu.VMEM((s, w), F32), pltpu.VMEM((s, w), F32),
                        pltpu.VMEM((hp, tq, 1), F32), pltpu.VMEM((hp, tq, 1), F32),
                        pltpu.SemaphoreType.DMA((N_DEV - 1,)), pltpu.SemaphoreType.DMA((N_DEV - 1,)),
                        pltpu.SemaphoreType.DMA],
        compiler_params=_params("arbitrary", "arbitrary", "arbitrary"),
    )(proj3, proj3, proj3, lom_total, dy, partials)


def _dil_near_tiles(tq):
    return (DIL_PAIRS[1][0] + tq - 1) // tq + 1


def _dil_fill_bias(bias_ref, sl_ref, tq):
    row, col = _iota2(tq)
    for hh in range(HEADS_PER_STEP):
        slope = sl_ref[hh, 0:1, 0:1]
        for d in range(_dil_near_tiles(tq) + 1):
            dist = d * tq + row - col
            cnt = jnp.zeros(dist.shape, jnp.int32)
            for window, dilation in DIL_PAIRS:
                cnt = cnt + (((dist & (dilation - 1)) == 0) & (dist <= window)).astype(jnp.int32)
            bias = jnp.where(cnt == 3, math.log(3.0), jnp.where(cnt == 2, math.log(2.0), 0.0))
            bias_ref[hh, d] = jnp.where((dist >= 0) & (cnt > 0), bias - slope * dist.astype(F32), NEG)


def _dil_scores(q, k, bias_ref, hh, slope, it, tq, inv):
    near = _dil_near_tiles(tq)
    beyond = jnp.maximum(it - near, 0).astype(F32) * float(tq)
    sc = lax.dot_general(q, k, NT_DIMS, preferred_element_type=F32) * inv
    return (sc + bias_ref[hh, jnp.minimum(it, near)]) - slope * beyond


def _dil_fwd(proj3, e, slopes):
    b, s, _ = proj3.shape
    n_heads = e // HEAD_DIM
    hp = HEADS_PER_STEP
    tq = _tile(s, ATT_TQ)
    nq = s // tq
    inv = 1.0 / math.sqrt(HEAD_DIM)
    assert s <= DIL_PAIRS[2][0]

    def body(q_ref, k_ref, v_ref, sl_ref, y_ref, lse_ref, acc_ref, m_ref, l_ref, bias_ref):
        i = pl.program_id(2)

        @pl.when(i == 0)
        def _():
            _dil_fill_bias(bias_ref, sl_ref, tq)

        acc_ref[...] = jnp.zeros_like(acc_ref)
        m_ref[...] = jnp.full_like(m_ref, NEG)
        l_ref[...] = jnp.zeros_like(l_ref)

        def step(it, carry):
            keys = pl.ds(pl.multiple_of((i - it) * tq, tq), tq)
            heads = list(enumerate(_head_slices()))
            scs = [_dil_scores(q_ref[:, hs], k_ref[keys, hs], bias_ref, hh, sl_ref[hh, 0:1, 0:1], it, tq, inv)
                   for hh, hs in heads]
            m_old = [m_ref[hh] for hh, _ in heads]
            m_new = [jnp.maximum(m_old[hh], jnp.max(scs[hh], axis=1, keepdims=True)) for hh, _ in heads]
            ps = [jnp.exp(scs[hh] - m_new[hh]) for hh, _ in heads]
            pvs = [jnp.dot(ps[hh].astype(BF16), v_ref[keys, hs], preferred_element_type=F32) for hh, hs in heads]
            for hh, _ in heads:
                alpha = jnp.exp(m_old[hh] - m_new[hh])
                l_ref[hh] = alpha * l_ref[hh] + (ps[hh][:, :tq // 2] + ps[hh][:, tq // 2:])
                acc_ref[hh] = alpha * acc_ref[hh] + pvs[hh]
                m_ref[hh] = m_new[hh]
            return carry

        lax.fori_loop(0, i + 1, step, 0)
        for hh, hs in enumerate(_head_slices()):
            l = jnp.sum(l_ref[hh], axis=1, keepdims=True)
            y_ref[:, hs] = acc_ref[hh] / l
            lse_ref[:, hs] = jnp.broadcast_to(m_ref[hh] + jnp.log(l), (tq, HEAD_DIM))

    q_spec, k_spec, v_spec = _att_specs(s, e, tq, 4 * n_heads)
    blk_q = pl.BlockSpec((None, tq, hp * HEAD_DIM), lambda i, h, j: (i, j, h))
    shp = jax.ShapeDtypeStruct((b, s, e), F32)
    return pl.pallas_call(
        body, name="dil_fwd", grid=(b, n_heads // hp, nq),
        in_specs=[q_spec, k_spec, v_spec, pl.BlockSpec((hp, 8, HEAD_DIM), lambda i, h, j: (h, 0, 0))],
        out_specs=(blk_q, blk_q), out_shape=(shp, shp),
        scratch_shapes=[pltpu.VMEM((hp, tq, HEAD_DIM), F32), pltpu.VMEM((hp, tq, 1), F32), pltpu.VMEM((hp, tq, tq // 2), F32),
                        pltpu.VMEM((hp, _dil_near_tiles(tq) + 1, tq, tq), F32)],
        compiler_params=_params("parallel", "parallel", "arbitrary"),
    )(proj3, proj3, proj3, slopes)


def _dil_bwd(proj3, y, lse, dy, slopes):
    b, s, e = y.shape
    n_heads = e // HEAD_DIM
    hp = HEADS_PER_STEP
    tq = _tile(s, ATT_TQ)
    nq = s // tq
    inv = 1.0 / math.sqrt(HEAD_DIM)

    def body(q_ref, k_ref, v_ref, sl_ref, y_ref, lse_ref, dy_ref, dq_ref, dk_ref, dv_ref, dqa, dka, dva, bias_ref):
        i = pl.program_id(2)

        @pl.when(i == 0)
        def _():
            dka[...] = jnp.zeros_like(dka)
            dva[...] = jnp.zeros_like(dva)
            _dil_fill_bias(bias_ref, sl_ref, tq)

        delta = [jnp.sum(dy_ref[:, hs].astype(F32) * y_ref[:, hs], axis=1, keepdims=True) for hs in _head_slices()]
        dqa[...] = jnp.zeros_like(dqa)

        def step(it, carry):
            keys = pl.ds(pl.multiple_of((i - it) * tq, tq), tq)
            heads = list(enumerate(_head_slices()))
            scs = [_dil_scores(q_ref[:, hs], k_ref[keys, hs], bias_ref, hh, sl_ref[hh, 0:1, 0:1], it, tq, inv)
                   for hh, hs in heads]
            dps = [lax.dot_general(dy_ref[:, hs], v_ref[keys, hs], NT_DIMS, preferred_element_type=F32) for _, hs in heads]
            ps = [jnp.exp(scs[hh] - lse_ref[:, hh * HEAD_DIM:hh * HEAD_DIM + 1]) for hh, _ in heads]
            dss = [((ps[hh] * (dps[hh] - delta[hh])) * inv).astype(BF16) for hh, _ in heads]
            dqs = [jnp.dot(dss[hh], k_ref[keys, hs], preferred_element_type=F32) for hh, hs in heads]
            dks = [lax.dot_general(dss[hh], q_ref[:, hs], TN_DIMS, preferred_element_type=F32) for hh, hs in heads]
            dvs = [lax.dot_general(ps[hh].astype(BF16), dy_ref[:, hs], TN_DIMS, preferred_element_type=F32)
                   for hh, hs in heads]
            for hh, hs in heads:
                dqa[hh] += dqs[hh]
                dka[keys, hs] += dks[hh]
                dva[keys, hs] += dvs[hh]
            return carry

        lax.fori_loop(0, i + 1, step, 0)
        for hh, hs in enumerate(_head_slices()):
            dq_ref[:, hs] = dqa[hh].astype(BF16)

        @pl.when(i == nq - 1)
        def _():
            dk_ref[...] = dka[...].astype(BF16)
            dv_ref[...] = dva[...].astype(BF16)

    q_spec, k_spec, v_spec = _att_specs(s, e, tq, 4 * n_heads)
    w = hp * HEAD_DIM
    blk_q = pl.BlockSpec((None, tq, w), lambda i, h, j: (i, j, h))
    blk_kv = pl.BlockSpec((None, s, w), lambda i, h, j: (i, 0, h))
    shp = jax.ShapeDtypeStruct((b, s, e), BF16)
    return pl.pallas_call(
        body, name="dil_bwd", grid=(b, n_heads // hp, nq),
        in_specs=[q_spec, k_spec, v_spec, pl.BlockSpec((hp, 8, HEAD_DIM), lambda i, h, j: (h, 0, 0)),
                  blk_q, blk_q, blk_q],
        out_specs=(blk_q, blk_kv, blk_kv), out_shape=(shp, shp, shp),
        scratch_shapes=[pltpu.VMEM((hp, tq, HEAD_DIM), F32), pltpu.VMEM((s, w), F32), pltpu.VMEM((s, w), F32),
                        pltpu.VMEM((hp, _dil_near_tiles(tq) + 1, tq, tq), F32)],
        compiler_params=_params("parallel", "parallel", "arbitrary"),
    )(proj3, proj3, proj3, slopes, y, lse, dy)


def kernel(x, c, w_ada, b_ada, g_norm, w_in, g_sb, g_dil, w_out, g_final, loss_target, m_w_ada, m_b_ada, m_g_norm, m_w_in, m_g_sb, m_g_dil, m_w_out, m_g_final, v_w_ada, v_b_ada, v_g_norm, v_w_in, v_g_sb, v_g_dil, v_w_out, v_g_final):
    b, s, d = x.shape
    t = b * s
    e = w_in.shape[2]
    n_heads = e // HEAD_DIM
    na = w_ada.shape[2]
    r_out = w_out.shape[1]
    assert g_sb.shape[1] == e and g_dil.shape[1] == e and N_DEV * r_out == 2 * e and N_DEV * na == 3 * d
    assert b <= SMALL_ROWS and 3 * b + 3 <= 2 * SMALL_ROWS
    ix, iy, ic = _mesh_pos()
    me = 4 * ix + 2 * iy + ic

    c_all = _allgather_rows(jnp.pad(c, ((0, SMALL_ROWS - b), (0, 0))), "ag_c")
    b_own = lax.dynamic_slice(b_ada, (0, me * na), (1, na))
    mod_cols = _ada_fwd(c_all, w_ada[0], b_own)
    mod_all = _allgather_rows(mod_cols, "ag_mod").reshape(N_DEV, N_DEV, SMALL_ROWS, na)
    mod_own = lax.dynamic_slice(mod_all, (0, me, 0, 0), (N_DEV, 1, b, na))[:, 0]
    mod = mod_own.transpose(1, 0, 2).reshape(b, 1, 3 * d)
    shift, scale, gate = mod[:, :, :d], mod[:, :, d:2 * d], mod[:, :, 2 * d:]

    w_in3, w_out3 = _allgather_weights(w_in[0].astype(BF16), w_out[0].astype(BF16))
    w_out1 = w_out3.reshape(1, N_DEV * r_out, d)

    h = _norm_mod(x, g_norm, scale, shift).reshape(t, d)
    proj = _mm_nn(h, w_in3, BF16, "mm_proj")
    proj3 = proj.reshape(b, s, N_DEV * e)
    slopes = jnp.exp2(-ALIBI_MAX_BIAS * jnp.arange(1, n_heads + 1, dtype=F32) / n_heads)
    slopes = jnp.broadcast_to(slopes[:, None, None], (n_heads, 8, HEAD_DIM))
    y_sb, lom_total = _sb_fwd(proj3, e)
    y_dl, lse = _dil_fwd(proj3, e, slopes)
    yg = _gate_fwd(y_sb.reshape(t, e), y_dl.reshape(t, e), proj, g_sb, g_dil)
    out = _mm_nn(yg, w_out1, F32, "mm_out").reshape(b, s, d)
    loss_p, dx2, d_out, dgate, gg_final = _final_fwd_bwd(x, out, gate, g_final.reshape(1, d), loss_target)

    d_out2 = d_out.reshape(t, d)
    dyg = _mm_nt(d_out2, w_out1, BF16, "mm_dy")
    gw_out_p = _mm_tn(yg, d_out2, 1, BF16, "mm_gw_out").reshape(N_DEV, r_out, d)
    dy_sb, dy_dl, dz_sb, dz_dl, gg_sb, gg_dl = _gate_bwd(dyg, y_sb.reshape(t, e), y_dl.reshape(t, e), proj, g_sb, g_dil)
    dq_sb, dk_sb, dv_sb, recv_out = _sb_bwd(proj3, lom_total, dy_sb.reshape(b, s, e), gw_out_p)
    dq_dl, dk_dl, dv_dl = _dil_bwd(proj3, y_dl, lse, dy_dl.reshape(b, s, e), slopes)
    dproj = jnp.concatenate(
        [a.reshape(t, e) for a in (dq_sb, dk_sb, dv_sb, dz_sb, dq_dl, dk_dl, dv_dl, dz_dl)], axis=1)
    gw_in_p = _mm_tn(h, dproj, N_DEV, BF16, "mm_gw_in")
    gw_in_sib = _swap_with_sibling(gw_in_p)
    chip_blocks = jnp.stack([4 * px + 2 * py + ic for px, py in _other_chips()]).astype(jnp.int32)
    gw_in_send = _chip_presum(chip_blocks, gw_in_p, gw_in_sib)
    dh, gw_in_recv = _mm_nt_with_chip_exchange(dproj, w_in3, F32, gw_in_send, "mm_dh")
    dh = dh.reshape(b, s, d)
    grad_x, dshift, dscale, gg_norm = _norm_bwd(x, dh, dx2, scale, g_norm)

    dmod = jnp.concatenate([dshift, dscale, dgate], axis=1).reshape(3 * b, d)
    pkg = jnp.concatenate([dmod, gg_norm, gg_final, jnp.concatenate([gg_sb, gg_dl], axis=1),
                           jnp.zeros((2 * SMALL_ROWS - 3 * b - 3, d), F32)], axis=0)
    pkg_all = _allgather_rows(pkg, "ag_small_grads").reshape(N_DEV, 2 * SMALL_ROWS, d)
    dmod_all = pkg_all[:, :3 * b].reshape(N_DEV * b, 3 * d)
    dmod_cols = lax.dynamic_slice(dmod_all, (0, me * na), (N_DEV * b, na))
    c_rows = c_all.reshape(N_DEV, SMALL_ROWS, d)[:, :b].reshape(N_DEV * b, d)
    g_w_ada, d_w_ada, nm_w_ada, nv_w_ada = _ada_bwd_adam(c_rows, dmod_cols, w_ada[0], m_w_ada[0], v_w_ada[0])

    def pack(b_ada_like, g_norm_like, g_sb_like, g_dil_like, g_final_like):
        return jnp.concatenate([b_ada_like.reshape(3, d), g_norm_like.reshape(1, d), g_final_like.reshape(1, d),
                                jnp.concatenate([g_sb_like, g_dil_like], axis=1).reshape(1, d),
                                jnp.zeros((2, d), F32)], axis=0)

    small = _small_adam(pkg_all, 3 * b, pack(b_ada, g_norm, g_sb, g_dil, g_final),
                        pack(m_b_ada, m_g_norm, m_g_sb, m_g_dil, m_g_final),
                        pack(v_b_ada, v_g_norm, v_g_sb, v_g_dil, v_g_final))

    def unpack(p):
        return (p[0:3].reshape(1, 3 * d), p[3:4], p[5:6, :e], p[5:6, e:], p[4])

    sm_g, sm_d, sm_m, sm_v = (unpack(p) for p in small)

    g_w_in, d_w_in, nm_w_in, nv_w_in = _adam_from_chip_sums(
        jnp.reshape(me, (1,)).astype(jnp.int32), gw_in_p, gw_in_sib, gw_in_recv, w_in[0], m_w_in[0], v_w_in[0], "adam_w_in")
    g_w_out, d_w_out, nm_w_out, nv_w_out = _adam_from_partials(recv_out, w_out[0], m_w_out[0], v_w_out[0], "adam_w_out")

    loss = lax.psum(loss_p[0, 0], ("x", "y", "c"))

    def weights(ada, small_parts, w_in_part, w_out_part):
        b_ada_p, g_norm_p, g_sb_p, g_dil_p, g_final_p = small_parts
        return (ada[None], b_ada_p, g_norm_p, w_in_part[None], g_sb_p, g_dil_p, w_out_part[None], g_final_p)

    return (loss, grad_x,
            *weights(g_w_ada, sm_g, g_w_in, g_w_out),
            *weights(d_w_ada, sm_d, d_w_in, d_w_out),
            *weights(nm_w_ada, sm_m, nm_w_in, nm_w_out),
            *weights(nv_w_ada, sm_v, nv_w_in, nv_w_out))
```

```python
import functools
import math

import jax
import jax.numpy as jnp
from jax import lax
from jax.experimental import pallas as pl
from jax.experimental.pallas import tpu as pltpu

F32 = jnp.float32
BF16 = jnp.bfloat16
MESH = pl.DeviceIdType.MESH

N_DEV = 8
HEAD_DIM = 128
EPS = 1e-6
ALIBI_MAX_BIAS = 8.0
DIL_PAIRS = ((128, 1), (512, 4), (2048, 16))
DIL_STEPS = 128
NEG = -1e30

ADAM_LR = 0.001
ADAM_B1 = 0.9
ADAM_B2 = 0.999
ADAM_EPS = 1e-08
ADAM_WD = 0.01
ADAM_STEP = 10

VMEM_LIMIT_BYTES = 56 * 1024 * 1024
SMALL_ROWS = 8

NT_DIMS = (((1,), (1,)), ((), ()))
TN_DIMS = (((0,), (0,)), ((), ()))


def _params(*semantics):
    return pltpu.CompilerParams(dimension_semantics=semantics, vmem_limit_bytes=VMEM_LIMIT_BYTES)


def _tile(n, want):
    t = min(n, want)
    assert n % t == 0, (n, want)
    return t


def _mesh_pos():
    return lax.axis_index("x"), lax.axis_index("y"), lax.axis_index("c")


def _allgather_rows(x_shard, name):
    m_per, n = x_shard.shape

    def body(x_ref, out_ref, send_sems, recv_sems, local_sem):
        x, y, c = _mesh_pos()
        me, sibling = (x, y, c), (x, y, 1 - c)
        chips = [(1 - x, y), (x, 1 - y), (1 - x, 1 - y)]

        def rows(px, py, pc):
            return out_ref.at[pl.ds((4 * px + 2 * py + pc) * m_per, m_per), :]

        def copy(k, block, to, src=None):
            return pltpu.make_async_remote_copy(
                src_ref=rows(*block) if src is None else src, dst_ref=rows(*block),
                send_sem=send_sems.at[k], recv_sem=recv_sems.at[k], device_id=to, device_id_type=MESH)

        mine = pltpu.make_async_copy(x_ref, rows(*me), local_sem)
        mine.start()
        first = [copy(0, me, sibling, src=x_ref)]
        first += [copy(1 + j, me, (*chip, c), src=x_ref) for j, chip in enumerate(chips)]
        for cp in first:
            cp.start()
        passed = [copy(4 + j, (*chip, c), sibling) for j, chip in enumerate(chips)]
        for j, chip in enumerate(chips):
            copy(1 + j, (*chip, c), me).wait_recv()
            passed[j].start()
        copy(0, sibling, me).wait_recv()
        for j, chip in enumerate(chips):
            copy(4 + j, (*chip, 1 - c), me).wait_recv()
        for cp in first + passed:
            cp.wait_send()
        mine.wait()

    return pl.pallas_call(
        body, name=name,
        out_shape=jax.ShapeDtypeStruct((N_DEV * m_per, n), x_shard.dtype),
        in_specs=[pl.BlockSpec(memory_space=pltpu.VMEM)],
        out_specs=pl.BlockSpec(memory_space=pltpu.VMEM),
        scratch_shapes=[pltpu.SemaphoreType.DMA((7,)), pltpu.SemaphoreType.DMA((7,)), pltpu.SemaphoreType.DMA],
    )(x_shard)


def _allgather_block(w, name):
    def body(w_ref, out_ref, send_sems, recv_sems, local_sem):
        x, y, c = _mesh_pos()
        me, sibling = (x, y, c), (x, y, 1 - c)
        chips = [(1 - x, y), (x, 1 - y), (1 - x, 1 - y)]

        def copy(k, block, to, src=None):
            dst = out_ref.at[4 * block[0] + 2 * block[1] + block[2]]
            return pltpu.make_async_remote_copy(
                src_ref=dst if src is None else src, dst_ref=dst,
                send_sem=send_sems.at[k], recv_sem=recv_sems.at[k], device_id=to, device_id_type=MESH)

        mine = pltpu.make_async_copy(w_ref, out_ref.at[4 * x + 2 * y + c], local_sem)
        mine.start()
        first = [copy(0, me, sibling, src=w_ref)]
        first += [copy(1 + j, me, (*chip, c), src=w_ref) for j, chip in enumerate(chips)]
        for cp in first:
            cp.start()
        passed = [copy(4 + j, (*chip, c), sibling) for j, chip in enumerate(chips)]
        for j, chip in enumerate(chips):
            copy(1 + j, (*chip, c), me).wait_recv()
            passed[j].start()
        copy(0, sibling, me).wait_recv()
        for j, chip in enumerate(chips):
            copy(4 + j, (*chip, 1 - c), me).wait_recv()
        for cp in first + passed:
            cp.wait_send()
        mine.wait()

    return pl.pallas_call(
        body, name=name,
        out_shape=jax.ShapeDtypeStruct((N_DEV,) + w.shape, w.dtype),
        in_specs=[ANY_SPEC], out_specs=ANY_SPEC,
        scratch_shapes=[pltpu.SemaphoreType.DMA((7,)), pltpu.SemaphoreType.DMA((7,)), pltpu.SemaphoreType.DMA],
    )(w)


PROJ_HALVES = 2


def _unit_schedule():
    sched = [("own", None, hf) for hf in range(PROJ_HALVES)] + [("sib", None, hf) for hf in range(PROJ_HALVES)]
    for hf in range(PROJ_HALVES):
        sched += [("direct", 0, hf), ("direct", 1, hf), ("fwd", 0, hf), ("fwd", 1, hf)]
    for hf in range(PROJ_HALVES):
        sched += [("direct", 2, hf), ("fwd", 2, hf)]
    return sched


def _unit_ids():
    x, y, c = _mesh_pos()
    chips = _other_chips()
    ids = []
    for kind, j, hf in _unit_schedule():
        px, py = (x, y) if j is None else chips[j]
        pc = c if kind in ("own", "direct") else 1 - c
        ids.append(PROJ_HALVES * (4 * px + 2 * py + pc) + hf)
    return ids


def _proj_with_allgather(h, w_own, order):
    t, d = h.shape
    nh, _, u = w_own.shape
    assert nh == PROJ_HALVES
    sched = _unit_schedule()
    n_units = len(sched)
    pos = {entry: p for p, entry in enumerate(sched)}
    tm = _tile(t, MM_TM)
    m_tiles = t // tm
    prep_m = max(m_tiles - 2, 0)
    grid = (n_units, m_tiles)

    def body(order_ref, h_ref, wown_ref, proj_ref, w3_ref, bbuf, bsems, send_sems, recv_sems, local_sems):
        n, m = pl.program_id(0), pl.program_id(1)
        x, y, c = _mesh_pos()
        chips = _other_chips()
        sibling = (x, y, 1 - c)

        def unit(p):
            return w3_ref.at[order_ref[p]]

        def arrival(p):
            return pltpu.make_async_remote_copy(
                src_ref=unit(p), dst_ref=unit(p), send_sem=send_sems.at[0], recv_sem=recv_sems.at[p - nh],
                device_id=sibling, device_id_type=MESH)

        def send(k, src, p_here, p_there, to):
            return pltpu.make_async_remote_copy(
                src_ref=src, dst_ref=unit(p_here), send_sem=send_sems.at[k], recv_sem=recv_sems.at[p_there - nh],
                device_id=to, device_id_type=MESH)

        sends, forwards = [], {}
        for hf in range(nh):
            sends.append(send(len(sends), wown_ref.at[hf], pos[("own", None, hf)], pos[("sib", None, hf)], sibling))
        for hf in range(nh):
            for j in range(3):
                sends.append(send(len(sends), wown_ref.at[hf], pos[("own", None, hf)], pos[("direct", j, hf)],
                                  (*chips[j], c)))
        for hf in range(nh):
            for j in range(3):
                p = pos[("direct", j, hf)]
                forwards[p] = send(len(sends) + len(forwards), unit(p), p, pos[("fwd", j, hf)], sibling)
        locals_ = [pltpu.make_async_copy(wown_ref.at[hf], unit(pos[("own", None, hf)]), local_sems.at[hf])
                   for hf in range(nh)]

        def fetch(p):
            src = wown_ref.at[sched[p][2]] if sched[p][0] == "own" else unit(p)
            return pltpu.make_async_copy(src, bbuf.at[p % 2], bsems.at[p % 2])

        @pl.when((n == 0) & (m == 0))
        def _():
            for cp in locals_ + sends:
                cp.start()
            fetch(0).start()

        for p in range(n_units):
            @pl.when((n == p) & (m == 0))
            def _(p=p):
                fetch(p).wait()

            if p + 1 < n_units:
                @pl.when((n == p) & (m == prep_m))
                def _(p=p):
                    if sched[p + 1][0] != "own":
                        arrival(p + 1).wait_recv()
                    if p + 1 in forwards:
                        forwards[p + 1].start()
                    fetch(p + 1).start()

        proj_ref[...] = jnp.dot(h_ref[...], bbuf[n % 2], preferred_element_type=F32).astype(proj_ref.dtype)

        @pl.when((n == n_units - 1) & (m == m_tiles - 1))
        def _():
            for cp in sends + list(forwards.values()):
                cp.wait_send()
            for cp in locals_:
                cp.wait()

    n_out = len(sched) - nh
    return pl.pallas_call(
        body, name="mm_proj_allgather",
        grid_spec=pltpu.PrefetchScalarGridSpec(
            num_scalar_prefetch=1, grid=grid,
            in_specs=[pl.BlockSpec((tm, d), lambda n, m, order: (m, 0)), ANY_SPEC],
            out_specs=(pl.BlockSpec((tm, u), lambda n, m, order: (m, order[n])), ANY_SPEC),
            scratch_shapes=[pltpu.VMEM((2, d, u), h.dtype), pltpu.SemaphoreType.DMA((2,)),
                            pltpu.SemaphoreType.DMA((n_out,)), pltpu.SemaphoreType.DMA((n_out,)),
                            pltpu.SemaphoreType.DMA((nh,))]),
        out_shape=(jax.ShapeDtypeStruct((t, n_units * u), h.dtype), jax.ShapeDtypeStruct((n_units, d, u), h.dtype)),
        compiler_params=_params("arbitrary", "arbitrary"),
    )(order, h, w_own)


ANY_SPEC = pl.BlockSpec(memory_space=pl.ANY)


def _grid_first_last(grid):
    ids = [pl.program_id(a) for a in range(len(grid))]
    first = functools.reduce(lambda p, q: p & q, [i == 0 for i in ids])
    last = functools.reduce(lambda p, q: p & q, [i == n - 1 for i, n in zip(ids, grid)])
    return first, last


def _all_to_all_copies(src, dst, send_sems, recv_sems, local_sem):
    x, y, c = _mesh_pos()
    my = 4 * x + 2 * y + c
    copies = [pltpu.make_async_copy(src.at[my], dst.at[my], local_sem)]
    for d in range(1, N_DEV):
        px = 1 - x if d & 4 else x
        py = 1 - y if d & 2 else y
        pc = 1 - c if d & 1 else c
        copies.append(pltpu.make_async_remote_copy(
            src_ref=src.at[4 * px + 2 * py + pc], dst_ref=dst.at[my],
            send_sem=send_sems.at[d - 1], recv_sem=recv_sems.at[d - 1],
            device_id=(px, py, pc), device_id_type=MESH))
    return copies


def _other_chips():
    x, y, _ = _mesh_pos()
    return [(1 - x, y), (x, 1 - y), (1 - x, 1 - y)]


def _same_core_copies(src, dst, send_sems, recv_sems):
    c = lax.axis_index("c")
    return [pltpu.make_async_remote_copy(
        src_ref=src.at[j], dst_ref=dst.at[j], send_sem=send_sems.at[j], recv_sem=recv_sems.at[j],
        device_id=(*chip, c), device_id_type=MESH) for j, chip in enumerate(_other_chips())]


def _swap_with_sibling(partials):
    _, r, cdim = partials.shape

    def body(p_ref, o_ref, send_sems, recv_sems):
        x, y, c = _mesh_pos()
        dests = [(x, y)] + _other_chips()
        copies = [pltpu.make_async_remote_copy(
            src_ref=p_ref.at[4 * px + 2 * py + (1 - c)], dst_ref=o_ref.at[j],
            send_sem=send_sems.at[j], recv_sem=recv_sems.at[j],
            device_id=(x, y, 1 - c), device_id_type=MESH) for j, (px, py) in enumerate(dests)]
        for cp in copies:
            cp.start()
        for cp in copies:
            cp.wait()

    return pl.pallas_call(
        body, name="swap_with_sibling",
        out_shape=jax.ShapeDtypeStruct((4, r, cdim), partials.dtype),
        in_specs=[ANY_SPEC], out_specs=ANY_SPEC,
        scratch_shapes=[pltpu.SemaphoreType.DMA((4,)), pltpu.SemaphoreType.DMA((4,))],
    )(partials)


def _chip_presum(blocks, partials, from_sibling):
    _, r, cdim = partials.shape
    tr = _tile(r, 256)

    def body(blocks_ref, p_ref, s_ref, o_ref):
        o_ref[...] = (p_ref[...].astype(F32) + s_ref[...].astype(F32)).astype(o_ref.dtype)

    return pl.pallas_call(
        body, name="chip_presum",
        grid_spec=pltpu.PrefetchScalarGridSpec(
            num_scalar_prefetch=1, grid=(3, r // tr),
            in_specs=[pl.BlockSpec((None, tr, cdim), lambda j, i, blk: (blk[j], i, 0)),
                      pl.BlockSpec((None, tr, cdim), lambda j, i, blk: (1 + j, i, 0))],
            out_specs=pl.BlockSpec((None, tr, cdim), lambda j, i, blk: (j, i, 0))),
        out_shape=jax.ShapeDtypeStruct((3, r, cdim), partials.dtype),
        compiler_params=_params("parallel", "parallel"),
    )(blocks, partials, from_sibling)


def _mm_call(a, b, dims, nk, grid, a_spec, b_spec, o_spec, out_shape, acc_shape, name):
    def body(a_ref, b_ref, o_ref, acc_ref):
        k = pl.program_id(2)

        @pl.when(k == 0)
        def _():
            acc_ref[...] = jnp.zeros_like(acc_ref)

        acc_ref[...] += lax.dot_general(a_ref[...], b_ref[...], dims, preferred_element_type=F32)

        @pl.when(k == nk - 1)
        def _():
            o_ref[...] = acc_ref[...].astype(o_ref.dtype)

    return pl.pallas_call(
        body, name=name, grid=grid, in_specs=[a_spec, b_spec], out_specs=o_spec, out_shape=out_shape,
        scratch_shapes=[pltpu.VMEM(acc_shape, F32)],
        compiler_params=_params("parallel", "parallel", "arbitrary"),
    )(a, b)


MM_TM, MM_TN, MM_TK = 1024, 2048, 1024


def _mm_nn(a, b3, out_dtype, name):
    m, kk = a.shape
    g, _, nb = b3.shape
    tm, tn, tk = _tile(m, MM_TM), _tile(nb, MM_TN), _tile(kk, MM_TK)
    npb = nb // tn
    return _mm_call(
        a, b3, (((1,), (0,)), ((), ())), kk // tk, (m // tm, g * npb, kk // tk),
        pl.BlockSpec((tm, tk), lambda i, j, k: (i, k)),
        pl.BlockSpec((None, tk, tn), lambda i, j, k: (j // npb, k, j % npb)),
        pl.BlockSpec((tm, tn), lambda i, j, k: (i, j)),
        jax.ShapeDtypeStruct((m, g * nb), out_dtype), (tm, tn), name)


def _mm_nt(a, b3, out_dtype, name):
    m, kk = a.shape
    g, n, kb = b3.shape
    tm, tn, tk = _tile(m, MM_TM), _tile(n, MM_TN), _tile(kb, MM_TK)
    kpb = kb // tk
    return _mm_call(
        a, b3, NT_DIMS, kk // tk, (m // tm, n // tn, kk // tk),
        pl.BlockSpec((tm, tk), lambda i, j, k: (i, k)),
        pl.BlockSpec((None, tn, tk), lambda i, j, k: (k // kpb, j, k % kpb)),
        pl.BlockSpec((tm, tn), lambda i, j, k: (i, j)),
        jax.ShapeDtypeStruct((m, n), out_dtype), (tm, tn), name)


def _mm_nt_with_chip_exchange(a, b3, out_dtype, send3, name):
    m, kk = a.shape
    g, n, kb = b3.shape
    tm, tn, tk = _tile(m, MM_TM), _tile(n, MM_TN), _tile(kb, MM_TK)
    kpb = kb // tk
    grid = (m // tm, n // tn, kk // tk)

    def body(a_ref, b_ref, s_ref, o_ref, r_ref, acc_ref, send_sems, recv_sems):
        first, last = _grid_first_last(grid)
        k = pl.program_id(2)

        @pl.when(first)
        def _():
            for cp in _same_core_copies(s_ref, r_ref, send_sems, recv_sems):
                cp.start()

        @pl.when(k == 0)
        def _():
            acc_ref[...] = jnp.zeros_like(acc_ref)

        acc_ref[...] += lax.dot_general(a_ref[...], b_ref[...], NT_DIMS, preferred_element_type=F32)

        @pl.when(k == grid[2] - 1)
        def _():
            o_ref[...] = acc_ref[...].astype(o_ref.dtype)

        @pl.when(last)
        def _():
            for cp in _same_core_copies(s_ref, r_ref, send_sems, recv_sems):
                cp.wait()

    return pl.pallas_call(
        body, name=name, grid=grid,
        in_specs=[pl.BlockSpec((tm, tk), lambda i, j, k: (i, k)),
                  pl.BlockSpec((None, tn, tk), lambda i, j, k: (k // kpb, j, k % kpb)), ANY_SPEC],
        out_specs=(pl.BlockSpec((tm, tn), lambda i, j, k: (i, j)), ANY_SPEC),
        out_shape=(jax.ShapeDtypeStruct((m, n), out_dtype), jax.ShapeDtypeStruct(send3.shape, send3.dtype)),
        scratch_shapes=[pltpu.VMEM((tm, tn), F32), pltpu.SemaphoreType.DMA((3,)), pltpu.SemaphoreType.DMA((3,))],
        compiler_params=_params("arbitrary", "arbitrary", "arbitrary"),
    )(a, b3, send3)


def _mm_tn(a, b, g, out_dtype, name):
    t, m = a.shape
    nb = b.shape[1] // g
    tm, tn, tk = _tile(m, MM_TM), _tile(nb, MM_TN), _tile(t, MM_TK)
    npb = nb // tn
    return _mm_call(
        a, b, TN_DIMS, t // tk, (m // tm, g * npb, t // tk),
        pl.BlockSpec((tk, tm), lambda i, j, k: (k, i)),
        pl.BlockSpec((tk, tn), lambda i, j, k: (k, j)),
        pl.BlockSpec((None, tm, tn), lambda i, j, k: (j // npb, i, j % npb)),
        jax.ShapeDtypeStruct((g, m, nb), out_dtype), (tm, tn), name)


def _silu(z):
    return z * jax.nn.sigmoid(z)


def _ada_fwd(c_all, w_shard, b_own):
    r, d = c_all.shape
    na = w_shard.shape[1]
    tk = _tile(d, 512)
    nk = d // tk

    def body(c_ref, w_ref, b_ref, o_ref):
        k = pl.program_id(0)

        @pl.when(k == 0)
        def _():
            o_ref[...] = jnp.zeros_like(o_ref) + b_ref[...]

        cs = _silu(c_ref[...]).astype(BF16)
        o_ref[...] += jnp.dot(cs, w_ref[...].astype(BF16), preferred_element_type=F32)

    return pl.pallas_call(
        body, name="ada_fwd", grid=(nk,),
        in_specs=[pl.BlockSpec((r, tk), lambda k: (0, k)), pl.BlockSpec((tk, na), lambda k: (k, 0)),
                  pl.BlockSpec((1, na), lambda k: (0, 0))],
        out_specs=pl.BlockSpec((r, na), lambda k: (0, 0)),
        out_shape=jax.ShapeDtypeStruct((r, na), F32),
        compiler_params=_params("arbitrary"),
    )(c_all, w_shard, b_own)


def _adam(w, g, m, v):
    nm = ADAM_B1 * m + (1.0 - ADAM_B1) * g
    nv = ADAM_B2 * v + (1.0 - ADAM_B2) * (g * g)
    m_hat = nm / (1.0 - ADAM_B1 ** ADAM_STEP)
    v_hat = nv / (1.0 - ADAM_B2 ** ADAM_STEP)
    delta = -ADAM_LR * (m_hat / (jnp.sqrt(v_hat) + ADAM_EPS) + ADAM_WD * w)
    return delta, nm, nv


def _ada_bwd_adam(c_rows, dmod_cols, w, m, v):
    bg, d = c_rows.shape
    na = w.shape[1]
    tr = _tile(d, 256)

    def body(c_ref, dm_ref, w_ref, m_ref, v_ref, g_ref, d_ref, nm_ref, nv_ref):
        cs = _silu(c_ref[...]).astype(BF16)
        g = lax.dot_general(cs, dm_ref[...].astype(BF16), TN_DIMS, preferred_element_type=F32)
        delta, nm, nv = _adam(w_ref[...], g, m_ref[...], v_ref[...])
        g_ref[...] = g
        d_ref[...] = delta
        nm_ref[...] = nm
        nv_ref[...] = nv

    blk = pl.BlockSpec((tr, na), lambda i: (i, 0))
    shp = jax.ShapeDtypeStruct((d, na), F32)
    return pl.pallas_call(
        body, name="ada_bwd_adam", grid=(d // tr,),
        in_specs=[pl.BlockSpec((bg, tr), lambda i: (0, i)), pl.BlockSpec((bg, na), lambda i: (0, 0)), blk, blk, blk],
        out_specs=(blk, blk, blk, blk), out_shape=(shp, shp, shp, shp),
        compiler_params=_params("parallel"),
    )(c_rows, dmod_cols, w, m, v)


def _small_adam(pkg_all, n_batch_rows, w, m, v):
    d = w.shape[1]

    def body(p_ref, w_ref, m_ref, v_ref, g_ref, d_ref, nm_ref, nv_ref):
        for part in range(3):
            acc = jnp.zeros((1, d), F32)
            for dev in range(N_DEV):
                for b in range(n_batch_rows // 3):
                    acc = acc + p_ref[dev, 3 * b + part:3 * b + part + 1, :]
            g_ref[part:part + 1, :] = acc
        for rrow in range(3):
            acc = jnp.zeros((1, d), F32)
            for dev in range(N_DEV):
                acc = acc + p_ref[dev, n_batch_rows + rrow:n_batch_rows + rrow + 1, :]
            g_ref[3 + rrow:4 + rrow, :] = acc
        g_ref[6:8, :] = jnp.zeros((2, d), F32)
        g = g_ref[...]
        delta, nm, nv = _adam(w_ref[...], g, m_ref[...], v_ref[...])
        d_ref[...] = delta
        nm_ref[...] = nm
        nv_ref[...] = nv

    vm = pl.BlockSpec(memory_space=pltpu.VMEM)
    shp = jax.ShapeDtypeStruct((SMALL_ROWS, d), F32)
    return pl.pallas_call(
        body, name="small_adam", in_specs=[vm, vm, vm, vm], out_specs=(vm, vm, vm, vm),
        out_shape=(shp, shp, shp, shp),
    )(pkg_all, w, m, v)


def _adam_from_chip_sums(own_slot, partials, from_sibling, from_chips, w, m, v, name):
    _, r, c = partials.shape
    tr = _tile(r, 128)

    def body(slot_ref, p_ref, s_ref, f_ref, w_ref, m_ref, v_ref, g_ref, d_ref, nm_ref, nv_ref):
        g = p_ref[...].astype(F32) + s_ref[...].astype(F32)
        for j in range(3):
            g = g + f_ref[j].astype(F32)
        delta, nm, nv = _adam(w_ref[...], g, m_ref[...], v_ref[...])
        g_ref[...] = g
        d_ref[...] = delta
        nm_ref[...] = nm
        nv_ref[...] = nv

    blk = pl.BlockSpec((tr, c), lambda i, slot: (i, 0))
    shp = jax.ShapeDtypeStruct((r, c), F32)
    return pl.pallas_call(
        body, name=name,
        grid_spec=pltpu.PrefetchScalarGridSpec(
            num_scalar_prefetch=1, grid=(r // tr,),
            in_specs=[pl.BlockSpec((None, tr, c), lambda i, slot: (slot[0], i, 0)),
                      pl.BlockSpec((None, tr, c), lambda i, slot: (0, i, 0)),
                      pl.BlockSpec((3, tr, c), lambda i, slot: (0, i, 0)), blk, blk, blk],
            out_specs=(blk, blk, blk, blk)),
        out_shape=(shp, shp, shp, shp),
        compiler_params=_params("parallel"),
    )(own_slot, partials, from_sibling, from_chips, w, m, v)


def _adam_from_partials(recv, w, m, v, name):
    _, r, c = recv.shape
    tr = _tile(r, 128)

    def body(p_ref, w_ref, m_ref, v_ref, g_ref, d_ref, nm_ref, nv_ref):
        g = p_ref[0].astype(F32)
        for dev in range(1, N_DEV):
            g = g + p_ref[dev].astype(F32)
        delta, nm, nv = _adam(w_ref[...], g, m_ref[...], v_ref[...])
        g_ref[...] = g
        d_ref[...] = delta
        nm_ref[...] = nm
        nv_ref[...] = nv

    blk = pl.BlockSpec((tr, c), lambda i: (i, 0))
    shp = jax.ShapeDtypeStruct((r, c), F32)
    return pl.pallas_call(
        body, name=name, grid=(r // tr,),
        in_specs=[pl.BlockSpec((N_DEV, tr, c), lambda i: (0, i, 0)), blk, blk, blk],
        out_specs=(blk, blk, blk, blk), out_shape=(shp, shp, shp, shp),
        compiler_params=_params("parallel"),
    )(recv, w, m, v)


def _norm_mod(x, g_norm, scale, shift):
    b, s, d = x.shape
    ts = _tile(s, 256)

    def body(x_ref, g_ref, sc_ref, sh_ref, h_ref):
        xv = x_ref[...]
        r = lax.rsqrt(jnp.mean(xv * xv, axis=-1, keepdims=True) + EPS)
        xn = (xv * r) * g_ref[...]
        h_ref[...] = (xn * (1.0 + sc_ref[...]) + sh_ref[...]).astype(BF16)

    tok = pl.BlockSpec((None, ts, d), lambda i, j: (i, j, 0))
    per_b = pl.BlockSpec((None, 1, d), lambda i, j: (i, 0, 0))
    return pl.pallas_call(
        body, name="norm_mod", grid=(b, s // ts),
        in_specs=[tok, pl.BlockSpec((1, d), lambda i, j: (0, 0)), per_b, per_b],
        out_specs=tok, out_shape=jax.ShapeDtypeStruct((b, s, d), BF16),
        compiler_params=_params("parallel", "parallel"),
    )(x, g_norm, scale, shift)


def _final_fwd_bwd(x, out, gate, g_final, target):
    b, s, d = x.shape
    ts = _tile(s, 256)

    def body(x_ref, o_ref, gt_ref, g_ref, t_ref, loss_ref, dx2_ref, dout_ref, dgate_ref, gg_ref):
        i, j = pl.program_id(0), pl.program_id(1)

        @pl.when((i == 0) & (j == 0))
        def _():
            loss_ref[...] = jnp.zeros_like(loss_ref)
            gg_ref[...] = jnp.zeros_like(gg_ref)

        @pl.when(j == 0)
        def _():
            dgate_ref[...] = jnp.zeros_like(dgate_ref)

        ov = o_ref[...]
        gt = gt_ref[...]
        x2 = x_ref[...] + gt * ov
        r = lax.rsqrt(jnp.mean(x2 * x2, axis=-1, keepdims=True) + EPS)
        xh = x2 * r
        err = xh * g_ref[...] - t_ref[...]
        loss_ref[...] += 0.5 * jnp.sum(jnp.mean(err * err, axis=-1, keepdims=True), axis=0, keepdims=True)
        dfin = err * (1.0 / d)
        gg_ref[...] += jnp.sum(dfin * xh, axis=0, keepdims=True)
        dxh = dfin * g_ref[...]
        dx2 = r * (dxh - xh * jnp.mean(dxh * xh, axis=-1, keepdims=True))
        dx2_ref[...] = dx2
        dout_ref[...] = (gt * dx2).astype(BF16)
        dgate_ref[...] += jnp.sum(dx2 * ov, axis=0, keepdims=True)

    tok = pl.BlockSpec((None, ts, d), lambda i, j: (i, j, 0))
    per_b = pl.BlockSpec((None, 1, d), lambda i, j: (i, 0, 0))
    vec = pl.BlockSpec((1, d), lambda i, j: (0, 0))
    return pl.pallas_call(
        body, name="final_fwd_bwd", grid=(b, s // ts),
        in_specs=[tok, tok, per_b, vec, tok],
        out_specs=(pl.BlockSpec((8, 128), lambda i, j: (0, 0)), tok, tok, per_b, vec),
        out_shape=(jax.ShapeDtypeStruct((8, 128), F32), jax.ShapeDtypeStruct((b, s, d), F32),
                   jax.ShapeDtypeStruct((b, s, d), BF16), jax.ShapeDtypeStruct((b, 1, d), F32),
                   jax.ShapeDtypeStruct((1, d), F32)),
        compiler_params=_params("arbitrary", "arbitrary"),
    )(x, out, gate, g_final, target)


def _norm_bwd(x, dh, dx2, scale, g_norm):
    b, s, d = x.shape
    ts = _tile(s, 256)

    def body(x_ref, dh_ref, dx2_ref, sc_ref, g_ref, gx_ref, dsh_ref, dsc_ref, gg_ref):
        i, j = pl.program_id(0), pl.program_id(1)

        @pl.when((i == 0) & (j == 0))
        def _():
            gg_ref[...] = jnp.zeros_like(gg_ref)

        @pl.when(j == 0)
        def _():
            dsh_ref[...] = jnp.zeros_like(dsh_ref)
            dsc_ref[...] = jnp.zeros_like(dsc_ref)

        xv = x_ref[...]
        dhv = dh_ref[...]
        r = lax.rsqrt(jnp.mean(xv * xv, axis=-1, keepdims=True) + EPS)
        xh = xv * r
        xn = xh * g_ref[...]
        dsh_ref[...] += jnp.sum(dhv, axis=0, keepdims=True)
        dsc_ref[...] += jnp.sum(dhv * xn, axis=0, keepdims=True)
        dxn = dhv * (1.0 + sc_ref[...])
        gg_ref[...] += jnp.sum(dxn * xh, axis=0, keepdims=True)
        dxh = dxn * g_ref[...]
        gx_ref[...] = dx2_ref[...] + r * (dxh - xh * jnp.mean(dxh * xh, axis=-1, keepdims=True))

    tok = pl.BlockSpec((None, ts, d), lambda i, j: (i, j, 0))
    per_b = pl.BlockSpec((None, 1, d), lambda i, j: (i, 0, 0))
    vec = pl.BlockSpec((1, d), lambda i, j: (0, 0))
    return pl.pallas_call(
        body, name="norm_bwd", grid=(b, s // ts),
        in_specs=[tok, tok, tok, per_b, vec],
        out_specs=(tok, per_b, per_b, vec),
        out_shape=(jax.ShapeDtypeStruct((b, s, d), F32), jax.ShapeDtypeStruct((b, 1, d), F32),
                   jax.ShapeDtypeStruct((b, 1, d), F32), jax.ShapeDtypeStruct((1, d), F32)),
        compiler_params=_params("arbitrary", "arbitrary"),
    )(x, dh, dx2, scale, g_norm)


def _gate_fwd(y_sb, y_dl, proj, g_sb, g_dl):
    t, e = y_sb.shape
    n_heads = e // HEAD_DIM
    tt = _tile(t, 256)

    def body(ys_ref, yd_ref, zs_ref, zd_ref, gs_ref, gd_ref, o_ref):
        for grp, (y_ref, z_ref, g_ref) in enumerate(((ys_ref, zs_ref, gs_ref), (yd_ref, zd_ref, gd_ref))):
            for h in range(n_heads):
                sl = slice(h * HEAD_DIM, (h + 1) * HEAD_DIM)
                y = y_ref[:, sl]
                r = lax.rsqrt(jnp.mean(y * y, axis=-1, keepdims=True) + EPS)
                yn = (y * r) * g_ref[:, sl]
                z = z_ref[:, sl].astype(F32)
                o_ref[:, grp * e + h * HEAD_DIM:grp * e + (h + 1) * HEAD_DIM] = (yn * _silu(z)).astype(BF16)

    yblk = pl.BlockSpec((tt, e), lambda i: (i, 0))
    gblk = pl.BlockSpec((1, e), lambda i: (0, 0))
    return pl.pallas_call(
        body, name="gate_fwd", grid=(t // tt,),
        in_specs=[yblk, yblk, pl.BlockSpec((tt, e), lambda i: (i, 3)), pl.BlockSpec((tt, e), lambda i: (i, 7)),
                  gblk, gblk],
        out_specs=pl.BlockSpec((tt, 2 * e), lambda i: (i, 0)),
        out_shape=jax.ShapeDtypeStruct((t, 2 * e), BF16),
        compiler_params=_params("parallel"),
    )(y_sb, y_dl, proj, proj, g_sb, g_dl)


def _gate_bwd(dyg, y_sb, y_dl, proj, g_sb, g_dl):
    t, e = y_sb.shape
    n_heads = e // HEAD_DIM
    tt = _tile(t, 256)

    def body(dg_ref, ys_ref, yd_ref, zs_ref, zd_ref, gs_ref, gd_ref,
             dys_ref, dyd_ref, dzs_ref, dzd_ref, ggs_ref, ggd_ref):
        @pl.when(pl.program_id(0) == 0)
        def _():
            ggs_ref[...] = jnp.zeros_like(ggs_ref)
            ggd_ref[...] = jnp.zeros_like(ggd_ref)

        groups = ((ys_ref, zs_ref, gs_ref, dys_ref, dzs_ref, ggs_ref), (yd_ref, zd_ref, gd_ref, dyd_ref, dzd_ref, ggd_ref))
        for grp, (y_ref, z_ref, g_ref, dy_ref, dz_ref, gg_ref) in enumerate(groups):
            for h in range(n_heads):
                sl = slice(h * HEAD_DIM, (h + 1) * HEAD_DIM)
                dg = dg_ref[:, grp * e + h * HEAD_DIM:grp * e + (h + 1) * HEAD_DIM].astype(F32)
                y = y_ref[:, sl]
                z = z_ref[:, sl].astype(F32)
                g = g_ref[:, sl]
                r = lax.rsqrt(jnp.mean(y * y, axis=-1, keepdims=True) + EPS)
                yh = y * r
                sig = jax.nn.sigmoid(z)
                dyn = dg * (z * sig)
                dz_ref[:, sl] = (dg * (yh * g) * (sig * (1.0 + z * (1.0 - sig)))).astype(BF16)
                gg_ref[:, sl] += jnp.sum(dyn * yh, axis=0, keepdims=True)
                dyh = dyn * g
                dy_ref[:, sl] = (r * (dyh - yh * jnp.mean(dyh * yh, axis=-1, keepdims=True))).astype(BF16)

    yblk = pl.BlockSpec((tt, e), lambda i: (i, 0))
    gblk = pl.BlockSpec((1, e), lambda i: (0, 0))
    act = jax.ShapeDtypeStruct((t, e), BF16)
    vec = jax.ShapeDtypeStruct((1, e), F32)
    return pl.pallas_call(
        body, name="gate_bwd", grid=(t // tt,),
        in_specs=[pl.BlockSpec((tt, 2 * e), lambda i: (i, 0)), yblk, yblk,
                  pl.BlockSpec((tt, e), lambda i: (i, 3)), pl.BlockSpec((tt, e), lambda i: (i, 7)), gblk, gblk],
        out_specs=(yblk, yblk, yblk, yblk, gblk, gblk),
        out_shape=(act, act, act, act, vec, vec),
        compiler_params=_params("arbitrary"),
    )(dyg, y_sb, y_dl, proj, proj, g_sb, g_dl)


ATT_TQ = 256
HEADS_PER_STEP = 4
SOFTPLUS_CLAMP = 30.0


def _split2_dot(x, u):
    hi = x.astype(BF16)
    lo = (x - hi.astype(F32)).astype(BF16)
    n = x.shape[0]
    both = jnp.dot(jnp.concatenate([hi, lo], axis=0), u, preferred_element_type=F32)
    return both[:n] + both[n:]


def _iota2(n):
    return lax.broadcasted_iota(jnp.int32, (n, n), 0), lax.broadcasted_iota(jnp.int32, (n, n), 1)


def _head_slices():
    return [slice(hh * HEAD_DIM, (hh + 1) * HEAD_DIM) for hh in range(HEADS_PER_STEP)]


def _att_specs(s, e, tq, col0):
    n_heads = e // HEAD_DIM
    hp = HEADS_PER_STEP
    assert n_heads % hp == 0 and col0 % hp == 0
    w = hp * HEAD_DIM
    q_spec = pl.BlockSpec((None, tq, w), lambda i, h, j: (i, j, col0 // hp + h))
    k_spec = pl.BlockSpec((None, s, w), lambda i, h, j: (i, 0, (col0 + n_heads) // hp + h))
    v_spec = pl.BlockSpec((None, s, w), lambda i, h, j: (i, 0, (col0 + 2 * n_heads) // hp + h))
    return q_spec, k_spec, v_spec


def _sb_fwd(proj3, e):
    b, s, _ = proj3.shape
    n_heads = e // HEAD_DIM
    hp = HEADS_PER_STEP
    tq = _tile(s, ATT_TQ)
    nq = s // tq
    inv = 1.0 / math.sqrt(HEAD_DIM)

    def body(q_ref, k_ref, v_ref, y_ref, tot_ref, acc_ref, car_ref):
        i = pl.program_id(2)
        row, col = _iota2(tq)
        before = row > col
        u_after = before.astype(BF16)
        acc_ref[...] = jnp.zeros_like(acc_ref)
        car_ref[...] = jnp.zeros_like(car_ref)

        def block(j, diagonal):
            keys = pl.ds(pl.multiple_of(j * tq, tq), tq)
            heads = list(enumerate(_head_slices()))
            zs = [lax.dot_general(q_ref[:, hs], k_ref[keys, hs], NT_DIMS, preferred_element_type=F32) * inv
                  for _, hs in heads]
            sps = [jnp.maximum(jnp.log(1.0 + jnp.exp(jnp.minimum(z, SOFTPLUS_CLAMP))), z) for z in zs]
            loms = [jnp.where(before, -sp, 0.0) if diagonal else -sp for sp in sps]
            sufs = [_split2_dot(loms[hh], u_after) + car_ref[hh] for hh, _ in heads]
            avs = [jnp.exp((zs[hh] - sps[hh]) + sufs[hh]) for hh, _ in heads]
            if diagonal:
                avs = [jnp.where(before, a, 0.0) for a in avs]
            pvs = [jnp.dot(avs[hh].astype(BF16), v_ref[keys, hs], preferred_element_type=F32) for hh, hs in heads]
            for hh, _ in heads:
                acc_ref[hh] += pvs[hh]
                car_ref[hh] += jnp.sum(loms[hh], axis=1, keepdims=True)

        block(i, True)

        def step(it, carry):
            block(i - it, False)
            return carry

        lax.fori_loop(1, i + 1, step, 0)
        for hh, hs in enumerate(_head_slices()):
            y_ref[:, hs] = acc_ref[hh]
            tot_ref[:, hs] = jnp.broadcast_to(car_ref[hh], (tq, HEAD_DIM))

    q_spec, k_spec, v_spec = _att_specs(s, e, tq, 0)
    blk_q = pl.BlockSpec((None, tq, hp * HEAD_DIM), lambda i, h, j: (i, j, h))
    shp = jax.ShapeDtypeStruct((b, s, e), F32)
    return pl.pallas_call(
        body, name="sb_fwd", grid=(b, n_heads // hp, nq),
        in_specs=[q_spec, k_spec, v_spec],
        out_specs=(blk_q, blk_q), out_shape=(shp, shp),
        scratch_shapes=[pltpu.VMEM((hp, tq, HEAD_DIM), F32), pltpu.VMEM((hp, tq, 1), F32)],
        compiler_params=_params("parallel", "parallel", "arbitrary"),
    )(proj3, proj3, proj3)


def _sb_bwd(proj3, lom_total, dy, partials):
    b, s, e = dy.shape
    n_heads = e // HEAD_DIM
    hp = HEADS_PER_STEP
    tq = _tile(s, ATT_TQ)
    nq = s // tq
    inv = 1.0 / math.sqrt(HEAD_DIM)
    grid = (b, n_heads // hp, nq)

    def body(q_ref, k_ref, v_ref, tot_ref, dy_ref, p_ref, dq_ref, dk_ref, dv_ref, r_ref,
             dqa, dka, dva, car, car2, send_sems, recv_sems, local_sem):
        i = pl.program_id(2)
        first, last = _grid_first_last(grid)

        @pl.when(first)
        def _():
            for cp in _all_to_all_copies(p_ref, r_ref, send_sems, recv_sems, local_sem):
                cp.start()

        @pl.when(i == 0)
        def _():
            dka[...] = jnp.zeros_like(dka)
            dva[...] = jnp.zeros_like(dva)

        row, col = _iota2(tq)
        before = row > col
        u_upto = (row <= col).astype(BF16)
        u_before = (row < col).astype(BF16)
        dqa[...] = jnp.zeros_like(dqa)
        car[...] = jnp.zeros_like(car)
        car2[...] = jnp.zeros_like(car2)

        def block(j, diagonal):
            keys = pl.ds(pl.multiple_of(j * tq, tq), tq)
            heads = list(enumerate(_head_slices()))
            zs = [lax.dot_general(q_ref[:, hs], k_ref[keys, hs], NT_DIMS, preferred_element_type=F32) * inv
                  for _, hs in heads]
            das = [lax.dot_general(dy_ref[:, hs], v_ref[keys, hs], NT_DIMS, preferred_element_type=F32) for _, hs in heads]
            ezs = [jnp.exp(jnp.minimum(z, SOFTPLUS_CLAMP)) for z in zs]
            sps = [jnp.maximum(jnp.log(1.0 + ezs[hh]), zs[hh]) for hh, _ in heads]
            loms = [jnp.where(before, -sp, 0.0) if diagonal else -sp for sp in sps]
            sufs = [tot_ref[:, hh * HEAD_DIM:hh * HEAD_DIM + 1] - (_split2_dot(loms[hh], u_upto) + car[hh])
                    for hh, _ in heads]
            avs = [jnp.exp((zs[hh] - sps[hh]) + sufs[hh]) for hh, _ in heads]
            if diagonal:
                avs = [jnp.where(before, a, 0.0) for a in avs]
            dls = [avs[hh] * das[hh] for hh, _ in heads]
            prefixes = [_split2_dot(dls[hh], u_before) + car2[hh] for hh, _ in heads]
            dzs = []
            for hh, _ in heads:
                one_minus_beta = 1.0 / (1.0 + ezs[hh])
                dz = (dls[hh] * one_minus_beta - prefixes[hh] * (ezs[hh] * one_minus_beta)) * inv
                if diagonal:
                    dz = jnp.where(before, dz, 0.0)
                dzs.append(dz.astype(BF16))
            dqs = [jnp.dot(dzs[hh], k_ref[keys, hs], preferred_element_type=F32) for hh, hs in heads]
            dks = [lax.dot_general(dzs[hh], q_ref[:, hs], TN_DIMS, preferred_element_type=F32) for hh, hs in heads]
            dvs = [lax.dot_general(avs[hh].astype(BF16), dy_ref[:, hs], TN_DIMS, preferred_element_type=F32)
                   for hh, hs in heads]
            for hh, hs in heads:
                dqa[hh] += dqs[hh]
                dka[keys, hs] += dks[hh]
                dva[keys, hs] += dvs[hh]
                car[hh] += jnp.sum(loms[hh], axis=1, keepdims=True)
                car2[hh] += jnp.sum(dls[hh], axis=1, keepdims=True)

        def step(j, carry):
            block(j, False)
            return carry

        lax.fori_loop(0, i, step, 0)
        block(i, True)
        for hh, hs in enumerate(_head_slices()):
            dq_ref[:, hs] = dqa[hh].astype(BF16)

        @pl.when(i == nq - 1)
        def _():
            dk_ref[...] = dka[...].astype(BF16)
            dv_ref[...] = dva[...].astype(BF16)

        @pl.when(last)
        def _():
            for cp in _all_to_all_copies(p_ref, r_ref, send_sems, recv_sems, local_sem):
                cp.wait()

    q_spec, k_spec, v_spec = _att_specs(s, e, tq, 0)
    w = hp * HEAD_DIM
    blk_q = pl.BlockSpec((None, tq, w), lambda i, h, j: (i, j, h))
    blk_kv = pl.BlockSpec((None, s, w), lambda i, h, j: (i, 0, h))
    shp = jax.ShapeDtypeStruct((b, s, e), BF16)
    return pl.pallas_call(
        body, name="sb_bwd", grid=grid,
        in_specs=[q_spec, k_spec, v_spec, blk_q, blk_q, ANY_SPEC],
        out_specs=(blk_q, blk_kv, blk_kv, ANY_SPEC),
        out_shape=(shp, shp, shp, jax.ShapeDtypeStruct(partials.shape, partials.dtype)),
        scratch_shapes=[pltpu.VMEM((hp, tq, HEAD_DIM), F32), pltpu.VMEM((s, w), F32), pltpu.VMEM((s, w), F32),
                        pltpu.VMEM((hp, tq, 1), F32), pltpu.VMEM((hp, tq, 1), F32),
                        pltpu.SemaphoreType.DMA((N_DEV - 1,)), pltpu.SemaphoreType.DMA((N_DEV - 1,)),
                        pltpu.SemaphoreType.DMA],
        compiler_params=_params("arbitrary", "arbitrary", "arbitrary"),
    )(proj3, proj3, proj3, lom_total, dy, partials)


def _dil_near_tiles(tq):
    return (DIL_PAIRS[1][0] + tq - 1) // tq + 1


def _dil_fill_bias(bias_ref, sl_ref, tq):
    row, col = _iota2(tq)
    for hh in range(HEADS_PER_STEP):
        slope = sl_ref[hh, 0:1, 0:1]
        for d in range(_dil_near_tiles(tq) + 1):
            dist = d * tq + row - col
            cnt = jnp.zeros(dist.shape, jnp.int32)
            for window, dilation in DIL_PAIRS:
                cnt = cnt + (((dist & (dilation - 1)) == 0) & (dist <= window)).astype(jnp.int32)
            bias = jnp.where(cnt == 3, math.log(3.0), jnp.where(cnt == 2, math.log(2.0), 0.0))
            bias_ref[hh, d] = jnp.where((dist >= 0) & (cnt > 0), bias - slope * dist.astype(F32), NEG)


def _dil_scores(q, k, bias_ref, hh, slope, it, tq, inv):
    near = _dil_near_tiles(tq)
    beyond = jnp.maximum(it - near, 0).astype(F32) * float(tq)
    sc = lax.dot_general(q, k, NT_DIMS, preferred_element_type=F32) * inv
    return (sc + bias_ref[hh, jnp.minimum(it, near)]) - slope * beyond


def _dil_fwd(proj3, e, slopes):
    b, s, _ = proj3.shape
    n_heads = e // HEAD_DIM
    hp = HEADS_PER_STEP
    tq = _tile(s, ATT_TQ)
    nq = s // tq
    inv = 1.0 / math.sqrt(HEAD_DIM)
    assert s <= DIL_PAIRS[2][0]

    def body(q_ref, k_ref, v_ref, sl_ref, y_ref, lse_ref, acc_ref, m_ref, l_ref, bias_ref):
        i = pl.program_id(2)

        @pl.when(i == 0)
        def _():
            _dil_fill_bias(bias_ref, sl_ref, tq)

        acc_ref[...] = jnp.zeros_like(acc_ref)
        m_ref[...] = jnp.full_like(m_ref, NEG)
        l_ref[...] = jnp.zeros_like(l_ref)

        def step(it, carry):
            keys = pl.ds(pl.multiple_of((i - it) * tq, tq), tq)
            heads = list(enumerate(_head_slices()))
            scs = [_dil_scores(q_ref[:, hs], k_ref[keys, hs], bias_ref, hh, sl_ref[hh, 0:1, 0:1], it, tq, inv)
                   for hh, hs in heads]
            m_old = [m_ref[hh] for hh, _ in heads]
            m_new = [jnp.maximum(m_old[hh], jnp.max(scs[hh], axis=1, keepdims=True)) for hh, _ in heads]
            ps = [jnp.exp(scs[hh] - m_new[hh]) for hh, _ in heads]
            pvs = [jnp.dot(ps[hh].astype(BF16), v_ref[keys, hs], preferred_element_type=F32) for hh, hs in heads]
            for hh, _ in heads:
                alpha = jnp.exp(m_old[hh] - m_new[hh])
                l_ref[hh] = alpha * l_ref[hh] + (ps[hh][:, :tq // 2] + ps[hh][:, tq // 2:])
                acc_ref[hh] = alpha * acc_ref[hh] + pvs[hh]
                m_ref[hh] = m_new[hh]
            return carry

        lax.fori_loop(0, i + 1, step, 0)
        for hh, hs in enumerate(_head_slices()):
            l = jnp.sum(l_ref[hh], axis=1, keepdims=True)
            y_ref[:, hs] = acc_ref[hh] / l
            lse_ref[:, hs] = jnp.broadcast_to(m_ref[hh] + jnp.log(l), (tq, HEAD_DIM))

    q_spec, k_spec, v_spec = _att_specs(s, e, tq, 4 * n_heads)
    blk_q = pl.BlockSpec((None, tq, hp * HEAD_DIM), lambda i, h, j: (i, j, h))
    shp = jax.ShapeDtypeStruct((b, s, e), F32)
    return pl.pallas_call(
        body, name="dil_fwd", grid=(b, n_heads // hp, nq),
        in_specs=[q_spec, k_spec, v_spec, pl.BlockSpec((hp, 8, HEAD_DIM), lambda i, h, j: (h, 0, 0))],
        out_specs=(blk_q, blk_q), out_shape=(shp, shp),
        scratch_shapes=[pltpu.VMEM((hp, tq, HEAD_DIM), F32), pltpu.VMEM((hp, tq, 1), F32), pltpu.VMEM((hp, tq, tq // 2), F32),
                        pltpu.VMEM((hp, _dil_near_tiles(tq) + 1, tq, tq), F32)],
        compiler_params=_params("parallel", "parallel", "arbitrary"),
    )(proj3, proj3, proj3, slopes)


def _dil_bwd(proj3, y, lse, dy, slopes):
    b, s, e = y.shape
    n_heads = e // HEAD_DIM
    hp = HEADS_PER_STEP
    tq = _tile(s, ATT_TQ)
    nq = s // tq
    inv = 1.0 / math.sqrt(HEAD_DIM)

    def body(q_ref, k_ref, v_ref, sl_ref, y_ref, lse_ref, dy_ref, dq_ref, dk_ref, dv_ref, dqa, dka, dva, bias_ref):
        i = pl.program_id(2)

        @pl.when(i == 0)
        def _():
            dka[...] = jnp.zeros_like(dka)
            dva[...] = jnp.zeros_like(dva)
            _dil_fill_bias(bias_ref, sl_ref, tq)

        delta = [jnp.sum(dy_ref[:, hs].astype(F32) * y_ref[:, hs], axis=1, keepdims=True) for hs in _head_slices()]
        dqa[...] = jnp.zeros_like(dqa)

        def step(it, carry):
            keys = pl.ds(pl.multiple_of((i - it) * tq, tq), tq)
            heads = list(enumerate(_head_slices()))
            scs = [_dil_scores(q_ref[:, hs], k_ref[keys, hs], bias_ref, hh, sl_ref[hh, 0:1, 0:1], it, tq, inv)
                   for hh, hs in heads]
            dps = [lax.dot_general(dy_ref[:, hs], v_ref[keys, hs], NT_DIMS, preferred_element_type=F32) for _, hs in heads]
            ps = [jnp.exp(scs[hh] - lse_ref[:, hh * HEAD_DIM:hh * HEAD_DIM + 1]) for hh, _ in heads]
            dss = [((ps[hh] * (dps[hh] - delta[hh])) * inv).astype(BF16) for hh, _ in heads]
            dqs = [jnp.dot(dss[hh], k_ref[keys, hs], preferred_element_type=F32) for hh, hs in heads]
            dks = [lax.dot_general(dss[hh], q_ref[:, hs], TN_DIMS, preferred_element_type=F32) for hh, hs in heads]
            dvs = [lax.dot_general(ps[hh].astype(BF16), dy_ref[:, hs], TN_DIMS, preferred_element_type=F32)
                   for hh, hs in heads]
            for hh, hs in heads:
                dqa[hh] += dqs[hh]
                dka[keys, hs] += dks[hh]
                dva[keys, hs] += dvs[hh]
            return carry

        lax.fori_loop(0, i + 1, step, 0)
        for hh, hs in enumerate(_head_slices()):
            dq_ref[:, hs] = dqa[hh].astype(BF16)

        @pl.when(i == nq - 1)
        def _():
            dk_ref[...] = dka[...].astype(BF16)
            dv_ref[...] = dva[...].astype(BF16)

    q_spec, k_spec, v_spec = _att_specs(s, e, tq, 4 * n_heads)
    w = hp * HEAD_DIM
    blk_q = pl.BlockSpec((None, tq, w), lambda i, h, j: (i, j, h))
    blk_kv = pl.BlockSpec((None, s, w), lambda i, h, j: (i, 0, h))
    shp = jax.ShapeDtypeStruct((b, s, e), BF16)
    return pl.pallas_call(
        body, name="dil_bwd", grid=(b, n_heads // hp, nq),
        in_specs=[q_spec, k_spec, v_spec, pl.BlockSpec((hp, 8, HEAD_DIM), lambda i, h, j: (h, 0, 0)),
                  blk_q, blk_q, blk_q],
        out_specs=(blk_q, blk_kv, blk_kv), out_shape=(shp, shp, shp),
        scratch_shapes=[pltpu.VMEM((hp, tq, HEAD_DIM), F32), pltpu.VMEM((s, w), F32), pltpu.VMEM((s, w), F32),
                        pltpu.VMEM((hp, _dil_near_tiles(tq) + 1, tq, tq), F32)],
        compiler_params=_params("parallel", "parallel", "arbitrary"),
    )(proj3, proj3, proj3, slopes, y, lse, dy)


def kernel(x, c, w_ada, b_ada, g_norm, w_in, g_sb, g_dil, w_out, g_final, loss_target, m_w_ada, m_b_ada, m_g_norm, m_w_in, m_g_sb, m_g_dil, m_w_out, m_g_final, v_w_ada, v_b_ada, v_g_norm, v_w_in, v_g_sb, v_g_dil, v_w_out, v_g_final):
    b, s, d = x.shape
    t = b * s
    e = w_in.shape[2]
    n_heads = e // HEAD_DIM
    na = w_ada.shape[2]
    r_out = w_out.shape[1]
    assert g_sb.shape[1] == e and g_dil.shape[1] == e and N_DEV * r_out == 2 * e and N_DEV * na == 3 * d
    assert b <= SMALL_ROWS and 3 * b + 3 <= 2 * SMALL_ROWS
    ix, iy, ic = _mesh_pos()
    me = 4 * ix + 2 * iy + ic

    c_all = _allgather_rows(jnp.pad(c, ((0, SMALL_ROWS - b), (0, 0))), "ag_c")
    b_own = lax.dynamic_slice(b_ada, (0, me * na), (1, na))
    mod_cols = _ada_fwd(c_all, w_ada[0], b_own)
    mod_all = _allgather_rows(mod_cols, "ag_mod").reshape(N_DEV, N_DEV, SMALL_ROWS, na)
    mod_own = lax.dynamic_slice(mod_all, (0, me, 0, 0), (N_DEV, 1, b, na))[:, 0]
    mod = mod_own.transpose(1, 0, 2).reshape(b, 1, 3 * d)
    shift, scale, gate = mod[:, :, :d], mod[:, :, d:2 * d], mod[:, :, 2 * d:]

    w_out1 = _allgather_block(w_out[0].astype(BF16), "ag_w_out").reshape(1, N_DEV * r_out, d)
    w_own = w_in[0].astype(BF16).reshape(d, PROJ_HALVES, e // PROJ_HALVES).transpose(1, 0, 2)

    h = _norm_mod(x, g_norm, scale, shift).reshape(t, d)
    proj, w_in3 = _proj_with_allgather(h, w_own, jnp.stack(_unit_ids()).astype(jnp.int32))
    proj3 = proj.reshape(b, s, N_DEV * e)
    slopes = jnp.exp2(-ALIBI_MAX_BIAS * jnp.arange(1, n_heads + 1, dtype=F32) / n_heads)
    slopes = jnp.broadcast_to(slopes[:, None, None], (n_heads, 8, HEAD_DIM))
    y_sb, lom_total = _sb_fwd(proj3, e)
    y_dl, lse = _dil_fwd(proj3, e, slopes)
    yg = _gate_fwd(y_sb.reshape(t, e), y_dl.reshape(t, e), proj, g_sb, g_dil)
    out = _mm_nn(yg, w_out1, F32, "mm_out").reshape(b, s, d)
    loss_p, dx2, d_out, dgate, gg_final = _final_fwd_bwd(x, out, gate, g_final.reshape(1, d), loss_target)

    d_out2 = d_out.reshape(t, d)
    dyg = _mm_nt(d_out2, w_out1, BF16, "mm_dy")
    gw_out_p = _mm_tn(yg, d_out2, 1, BF16, "mm_gw_out").reshape(N_DEV, r_out, d)
    dy_sb, dy_dl, dz_sb, dz_dl, gg_sb, gg_dl = _gate_bwd(dyg, y_sb.reshape(t, e), y_dl.reshape(t, e), proj, g_sb, g_dil)
    dq_sb, dk_sb, dv_sb, recv_out = _sb_bwd(proj3, lom_total, dy_sb.reshape(b, s, e), gw_out_p)
    dq_dl, dk_dl, dv_dl = _dil_bwd(proj3, y_dl, lse, dy_dl.reshape(b, s, e), slopes)
    dproj = jnp.concatenate(
        [a.reshape(t, e) for a in (dq_sb, dk_sb, dv_sb, dz_sb, dq_dl, dk_dl, dv_dl, dz_dl)], axis=1)
    gw_in_p = _mm_tn(h, dproj, N_DEV, BF16, "mm_gw_in")
    gw_in_sib = _swap_with_sibling(gw_in_p)
    chip_blocks = jnp.stack([4 * px + 2 * py + ic for px, py in _other_chips()]).astype(jnp.int32)
    gw_in_send = _chip_presum(chip_blocks, gw_in_p, gw_in_sib)
    dh, gw_in_recv = _mm_nt_with_chip_exchange(dproj, w_in3, F32, gw_in_send, "mm_dh")
    dh = dh.reshape(b, s, d)
    grad_x, dshift, dscale, gg_norm = _norm_bwd(x, dh, dx2, scale, g_norm)

    dmod = jnp.concatenate([dshift, dscale, dgate], axis=1).reshape(3 * b, d)
    pkg = jnp.concatenate([dmod, gg_norm, gg_final, jnp.concatenate([gg_sb, gg_dl], axis=1),
                           jnp.zeros((2 * SMALL_ROWS - 3 * b - 3, d), F32)], axis=0)
    pkg_all = _allgather_rows(pkg, "ag_small_grads").reshape(N_DEV, 2 * SMALL_ROWS, d)
    dmod_all = pkg_all[:, :3 * b].reshape(N_DEV * b, 3 * d)
    dmod_cols = lax.dynamic_slice(dmod_all, (0, me * na), (N_DEV * b, na))
    c_rows = c_all.reshape(N_DEV, SMALL_ROWS, d)[:, :b].reshape(N_DEV * b, d)
    g_w_ada, d_w_ada, nm_w_ada, nv_w_ada = _ada_bwd_adam(c_rows, dmod_cols, w_ada[0], m_w_ada[0], v_w_ada[0])

    def pack(b_ada_like, g_norm_like, g_sb_like, g_dil_like, g_final_like):
        return jnp.concatenate([b_ada_like.reshape(3, d), g_norm_like.reshape(1, d), g_final_like.reshape(1, d),
                                jnp.concatenate([g_sb_like, g_dil_like], axis=1).reshape(1, d),
                                jnp.zeros((2, d), F32)], axis=0)

    small = _small_adam(pkg_all, 3 * b, pack(b_ada, g_norm, g_sb, g_dil, g_final),
                        pack(m_b_ada, m_g_norm, m_g_sb, m_g_dil, m_g_final),
                        pack(v_b_ada, v_g_norm, v_g_sb, v_g_dil, v_g_final))

    def unpack(p):
        return (p[0:3].reshape(1, 3 * d), p[3:4], p[5:6, :e], p[5:6, e:], p[4])

    sm_g, sm_d, sm_m, sm_v = (unpack(p) for p in small)

    g_w_in, d_w_in, nm_w_in, nv_w_in = _adam_from_chip_sums(
        jnp.reshape(me, (1,)).astype(jnp.int32), gw_in_p, gw_in_sib, gw_in_recv, w_in[0], m_w_in[0], v_w_in[0], "adam_w_in")
    g_w_out, d_w_out, nm_w_out, nv_w_out = _adam_from_partials(recv_out, w_out[0], m_w_out[0], v_w_out[0], "adam_w_out")

    loss = lax.psum(loss_p[0, 0], ("x", "y", "c"))

    def weights(ada, small_parts, w_in_part, w_out_part):
        b_ada_p, g_norm_p, g_sb_p, g_dil_p, g_final_p = small_parts
        return (ada[None], b_ada_p, g_norm_p, w_in_part[None], g_sb_p, g_dil_p, w_out_part[None], g_final_p)

    return (loss, grad_x,
            *weights(g_w_ada, sm_g, g_w_in, g_w_out),
            *weights(d_w_ada, sm_d, d_w_in, d_w_out),
            *weights(nm_w_ada, sm_m, nm_w_in, nm_w_out),
            *weights(nv_w_ada, sm_v, nv_w_in, nv_w_out))
```

```python
import functools
import math

import jax
import jax.numpy as jnp
from jax import lax
from jax.experimental import pallas as pl
from jax.experimental.pallas import tpu as pltpu

F32 = jnp.float32
BF16 = jnp.bfloat16
MESH = pl.DeviceIdType.MESH

N_DEV = 8
HEAD_DIM = 128
EPS = 1e-6
ALIBI_MAX_BIAS = 8.0
DIL_PAIRS = ((128, 1), (512, 4), (2048, 16))
DIL_STEPS = 128
NEG = -1e30

ADAM_LR = 0.001
ADAM_B1 = 0.9
ADAM_B2 = 0.999
ADAM_EPS = 1e-08
ADAM_WD = 0.01
ADAM_STEP = 10

VMEM_LIMIT_BYTES = 56 * 1024 * 1024
SMALL_ROWS = 8

NT_DIMS = (((1,), (1,)), ((), ()))
TN_DIMS = (((0,), (0,)), ((), ()))


def _params(*semantics):
    return pltpu.CompilerParams(dimension_semantics=semantics, vmem_limit_bytes=VMEM_LIMIT_BYTES)


def _tile(n, want):
    t = min(n, want)
    assert n % t == 0, (n, want)
    return t


def _mesh_pos():
    return lax.axis_index("x"), lax.axis_index("y"), lax.axis_index("c")


def _allgather_rows(x_shard, name):
    m_per, n = x_shard.shape

    def body(x_ref, out_ref, send_sems, recv_sems, local_sem):
        x, y, c = _mesh_pos()
        me, sibling = (x, y, c), (x, y, 1 - c)
        chips = [(1 - x, y), (x, 1 - y), (1 - x, 1 - y)]

        def rows(px, py, pc):
            return out_ref.at[pl.ds((4 * px + 2 * py + pc) * m_per, m_per), :]

        def copy(k, block, to, src=None):
            return pltpu.make_async_remote_copy(
                src_ref=rows(*block) if src is None else src, dst_ref=rows(*block),
                send_sem=send_sems.at[k], recv_sem=recv_sems.at[k], device_id=to, device_id_type=MESH)

        mine = pltpu.make_async_copy(x_ref, rows(*me), local_sem)
        mine.start()
        first = [copy(0, me, sibling, src=x_ref)]
        first += [copy(1 + j, me, (*chip, c), src=x_ref) for j, chip in enumerate(chips)]
        for cp in first:
            cp.start()
        passed = [copy(4 + j, (*chip, c), sibling) for j, chip in enumerate(chips)]
        for j, chip in enumerate(chips):
            copy(1 + j, (*chip, c), me).wait_recv()
            passed[j].start()
        copy(0, sibling, me).wait_recv()
        for j, chip in enumerate(chips):
            copy(4 + j, (*chip, 1 - c), me).wait_recv()
        for cp in first + passed:
            cp.wait_send()
        mine.wait()

    return pl.pallas_call(
        body, name=name,
        out_shape=jax.ShapeDtypeStruct((N_DEV * m_per, n), x_shard.dtype),
        in_specs=[pl.BlockSpec(memory_space=pltpu.VMEM)],
        out_specs=pl.BlockSpec(memory_space=pltpu.VMEM),
        scratch_shapes=[pltpu.SemaphoreType.DMA((7,)), pltpu.SemaphoreType.DMA((7,)), pltpu.SemaphoreType.DMA],
    )(x_shard)


def _allgather_block(w, name):
    def body(w_ref, out_ref, send_sems, recv_sems, local_sem):
        x, y, c = _mesh_pos()
        me, sibling = (x, y, c), (x, y, 1 - c)
        chips = [(1 - x, y), (x, 1 - y), (1 - x, 1 - y)]

        def copy(k, block, to, src=None):
            dst = out_ref.at[4 * block[0] + 2 * block[1] + block[2]]
            return pltpu.make_async_remote_copy(
                src_ref=dst if src is None else src, dst_ref=dst,
                send_sem=send_sems.at[k], recv_sem=recv_sems.at[k], device_id=to, device_id_type=MESH)

        mine = pltpu.make_async_copy(w_ref, out_ref.at[4 * x + 2 * y + c], local_sem)
        mine.start()
        first = [copy(0, me, sibling, src=w_ref)]
        first += [copy(1 + j, me, (*chip, c), src=w_ref) for j, chip in enumerate(chips)]
        for cp in first:
            cp.start()
        passed = [copy(4 + j, (*chip, c), sibling) for j, chip in enumerate(chips)]
        for j, chip in enumerate(chips):
            copy(1 + j, (*chip, c), me).wait_recv()
            passed[j].start()
        copy(0, sibling, me).wait_recv()
        for j, chip in enumerate(chips):
            copy(4 + j, (*chip, 1 - c), me).wait_recv()
        for cp in first + passed:
            cp.wait_send()
        mine.wait()

    return pl.pallas_call(
        body, name=name,
        out_shape=jax.ShapeDtypeStruct((N_DEV,) + w.shape, w.dtype),
        in_specs=[ANY_SPEC], out_specs=ANY_SPEC,
        scratch_shapes=[pltpu.SemaphoreType.DMA((7,)), pltpu.SemaphoreType.DMA((7,)), pltpu.SemaphoreType.DMA],
    )(w)


PROJ_HALVES = 2


def _unit_schedule():
    assert PROJ_HALVES == 2
    sched = [("own", None, hf) for hf in range(PROJ_HALVES)] + [("sib", None, hf) for hf in range(PROJ_HALVES)]
    for rnd in ([(0, 0), (1, 1)], [(2, 0), (2, 1)], [(0, 1), (1, 0)]):
        sched += [("direct", j, hf) for j, hf in rnd] + [("fwd", j, hf) for j, hf in rnd]
    return sched


def _unit_ids():
    x, y, c = _mesh_pos()
    chips = _other_chips()
    ids = []
    for kind, j, hf in _unit_schedule():
        px, py = (x, y) if j is None else chips[j]
        pc = c if kind in ("own", "direct") else 1 - c
        ids.append(PROJ_HALVES * (4 * px + 2 * py + pc) + hf)
    return ids


def _proj_with_allgather(h, w_own, order):
    t, d = h.shape
    nh, _, u = w_own.shape
    assert nh == PROJ_HALVES
    sched = _unit_schedule()
    n_units = len(sched)
    pos = {entry: p for p, entry in enumerate(sched)}
    tm = _tile(t, MM_TM)
    m_tiles = t // tm
    prep_m = max(m_tiles - 2, 0)
    grid = (n_units, m_tiles)

    def body(order_ref, h_ref, wown_ref, proj_ref, w3_ref, bbuf, bsems, send_sems, recv_sems, local_sems):
        n, m = pl.program_id(0), pl.program_id(1)
        x, y, c = _mesh_pos()
        chips = _other_chips()
        sibling = (x, y, 1 - c)

        def unit(p):
            return w3_ref.at[order_ref[p]]

        def arrival(p):
            return pltpu.make_async_remote_copy(
                src_ref=unit(p), dst_ref=unit(p), send_sem=send_sems.at[0], recv_sem=recv_sems.at[p - nh],
                device_id=sibling, device_id_type=MESH)

        def send(k, src, p_here, p_there, to):
            return pltpu.make_async_remote_copy(
                src_ref=src, dst_ref=unit(p_here), send_sem=send_sems.at[k], recv_sem=recv_sems.at[p_there - nh],
                device_id=to, device_id_type=MESH)

        copies = []

        def out(src, p_here, p_there, to):
            copies.append(send(len(copies), src, p_here, p_there, to))
            return copies[-1]

        def own(hf, p_there, to):
            return out(wown_ref.at[hf], pos[("own", None, hf)], p_there, to)

        x_nbr, y_nbr = (*chips[0], c), (*chips[1], c)
        first = [own(hf, pos[("sib", None, hf)], sibling) for hf in range(nh)]
        first += [own(0, pos[("direct", 0, 0)], x_nbr), own(1, pos[("direct", 1, 1)], y_nbr)]
        after = {}
        for kind, j, hf in sched:
            if kind == "direct":
                p = pos[(kind, j, hf)]
                after[p] = [out(unit(p), p, pos[("fwd", j, hf)], sibling)]
        p = pos[("direct", 0, 0)]
        after[p] += [out(unit(p), p, pos[("direct", 2, 0)], y_nbr), own(0, pos[("direct", 1, 0)], y_nbr)]
        p = pos[("direct", 1, 1)]
        after[p] += [out(unit(p), p, pos[("direct", 2, 1)], x_nbr), own(1, pos[("direct", 0, 1)], x_nbr)]
        locals_ = [pltpu.make_async_copy(wown_ref.at[hf], unit(pos[("own", None, hf)]), local_sems.at[hf])
                   for hf in range(nh)]

        def fetch(p):
            src = wown_ref.at[sched[p][2]] if sched[p][0] == "own" else unit(p)
            return pltpu.make_async_copy(src, bbuf.at[p % 2], bsems.at[p % 2])

        @pl.when((n == 0) & (m == 0))
        def _():
            for cp in locals_ + first:
                cp.start()
            fetch(0).start()

        for p in range(n_units):
            @pl.when((n == p) & (m == 0))
            def _(p=p):
                fetch(p).wait()

            if p + 1 < n_units:
                @pl.when((n == p) & (m == prep_m))
                def _(p=p):
                    if sched[p + 1][0] != "own":
                        arrival(p + 1).wait_recv()
                    for cp in after.get(p + 1, []):
                        cp.start()
                    fetch(p + 1).start()

        proj_ref[...] = jnp.dot(h_ref[...], bbuf[n % 2], preferred_element_type=F32).astype(proj_ref.dtype)

        @pl.when((n == n_units - 1) & (m == m_tiles - 1))
        def _():
            for cp in copies:
                cp.wait_send()
            for cp in locals_:
                cp.wait()

    n_out = len(sched) - nh
    return pl.pallas_call(
        body, name="mm_proj_allgather",
        grid_spec=pltpu.PrefetchScalarGridSpec(
            num_scalar_prefetch=1, grid=grid,
            in_specs=[pl.BlockSpec((tm, d), lambda n, m, order: (m, 0)), ANY_SPEC],
            out_specs=(pl.BlockSpec((tm, u), lambda n, m, order: (m, order[n])), ANY_SPEC),
            scratch_shapes=[pltpu.VMEM((2, d, u), h.dtype), pltpu.SemaphoreType.DMA((2,)),
                            pltpu.SemaphoreType.DMA((n_out,)), pltpu.SemaphoreType.DMA((n_out,)),
                            pltpu.SemaphoreType.DMA((nh,))]),
        out_shape=(jax.ShapeDtypeStruct((t, n_units * u), h.dtype), jax.ShapeDtypeStruct((n_units, d, u), h.dtype)),
        compiler_params=_params("arbitrary", "arbitrary"),
    )(order, h, w_own)


ANY_SPEC = pl.BlockSpec(memory_space=pl.ANY)


def _grid_first_last(grid):
    ids = [pl.program_id(a) for a in range(len(grid))]
    first = functools.reduce(lambda p, q: p & q, [i == 0 for i in ids])
    last = functools.reduce(lambda p, q: p & q, [i == n - 1 for i, n in zip(ids, grid)])
    return first, last


def _all_to_all_copies(src, dst, send_sems, recv_sems, local_sem):
    x, y, c = _mesh_pos()
    my = 4 * x + 2 * y + c
    copies = [pltpu.make_async_copy(src.at[my], dst.at[my], local_sem)]
    for d in range(1, N_DEV):
        px = 1 - x if d & 4 else x
        py = 1 - y if d & 2 else y
        pc = 1 - c if d & 1 else c
        copies.append(pltpu.make_async_remote_copy(
            src_ref=src.at[4 * px + 2 * py + pc], dst_ref=dst.at[my],
            send_sem=send_sems.at[d - 1], recv_sem=recv_sems.at[d - 1],
            device_id=(px, py, pc), device_id_type=MESH))
    return copies


def _other_chips():
    x, y, _ = _mesh_pos()
    return [(1 - x, y), (x, 1 - y), (1 - x, 1 - y)]


def _same_core_copies(src, dst, send_sems, recv_sems):
    c = lax.axis_index("c")
    return [pltpu.make_async_remote_copy(
        src_ref=src.at[j], dst_ref=dst.at[j], send_sem=send_sems.at[j], recv_sem=recv_sems.at[j],
        device_id=(*chip, c), device_id_type=MESH) for j, chip in enumerate(_other_chips())]


def _swap_with_sibling(partials):
    _, r, cdim = partials.shape

    def body(p_ref, o_ref, send_sems, recv_sems):
        x, y, c = _mesh_pos()
        dests = [(x, y)] + _other_chips()
        copies = [pltpu.make_async_remote_copy(
            src_ref=p_ref.at[4 * px + 2 * py + (1 - c)], dst_ref=o_ref.at[j],
            send_sem=send_sems.at[j], recv_sem=recv_sems.at[j],
            device_id=(x, y, 1 - c), device_id_type=MESH) for j, (px, py) in enumerate(dests)]
        for cp in copies:
            cp.start()
        for cp in copies:
            cp.wait()

    return pl.pallas_call(
        body, name="swap_with_sibling",
        out_shape=jax.ShapeDtypeStruct((4, r, cdim), partials.dtype),
        in_specs=[ANY_SPEC], out_specs=ANY_SPEC,
        scratch_shapes=[pltpu.SemaphoreType.DMA((4,)), pltpu.SemaphoreType.DMA((4,))],
    )(partials)


def _chip_presum(blocks, partials, from_sibling):
    _, r, cdim = partials.shape
    tr = _tile(r, 256)

    def body(blocks_ref, p_ref, s_ref, o_ref):
        o_ref[...] = (p_ref[...].astype(F32) + s_ref[...].astype(F32)).astype(o_ref.dtype)

    return pl.pallas_call(
        body, name="chip_presum",
        grid_spec=pltpu.PrefetchScalarGridSpec(
            num_scalar_prefetch=1, grid=(3, r // tr),
            in_specs=[pl.BlockSpec((None, tr, cdim), lambda j, i, blk: (blk[j], i, 0)),
                      pl.BlockSpec((None, tr, cdim), lambda j, i, blk: (1 + j, i, 0))],
            out_specs=pl.BlockSpec((None, tr, cdim), lambda j, i, blk: (j, i, 0))),
        out_shape=jax.ShapeDtypeStruct((3, r, cdim), partials.dtype),
        compiler_params=_params("parallel", "parallel"),
    )(blocks, partials, from_sibling)


def _mm_call(a, b, dims, nk, grid, a_spec, b_spec, o_spec, out_shape, acc_shape, name):
    def body(a_ref, b_ref, o_ref, acc_ref):
        k = pl.program_id(2)

        @pl.when(k == 0)
        def _():
            acc_ref[...] = jnp.zeros_like(acc_ref)

        acc_ref[...] += lax.dot_general(a_ref[...], b_ref[...], dims, preferred_element_type=F32)

        @pl.when(k == nk - 1)
        def _():
            o_ref[...] = acc_ref[...].astype(o_ref.dtype)

    return pl.pallas_call(
        body, name=name, grid=grid, in_specs=[a_spec, b_spec], out_specs=o_spec, out_shape=out_shape,
        scratch_shapes=[pltpu.VMEM(acc_shape, F32)],
        compiler_params=_params("parallel", "parallel", "arbitrary"),
    )(a, b)


MM_TM, MM_TN, MM_TK = 1024, 2048, 1024


def _mm_nn(a, b3, out_dtype, name):
    m, kk = a.shape
    g, _, nb = b3.shape
    tm, tn, tk = _tile(m, MM_TM), _tile(nb, MM_TN), _tile(kk, MM_TK)
    npb = nb // tn
    return _mm_call(
        a, b3, (((1,), (0,)), ((), ())), kk // tk, (m // tm, g * npb, kk // tk),
        pl.BlockSpec((tm, tk), lambda i, j, k: (i, k)),
        pl.BlockSpec((None, tk, tn), lambda i, j, k: (j // npb, k, j % npb)),
        pl.BlockSpec((tm, tn), lambda i, j, k: (i, j)),
        jax.ShapeDtypeStruct((m, g * nb), out_dtype), (tm, tn), name)


def _mm_nt(a, b3, out_dtype, name):
    m, kk = a.shape
    g, n, kb = b3.shape
    tm, tn, tk = _tile(m, MM_TM), _tile(n, MM_TN), _tile(kb, MM_TK)
    kpb = kb // tk
    return _mm_call(
        a, b3, NT_DIMS, kk // tk, (m // tm, n // tn, kk // tk),
        pl.BlockSpec((tm, tk), lambda i, j, k: (i, k)),
        pl.BlockSpec((None, tn, tk), lambda i, j, k: (k // kpb, j, k % kpb)),
        pl.BlockSpec((tm, tn), lambda i, j, k: (i, j)),
        jax.ShapeDtypeStruct((m, n), out_dtype), (tm, tn), name)


def _mm_nt_with_chip_exchange(a, b3, out_dtype, send3, name):
    m, kk = a.shape
    g, n, kb = b3.shape
    tm, tn, tk = _tile(m, MM_TM), _tile(n, MM_TN), _tile(kb, MM_TK)
    kpb = kb // tk
    grid = (m // tm, n // tn, kk // tk)

    def body(a_ref, b_ref, s_ref, o_ref, r_ref, acc_ref, send_sems, recv_sems):
        first, last = _grid_first_last(grid)
        k = pl.program_id(2)

        @pl.when(first)
        def _():
            for cp in _same_core_copies(s_ref, r_ref, send_sems, recv_sems):
                cp.start()

        @pl.when(k == 0)
        def _():
            acc_ref[...] = jnp.zeros_like(acc_ref)

        acc_ref[...] += lax.dot_general(a_ref[...], b_ref[...], NT_DIMS, preferred_element_type=F32)

        @pl.when(k == grid[2] - 1)
        def _():
            o_ref[...] = acc_ref[...].astype(o_ref.dtype)

        @pl.when(last)
        def _():
            for cp in _same_core_copies(s_ref, r_ref, send_sems, recv_sems):
                cp.wait()

    return pl.pallas_call(
        body, name=name, grid=grid,
        in_specs=[pl.BlockSpec((tm, tk), lambda i, j, k: (i, k)),
                  pl.BlockSpec((None, tn, tk), lambda i, j, k: (k // kpb, j, k % kpb)), ANY_SPEC],
        out_specs=(pl.BlockSpec((tm, tn), lambda i, j, k: (i, j)), ANY_SPEC),
        out_shape=(jax.ShapeDtypeStruct((m, n), out_dtype), jax.ShapeDtypeStruct(send3.shape, send3.dtype)),
        scratch_shapes=[pltpu.VMEM((tm, tn), F32), pltpu.SemaphoreType.DMA((3,)), pltpu.SemaphoreType.DMA((3,))],
        compiler_params=_params("arbitrary", "arbitrary", "arbitrary"),
    )(a, b3, send3)


def _mm_tn(a, b, g, out_dtype, name):
    t, m = a.shape
    nb = b.shape[1] // g
    tm, tn, tk = _tile(m, MM_TM), _tile(nb, MM_TN), _tile(t, MM_TK)
    npb = nb // tn
    return _mm_call(
        a, b, TN_DIMS, t // tk, (m // tm, g * npb, t // tk),
        pl.BlockSpec((tk, tm), lambda i, j, k: (k, i)),
        pl.BlockSpec((tk, tn), lambda i, j, k: (k, j)),
        pl.BlockSpec((None, tm, tn), lambda i, j, k: (j // npb, i, j % npb)),
        jax.ShapeDtypeStruct((g, m, nb), out_dtype), (tm, tn), name)


def _silu(z):
    return z * jax.nn.sigmoid(z)


def _ada_fwd(c_all, w_shard, b_own):
    r, d = c_all.shape
    na = w_shard.shape[1]
    tk = _tile(d, 512)
    nk = d // tk

    def body(c_ref, w_ref, b_ref, o_ref):
        k = pl.program_id(0)

        @pl.when(k == 0)
        def _():
            o_ref[...] = jnp.zeros_like(o_ref) + b_ref[...]

        cs = _silu(c_ref[...]).astype(BF16)
        o_ref[...] += jnp.dot(cs, w_ref[...].astype(BF16), preferred_element_type=F32)

    return pl.pallas_call(
        body, name="ada_fwd", grid=(nk,),
        in_specs=[pl.BlockSpec((r, tk), lambda k: (0, k)), pl.BlockSpec((tk, na), lambda k: (k, 0)),
                  pl.BlockSpec((1, na), lambda k: (0, 0))],
        out_specs=pl.BlockSpec((r, na), lambda k: (0, 0)),
        out_shape=jax.ShapeDtypeStruct((r, na), F32),
        compiler_params=_params("arbitrary"),
    )(c_all, w_shard, b_own)


def _adam(w, g, m, v):
    nm = ADAM_B1 * m + (1.0 - ADAM_B1) * g
    nv = ADAM_B2 * v + (1.0 - ADAM_B2) * (g * g)
    m_hat = nm / (1.0 - ADAM_B1 ** ADAM_STEP)
    v_hat = nv / (1.0 - ADAM_B2 ** ADAM_STEP)
    delta = -ADAM_LR * (m_hat / (jnp.sqrt(v_hat) + ADAM_EPS) + ADAM_WD * w)
    return delta, nm, nv


def _ada_bwd_adam(c_rows, dmod_cols, w, m, v):
    bg, d = c_rows.shape
    na = w.shape[1]
    tr = _tile(d, 256)

    def body(c_ref, dm_ref, w_ref, m_ref, v_ref, g_ref, d_ref, nm_ref, nv_ref):
        cs = _silu(c_ref[...]).astype(BF16)
        g = lax.dot_general(cs, dm_ref[...].astype(BF16), TN_DIMS, preferred_element_type=F32)
        delta, nm, nv = _adam(w_ref[...], g, m_ref[...], v_ref[...])
        g_ref[...] = g
        d_ref[...] = delta
        nm_ref[...] = nm
        nv_ref[...] = nv

    blk = pl.BlockSpec((tr, na), lambda i: (i, 0))
    shp = jax.ShapeDtypeStruct((d, na), F32)
    return pl.pallas_call(
        body, name="ada_bwd_adam", grid=(d // tr,),
        in_specs=[pl.BlockSpec((bg, tr), lambda i: (0, i)), pl.BlockSpec((bg, na), lambda i: (0, 0)), blk, blk, blk],
        out_specs=(blk, blk, blk, blk), out_shape=(shp, shp, shp, shp),
        compiler_params=_params("parallel"),
    )(c_rows, dmod_cols, w, m, v)


def _small_adam(pkg_all, n_batch_rows, w, m, v):
    d = w.shape[1]

    def body(p_ref, w_ref, m_ref, v_ref, g_ref, d_ref, nm_ref, nv_ref):
        for part in range(3):
            acc = jnp.zeros((1, d), F32)
            for dev in range(N_DEV):
                for b in range(n_batch_rows // 3):
                    acc = acc + p_ref[dev, 3 * b + part:3 * b + part + 1, :]
            g_ref[part:part + 1, :] = acc
        for rrow in range(3):
            acc = jnp.zeros((1, d), F32)
            for dev in range(N_DEV):
                acc = acc + p_ref[dev, n_batch_rows + rrow:n_batch_rows + rrow + 1, :]
            g_ref[3 + rrow:4 + rrow, :] = acc
        g_ref[6:8, :] = jnp.zeros((2, d), F32)
        g = g_ref[...]
        delta, nm, nv = _adam(w_ref[...], g, m_ref[...], v_ref[...])
        d_ref[...] = delta
        nm_ref[...] = nm
        nv_ref[...] = nv

    vm = pl.BlockSpec(memory_space=pltpu.VMEM)
    shp = jax.ShapeDtypeStruct((SMALL_ROWS, d), F32)
    return pl.pallas_call(
        body, name="small_adam", in_specs=[vm, vm, vm, vm], out_specs=(vm, vm, vm, vm),
        out_shape=(shp, shp, shp, shp),
    )(pkg_all, w, m, v)


def _adam_from_chip_sums(own_slot, partials, from_sibling, from_chips, w, m, v, name):
    _, r, c = partials.shape
    tr = _tile(r, 128)

    def body(slot_ref, p_ref, s_ref, f_ref, w_ref, m_ref, v_ref, g_ref, d_ref, nm_ref, nv_ref):
        g = p_ref[...].astype(F32) + s_ref[...].astype(F32)
        for j in range(3):
            g = g + f_ref[j].astype(F32)
        delta, nm, nv = _adam(w_ref[...], g, m_ref[...], v_ref[...])
        g_ref[...] = g
        d_ref[...] = delta
        nm_ref[...] = nm
        nv_ref[...] = nv

    blk = pl.BlockSpec((tr, c), lambda i, slot: (i, 0))
    shp = jax.ShapeDtypeStruct((r, c), F32)
    return pl.pallas_call(
        body, name=name,
        grid_spec=pltpu.PrefetchScalarGridSpec(
            num_scalar_prefetch=1, grid=(r // tr,),
            in_specs=[pl.BlockSpec((None, tr, c), lambda i, slot: (slot[0], i, 0)),
                      pl.BlockSpec((None, tr, c), lambda i, slot: (0, i, 0)),
                      pl.BlockSpec((3, tr, c), lambda i, slot: (0, i, 0)), blk, blk, blk],
            out_specs=(blk, blk, blk, blk)),
        out_shape=(shp, shp, shp, shp),
        compiler_params=_params("parallel"),
    )(own_slot, partials, from_sibling, from_chips, w, m, v)


def _adam_from_partials(recv, w, m, v, name):
    _, r, c = recv.shape
    tr = _tile(r, 128)

    def body(p_ref, w_ref, m_ref, v_ref, g_ref, d_ref, nm_ref, nv_ref):
        g = p_ref[0].astype(F32)
        for dev in range(1, N_DEV):
            g = g + p_ref[dev].astype(F32)
        delta, nm, nv = _adam(w_ref[...], g, m_ref[...], v_ref[...])
        g_ref[...] = g
        d_ref[...] = delta
        nm_ref[...] = nm
        nv_ref[...] = nv

    blk = pl.BlockSpec((tr, c), lambda i: (i, 0))
    shp = jax.ShapeDtypeStruct((r, c), F32)
    return pl.pallas_call(
        body, name=name, grid=(r // tr,),
        in_specs=[pl.BlockSpec((N_DEV, tr, c), lambda i: (0, i, 0)), blk, blk, blk],
        out_specs=(blk, blk, blk, blk), out_shape=(shp, shp, shp, shp),
        compiler_params=_params("parallel"),
    )(recv, w, m, v)


def _norm_mod(x, g_norm, scale, shift):
    b, s, d = x.shape
    ts = _tile(s, 256)

    def body(x_ref, g_ref, sc_ref, sh_ref, h_ref):
        xv = x_ref[...]
        r = lax.rsqrt(jnp.mean(xv * xv, axis=-1, keepdims=True) + EPS)
        xn = (xv * r) * g_ref[...]
        h_ref[...] = (xn * (1.0 + sc_ref[...]) + sh_ref[...]).astype(BF16)

    tok = pl.BlockSpec((None, ts, d), lambda i, j: (i, j, 0))
    per_b = pl.BlockSpec((None, 1, d), lambda i, j: (i, 0, 0))
    return pl.pallas_call(
        body, name="norm_mod", grid=(b, s // ts),
        in_specs=[tok, pl.BlockSpec((1, d), lambda i, j: (0, 0)), per_b, per_b],
        out_specs=tok, out_shape=jax.ShapeDtypeStruct((b, s, d), BF16),
        compiler_params=_params("parallel", "parallel"),
    )(x, g_norm, scale, shift)


def _final_fwd_bwd(x, out, gate, g_final, target):
    b, s, d = x.shape
    ts = _tile(s, 256)

    def body(x_ref, o_ref, gt_ref, g_ref, t_ref, loss_ref, dx2_ref, dout_ref, dgate_ref, gg_ref):
        i, j = pl.program_id(0), pl.program_id(1)

        @pl.when((i == 0) & (j == 0))
        def _():
            loss_ref[...] = jnp.zeros_like(loss_ref)
            gg_ref[...] = jnp.zeros_like(gg_ref)

        @pl.when(j == 0)
        def _():
            dgate_ref[...] = jnp.zeros_like(dgate_ref)

        ov = o_ref[...]
        gt = gt_ref[...]
        x2 = x_ref[...] + gt * ov
        r = lax.rsqrt(jnp.mean(x2 * x2, axis=-1, keepdims=True) + EPS)
        xh = x2 * r
        err = xh * g_ref[...] - t_ref[...]
        loss_ref[...] += 0.5 * jnp.sum(jnp.mean(err * err, axis=-1, keepdims=True), axis=0, keepdims=True)
        dfin = err * (1.0 / d)
        gg_ref[...] += jnp.sum(dfin * xh, axis=0, keepdims=True)
        dxh = dfin * g_ref[...]
        dx2 = r * (dxh - xh * jnp.mean(dxh * xh, axis=-1, keepdims=True))
        dx2_ref[...] = dx2
        dout_ref[...] = (gt * dx2).astype(BF16)
        dgate_ref[...] += jnp.sum(dx2 * ov, axis=0, keepdims=True)

    tok = pl.BlockSpec((None, ts, d), lambda i, j: (i, j, 0))
    per_b = pl.BlockSpec((None, 1, d), lambda i, j: (i, 0, 0))
    vec = pl.BlockSpec((1, d), lambda i, j: (0, 0))
    return pl.pallas_call(
        body, name="final_fwd_bwd", grid=(b, s // ts),
        in_specs=[tok, tok, per_b, vec, tok],
        out_specs=(pl.BlockSpec((8, 128), lambda i, j: (0, 0)), tok, tok, per_b, vec),
        out_shape=(jax.ShapeDtypeStruct((8, 128), F32), jax.ShapeDtypeStruct((b, s, d), F32),
                   jax.ShapeDtypeStruct((b, s, d), BF16), jax.ShapeDtypeStruct((b, 1, d), F32),
                   jax.ShapeDtypeStruct((1, d), F32)),
        compiler_params=_params("arbitrary", "arbitrary"),
    )(x, out, gate, g_final, target)


def _norm_bwd(x, dh, dx2, scale, g_norm):
    b, s, d = x.shape
    ts = _tile(s, 256)

    def body(x_ref, dh_ref, dx2_ref, sc_ref, g_ref, gx_ref, dsh_ref, dsc_ref, gg_ref):
        i, j = pl.program_id(0), pl.program_id(1)

        @pl.when((i == 0) & (j == 0))
        def _():
            gg_ref[...] = jnp.zeros_like(gg_ref)

        @pl.when(j == 0)
        def _():
            dsh_ref[...] = jnp.zeros_like(dsh_ref)
            dsc_ref[...] = jnp.zeros_like(dsc_ref)

        xv = x_ref[...]
        dhv = dh_ref[...]
        r = lax.rsqrt(jnp.mean(xv * xv, axis=-1, keepdims=True) + EPS)
        xh = xv * r
        xn = xh * g_ref[...]
        dsh_ref[...] += jnp.sum(dhv, axis=0, keepdims=True)
        dsc_ref[...] += jnp.sum(dhv * xn, axis=0, keepdims=True)
        dxn = dhv * (1.0 + sc_ref[...])
        gg_ref[...] += jnp.sum(dxn * xh, axis=0, keepdims=True)
        dxh = dxn * g_ref[...]
        gx_ref[...] = dx2_ref[...] + r * (dxh - xh * jnp.mean(dxh * xh, axis=-1, keepdims=True))

    tok = pl.BlockSpec((None, ts, d), lambda i, j: (i, j, 0))
    per_b = pl.BlockSpec((None, 1, d), lambda i, j: (i, 0, 0))
    vec = pl.BlockSpec((1, d), lambda i, j: (0, 0))
    return pl.pallas_call(
        body, name="norm_bwd", grid=(b, s // ts),
        in_specs=[tok, tok, tok, per_b, vec],
        out_specs=(tok, per_b, per_b, vec),
        out_shape=(jax.ShapeDtypeStruct((b, s, d), F32), jax.ShapeDtypeStruct((b, 1, d), F32),
                   jax.ShapeDtypeStruct((b, 1, d), F32), jax.ShapeDtypeStruct((1, d), F32)),
        compiler_params=_params("arbitrary", "arbitrary"),
    )(x, dh, dx2, scale, g_norm)


def _gate_fwd(y_sb, y_dl, proj, g_sb, g_dl):
    t, e = y_sb.shape
    n_heads = e // HEAD_DIM
    tt = _tile(t, 256)

    def body(ys_ref, yd_ref, zs_ref, zd_ref, gs_ref, gd_ref, o_ref):
        for grp, (y_ref, z_ref, g_ref) in enumerate(((ys_ref, zs_ref, gs_ref), (yd_ref, zd_ref, gd_ref))):
            for h in range(n_heads):
                sl = slice(h * HEAD_DIM, (h + 1) * HEAD_DIM)
                y = y_ref[:, sl]
                r = lax.rsqrt(jnp.mean(y * y, axis=-1, keepdims=True) + EPS)
                yn = (y * r) * g_ref[:, sl]
                z = z_ref[:, sl].astype(F32)
                o_ref[:, grp * e + h * HEAD_DIM:grp * e + (h + 1) * HEAD_DIM] = (yn * _silu(z)).astype(BF16)

    yblk = pl.BlockSpec((tt, e), lambda i: (i, 0))
    gblk = pl.BlockSpec((1, e), lambda i: (0, 0))
    return pl.pallas_call(
        body, name="gate_fwd", grid=(t // tt,),
        in_specs=[yblk, yblk, pl.BlockSpec((tt, e), lambda i: (i, 3)), pl.BlockSpec((tt, e), lambda i: (i, 7)),
                  gblk, gblk],
        out_specs=pl.BlockSpec((tt, 2 * e), lambda i: (i, 0)),
        out_shape=jax.ShapeDtypeStruct((t, 2 * e), BF16),
        compiler_params=_params("parallel"),
    )(y_sb, y_dl, proj, proj, g_sb, g_dl)


def _gate_bwd(dyg, y_sb, y_dl, proj, g_sb, g_dl):
    t, e = y_sb.shape
    n_heads = e // HEAD_DIM
    tt = _tile(t, 256)

    def body(dg_ref, ys_ref, yd_ref, zs_ref, zd_ref, gs_ref, gd_ref,
             dys_ref, dyd_ref, dzs_ref, dzd_ref, ggs_ref, ggd_ref):
        @pl.when(pl.program_id(0) == 0)
        def _():
            ggs_ref[...] = jnp.zeros_like(ggs_ref)
            ggd_ref[...] = jnp.zeros_like(ggd_ref)

        groups = ((ys_ref, zs_ref, gs_ref, dys_ref, dzs_ref, ggs_ref), (yd_ref, zd_ref, gd_ref, dyd_ref, dzd_ref, ggd_ref))
        for grp, (y_ref, z_ref, g_ref, dy_ref, dz_ref, gg_ref) in enumerate(groups):
            for h in range(n_heads):
                sl = slice(h * HEAD_DIM, (h + 1) * HEAD_DIM)
                dg = dg_ref[:, grp * e + h * HEAD_DIM:grp * e + (h + 1) * HEAD_DIM].astype(F32)
                y = y_ref[:, sl]
                z = z_ref[:, sl].astype(F32)
                g = g_ref[:, sl]
                r = lax.rsqrt(jnp.mean(y * y, axis=-1, keepdims=True) + EPS)
                yh = y * r
                sig = jax.nn.sigmoid(z)
                dyn = dg * (z * sig)
                dz_ref[:, sl] = (dg * (yh * g) * (sig * (1.0 + z * (1.0 - sig)))).astype(BF16)
                gg_ref[:, sl] += jnp.sum(dyn * yh, axis=0, keepdims=True)
                dyh = dyn * g
                dy_ref[:, sl] = (r * (dyh - yh * jnp.mean(dyh * yh, axis=-1, keepdims=True))).astype(BF16)

    yblk = pl.BlockSpec((tt, e), lambda i: (i, 0))
    gblk = pl.BlockSpec((1, e), lambda i: (0, 0))
    act = jax.ShapeDtypeStruct((t, e), BF16)
    vec = jax.ShapeDtypeStruct((1, e), F32)
    return pl.pallas_call(
        body, name="gate_bwd", grid=(t // tt,),
        in_specs=[pl.BlockSpec((tt, 2 * e), lambda i: (i, 0)), yblk, yblk,
                  pl.BlockSpec((tt, e), lambda i: (i, 3)), pl.BlockSpec((tt, e), lambda i: (i, 7)), gblk, gblk],
        out_specs=(yblk, yblk, yblk, yblk, gblk, gblk),
        out_shape=(act, act, act, act, vec, vec),
        compiler_params=_params("arbitrary"),
    )(dyg, y_sb, y_dl, proj, proj, g_sb, g_dl)


ATT_TQ = 256
HEADS_PER_STEP = 4
SOFTPLUS_CLAMP = 30.0


def _split2_dot(x, u):
    hi = x.astype(BF16)
    lo = (x - hi.astype(F32)).astype(BF16)
    n = x.shape[0]
    both = jnp.dot(jnp.concatenate([hi, lo], axis=0), u, preferred_element_type=F32)
    return both[:n] + both[n:]


def _iota2(n):
    return lax.broadcasted_iota(jnp.int32, (n, n), 0), lax.broadcasted_iota(jnp.int32, (n, n), 1)


def _head_slices():
    return [slice(hh * HEAD_DIM, (hh + 1) * HEAD_DIM) for hh in range(HEADS_PER_STEP)]


def _att_specs(s, e, tq, col0):
    n_heads = e // HEAD_DIM
    hp = HEADS_PER_STEP
    assert n_heads % hp == 0 and col0 % hp == 0
    w = hp * HEAD_DIM
    q_spec = pl.BlockSpec((None, tq, w), lambda i, h, j: (i, j, col0 // hp + h))
    k_spec = pl.BlockSpec((None, s, w), lambda i, h, j: (i, 0, (col0 + n_heads) // hp + h))
    v_spec = pl.BlockSpec((None, s, w), lambda i, h, j: (i, 0, (col0 + 2 * n_heads) // hp + h))
    return q_spec, k_spec, v_spec


def _sb_fwd(proj3, e):
    b, s, _ = proj3.shape
    n_heads = e // HEAD_DIM
    hp = HEADS_PER_STEP
    tq = _tile(s, ATT_TQ)
    nq = s // tq
    inv = 1.0 / math.sqrt(HEAD_DIM)

    def body(q_ref, k_ref, v_ref, y_ref, tot_ref, acc_ref, car_ref):
        i = pl.program_id(2)
        row, col = _iota2(tq)
        before = row > col
        u_after = before.astype(BF16)
        acc_ref[...] = jnp.zeros_like(acc_ref)
        car_ref[...] = jnp.zeros_like(car_ref)

        def block(j, diagonal):
            keys = pl.ds(pl.multiple_of(j * tq, tq), tq)
            heads = list(enumerate(_head_slices()))
            zs = [lax.dot_general(q_ref[:, hs], k_ref[keys, hs], NT_DIMS, preferred_element_type=F32) * inv
                  for _, hs in heads]
            sps = [jnp.maximum(jnp.log(1.0 + jnp.exp(jnp.minimum(z, SOFTPLUS_CLAMP))), z) for z in zs]
            loms = [jnp.where(before, -sp, 0.0) if diagonal else -sp for sp in sps]
            sufs = [_split2_dot(loms[hh], u_after) + car_ref[hh] for hh, _ in heads]
            avs = [jnp.exp((zs[hh] - sps[hh]) + sufs[hh]) for hh, _ in heads]
            if diagonal:
                avs = [jnp.where(before, a, 0.0) for a in avs]
            pvs = [jnp.dot(avs[hh].astype(BF16), v_ref[keys, hs], preferred_element_type=F32) for hh, hs in heads]
            for hh, _ in heads:
                acc_ref[hh] += pvs[hh]
                car_ref[hh] += jnp.sum(loms[hh], axis=1, keepdims=True)

        block(i, True)

        def step(it, carry):
            block(i - it, False)
            return carry

        lax.fori_loop(1, i + 1, step, 0)
        for hh, hs in enumerate(_head_slices()):
            y_ref[:, hs] = acc_ref[hh]
            tot_ref[:, hs] = jnp.broadcast_to(car_ref[hh], (tq, HEAD_DIM))

    q_spec, k_spec, v_spec = _att_specs(s, e, tq, 0)
    blk_q = pl.BlockSpec((None, tq, hp * HEAD_DIM), lambda i, h, j: (i, j, h))
    shp = jax.ShapeDtypeStruct((b, s, e), F32)
    return pl.pallas_call(
        body, name="sb_fwd", grid=(b, n_heads // hp, nq),
        in_specs=[q_spec, k_spec, v_spec],
        out_specs=(blk_q, blk_q), out_shape=(shp, shp),
        scratch_shapes=[pltpu.VMEM((hp, tq, HEAD_DIM), F32), pltpu.VMEM((hp, tq, 1), F32)],
        compiler_params=_params("parallel", "parallel", "arbitrary"),
    )(proj3, proj3, proj3)


def _sb_bwd(proj3, lom_total, dy, partials):
    b, s, e = dy.shape
    n_heads = e // HEAD_DIM
    hp = HEADS_PER_STEP
    tq = _tile(s, ATT_TQ)
    nq = s // tq
    inv = 1.0 / math.sqrt(HEAD_DIM)
    grid = (b, n_heads // hp, nq)

    def body(q_ref, k_ref, v_ref, tot_ref, dy_ref, p_ref, dq_ref, dk_ref, dv_ref, r_ref,
             dqa, dka, dva, car, car2, send_sems, recv_sems, local_sem):
        i = pl.program_id(2)
        first, last = _grid_first_last(grid)

        @pl.when(first)
        def _():
            for cp in _all_to_all_copies(p_ref, r_ref, send_sems, recv_sems, local_sem):
                cp.start()

        @pl.when(i == 0)
        def _():
            dka[...] = jnp.zeros_like(dka)
            dva[...] = jnp.zeros_like(dva)

        row, col = _iota2(tq)
        before = row > col
        u_upto = (row <= col).astype(BF16)
        u_before = (row < col).astype(BF16)
        dqa[...] = jnp.zeros_like(dqa)
        car[...] = jnp.zeros_like(car)
        car2[...] = jnp.zeros_like(car2)

        def block(j, diagonal):
            keys = pl.ds(pl.multiple_of(j * tq, tq), tq)
            heads = list(enumerate(_head_slices()))
            zs = [lax.dot_general(q_ref[:, hs], k_ref[keys, hs], NT_DIMS, preferred_element_type=F32) * inv
                  for _, hs in heads]
            das = [lax.dot_general(dy_ref[:, hs], v_ref[keys, hs], NT_DIMS, preferred_element_type=F32) for _, hs in heads]
            ezs = [jnp.exp(jnp.minimum(z, SOFTPLUS_CLAMP)) for z in zs]
            sps = [jnp.maximum(jnp.log(1.0 + ezs[hh]), zs[hh]) for hh, _ in heads]
            loms = [jnp.where(before, -sp, 0.0) if diagonal else -sp for sp in sps]
            sufs = [tot_ref[:, hh * HEAD_DIM:hh * HEAD_DIM + 1] - (_split2_dot(loms[hh], u_upto) + car[hh])
                    for hh, _ in heads]
            avs = [jnp.exp((zs[hh] - sps[hh]) + sufs[hh]) for hh, _ in heads]
            if diagonal:
                avs = [jnp.where(before, a, 0.0) for a in avs]
            dls = [avs[hh] * das[hh] for hh, _ in heads]
            prefixes = [_split2_dot(dls[hh], u_before) + car2[hh] for hh, _ in heads]
            dzs = []
            for hh, _ in heads:
                one_minus_beta = 1.0 / (1.0 + ezs[hh])
                dz = (dls[hh] * one_minus_beta - prefixes[hh] * (ezs[hh] * one_minus_beta)) * inv
                if diagonal:
                    dz = jnp.where(before, dz, 0.0)
                dzs.append(dz.astype(BF16))
            dqs = [jnp.dot(dzs[hh], k_ref[keys, hs], preferred_element_type=F32) for hh, hs in heads]
            dks = [lax.dot_general(dzs[hh], q_ref[:, hs], TN_DIMS, preferred_element_type=F32) for hh, hs in heads]
            dvs = [lax.dot_general(avs[hh].astype(BF16), dy_ref[:, hs], TN_DIMS, preferred_element_type=F32)
                   for hh, hs in heads]
            for hh, hs in heads:
                dqa[hh] += dqs[hh]
                dka[keys, hs] += dks[hh]
                dva[keys, hs] += dvs[hh]
                car[hh] += jnp.sum(loms[hh], axis=1, keepdims=True)
                car2[hh] += jnp.sum(dls[hh], axis=1, keepdims=True)

        def step(j, carry):
            block(j, False)
            return carry

        lax.fori_loop(0, i, step, 0)
        block(i, True)
        for hh, hs in enumerate(_head_slices()):
            dq_ref[:, hs] = dqa[hh].astype(BF16)

        @pl.when(i == nq - 1)
        def _():
            dk_ref[...] = dka[...].astype(BF16)
            dv_ref[...] = dva[...].astype(BF16)

        @pl.when(last)
        def _():
            for cp in _all_to_all_copies(p_ref, r_ref, send_sems, recv_sems, local_sem):
                cp.wait()

    q_spec, k_spec, v_spec = _att_specs(s, e, tq, 0)
    w = hp * HEAD_DIM
    blk_q = pl.BlockSpec((None, tq, w), lambda i, h, j: (i, j, h))
    blk_kv = pl.BlockSpec((None, s, w), lambda i, h, j: (i, 0, h))
    shp = jax.ShapeDtypeStruct((b, s, e), BF16)
    return pl.pallas_call(
        body, name="sb_bwd", grid=grid,
        in_specs=[q_spec, k_spec, v_spec, blk_q, blk_q, ANY_SPEC],
        out_specs=(blk_q, blk_kv, blk_kv, ANY_SPEC),
        out_shape=(shp, shp, shp, jax.ShapeDtypeStruct(partials.shape, partials.dtype)),
        scratch_shapes=[pltpu.VMEM((hp, tq, HEAD_DIM), F32), pltpu.VMEM((s, w), F32), pltpu.VMEM((s, w), F32),
                        pltpu.VMEM((hp, tq, 1), F32), pltpu.VMEM((hp, tq, 1), F32),
                        pltpu.SemaphoreType.DMA((N_DEV - 1,)), pltpu.SemaphoreType.DMA((N_DEV - 1,)),
                        pltpu.SemaphoreType.DMA],
        compiler_params=_params("arbitrary", "arbitrary", "arbitrary"),
    )(proj3, proj3, proj3, lom_total, dy, partials)


def _dil_near_tiles(tq):
    return (DIL_PAIRS[1][0] + tq - 1) // tq + 1


def _dil_fill_bias(bias_ref, sl_ref, tq):
    row, col = _iota2(tq)
    for hh in range(HEADS_PER_STEP):
        slope = sl_ref[hh, 0:1, 0:1]
        for d in range(_dil_near_tiles(tq) + 1):
            dist = d * tq + row - col
            cnt = jnp.zeros(dist.shape, jnp.int32)
            for window, dilation in DIL_PAIRS:
                cnt = cnt + (((dist & (dilation - 1)) == 0) & (dist <= window)).astype(jnp.int32)
            bias = jnp.where(cnt == 3, math.log(3.0), jnp.where(cnt == 2, math.log(2.0), 0.0))
            bias_ref[hh, d] = jnp.where((dist >= 0) & (cnt > 0), bias - slope * dist.astype(F32), NEG)


def _dil_scores(q, k, bias_ref, hh, slope, it, tq, inv):
    near = _dil_near_tiles(tq)
    beyond = jnp.maximum(it - near, 0).astype(F32) * float(tq)
    sc = lax.dot_general(q, k, NT_DIMS, preferred_element_type=F32) * inv
    return (sc + bias_ref[hh, jnp.minimum(it, near)]) - slope * beyond


def _dil_fwd(proj3, e, slopes):
    b, s, _ = proj3.shape
    n_heads = e // HEAD_DIM
    hp = HEADS_PER_STEP
    tq = _tile(s, ATT_TQ)
    nq = s // tq
    inv = 1.0 / math.sqrt(HEAD_DIM)
    assert s <= DIL_PAIRS[2][0]

    def body(q_ref, k_ref, v_ref, sl_ref, y_ref, lse_ref, acc_ref, m_ref, l_ref, bias_ref):
        i = pl.program_id(2)

        @pl.when(i == 0)
        def _():
            _dil_fill_bias(bias_ref, sl_ref, tq)

        acc_ref[...] = jnp.zeros_like(acc_ref)
        m_ref[...] = jnp.full_like(m_ref, NEG)
        l_ref[...] = jnp.zeros_like(l_ref)

        def step(it, carry):
            keys = pl.ds(pl.multiple_of((i - it) * tq, tq), tq)
            heads = list(enumerate(_head_slices()))
            scs = [_dil_scores(q_ref[:, hs], k_ref[keys, hs], bias_ref, hh, sl_ref[hh, 0:1, 0:1], it, tq, inv)
                   for hh, hs in heads]
            m_old = [m_ref[hh] for hh, _ in heads]
            m_new = [jnp.maximum(m_old[hh], jnp.max(scs[hh], axis=1, keepdims=True)) for hh, _ in heads]
            ps = [jnp.exp(scs[hh] - m_new[hh]) for hh, _ in heads]
            pvs = [jnp.dot(ps[hh].astype(BF16), v_ref[keys, hs], preferred_element_type=F32) for hh, hs in heads]
            for hh, _ in heads:
                alpha = jnp.exp(m_old[hh] - m_new[hh])
                l_ref[hh] = alpha * l_ref[hh] + (ps[hh][:, :tq // 2] + ps[hh][:, tq // 2:])
                acc_ref[hh] = alpha * acc_ref[hh] + pvs[hh]
                m_ref[hh] = m_new[hh]
            return carry

        lax.fori_loop(0, i + 1, step, 0)
        for hh, hs in enumerate(_head_slices()):
            l = jnp.sum(l_ref[hh], axis=1, keepdims=True)
            y_ref[:, hs] = acc_ref[hh] / l
            lse_ref[:, hs] = jnp.broadcast_to(m_ref[hh] + jnp.log(l), (tq, HEAD_DIM))

    q_spec, k_spec, v_spec = _att_specs(s, e, tq, 4 * n_heads)
    blk_q = pl.BlockSpec((None, tq, hp * HEAD_DIM), lambda i, h, j: (i, j, h))
    shp = jax.ShapeDtypeStruct((b, s, e), F32)
    return pl.pallas_call(
        body, name="dil_fwd", grid=(b, n_heads // hp, nq),
        in_specs=[q_spec, k_spec, v_spec, pl.BlockSpec((hp, 8, HEAD_DIM), lambda i, h, j: (h, 0, 0))],
        out_specs=(blk_q, blk_q), out_shape=(shp, shp),
        scratch_shapes=[pltpu.VMEM((hp, tq, HEAD_DIM), F32), pltpu.VMEM((hp, tq, 1), F32), pltpu.VMEM((hp, tq, tq // 2), F32),
                        pltpu.VMEM((hp, _dil_near_tiles(tq) + 1, tq, tq), F32)],
        compiler_params=_params("parallel", "parallel", "arbitrary"),
    )(proj3, proj3, proj3, slopes)


def _dil_bwd(proj3, y, lse, dy, slopes):
    b, s, e = y.shape
    n_heads = e // HEAD_DIM
    hp = HEADS_PER_STEP
    tq = _tile(s, ATT_TQ)
    nq = s // tq
    inv = 1.0 / math.sqrt(HEAD_DIM)

    def body(q_ref, k_ref, v_ref, sl_ref, y_ref, lse_ref, dy_ref, dq_ref, dk_ref, dv_ref, dqa, dka, dva, bias_ref):
        i = pl.program_id(2)

        @pl.when(i == 0)
        def _():
            dka[...] = jnp.zeros_like(dka)
            dva[...] = jnp.zeros_like(dva)
            _dil_fill_bias(bias_ref, sl_ref, tq)

        delta = [jnp.sum(dy_ref[:, hs].astype(F32) * y_ref[:, hs], axis=1, keepdims=True) for hs in _head_slices()]
        dqa[...] = jnp.zeros_like(dqa)

        def step(it, carry):
            keys = pl.ds(pl.multiple_of((i - it) * tq, tq), tq)
            heads = list(enumerate(_head_slices()))
            scs = [_dil_scores(q_ref[:, hs], k_ref[keys, hs], bias_ref, hh, sl_ref[hh, 0:1, 0:1], it, tq, inv)
                   for hh, hs in heads]
            dps = [lax.dot_general(dy_ref[:, hs], v_ref[keys, hs], NT_DIMS, preferred_element_type=F32) for _, hs in heads]
            ps = [jnp.exp(scs[hh] - lse_ref[:, hh * HEAD_DIM:hh * HEAD_DIM + 1]) for hh, _ in heads]
            dss = [((ps[hh] * (dps[hh] - delta[hh])) * inv).astype(BF16) for hh, _ in heads]
            dqs = [jnp.dot(dss[hh], k_ref[keys, hs], preferred_element_type=F32) for hh, hs in heads]
            dks = [lax.dot_general(dss[hh], q_ref[:, hs], TN_DIMS, preferred_element_type=F32) for hh, hs in heads]
            dvs = [lax.dot_general(ps[hh].astype(BF16), dy_ref[:, hs], TN_DIMS, preferred_element_type=F32)
                   for hh, hs in heads]
            for hh, hs in heads:
                dqa[hh] += dqs[hh]
                dka[keys, hs] += dks[hh]
                dva[keys, hs] += dvs[hh]
            return carry

        lax.fori_loop(0, i + 1, step, 0)
        for hh, hs in enumerate(_head_slices()):
            dq_ref[:, hs] = dqa[hh].astype(BF16)

        @pl.when(i == nq - 1)
        def _():
            dk_ref[...] = dka[...].astype(BF16)
            dv_ref[...] = dva[...].astype(BF16)

    q_spec, k_spec, v_spec = _att_specs(s, e, tq, 4 * n_heads)
    w = hp * HEAD_DIM
    blk_q = pl.BlockSpec((None, tq, w), lambda i, h, j: (i, j, h))
    blk_kv = pl.BlockSpec((None, s, w), lambda i, h, j: (i, 0, h))
    shp = jax.ShapeDtypeStruct((b, s, e), BF16)
    return pl.pallas_call(
        body, name="dil_bwd", grid=(b, n_heads // hp, nq),
        in_specs=[q_spec, k_spec, v_spec, pl.BlockSpec((hp, 8, HEAD_DIM), lambda i, h, j: (h, 0, 0)),
                  blk_q, blk_q, blk_q],
        out_specs=(blk_q, blk_kv, blk_kv), out_shape=(shp, shp, shp),
        scratch_shapes=[pltpu.VMEM((hp, tq, HEAD_DIM), F32), pltpu.VMEM((s, w), F32), pltpu.VMEM((s, w), F32),
                        pltpu.VMEM((hp, _dil_near_tiles(tq) + 1, tq, tq), F32)],
        compiler_params=_params("parallel", "parallel", "arbitrary"),
    )(proj3, proj3, proj3, slopes, y, lse, dy)


def kernel(x, c, w_ada, b_ada, g_norm, w_in, g_sb, g_dil, w_out, g_final, loss_target, m_w_ada, m_b_ada, m_g_norm, m_w_in, m_g_sb, m_g_dil, m_w_out, m_g_final, v_w_ada, v_b_ada, v_g_norm, v_w_in, v_g_sb, v_g_dil, v_w_out, v_g_final):
    b, s, d = x.shape
    t = b * s
    e = w_in.shape[2]
    n_heads = e // HEAD_DIM
    na = w_ada.shape[2]
    r_out = w_out.shape[1]
    assert g_sb.shape[1] == e and g_dil.shape[1] == e and N_DEV * r_out == 2 * e and N_DEV * na == 3 * d
    assert b <= SMALL_ROWS and 3 * b + 3 <= 2 * SMALL_ROWS
    ix, iy, ic = _mesh_pos()
    me = 4 * ix + 2 * iy + ic

    c_all = _allgather_rows(jnp.pad(c, ((0, SMALL_ROWS - b), (0, 0))), "ag_c")
    b_own = lax.dynamic_slice(b_ada, (0, me * na), (1, na))
    mod_cols = _ada_fwd(c_all, w_ada[0], b_own)
    mod_all = _allgather_rows(mod_cols, "ag_mod").reshape(N_DEV, N_DEV, SMALL_ROWS, na)
    mod_own = lax.dynamic_slice(mod_all, (0, me, 0, 0), (N_DEV, 1, b, na))[:, 0]
    mod = mod_own.transpose(1, 0, 2).reshape(b, 1, 3 * d)
    shift, scale, gate = mod[:, :, :d], mod[:, :, d:2 * d], mod[:, :, 2 * d:]

    w_out1 = _allgather_block(w_out[0].astype(BF16), "ag_w_out").reshape(1, N_DEV * r_out, d)
    w_own = w_in[0].astype(BF16).reshape(d, PROJ_HALVES, e // PROJ_HALVES).transpose(1, 0, 2)

    h = _norm_mod(x, g_norm, scale, shift).reshape(t, d)
    proj, w_in3 = _proj_with_allgather(h, w_own, jnp.stack(_unit_ids()).astype(jnp.int32))
    proj3 = proj.reshape(b, s, N_DEV * e)
    slopes = jnp.exp2(-ALIBI_MAX_BIAS * jnp.arange(1, n_heads + 1, dtype=F32) / n_heads)
    slopes = jnp.broadcast_to(slopes[:, None, None], (n_heads, 8, HEAD_DIM))
    y_sb, lom_total = _sb_fwd(proj3, e)
    y_dl, lse = _dil_fwd(proj3, e, slopes)
    yg = _gate_fwd(y_sb.reshape(t, e), y_dl.reshape(t, e), proj, g_sb, g_dil)
    out = _mm_nn(yg, w_out1, F32, "mm_out").reshape(b, s, d)
    loss_p, dx2, d_out, dgate, gg_final = _final_fwd_bwd(x, out, gate, g_final.reshape(1, d), loss_target)

    d_out2 = d_out.reshape(t, d)
    dyg = _mm_nt(d_out2, w_out1, BF16, "mm_dy")
    gw_out_p = _mm_tn(yg, d_out2, 1, BF16, "mm_gw_out").reshape(N_DEV, r_out, d)
    dy_sb, dy_dl, dz_sb, dz_dl, gg_sb, gg_dl = _gate_bwd(dyg, y_sb.reshape(t, e), y_dl.reshape(t, e), proj, g_sb, g_dil)
    dq_sb, dk_sb, dv_sb, recv_out = _sb_bwd(proj3, lom_total, dy_sb.reshape(b, s, e), gw_out_p)
    dq_dl, dk_dl, dv_dl = _dil_bwd(proj3, y_dl, lse, dy_dl.reshape(b, s, e), slopes)
    dproj = jnp.concatenate(
        [a.reshape(t, e) for a in (dq_sb, dk_sb, dv_sb, dz_sb, dq_dl, dk_dl, dv_dl, dz_dl)], axis=1)
    gw_in_p = _mm_tn(h, dproj, N_DEV, BF16, "mm_gw_in")
    gw_in_sib = _swap_with_sibling(gw_in_p)
    chip_blocks = jnp.stack([4 * px + 2 * py + ic for px, py in _other_chips()]).astype(jnp.int32)
    gw_in_send = _chip_presum(chip_blocks, gw_in_p, gw_in_sib)
    dh, gw_in_recv = _mm_nt_with_chip_exchange(dproj, w_in3, F32, gw_in_send, "mm_dh")
    dh = dh.reshape(b, s, d)
    grad_x, dshift, dscale, gg_norm = _norm_bwd(x, dh, dx2, scale, g_norm)

    dmod = jnp.concatenate([dshift, dscale, dgate], axis=1).reshape(3 * b, d)
    pkg = jnp.concatenate([dmod, gg_norm, gg_final, jnp.concatenate([gg_sb, gg_dl], axis=1),
                           jnp.zeros((2 * SMALL_ROWS - 3 * b - 3, d), F32)], axis=0)
    pkg_all = _allgather_rows(pkg, "ag_small_grads").reshape(N_DEV, 2 * SMALL_ROWS, d)
    dmod_all = pkg_all[:, :3 * b].reshape(N_DEV * b, 3 * d)
    dmod_cols = lax.dynamic_slice(dmod_all, (0, me * na), (N_DEV * b, na))
    c_rows = c_all.reshape(N_DEV, SMALL_ROWS, d)[:, :b].reshape(N_DEV * b, d)
    g_w_ada, d_w_ada, nm_w_ada, nv_w_ada = _ada_bwd_adam(c_rows, dmod_cols, w_ada[0], m_w_ada[0], v_w_ada[0])

    def pack(b_ada_like, g_norm_like, g_sb_like, g_dil_like, g_final_like):
        return jnp.concatenate([b_ada_like.reshape(3, d), g_norm_like.reshape(1, d), g_final_like.reshape(1, d),
                                jnp.concatenate([g_sb_like, g_dil_like], axis=1).reshape(1, d),
                                jnp.zeros((2, d), F32)], axis=0)

    small = _small_adam(pkg_all, 3 * b, pack(b_ada, g_norm, g_sb, g_dil, g_final),
                        pack(m_b_ada, m_g_norm, m_g_sb, m_g_dil, m_g_final),
                        pack(v_b_ada, v_g_norm, v_g_sb, v_g_dil, v_g_final))

    def unpack(p):
        return (p[0:3].reshape(1, 3 * d), p[3:4], p[5:6, :e], p[5:6, e:], p[4])

    sm_g, sm_d, sm_m, sm_v = (unpack(p) for p in small)

    g_w_in, d_w_in, nm_w_in, nv_w_in = _adam_from_chip_sums(
        jnp.reshape(me, (1,)).astype(jnp.int32), gw_in_p, gw_in_sib, gw_in_recv, w_in[0], m_w_in[0], v_w_in[0], "adam_w_in")
    g_w_out, d_w_out, nm_w_out, nv_w_out = _adam_from_partials(recv_out, w_out[0], m_w_out[0], v_w_out[0], "adam_w_out")

    loss = lax.psum(loss_p[0, 0], ("x", "y", "c"))

    def weights(ada, small_parts, w_in_part, w_out_part):
        b_ada_p, g_norm_p, g_sb_p, g_dil_p, g_final_p = small_parts
        return (ada[None], b_ada_p, g_norm_p, w_in_part[None], g_sb_p, g_dil_p, w_out_part[None], g_final_p)

    return (loss, grad_x,
            *weights(g_w_ada, sm_g, g_w_in, g_w_out),
            *weights(d_w_ada, sm_d, d_w_in, d_w_out),
            *weights(nm_w_ada, sm_m, nm_w_in, nm_w_out),
            *weights(nv_w_ada, sm_v, nv_w_in, nv_w_out))
```

```python
import functools
import math

import jax
import jax.numpy as jnp
from jax import lax
from jax.experimental import pallas as pl
from jax.experimental.pallas import tpu as pltpu

F32 = jnp.float32
BF16 = jnp.bfloat16
MESH = pl.DeviceIdType.MESH

N_DEV = 8
HEAD_DIM = 128
EPS = 1e-6
ALIBI_MAX_BIAS = 8.0
DIL_PAIRS = ((128, 1), (512, 4), (2048, 16))
DIL_STEPS = 128
NEG = -1e30

ADAM_LR = 0.001
ADAM_B1 = 0.9
ADAM_B2 = 0.999
ADAM_EPS = 1e-08
ADAM_WD = 0.01
ADAM_STEP = 10

VMEM_LIMIT_BYTES = 56 * 1024 * 1024
SMALL_ROWS = 8

NT_DIMS = (((1,), (1,)), ((), ()))
TN_DIMS = (((0,), (0,)), ((), ()))


def _params(*semantics):
    return pltpu.CompilerParams(dimension_semantics=semantics, vmem_limit_bytes=VMEM_LIMIT_BYTES)


def _tile(n, want):
    t = min(n, want)
    assert n % t == 0, (n, want)
    return t


def _mesh_pos():
    return lax.axis_index("x"), lax.axis_index("y"), lax.axis_index("c")


def _allgather_rows(x_shard, name):
    m_per, n = x_shard.shape

    def body(x_ref, out_ref, send_sems, recv_sems, local_sem):
        x, y, c = _mesh_pos()
        me, sibling = (x, y, c), (x, y, 1 - c)
        chips = [(1 - x, y), (x, 1 - y), (1 - x, 1 - y)]

        def rows(px, py, pc):
            return out_ref.at[pl.ds((4 * px + 2 * py + pc) * m_per, m_per), :]

        def copy(k, block, to, src=None):
            return pltpu.make_async_remote_copy(
                src_ref=rows(*block) if src is None else src, dst_ref=rows(*block),
                send_sem=send_sems.at[k], recv_sem=recv_sems.at[k], device_id=to, device_id_type=MESH)

        mine = pltpu.make_async_copy(x_ref, rows(*me), local_sem)
        mine.start()
        first = [copy(0, me, sibling, src=x_ref)]
        first += [copy(1 + j, me, (*chip, c), src=x_ref) for j, chip in enumerate(chips)]
        for cp in first:
            cp.start()
        passed = [copy(4 + j, (*chip, c), sibling) for j, chip in enumerate(chips)]
        for j, chip in enumerate(chips):
            copy(1 + j, (*chip, c), me).wait_recv()
            passed[j].start()
        copy(0, sibling, me).wait_recv()
        for j, chip in enumerate(chips):
            copy(4 + j, (*chip, 1 - c), me).wait_recv()
        for cp in first + passed:
            cp.wait_send()
        mine.wait()

    return pl.pallas_call(
        body, name=name,
        out_shape=jax.ShapeDtypeStruct((N_DEV * m_per, n), x_shard.dtype),
        in_specs=[pl.BlockSpec(memory_space=pltpu.VMEM)],
        out_specs=pl.BlockSpec(memory_space=pltpu.VMEM),
        scratch_shapes=[pltpu.SemaphoreType.DMA((7,)), pltpu.SemaphoreType.DMA((7,)), pltpu.SemaphoreType.DMA],
    )(x_shard)


def _two_level_gather(w_ref, out_ref, send_sems, recv_sems, local_sem):
    x, y, c = _mesh_pos()
    me, sibling = (x, y, c), (x, y, 1 - c)
    chips = [(1 - x, y), (x, 1 - y), (1 - x, 1 - y)]

    def copy(k, block, to, src=None):
        dst = out_ref.at[4 * block[0] + 2 * block[1] + block[2]]
        return pltpu.make_async_remote_copy(
            src_ref=dst if src is None else src, dst_ref=dst,
            send_sem=send_sems.at[k], recv_sem=recv_sems.at[k], device_id=to, device_id_type=MESH)

    mine = pltpu.make_async_copy(w_ref, out_ref.at[4 * x + 2 * y + c], local_sem)
    to_sibling = copy(0, me, sibling, src=w_ref)
    to_chips = [copy(1 + j, me, (*chip, c), src=w_ref) for j, chip in enumerate(chips)]
    passed = [copy(4 + j, (*chip, c), sibling) for j, chip in enumerate(chips)]

    def early():
        mine.start()
        to_sibling.start()

    def ici():
        for cp in to_chips:
            cp.start()

    def finish():
        for j, chip in enumerate(chips):
            copy(1 + j, (*chip, c), me).wait_recv()
            passed[j].start()
        copy(0, sibling, me).wait_recv()
        for j, chip in enumerate(chips):
            copy(4 + j, (*chip, 1 - c), me).wait_recv()
        for cp in [to_sibling] + to_chips + passed:
            cp.wait_send()
        mine.wait()

    return early, ici, finish


PROJ_HALVES = 2


def _unit_schedule():
    assert PROJ_HALVES == 2
    sched = [("own", None, hf) for hf in range(PROJ_HALVES)] + [("sib", None, hf) for hf in range(PROJ_HALVES)]
    for rnd in ([(0, 0), (1, 1)], [(2, 0), (2, 1)], [(0, 1), (1, 0)]):
        sched += [("direct", j, hf) for j, hf in rnd] + [("fwd", j, hf) for j, hf in rnd]
    return sched


def _unit_ids():
    x, y, c = _mesh_pos()
    chips = _other_chips()
    ids = []
    for kind, j, hf in _unit_schedule():
        px, py = (x, y) if j is None else chips[j]
        pc = c if kind in ("own", "direct") else 1 - c
        ids.append(PROJ_HALVES * (4 * px + 2 * py + pc) + hf)
    return ids


def _proj_with_allgather(h, w_own, order, w_later):
    t, d = h.shape
    nh, _, u = w_own.shape
    assert nh == PROJ_HALVES
    sched = _unit_schedule()
    n_units = len(sched)
    pos = {entry: p for p, entry in enumerate(sched)}
    tm = _tile(t, MM_TM)
    m_tiles = t // tm
    prep_m = max(m_tiles - 2, 0)
    grid = (n_units, m_tiles)

    later_ici_pos = pos[("direct", 2, 0)]

    def body(order_ref, h_ref, wown_ref, later_ref, proj_ref, w3_ref, later3_ref, bbuf, bsems, send_sems, recv_sems,
             local_sems, later_send_sems, later_recv_sems, later_local_sem):
        n, m = pl.program_id(0), pl.program_id(1)
        x, y, c = _mesh_pos()
        chips = _other_chips()
        sibling = (x, y, 1 - c)
        later_early, later_ici, later_finish = _two_level_gather(
            later_ref, later3_ref, later_send_sems, later_recv_sems, later_local_sem)

        def unit(p):
            return w3_ref.at[order_ref[p]]

        def arrival(p):
            return pltpu.make_async_remote_copy(
                src_ref=unit(p), dst_ref=unit(p), send_sem=send_sems.at[0], recv_sem=recv_sems.at[p - nh],
                device_id=sibling, device_id_type=MESH)

        def send(k, src, p_here, p_there, to):
            return pltpu.make_async_remote_copy(
                src_ref=src, dst_ref=unit(p_here), send_sem=send_sems.at[k], recv_sem=recv_sems.at[p_there - nh],
                device_id=to, device_id_type=MESH)

        copies = []

        def out(src, p_here, p_there, to):
            copies.append(send(len(copies), src, p_here, p_there, to))
            return copies[-1]

        def own(hf, p_there, to):
            return out(wown_ref.at[hf], pos[("own", None, hf)], p_there, to)

        x_nbr, y_nbr = (*chips[0], c), (*chips[1], c)
        first = [own(hf, pos[("sib", None, hf)], sibling) for hf in range(nh)]
        first += [own(0, pos[("direct", 0, 0)], x_nbr), own(1, pos[("direct", 1, 1)], y_nbr)]
        after = {}
        for kind, j, hf in sched:
            if kind == "direct":
                p = pos[(kind, j, hf)]
                after[p] = [out(unit(p), p, pos[("fwd", j, hf)], sibling)]
        p = pos[("direct", 0, 0)]
        after[p] += [out(unit(p), p, pos[("direct", 2, 0)], y_nbr), own(0, pos[("direct", 1, 0)], y_nbr)]
        p = pos[("direct", 1, 1)]
        after[p] += [out(unit(p), p, pos[("direct", 2, 1)], x_nbr), own(1, pos[("direct", 0, 1)], x_nbr)]
        locals_ = [pltpu.make_async_copy(wown_ref.at[hf], unit(pos[("own", None, hf)]), local_sems.at[hf])
                   for hf in range(nh)]

        def fetch(p):
            src = wown_ref.at[sched[p][2]] if sched[p][0] == "own" else unit(p)
            return pltpu.make_async_copy(src, bbuf.at[p % 2], bsems.at[p % 2])

        @pl.when((n == 0) & (m == 0))
        def _():
            for cp in locals_ + first:
                cp.start()
            fetch(0).start()
            later_early()

        for p in range(n_units):
            @pl.when((n == p) & (m == 0))
            def _(p=p):
                fetch(p).wait()

            if p + 1 < n_units:
                @pl.when((n == p) & (m == prep_m))
                def _(p=p):
                    if sched[p + 1][0] != "own":
                        arrival(p + 1).wait_recv()
                    for cp in after.get(p + 1, []):
                        cp.start()
                    fetch(p + 1).start()
                    if p + 1 == later_ici_pos:
                        later_ici()

        proj_ref[...] = jnp.dot(h_ref[...], bbuf[n % 2], preferred_element_type=F32).astype(proj_ref.dtype)

        @pl.when((n == n_units - 1) & (m == m_tiles - 1))
        def _():
            for cp in copies:
                cp.wait_send()
            for cp in locals_:
                cp.wait()
            later_finish()

    n_out = len(sched) - nh
    return pl.pallas_call(
        body, name="mm_proj_allgather",
        grid_spec=pltpu.PrefetchScalarGridSpec(
            num_scalar_prefetch=1, grid=grid,
            in_specs=[pl.BlockSpec((tm, d), lambda n, m, order: (m, 0)), ANY_SPEC, ANY_SPEC],
            out_specs=(pl.BlockSpec((tm, u), lambda n, m, order: (m, order[n])), ANY_SPEC, ANY_SPEC),
            scratch_shapes=[pltpu.VMEM((2, d, u), h.dtype), pltpu.SemaphoreType.DMA((2,)),
                            pltpu.SemaphoreType.DMA((n_out,)), pltpu.SemaphoreType.DMA((n_out,)),
                            pltpu.SemaphoreType.DMA((nh,)),
                            pltpu.SemaphoreType.DMA((N_DEV - 1,)), pltpu.SemaphoreType.DMA((N_DEV - 1,)),
                            pltpu.SemaphoreType.DMA]),
        out_shape=(jax.ShapeDtypeStruct((t, n_units * u), h.dtype), jax.ShapeDtypeStruct((n_units, d, u), h.dtype),
                   jax.ShapeDtypeStruct((N_DEV,) + w_later.shape, w_later.dtype)),
        compiler_params=_params("arbitrary", "arbitrary"),
    )(order, h, w_own, w_later)


ANY_SPEC = pl.BlockSpec(memory_space=pl.ANY)


def _grid_first_last(grid):
    ids = [pl.program_id(a) for a in range(len(grid))]
    first = functools.reduce(lambda p, q: p & q, [i == 0 for i in ids])
    last = functools.reduce(lambda p, q: p & q, [i == n - 1 for i, n in zip(ids, grid)])
    return first, last


def _all_to_all_copies(src, dst, send_sems, recv_sems, local_sem):
    x, y, c = _mesh_pos()
    my = 4 * x + 2 * y + c
    copies = [pltpu.make_async_copy(src.at[my], dst.at[my], local_sem)]
    for d in range(1, N_DEV):
        px = 1 - x if d & 4 else x
        py = 1 - y if d & 2 else y
        pc = 1 - c if d & 1 else c
        copies.append(pltpu.make_async_remote_copy(
            src_ref=src.at[4 * px + 2 * py + pc], dst_ref=dst.at[my],
            send_sem=send_sems.at[d - 1], recv_sem=recv_sems.at[d - 1],
            device_id=(px, py, pc), device_id_type=MESH))
    return copies


def _other_chips():
    x, y, _ = _mesh_pos()
    return [(1 - x, y), (x, 1 - y), (1 - x, 1 - y)]


def _same_core_copies(src, dst, send_sems, recv_sems):
    c = lax.axis_index("c")
    return [pltpu.make_async_remote_copy(
        src_ref=src.at[j], dst_ref=dst.at[j], send_sem=send_sems.at[j], recv_sem=recv_sems.at[j],
        device_id=(*chip, c), device_id_type=MESH) for j, chip in enumerate(_other_chips())]


def _swap_with_sibling(partials):
    _, r, cdim = partials.shape

    def body(p_ref, o_ref, send_sems, recv_sems):
        x, y, c = _mesh_pos()
        dests = [(x, y)] + _other_chips()
        copies = [pltpu.make_async_remote_copy(
            src_ref=p_ref.at[4 * px + 2 * py + (1 - c)], dst_ref=o_ref.at[j],
            send_sem=send_sems.at[j], recv_sem=recv_sems.at[j],
            device_id=(x, y, 1 - c), device_id_type=MESH) for j, (px, py) in enumerate(dests)]
        for cp in copies:
            cp.start()
        for cp in copies:
            cp.wait()

    return pl.pallas_call(
        body, name="swap_with_sibling",
        out_shape=jax.ShapeDtypeStruct((4, r, cdim), partials.dtype),
        in_specs=[ANY_SPEC], out_specs=ANY_SPEC,
        scratch_shapes=[pltpu.SemaphoreType.DMA((4,)), pltpu.SemaphoreType.DMA((4,))],
    )(partials)


def _chip_presum(blocks, partials, from_sibling):
    _, r, cdim = partials.shape
    tr = _tile(r, 256)

    def body(blocks_ref, p_ref, s_ref, o_ref):
        o_ref[...] = (p_ref[...].astype(F32) + s_ref[...].astype(F32)).astype(o_ref.dtype)

    return pl.pallas_call(
        body, name="chip_presum",
        grid_spec=pltpu.PrefetchScalarGridSpec(
            num_scalar_prefetch=1, grid=(3, r // tr),
            in_specs=[pl.BlockSpec((None, tr, cdim), lambda j, i, blk: (blk[j], i, 0)),
                      pl.BlockSpec((None, tr, cdim), lambda j, i, blk: (1 + j, i, 0))],
            out_specs=pl.BlockSpec((None, tr, cdim), lambda j, i, blk: (j, i, 0))),
        out_shape=jax.ShapeDtypeStruct((3, r, cdim), partials.dtype),
        compiler_params=_params("parallel", "parallel"),
    )(blocks, partials, from_sibling)


def _mm_call(a, b, dims, nk, grid, a_spec, b_spec, o_spec, out_shape, acc_shape, name):
    def body(a_ref, b_ref, o_ref, acc_ref):
        k = pl.program_id(2)

        @pl.when(k == 0)
        def _():
            acc_ref[...] = jnp.zeros_like(acc_ref)

        acc_ref[...] += lax.dot_general(a_ref[...], b_ref[...], dims, preferred_element_type=F32)

        @pl.when(k == nk - 1)
        def _():
            o_ref[...] = acc_ref[...].astype(o_ref.dtype)

    return pl.pallas_call(
        body, name=name, grid=grid, in_specs=[a_spec, b_spec], out_specs=o_spec, out_shape=out_shape,
        scratch_shapes=[pltpu.VMEM(acc_shape, F32)],
        compiler_params=_params("parallel", "parallel", "arbitrary"),
    )(a, b)


MM_TM, MM_TN, MM_TK = 1024, 2048, 1024


def _mm_nn(a, b3, out_dtype, name):
    m, kk = a.shape
    g, _, nb = b3.shape
    tm, tn, tk = _tile(m, MM_TM), _tile(nb, MM_TN), _tile(kk, MM_TK)
    npb = nb // tn
    return _mm_call(
        a, b3, (((1,), (0,)), ((), ())), kk // tk, (m // tm, g * npb, kk // tk),
        pl.BlockSpec((tm, tk), lambda i, j, k: (i, k)),
        pl.BlockSpec((None, tk, tn), lambda i, j, k: (j // npb, k, j % npb)),
        pl.BlockSpec((tm, tn), lambda i, j, k: (i, j)),
        jax.ShapeDtypeStruct((m, g * nb), out_dtype), (tm, tn), name)


def _mm_nt(a, b3, out_dtype, name):
    m, kk = a.shape
    g, n, kb = b3.shape
    tm, tn, tk = _tile(m, MM_TM), _tile(n, MM_TN), _tile(kb, MM_TK)
    kpb = kb // tk
    return _mm_call(
        a, b3, NT_DIMS, kk // tk, (m // tm, n // tn, kk // tk),
        pl.BlockSpec((tm, tk), lambda i, j, k: (i, k)),
        pl.BlockSpec((None, tn, tk), lambda i, j, k: (k // kpb, j, k % kpb)),
        pl.BlockSpec((tm, tn), lambda i, j, k: (i, j)),
        jax.ShapeDtypeStruct((m, n), out_dtype), (tm, tn), name)


def _mm_nt_with_chip_exchange(a, b3, out_dtype, send3, name):
    m, kk = a.shape
    g, n, kb = b3.shape
    tm, tn, tk = _tile(m, MM_TM), _tile(n, MM_TN), _tile(kb, MM_TK)
    kpb = kb // tk
    grid = (m // tm, n // tn, kk // tk)

    def body(a_ref, b_ref, s_ref, o_ref, r_ref, acc_ref, send_sems, recv_sems):
        first, last = _grid_first_last(grid)
        k = pl.program_id(2)

        @pl.when(first)
        def _():
            for cp in _same_core_copies(s_ref, r_ref, send_sems, recv_sems):
                cp.start()

        @pl.when(k == 0)
        def _():
            acc_ref[...] = jnp.zeros_like(acc_ref)

        acc_ref[...] += lax.dot_general(a_ref[...], b_ref[...], NT_DIMS, preferred_element_type=F32)

        @pl.when(k == grid[2] - 1)
        def _():
            o_ref[...] = acc_ref[...].astype(o_ref.dtype)

        @pl.when(last)
        def _():
            for cp in _same_core_copies(s_ref, r_ref, send_sems, recv_sems):
                cp.wait()

    return pl.pallas_call(
        body, name=name, grid=grid,
        in_specs=[pl.BlockSpec((tm, tk), lambda i, j, k: (i, k)),
                  pl.BlockSpec((None, tn, tk), lambda i, j, k: (k // kpb, j, k % kpb)), ANY_SPEC],
        out_specs=(pl.BlockSpec((tm, tn), lambda i, j, k: (i, j)), ANY_SPEC),
        out_shape=(jax.ShapeDtypeStruct((m, n), out_dtype), jax.ShapeDtypeStruct(send3.shape, send3.dtype)),
        scratch_shapes=[pltpu.VMEM((tm, tn), F32), pltpu.SemaphoreType.DMA((3,)), pltpu.SemaphoreType.DMA((3,))],
        compiler_params=_params("arbitrary", "arbitrary", "arbitrary"),
    )(a, b3, send3)


def _mm_tn(a, b, g, out_dtype, name):
    t, m = a.shape
    nb = b.shape[1] // g
    tm, tn, tk = _tile(m, MM_TM), _tile(nb, MM_TN), _tile(t, MM_TK)
    npb = nb // tn
    return _mm_call(
        a, b, TN_DIMS, t // tk, (m // tm, g * npb, t // tk),
        pl.BlockSpec((tk, tm), lambda i, j, k: (k, i)),
        pl.BlockSpec((tk, tn), lambda i, j, k: (k, j)),
        pl.BlockSpec((None, tm, tn), lambda i, j, k: (j // npb, i, j % npb)),
        jax.ShapeDtypeStruct((g, m, nb), out_dtype), (tm, tn), name)


def _silu(z):
    return z * jax.nn.sigmoid(z)


def _ada_fwd(c_all, w_shard, b_own):
    r, d = c_all.shape
    na = w_shard.shape[1]
    tk = _tile(d, 512)
    nk = d // tk

    def body(c_ref, w_ref, b_ref, o_ref):
        k = pl.program_id(0)

        @pl.when(k == 0)
        def _():
            o_ref[...] = jnp.zeros_like(o_ref) + b_ref[...]

        cs = _silu(c_ref[...]).astype(BF16)
        o_ref[...] += jnp.dot(cs, w_ref[...].astype(BF16), preferred_element_type=F32)

    return pl.pallas_call(
        body, name="ada_fwd", grid=(nk,),
        in_specs=[pl.BlockSpec((r, tk), lambda k: (0, k)), pl.BlockSpec((tk, na), lambda k: (k, 0)),
                  pl.BlockSpec((1, na), lambda k: (0, 0))],
        out_specs=pl.BlockSpec((r, na), lambda k: (0, 0)),
        out_shape=jax.ShapeDtypeStruct((r, na), F32),
        compiler_params=_params("arbitrary"),
    )(c_all, w_shard, b_own)


def _adam(w, g, m, v):
    nm = ADAM_B1 * m + (1.0 - ADAM_B1) * g
    nv = ADAM_B2 * v + (1.0 - ADAM_B2) * (g * g)
    m_hat = nm / (1.0 - ADAM_B1 ** ADAM_STEP)
    v_hat = nv / (1.0 - ADAM_B2 ** ADAM_STEP)
    delta = -ADAM_LR * (m_hat / (jnp.sqrt(v_hat) + ADAM_EPS) + ADAM_WD * w)
    return delta, nm, nv


def _ada_bwd_adam(c_rows, dmod_cols, w, m, v):
    bg, d = c_rows.shape
    na = w.shape[1]
    tr = _tile(d, 256)

    def body(c_ref, dm_ref, w_ref, m_ref, v_ref, g_ref, d_ref, nm_ref, nv_ref):
        cs = _silu(c_ref[...]).astype(BF16)
        g = lax.dot_general(cs, dm_ref[...].astype(BF16), TN_DIMS, preferred_element_type=F32)
        delta, nm, nv = _adam(w_ref[...], g, m_ref[...], v_ref[...])
        g_ref[...] = g
        d_ref[...] = delta
        nm_ref[...] = nm
        nv_ref[...] = nv

    blk = pl.BlockSpec((tr, na), lambda i: (i, 0))
    shp = jax.ShapeDtypeStruct((d, na), F32)
    return pl.pallas_call(
        body, name="ada_bwd_adam", grid=(d // tr,),
        in_specs=[pl.BlockSpec((bg, tr), lambda i: (0, i)), pl.BlockSpec((bg, na), lambda i: (0, 0)), blk, blk, blk],
        out_specs=(blk, blk, blk, blk), out_shape=(shp, shp, shp, shp),
        compiler_params=_params("parallel"),
    )(c_rows, dmod_cols, w, m, v)


def _small_adam(pkg_all, n_batch_rows, w, m, v):
    d = w.shape[1]

    def body(p_ref, w_ref, m_ref, v_ref, g_ref, d_ref, nm_ref, nv_ref):
        for part in range(3):
            acc = jnp.zeros((1, d), F32)
            for dev in range(N_DEV):
                for b in range(n_batch_rows // 3):
                    acc = acc + p_ref[dev, 3 * b + part:3 * b + part + 1, :]
            g_ref[part:part + 1, :] = acc
        for rrow in range(3):
            acc = jnp.zeros((1, d), F32)
            for dev in range(N_DEV):
                acc = acc + p_ref[dev, n_batch_rows + rrow:n_batch_rows + rrow + 1, :]
            g_ref[3 + rrow:4 + rrow, :] = acc
        g_ref[6:8, :] = jnp.zeros((2, d), F32)
        g = g_ref[...]
        delta, nm, nv = _adam(w_ref[...], g, m_ref[...], v_ref[...])
        d_ref[...] = delta
        nm_ref[...] = nm
        nv_ref[...] = nv

    vm = pl.BlockSpec(memory_space=pltpu.VMEM)
    shp = jax.ShapeDtypeStruct((SMALL_ROWS, d), F32)
    return pl.pallas_call(
        body, name="small_adam", in_specs=[vm, vm, vm, vm], out_specs=(vm, vm, vm, vm),
        out_shape=(shp, shp, shp, shp),
    )(pkg_all, w, m, v)


def _adam_from_chip_sums(own_slot, partials, from_sibling, from_chips, w, m, v, name):
    _, r, c = partials.shape
    tr = _tile(r, 128)

    def body(slot_ref, p_ref, s_ref, f_ref, w_ref, m_ref, v_ref, g_ref, d_ref, nm_ref, nv_ref):
        g = p_ref[...].astype(F32) + s_ref[...].astype(F32)
        for j in range(3):
            g = g + f_ref[j].astype(F32)
        delta, nm, nv = _adam(w_ref[...], g, m_ref[...], v_ref[...])
        g_ref[...] = g
        d_ref[...] = delta
        nm_ref[...] = nm
        nv_ref[...] = nv

    blk = pl.BlockSpec((tr, c), lambda i, slot: (i, 0))
    shp = jax.ShapeDtypeStruct((r, c), F32)
    return pl.pallas_call(
        body, name=name,
        grid_spec=pltpu.PrefetchScalarGridSpec(
            num_scalar_prefetch=1, grid=(r // tr,),
            in_specs=[pl.BlockSpec((None, tr, c), lambda i, slot: (slot[0], i, 0)),
                      pl.BlockSpec((None, tr, c), lambda i, slot: (0, i, 0)),
                      pl.BlockSpec((3, tr, c), lambda i, slot: (0, i, 0)), blk, blk, blk],
            out_specs=(blk, blk, blk, blk)),
        out_shape=(shp, shp, shp, shp),
        compiler_params=_params("parallel"),
    )(own_slot, partials, from_sibling, from_chips, w, m, v)


def _adam_from_partials(recv, w, m, v, name):
    _, r, c = recv.shape
    tr = _tile(r, 128)

    def body(p_ref, w_ref, m_ref, v_ref, g_ref, d_ref, nm_ref, nv_ref):
        g = p_ref[0].astype(F32)
        for dev in range(1, N_DEV):
            g = g + p_ref[dev].astype(F32)
        delta, nm, nv = _adam(w_ref[...], g, m_ref[...], v_ref[...])
        g_ref[...] = g
        d_ref[...] = delta
        nm_ref[...] = nm
        nv_ref[...] = nv

    blk = pl.BlockSpec((tr, c), lambda i: (i, 0))
    shp = jax.ShapeDtypeStruct((r, c), F32)
    return pl.pallas_call(
        body, name=name, grid=(r // tr,),
        in_specs=[pl.BlockSpec((N_DEV, tr, c), lambda i: (0, i, 0)), blk, blk, blk],
        out_specs=(blk, blk, blk, blk), out_shape=(shp, shp, shp, shp),
        compiler_params=_params("parallel"),
    )(recv, w, m, v)


def _norm_mod(x, g_norm, scale, shift):
    b, s, d = x.shape
    ts = _tile(s, 256)

    def body(x_ref, g_ref, sc_ref, sh_ref, h_ref):
        xv = x_ref[...]
        r = lax.rsqrt(jnp.mean(xv * xv, axis=-1, keepdims=True) + EPS)
        xn = (xv * r) * g_ref[...]
        h_ref[...] = (xn * (1.0 + sc_ref[...]) + sh_ref[...]).astype(BF16)

    tok = pl.BlockSpec((None, ts, d), lambda i, j: (i, j, 0))
    per_b = pl.BlockSpec((None, 1, d), lambda i, j: (i, 0, 0))
    return pl.pallas_call(
        body, name="norm_mod", grid=(b, s // ts),
        in_specs=[tok, pl.BlockSpec((1, d), lambda i, j: (0, 0)), per_b, per_b],
        out_specs=tok, out_shape=jax.ShapeDtypeStruct((b, s, d), BF16),
        compiler_params=_params("parallel", "parallel"),
    )(x, g_norm, scale, shift)


def _final_fwd_bwd(x, out, gate, g_final, target):
    b, s, d = x.shape
    ts = _tile(s, 256)

    def body(x_ref, o_ref, gt_ref, g_ref, t_ref, loss_ref, dx2_ref, dout_ref, dgate_ref, gg_ref):
        i, j = pl.program_id(0), pl.program_id(1)

        @pl.when((i == 0) & (j == 0))
        def _():
            loss_ref[...] = jnp.zeros_like(loss_ref)
            gg_ref[...] = jnp.zeros_like(gg_ref)

        @pl.when(j == 0)
        def _():
            dgate_ref[...] = jnp.zeros_like(dgate_ref)

        ov = o_ref[...]
        gt = gt_ref[...]
        x2 = x_ref[...] + gt * ov
        r = lax.rsqrt(jnp.mean(x2 * x2, axis=-1, keepdims=True) + EPS)
        xh = x2 * r
        err = xh * g_ref[...] - t_ref[...]
        loss_ref[...] += 0.5 * jnp.sum(jnp.mean(err * err, axis=-1, keepdims=True), axis=0, keepdims=True)
        dfin = err * (1.0 / d)
        gg_ref[...] += jnp.sum(dfin * xh, axis=0, keepdims=True)
        dxh = dfin * g_ref[...]
        dx2 = r * (dxh - xh * jnp.mean(dxh * xh, axis=-1, keepdims=True))
        dx2_ref[...] = dx2
        dout_ref[...] = (gt * dx2).astype(BF16)
        dgate_ref[...] += jnp.sum(dx2 * ov, axis=0, keepdims=True)

    tok = pl.BlockSpec((None, ts, d), lambda i, j: (i, j, 0))
    per_b = pl.BlockSpec((None, 1, d), lambda i, j: (i, 0, 0))
    vec = pl.BlockSpec((1, d), lambda i, j: (0, 0))
    return pl.pallas_call(
        body, name="final_fwd_bwd", grid=(b, s // ts),
        in_specs=[tok, tok, per_b, vec, tok],
        out_specs=(pl.BlockSpec((8, 128), lambda i, j: (0, 0)), tok, tok, per_b, vec),
        out_shape=(jax.ShapeDtypeStruct((8, 128), F32), jax.ShapeDtypeStruct((b, s, d), F32),
                   jax.ShapeDtypeStruct((b, s, d), BF16), jax.ShapeDtypeStruct((b, 1, d), F32),
                   jax.ShapeDtypeStruct((1, d), F32)),
        compiler_params=_params("arbitrary", "arbitrary"),
    )(x, out, gate, g_final, target)


def _norm_bwd(x, dh, dx2, scale, g_norm):
    b, s, d = x.shape
    ts = _tile(s, 256)

    def body(x_ref, dh_ref, dx2_ref, sc_ref, g_ref, gx_ref, dsh_ref, dsc_ref, gg_ref):
        i, j = pl.program_id(0), pl.program_id(1)

        @pl.when((i == 0) & (j == 0))
        def _():
            gg_ref[...] = jnp.zeros_like(gg_ref)

        @pl.when(j == 0)
        def _():
            dsh_ref[...] = jnp.zeros_like(dsh_ref)
            dsc_ref[...] = jnp.zeros_like(dsc_ref)

        xv = x_ref[...]
        dhv = dh_ref[...]
        r = lax.rsqrt(jnp.mean(xv * xv, axis=-1, keepdims=True) + EPS)
        xh = xv * r
        xn = xh * g_ref[...]
        dsh_ref[...] += jnp.sum(dhv, axis=0, keepdims=True)
        dsc_ref[...] += jnp.sum(dhv * xn, axis=0, keepdims=True)
        dxn = dhv * (1.0 + sc_ref[...])
        gg_ref[...] += jnp.sum(dxn * xh, axis=0, keepdims=True)
        dxh = dxn * g_ref[...]
        gx_ref[...] = dx2_ref[...] + r * (dxh - xh * jnp.mean(dxh * xh, axis=-1, keepdims=True))

    tok = pl.BlockSpec((None, ts, d), lambda i, j: (i, j, 0))
    per_b = pl.BlockSpec((None, 1, d), lambda i, j: (i, 0, 0))
    vec = pl.BlockSpec((1, d), lambda i, j: (0, 0))
    return pl.pallas_call(
        body, name="norm_bwd", grid=(b, s // ts),
        in_specs=[tok, tok, tok, per_b, vec],
        out_specs=(tok, per_b, per_b, vec),
        out_shape=(jax.ShapeDtypeStruct((b, s, d), F32), jax.ShapeDtypeStruct((b, 1, d), F32),
                   jax.ShapeDtypeStruct((b, 1, d), F32), jax.ShapeDtypeStruct((1, d), F32)),
        compiler_params=_params("arbitrary", "arbitrary"),
    )(x, dh, dx2, scale, g_norm)


def _gate_fwd(y_sb, y_dl, proj, g_sb, g_dl):
    t, e = y_sb.shape
    n_heads = e // HEAD_DIM
    tt = _tile(t, 256)

    def body(ys_ref, yd_ref, zs_ref, zd_ref, gs_ref, gd_ref, o_ref):
        for grp, (y_ref, z_ref, g_ref) in enumerate(((ys_ref, zs_ref, gs_ref), (yd_ref, zd_ref, gd_ref))):
            for h in range(n_heads):
                sl = slice(h * HEAD_DIM, (h + 1) * HEAD_DIM)
                y = y_ref[:, sl]
                r = lax.rsqrt(jnp.mean(y * y, axis=-1, keepdims=True) + EPS)
                yn = (y * r) * g_ref[:, sl]
                z = z_ref[:, sl].astype(F32)
                o_ref[:, grp * e + h * HEAD_DIM:grp * e + (h + 1) * HEAD_DIM] = (yn * _silu(z)).astype(BF16)

    yblk = pl.BlockSpec((tt, e), lambda i: (i, 0))
    gblk = pl.BlockSpec((1, e), lambda i: (0, 0))
    return pl.pallas_call(
        body, name="gate_fwd", grid=(t // tt,),
        in_specs=[yblk, yblk, pl.BlockSpec((tt, e), lambda i: (i, 3)), pl.BlockSpec((tt, e), lambda i: (i, 7)),
                  gblk, gblk],
        out_specs=pl.BlockSpec((tt, 2 * e), lambda i: (i, 0)),
        out_shape=jax.ShapeDtypeStruct((t, 2 * e), BF16),
        compiler_params=_params("parallel"),
    )(y_sb, y_dl, proj, proj, g_sb, g_dl)


def _gate_bwd(dyg, y_sb, y_dl, proj, g_sb, g_dl):
    t, e = y_sb.shape
    n_heads = e // HEAD_DIM
    tt = _tile(t, 256)

    def body(dg_ref, ys_ref, yd_ref, zs_ref, zd_ref, gs_ref, gd_ref,
             dys_ref, dyd_ref, dzs_ref, dzd_ref, ggs_ref, ggd_ref):
        @pl.when(pl.program_id(0) == 0)
        def _():
            ggs_ref[...] = jnp.zeros_like(ggs_ref)
            ggd_ref[...] = jnp.zeros_like(ggd_ref)

        groups = ((ys_ref, zs_ref, gs_ref, dys_ref, dzs_ref, ggs_ref), (yd_ref, zd_ref, gd_ref, dyd_ref, dzd_ref, ggd_ref))
        for grp, (y_ref, z_ref, g_ref, dy_ref, dz_ref, gg_ref) in enumerate(groups):
            for h in range(n_heads):
                sl = slice(h * HEAD_DIM, (h + 1) * HEAD_DIM)
                dg = dg_ref[:, grp * e + h * HEAD_DIM:grp * e + (h + 1) * HEAD_DIM].astype(F32)
                y = y_ref[:, sl]
                z = z_ref[:, sl].astype(F32)
                g = g_ref[:, sl]
                r = lax.rsqrt(jnp.mean(y * y, axis=-1, keepdims=True) + EPS)
                yh = y * r
                sig = jax.nn.sigmoid(z)
                dyn = dg * (z * sig)
                dz_ref[:, sl] = (dg * (yh * g) * (sig * (1.0 + z * (1.0 - sig)))).astype(BF16)
                gg_ref[:, sl] += jnp.sum(dyn * yh, axis=0, keepdims=True)
                dyh = dyn * g
                dy_ref[:, sl] = (r * (dyh - yh * jnp.mean(dyh * yh, axis=-1, keepdims=True))).astype(BF16)

    yblk = pl.BlockSpec((tt, e), lambda i: (i, 0))
    gblk = pl.BlockSpec((1, e), lambda i: (0, 0))
    act = jax.ShapeDtypeStruct((t, e), BF16)
    vec = jax.ShapeDtypeStruct((1, e), F32)
    return pl.pallas_call(
        body, name="gate_bwd", grid=(t // tt,),
        in_specs=[pl.BlockSpec((tt, 2 * e), lambda i: (i, 0)), yblk, yblk,
                  pl.BlockSpec((tt, e), lambda i: (i, 3)), pl.BlockSpec((tt, e), lambda i: (i, 7)), gblk, gblk],
        out_specs=(yblk, yblk, yblk, yblk, gblk, gblk),
        out_shape=(act, act, act, act, vec, vec),
        compiler_params=_params("arbitrary"),
    )(dyg, y_sb, y_dl, proj, proj, g_sb, g_dl)


ATT_TQ = 256
HEADS_PER_STEP = 4
SOFTPLUS_CLAMP = 30.0


def _split2_dot(x, u):
    hi = x.astype(BF16)
    lo = (x - hi.astype(F32)).astype(BF16)
    n = x.shape[0]
    both = jnp.dot(jnp.concatenate([hi, lo], axis=0), u, preferred_element_type=F32)
    return both[:n] + both[n:]


def _iota2(n):
    return lax.broadcasted_iota(jnp.int32, (n, n), 0), lax.broadcasted_iota(jnp.int32, (n, n), 1)


def _head_slices():
    return [slice(hh * HEAD_DIM, (hh + 1) * HEAD_DIM) for hh in range(HEADS_PER_STEP)]


def _att_specs(s, e, tq, col0):
    n_heads = e // HEAD_DIM
    hp = HEADS_PER_STEP
    assert n_heads % hp == 0 and col0 % hp == 0
    w = hp * HEAD_DIM
    q_spec = pl.BlockSpec((None, tq, w), lambda i, h, j: (i, j, col0 // hp + h))
    k_spec = pl.BlockSpec((None, s, w), lambda i, h, j: (i, 0, (col0 + n_heads) // hp + h))
    v_spec = pl.BlockSpec((None, s, w), lambda i, h, j: (i, 0, (col0 + 2 * n_heads) // hp + h))
    return q_spec, k_spec, v_spec


def _sb_fwd(proj3, e):
    b, s, _ = proj3.shape
    n_heads = e // HEAD_DIM
    hp = HEADS_PER_STEP
    tq = _tile(s, ATT_TQ)
    nq = s // tq
    inv = 1.0 / math.sqrt(HEAD_DIM)

    def body(q_ref, k_ref, v_ref, y_ref, tot_ref, acc_ref, car_ref):
        i = pl.program_id(2)
        row, col = _iota2(tq)
        before = row > col
        u_after = before.astype(BF16)
        acc_ref[...] = jnp.zeros_like(acc_ref)
        car_ref[...] = jnp.zeros_like(car_ref)

        def block(j, diagonal):
            keys = pl.ds(pl.multiple_of(j * tq, tq), tq)
            heads = list(enumerate(_head_slices()))
            zs = [lax.dot_general(q_ref[:, hs], k_ref[keys, hs], NT_DIMS, preferred_element_type=F32) * inv
                  for _, hs in heads]
            sps = [jnp.maximum(jnp.log(1.0 + jnp.exp(jnp.minimum(z, SOFTPLUS_CLAMP))), z) for z in zs]
            loms = [jnp.where(before, -sp, 0.0) if diagonal else -sp for sp in sps]
            sufs = [_split2_dot(loms[hh], u_after) + car_ref[hh] for hh, _ in heads]
            avs = [jnp.exp((zs[hh] - sps[hh]) + sufs[hh]) for hh, _ in heads]
            if diagonal:
                avs = [jnp.where(before, a, 0.0) for a in avs]
            pvs = [jnp.dot(avs[hh].astype(BF16), v_ref[keys, hs], preferred_element_type=F32) for hh, hs in heads]
            for hh, _ in heads:
                acc_ref[hh] += pvs[hh]
                car_ref[hh] += jnp.sum(loms[hh], axis=1, keepdims=True)

        block(i, True)

        def step(it, carry):
            block(i - it, False)
            return carry

        lax.fori_loop(1, i + 1, step, 0)
        for hh, hs in enumerate(_head_slices()):
            y_ref[:, hs] = acc_ref[hh]
            tot_ref[:, hs] = jnp.broadcast_to(car_ref[hh], (tq, HEAD_DIM))

    q_spec, k_spec, v_spec = _att_specs(s, e, tq, 0)
    blk_q = pl.BlockSpec((None, tq, hp * HEAD_DIM), lambda i, h, j: (i, j, h))
    shp = jax.ShapeDtypeStruct((b, s, e), F32)
    return pl.pallas_call(
        body, name="sb_fwd", grid=(b, n_heads // hp, nq),
        in_specs=[q_spec, k_spec, v_spec],
        out_specs=(blk_q, blk_q), out_shape=(shp, shp),
        scratch_shapes=[pltpu.VMEM((hp, tq, HEAD_DIM), F32), pltpu.VMEM((hp, tq, 1), F32)],
        compiler_params=_params("parallel", "parallel", "arbitrary"),
    )(proj3, proj3, proj3)


def _sb_bwd(proj3, lom_total, dy, partials):
    b, s, e = dy.shape
    n_heads = e // HEAD_DIM
    hp = HEADS_PER_STEP
    tq = _tile(s, ATT_TQ)
    nq = s // tq
    inv = 1.0 / math.sqrt(HEAD_DIM)
    grid = (b, n_heads // hp, nq)

    def body(q_ref, k_ref, v_ref, tot_ref, dy_ref, p_ref, dq_ref, dk_ref, dv_ref, r_ref,
             dqa, dka, dva, car, car2, send_sems, recv_sems, local_sem):
        i = pl.program_id(2)
        first, last = _grid_first_last(grid)

        @pl.when(first)
        def _():
            for cp in _all_to_all_copies(p_ref, r_ref, send_sems, recv_sems, local_sem):
                cp.start()

        @pl.when(i == 0)
        def _():
            dka[...] = jnp.zeros_like(dka)
            dva[...] = jnp.zeros_like(dva)

        row, col = _iota2(tq)
        before = row > col
        u_upto = (row <= col).astype(BF16)
        u_before = (row < col).astype(BF16)
        dqa[...] = jnp.zeros_like(dqa)
        car[...] = jnp.zeros_like(car)
        car2[...] = jnp.zeros_like(car2)

        def block(j, diagonal):
            keys = pl.ds(pl.multiple_of(j * tq, tq), tq)
            heads = list(enumerate(_head_slices()))
            zs = [lax.dot_general(q_ref[:, hs], k_ref[keys, hs], NT_DIMS, preferred_element_type=F32) * inv
                  for _, hs in heads]
            das = [lax.dot_general(dy_ref[:, hs], v_ref[keys, hs], NT_DIMS, preferred_element_type=F32) for _, hs in heads]
            ezs = [jnp.exp(jnp.minimum(z, SOFTPLUS_CLAMP)) for z in zs]
            sps = [jnp.maximum(jnp.log(1.0 + ezs[hh]), zs[hh]) for hh, _ in heads]
            loms = [jnp.where(before, -sp, 0.0) if diagonal else -sp for sp in sps]
            sufs = [tot_ref[:, hh * HEAD_DIM:hh * HEAD_DIM + 1] - (_split2_dot(loms[hh], u_upto) + car[hh])
                    for hh, _ in heads]
            avs = [jnp.exp((zs[hh] - sps[hh]) + sufs[hh]) for hh, _ in heads]
            if diagonal:
                avs = [jnp.where(before, a, 0.0) for a in avs]
            dls = [avs[hh] * das[hh] for hh, _ in heads]
            prefixes = [_split2_dot(dls[hh], u_before) + car2[hh] for hh, _ in heads]
            dzs = []
            for hh, _ in heads:
                one_minus_beta = 1.0 / (1.0 + ezs[hh])
                dz = (dls[hh] * one_minus_beta - prefixes[hh] * (ezs[hh] * one_minus_beta)) * inv
                if diagonal:
                    dz = jnp.where(before, dz, 0.0)
                dzs.append(dz.astype(BF16))
            dqs = [jnp.dot(dzs[hh], k_ref[keys, hs], preferred_element_type=F32) for hh, hs in heads]
            dks = [lax.dot_general(dzs[hh], q_ref[:, hs], TN_DIMS, preferred_element_type=F32) for hh, hs in heads]
            dvs = [lax.dot_general(avs[hh].astype(BF16), dy_ref[:, hs], TN_DIMS, preferred_element_type=F32)
                   for hh, hs in heads]
            for hh, hs in heads:
                dqa[hh] += dqs[hh]
                dka[keys, hs] += dks[hh]
                dva[keys, hs] += dvs[hh]
                car[hh] += jnp.sum(loms[hh], axis=1, keepdims=True)
                car2[hh] += jnp.sum(dls[hh], axis=1, keepdims=True)

        def step(j, carry):
            block(j, False)
            return carry

        lax.fori_loop(0, i, step, 0)
        block(i, True)
        for hh, hs in enumerate(_head_slices()):
            dq_ref[:, hs] = dqa[hh].astype(BF16)

        @pl.when(i == nq - 1)
        def _():
            dk_ref[...] = dka[...].astype(BF16)
            dv_ref[...] = dva[...].astype(BF16)

        @pl.when(last)
        def _():
            for cp in _all_to_all_copies(p_ref, r_ref, send_sems, recv_sems, local_sem):
                cp.wait()

    q_spec, k_spec, v_spec = _att_specs(s, e, tq, 0)
    w = hp * HEAD_DIM
    blk_q = pl.BlockSpec((None, tq, w), lambda i, h, j: (i, j, h))
    blk_kv = pl.BlockSpec((None, s, w), lambda i, h, j: (i, 0, h))
    shp = jax.ShapeDtypeStruct((b, s, e), BF16)
    return pl.pallas_call(
        body, name="sb_bwd", grid=grid,
        in_specs=[q_spec, k_spec, v_spec, blk_q, blk_q, ANY_SPEC],
        out_specs=(blk_q, blk_kv, blk_kv, ANY_SPEC),
        out_shape=(shp, shp, shp, jax.ShapeDtypeStruct(partials.shape, partials.dtype)),
        scratch_shapes=[pltpu.VMEM((hp, tq, HEAD_DIM), F32), pltpu.VMEM((s, w), F32), pltpu.VMEM((s, w), F32),
                        pltpu.VMEM((hp, tq, 1), F32), pltpu.VMEM((hp, tq, 1), F32),
                        pltpu.SemaphoreType.DMA((N_DEV - 1,)), pltpu.SemaphoreType.DMA((N_DEV - 1,)),
                        pltpu.SemaphoreType.DMA],
        compiler_params=_params("arbitrary", "arbitrary", "arbitrary"),
    )(proj3, proj3, proj3, lom_total, dy, partials)


def _dil_near_tiles(tq):
    return (DIL_PAIRS[1][0] + tq - 1) // tq + 1


def _dil_fill_bias(bias_ref, sl_ref, tq):
    row, col = _iota2(tq)
    for hh in range(HEADS_PER_STEP):
        slope = sl_ref[hh, 0:1, 0:1]
        for d in range(_dil_near_tiles(tq) + 1):
            dist = d * tq + row - col
            cnt = jnp.zeros(dist.shape, jnp.int32)
            for window, dilation in DIL_PAIRS:
                cnt = cnt + (((dist & (dilation - 1)) == 0) & (dist <= window)).astype(jnp.int32)
            bias = jnp.where(cnt == 3, math.log(3.0), jnp.where(cnt == 2, math.log(2.0), 0.0))
            bias_ref[hh, d] = jnp.where((dist >= 0) & (cnt > 0), bias - slope * dist.astype(F32), NEG)


def _dil_scores(q, k, bias_ref, hh, slope, it, tq, inv):
    near = _dil_near_tiles(tq)
    beyond = jnp.maximum(it - near, 0).astype(F32) * float(tq)
    sc = lax.dot_general(q, k, NT_DIMS, preferred_element_type=F32) * inv
    return (sc + bias_ref[hh, jnp.minimum(it, near)]) - slope * beyond


def _dil_fwd(proj3, e, slopes):
    b, s, _ = proj3.shape
    n_heads = e // HEAD_DIM
    hp = HEADS_PER_STEP
    tq = _tile(s, ATT_TQ)
    nq = s // tq
    inv = 1.0 / math.sqrt(HEAD_DIM)
    assert s <= DIL_PAIRS[2][0]

    def body(q_ref, k_ref, v_ref, sl_ref, y_ref, lse_ref, acc_ref, m_ref, l_ref, bias_ref):
        i = pl.program_id(2)

        @pl.when(i == 0)
        def _():
            _dil_fill_bias(bias_ref, sl_ref, tq)

        acc_ref[...] = jnp.zeros_like(acc_ref)
        m_ref[...] = jnp.full_like(m_ref, NEG)
        l_ref[...] = jnp.zeros_like(l_ref)

        def step(it, carry):
            keys = pl.ds(pl.multiple_of((i - it) * tq, tq), tq)
            heads = list(enumerate(_head_slices()))
            scs = [_dil_scores(q_ref[:, hs], k_ref[keys, hs], bias_ref, hh, sl_ref[hh, 0:1, 0:1], it, tq, inv)
                   for hh, hs in heads]
            m_old = [m_ref[hh] for hh, _ in heads]
            m_new = [jnp.maximum(m_old[hh], jnp.max(scs[hh], axis=1, keepdims=True)) for hh, _ in heads]
            ps = [jnp.exp(scs[hh] - m_new[hh]) for hh, _ in heads]
            pvs = [jnp.dot(ps[hh].astype(BF16), v_ref[keys, hs], preferred_element_type=F32) for hh, hs in heads]
            for hh, _ in heads:
                alpha = jnp.exp(m_old[hh] - m_new[hh])
                l_ref[hh] = alpha * l_ref[hh] + (ps[hh][:, :tq // 2] + ps[hh][:, tq // 2:])
                acc_ref[hh] = alpha * acc_ref[hh] + pvs[hh]
                m_ref[hh] = m_new[hh]
            return carry

        lax.fori_loop(0, i + 1, step, 0)
        for hh, hs in enumerate(_head_slices()):
            l = jnp.sum(l_ref[hh], axis=1, keepdims=True)
            y_ref[:, hs] = acc_ref[hh] / l
            lse_ref[:, hs] = jnp.broadcast_to(m_ref[hh] + jnp.log(l), (tq, HEAD_DIM))

    q_spec, k_spec, v_spec = _att_specs(s, e, tq, 4 * n_heads)
    blk_q = pl.BlockSpec((None, tq, hp * HEAD_DIM), lambda i, h, j: (i, j, h))
    shp = jax.ShapeDtypeStruct((b, s, e), F32)
    return pl.pallas_call(
        body, name="dil_fwd", grid=(b, n_heads // hp, nq),
        in_specs=[q_spec, k_spec, v_spec, pl.BlockSpec((hp, 8, HEAD_DIM), lambda i, h, j: (h, 0, 0))],
        out_specs=(blk_q, blk_q), out_shape=(shp, shp),
        scratch_shapes=[pltpu.VMEM((hp, tq, HEAD_DIM), F32), pltpu.VMEM((hp, tq, 1), F32), pltpu.VMEM((hp, tq, tq // 2), F32),
                        pltpu.VMEM((hp, _dil_near_tiles(tq) + 1, tq, tq), F32)],
        compiler_params=_params("parallel", "parallel", "arbitrary"),
    )(proj3, proj3, proj3, slopes)


def _dil_bwd(proj3, y, lse, dy, slopes):
    b, s, e = y.shape
    n_heads = e // HEAD_DIM
    hp = HEADS_PER_STEP
    tq = _tile(s, ATT_TQ)
    nq = s // tq
    inv = 1.0 / math.sqrt(HEAD_DIM)

    def body(q_ref, k_ref, v_ref, sl_ref, y_ref, lse_ref, dy_ref, dq_ref, dk_ref, dv_ref, dqa, dka, dva, bias_ref):
        i = pl.program_id(2)

        @pl.when(i == 0)
        def _():
            dka[...] = jnp.zeros_like(dka)
            dva[...] = jnp.zeros_like(dva)
            _dil_fill_bias(bias_ref, sl_ref, tq)

        delta = [jnp.sum(dy_ref[:, hs].astype(F32) * y_ref[:, hs], axis=1, keepdims=True) for hs in _head_slices()]
        dqa[...] = jnp.zeros_like(dqa)

        def step(it, carry):
            keys = pl.ds(pl.multiple_of((i - it) * tq, tq), tq)
            heads = list(enumerate(_head_slices()))
            scs = [_dil_scores(q_ref[:, hs], k_ref[keys, hs], bias_ref, hh, sl_ref[hh, 0:1, 0:1], it, tq, inv)
                   for hh, hs in heads]
            dps = [lax.dot_general(dy_ref[:, hs], v_ref[keys, hs], NT_DIMS, preferred_element_type=F32) for _, hs in heads]
            ps = [jnp.exp(scs[hh] - lse_ref[:, hh * HEAD_DIM:hh * HEAD_DIM + 1]) for hh, _ in heads]
            dss = [((ps[hh] * (dps[hh] - delta[hh])) * inv).astype(BF16) for hh, _ in heads]
            dqs = [jnp.dot(dss[hh], k_ref[keys, hs], preferred_element_type=F32) for hh, hs in heads]
            dks = [lax.dot_general(dss[hh], q_ref[:, hs], TN_DIMS, preferred_element_type=F32) for hh, hs in heads]
            dvs = [lax.dot_general(ps[hh].astype(BF16), dy_ref[:, hs], TN_DIMS, preferred_element_type=F32)
                   for hh, hs in heads]
            for hh, hs in heads:
                dqa[hh] += dqs[hh]
                dka[keys, hs] += dks[hh]
                dva[keys, hs] += dvs[hh]
            return carry

        lax.fori_loop(0, i + 1, step, 0)
        for hh, hs in enumerate(_head_slices()):
            dq_ref[:, hs] = dqa[hh].astype(BF16)

        @pl.when(i == nq - 1)
        def _():
            dk_ref[...] = dka[...].astype(BF16)
            dv_ref[...] = dva[...].astype(BF16)

    q_spec, k_spec, v_spec = _att_specs(s, e, tq, 4 * n_heads)
    w = hp * HEAD_DIM
    blk_q = pl.BlockSpec((None, tq, w), lambda i, h, j: (i, j, h))
    blk_kv = pl.BlockSpec((None, s, w), lambda i, h, j: (i, 0, h))
    shp = jax.ShapeDtypeStruct((b, s, e), BF16)
    return pl.pallas_call(
        body, name="dil_bwd", grid=(b, n_heads // hp, nq),
        in_specs=[q_spec, k_spec, v_spec, pl.BlockSpec((hp, 8, HEAD_DIM), lambda i, h, j: (h, 0, 0)),
                  blk_q, blk_q, blk_q],
        out_specs=(blk_q, blk_kv, blk_kv), out_shape=(shp, shp, shp),
        scratch_shapes=[pltpu.VMEM((hp, tq, HEAD_DIM), F32), pltpu.VMEM((s, w), F32), pltpu.VMEM((s, w), F32),
                        pltpu.VMEM((hp, _dil_near_tiles(tq) + 1, tq, tq), F32)],
        compiler_params=_params("parallel", "parallel", "arbitrary"),
    )(proj3, proj3, proj3, slopes, y, lse, dy)


def kernel(x, c, w_ada, b_ada, g_norm, w_in, g_sb, g_dil, w_out, g_final, loss_target, m_w_ada, m_b_ada, m_g_norm, m_w_in, m_g_sb, m_g_dil, m_w_out, m_g_final, v_w_ada, v_b_ada, v_g_norm, v_w_in, v_g_sb, v_g_dil, v_w_out, v_g_final):
    b, s, d = x.shape
    t = b * s
    e = w_in.shape[2]
    n_heads = e // HEAD_DIM
    na = w_ada.shape[2]
    r_out = w_out.shape[1]
    assert g_sb.shape[1] == e and g_dil.shape[1] == e and N_DEV * r_out == 2 * e and N_DEV * na == 3 * d
    assert b <= SMALL_ROWS and 3 * b + 3 <= 2 * SMALL_ROWS
    ix, iy, ic = _mesh_pos()
    me = 4 * ix + 2 * iy + ic

    c_all = _allgather_rows(jnp.pad(c, ((0, SMALL_ROWS - b), (0, 0))), "ag_c")
    b_own = lax.dynamic_slice(b_ada, (0, me * na), (1, na))
    mod_cols = _ada_fwd(c_all, w_ada[0], b_own)
    mod_all = _allgather_rows(mod_cols, "ag_mod").reshape(N_DEV, N_DEV, SMALL_ROWS, na)
    mod_own = lax.dynamic_slice(mod_all, (0, me, 0, 0), (N_DEV, 1, b, na))[:, 0]
    mod = mod_own.transpose(1, 0, 2).reshape(b, 1, 3 * d)
    shift, scale, gate = mod[:, :, :d], mod[:, :, d:2 * d], mod[:, :, 2 * d:]

    w_own =w_in[0].astype(BF16).reshape(d, PROJ_HALVES, e // PROJ_HALVES).transpose(1, 0, 2)

    h = _norm_mod(x, g_norm, scale, shift).reshape(t, d)
    proj, w_in3, w_out3 = _proj_with_allgather(
        h, w_own, jnp.stack(_unit_ids()).astype(jnp.int32), w_out[0].astype(BF16))
    w_out1 = w_out3.reshape(1, N_DEV * r_out, d)
    proj3 = proj.reshape(b, s, N_DEV * e)
    slopes = jnp.exp2(-ALIBI_MAX_BIAS * jnp.arange(1, n_heads + 1, dtype=F32) / n_heads)
    slopes = jnp.broadcast_to(slopes[:, None, None], (n_heads, 8, HEAD_DIM))
    y_sb, lom_total = _sb_fwd(proj3, e)
    y_dl, lse = _dil_fwd(proj3, e, slopes)
    yg = _gate_fwd(y_sb.reshape(t, e), y_dl.reshape(t, e), proj, g_sb, g_dil)
    out = _mm_nn(yg, w_out1, F32, "mm_out").reshape(b, s, d)
    loss_p, dx2, d_out, dgate, gg_final = _final_fwd_bwd(x, out, gate, g_final.reshape(1, d), loss_target)

    d_out2 = d_out.reshape(t, d)
    dyg = _mm_nt(d_out2, w_out1, BF16, "mm_dy")
    gw_out_p = _mm_tn(yg, d_out2, 1, BF16, "mm_gw_out").reshape(N_DEV, r_out, d)
    dy_sb, dy_dl, dz_sb, dz_dl, gg_sb, gg_dl = _gate_bwd(dyg, y_sb.reshape(t, e), y_dl.reshape(t, e), proj, g_sb, g_dil)
    dq_sb, dk_sb, dv_sb, recv_out = _sb_bwd(proj3, lom_total, dy_sb.reshape(b, s, e), gw_out_p)
    dq_dl, dk_dl, dv_dl = _dil_bwd(proj3, y_dl, lse, dy_dl.reshape(b, s, e), slopes)
    dproj = jnp.concatenate(
        [a.reshape(t, e) for a in (dq_sb, dk_sb, dv_sb, dz_sb, dq_dl, dk_dl, dv_dl, dz_dl)], axis=1)
    gw_in_p = _mm_tn(h, dproj, N_DEV, BF16, "mm_gw_in")
    gw_in_sib = _swap_with_sibling(gw_in_p)
    chip_blocks = jnp.stack([4 * px + 2 * py + ic for px, py in _other_chips()]).astype(jnp.int32)
    gw_in_send = _chip_presum(chip_blocks, gw_in_p, gw_in_sib)
    dh, gw_in_recv = _mm_nt_with_chip_exchange(dproj, w_in3, F32, gw_in_send, "mm_dh")
    dh = dh.reshape(b, s, d)
    grad_x, dshift, dscale, gg_norm = _norm_bwd(x, dh, dx2, scale, g_norm)

    dmod = jnp.concatenate([dshift, dscale, dgate], axis=1).reshape(3 * b, d)
    pkg = jnp.concatenate([dmod, gg_norm, gg_final, jnp.concatenate([gg_sb, gg_dl], axis=1),
                           jnp.zeros((2 * SMALL_ROWS - 3 * b - 3, d), F32)], axis=0)
    pkg_all = _allgather_rows(pkg, "ag_small_grads").reshape(N_DEV, 2 * SMALL_ROWS, d)
    dmod_all = pkg_all[:, :3 * b].reshape(N_DEV * b, 3 * d)
    dmod_cols = lax.dynamic_slice(dmod_all, (0, me * na), (N_DEV * b, na))
    c_rows = c_all.reshape(N_DEV, SMALL_ROWS, d)[:, :b].reshape(N_DEV * b, d)
    g_w_ada, d_w_ada, nm_w_ada, nv_w_ada = _ada_bwd_adam(c_rows, dmod_cols, w_ada[0], m_w_ada[0], v_w_ada[0])

    def pack(b_ada_like, g_norm_like, g_sb_like, g_dil_like, g_final_like):
        return jnp.concatenate([b_ada_like.reshape(3, d), g_norm_like.reshape(1, d), g_final_like.reshape(1, d),
                                jnp.concatenate([g_sb_like, g_dil_like], axis=1).reshape(1, d),
                                jnp.zeros((2, d), F32)], axis=0)

    small = _small_adam(pkg_all, 3 * b, pack(b_ada, g_norm, g_sb, g_dil, g_final),
                        pack(m_b_ada, m_g_norm, m_g_sb, m_g_dil, m_g_final),
                        pack(v_b_ada, v_g_norm, v_g_sb, v_g_dil, v_g_final))

    def unpack(p):
        return (p[0:3].reshape(1, 3 * d), p[3:4], p[5:6, :e], p[5:6, e:], p[4])

    sm_g, sm_d, sm_m, sm_v = (unpack(p) for p in small)

    g_w_in, d_w_in, nm_w_in, nv_w_in = _adam_from_chip_sums(
        jnp.reshape(me, (1,)).astype(jnp.int32), gw_in_p, gw_in_sib, gw_in_recv, w_in[0], m_w_in[0], v_w_in[0], "adam_w_in")
    g_w_out, d_w_out, nm_w_out, nv_w_out = _adam_from_partials(recv_out, w_out[0], m_w_out[0], v_w_out[0], "adam_w_out")

    loss = lax.psum(loss_p[0, 0], ("x", "y", "c"))

    def weights(ada, small_parts, w_in_part, w_out_part):
        b_ada_p, g_norm_p, g_sb_p, g_dil_p, g_final_p = small_parts
        return (ada[None], b_ada_p, g_norm_p, w_in_part[None], g_sb_p, g_dil_p, w_out_part[None], g_final_p)

    return (loss, grad_x,
            *weights(g_w_ada, sm_g, g_w_in, g_w_out),
            *weights(d_w_ada, sm_d, d_w_in, d_w_out),
            *weights(nm_w_ada, sm_m, nm_w_in, nm_w_out),
            *weights(nv_w_ada, sm_v, nv_w_in, nv_w_out))
```

```python
import functools
import math

import jax
import jax.numpy as jnp
from jax import lax
from jax.experimental import pallas as pl
from jax.experimental.pallas import tpu as pltpu

F32 = jnp.float32
BF16 = jnp.bfloat16
MESH = pl.DeviceIdType.MESH

N_DEV = 8
HEAD_DIM = 128
EPS = 1e-6
ALIBI_MAX_BIAS = 8.0
DIL_PAIRS = ((128, 1), (512, 4), (2048, 16))
DIL_STEPS = 128
NEG = -1e30

ADAM_LR = 0.001
ADAM_B1 = 0.9
ADAM_B2 = 0.999
ADAM_EPS = 1e-08
ADAM_WD = 0.01
ADAM_STEP = 10

VMEM_LIMIT_BYTES = 56 * 1024 * 1024
SMALL_ROWS = 8

NT_DIMS = (((1,), (1,)), ((), ()))
TN_DIMS = (((0,), (0,)), ((), ()))


def _params(*semantics):
    return pltpu.CompilerParams(dimension_semantics=semantics, vmem_limit_bytes=VMEM_LIMIT_BYTES)


def _tile(n, want):
    t = min(n, want)
    assert n % t == 0, (n, want)
    return t


def _mesh_pos():
    return lax.axis_index("x"), lax.axis_index("y"), lax.axis_index("c")


def _allgather_rows(x_shard, name):
    m_per, n = x_shard.shape

    def body(x_ref, out_ref, send_sems, recv_sems, local_sem):
        x, y, c = _mesh_pos()
        me, sibling = (x, y, c), (x, y, 1 - c)
        chips = [(1 - x, y), (x, 1 - y), (1 - x, 1 - y)]

        def rows(px, py, pc):
            return out_ref.at[pl.ds((4 * px + 2 * py + pc) * m_per, m_per), :]

        def copy(k, block, to, src=None):
            return pltpu.make_async_remote_copy(
                src_ref=rows(*block) if src is None else src, dst_ref=rows(*block),
                send_sem=send_sems.at[k], recv_sem=recv_sems.at[k], device_id=to, device_id_type=MESH)

        mine = pltpu.make_async_copy(x_ref, rows(*me), local_sem)
        mine.start()
        first = [copy(0, me, sibling, src=x_ref)]
        first += [copy(1 + j, me, (*chip, c), src=x_ref) for j, chip in enumerate(chips)]
        for cp in first:
            cp.start()
        passed = [copy(4 + j, (*chip, c), sibling) for j, chip in enumerate(chips)]
        for j, chip in enumerate(chips):
            copy(1 + j, (*chip, c), me).wait_recv()
            passed[j].start()
        copy(0, sibling, me).wait_recv()
        for j, chip in enumerate(chips):
            copy(4 + j, (*chip, 1 - c), me).wait_recv()
        for cp in first + passed:
            cp.wait_send()
        mine.wait()

    return pl.pallas_call(
        body, name=name,
        out_shape=jax.ShapeDtypeStruct((N_DEV * m_per, n), x_shard.dtype),
        in_specs=[pl.BlockSpec(memory_space=pltpu.VMEM)],
        out_specs=pl.BlockSpec(memory_space=pltpu.VMEM),
        scratch_shapes=[pltpu.SemaphoreType.DMA((7,)), pltpu.SemaphoreType.DMA((7,)), pltpu.SemaphoreType.DMA],
    )(x_shard)


def _two_level_gather(w_ref, out_ref, send_sems, recv_sems, local_sem):
    x, y, c = _mesh_pos()
    me, sibling = (x, y, c), (x, y, 1 - c)
    chips = [(1 - x, y), (x, 1 - y), (1 - x, 1 - y)]

    def copy(k, block, to, src=None):
        dst = out_ref.at[4 * block[0] + 2 * block[1] + block[2]]
        return pltpu.make_async_remote_copy(
            src_ref=dst if src is None else src, dst_ref=dst,
            send_sem=send_sems.at[k], recv_sem=recv_sems.at[k], device_id=to, device_id_type=MESH)

    mine = pltpu.make_async_copy(w_ref, out_ref.at[4 * x + 2 * y + c], local_sem)
    to_sibling = copy(0, me, sibling, src=w_ref)
    to_chips = [copy(1 + j, me, (*chip, c), src=w_ref) for j, chip in enumerate(chips)]
    passed = [copy(4 + j, (*chip, c), sibling) for j, chip in enumerate(chips)]

    def early():
        mine.start()
        to_sibling.start()

    def ici():
        for cp in to_chips:
            cp.start()

    def finish():
        for j, chip in enumerate(chips):
            copy(1 + j, (*chip, c), me).wait_recv()
            passed[j].start()
        copy(0, sibling, me).wait_recv()
        for j, chip in enumerate(chips):
            copy(4 + j, (*chip, 1 - c), me).wait_recv()
        for cp in [to_sibling] + to_chips + passed:
            cp.wait_send()
        mine.wait()

    return early, ici, finish


PROJ_HALVES = 2


def _unit_schedule():
    assert PROJ_HALVES == 2
    sched = [("own", None, hf) for hf in range(PROJ_HALVES)] + [("sib", None, hf) for hf in range(PROJ_HALVES)]
    for rnd in ([(0, 0), (1, 1)], [(2, 0), (2, 1)], [(0, 1), (1, 0)]):
        sched += [("direct", j, hf) for j, hf in rnd] + [("fwd", j, hf) for j, hf in rnd]
    return sched


def _unit_ids():
    x, y, c = _mesh_pos()
    chips = _other_chips()
    ids = []
    for kind, j, hf in _unit_schedule():
        px, py = (x, y) if j is None else chips[j]
        pc = c if kind in ("own", "direct") else 1 - c
        ids.append(PROJ_HALVES * (4 * px + 2 * py + pc) + hf)
    return ids


def _proj_with_allgather(h, w_own, order, w_later):
    t, d = h.shape
    nh, _, u = w_own.shape
    assert nh == PROJ_HALVES
    sched = _unit_schedule()
    n_units = len(sched)
    pos = {entry: p for p, entry in enumerate(sched)}
    tm = _tile(t, MM_TM)
    m_tiles = t // tm
    prep_m = max(m_tiles - 2, 0)
    grid = (n_units, m_tiles)

    later_ici_pos = pos[("direct", 2, 0)]

    def body(order_ref, h_ref, wown_ref, later_ref, proj_ref, w3_ref, later3_ref, bbuf, bsems, send_sems, recv_sems,
             local_sems, later_send_sems, later_recv_sems, later_local_sem):
        n, m = pl.program_id(0), pl.program_id(1)
        x, y, c = _mesh_pos()
        chips = _other_chips()
        sibling = (x, y, 1 - c)
        later_early, later_ici, later_finish = _two_level_gather(
            later_ref, later3_ref, later_send_sems, later_recv_sems, later_local_sem)

        def unit(p):
            return w3_ref.at[order_ref[p]]

        def arrival(p):
            return pltpu.make_async_remote_copy(
                src_ref=unit(p), dst_ref=unit(p), send_sem=send_sems.at[0], recv_sem=recv_sems.at[p - nh],
                device_id=sibling, device_id_type=MESH)

        def send(k, src, p_here, p_there, to):
            return pltpu.make_async_remote_copy(
                src_ref=src, dst_ref=unit(p_here), send_sem=send_sems.at[k], recv_sem=recv_sems.at[p_there - nh],
                device_id=to, device_id_type=MESH)

        copies = []

        def out(src, p_here, p_there, to):
            copies.append(send(len(copies), src, p_here, p_there, to))
            return copies[-1]

        def own(hf, p_there, to):
            return out(wown_ref.at[hf], pos[("own", None, hf)], p_there, to)

        x_nbr, y_nbr = (*chips[0], c), (*chips[1], c)
        first = [own(hf, pos[("sib", None, hf)], sibling) for hf in range(nh)]
        first += [own(0, pos[("direct", 0, 0)], x_nbr), own(1, pos[("direct", 1, 1)], y_nbr)]
        after = {}
        for kind, j, hf in sched:
            if kind == "direct":
                p = pos[(kind, j, hf)]
                after[p] = [out(unit(p), p, pos[("fwd", j, hf)], sibling)]
        p = pos[("direct", 0, 0)]
        after[p] += [out(unit(p), p, pos[("direct", 2, 0)], y_nbr), own(0, pos[("direct", 1, 0)], y_nbr)]
        p = pos[("direct", 1, 1)]
        after[p] += [out(unit(p), p, pos[("direct", 2, 1)], x_nbr), own(1, pos[("direct", 0, 1)], x_nbr)]
        locals_ = [pltpu.make_async_copy(wown_ref.at[hf], unit(pos[("own", None, hf)]), local_sems.at[hf])
                   for hf in range(nh)]

        def fetch(p):
            src = wown_ref.at[sched[p][2]] if sched[p][0] == "own" else unit(p)
            return pltpu.make_async_copy(src, bbuf.at[p % 2], bsems.at[p % 2])

        @pl.when((n == 0) & (m == 0))
        def _():
            for cp in locals_ + first:
                cp.start()
            fetch(0).start()
            later_early()

        for p in range(n_units):
            @pl.when((n == p) & (m == 0))
            def _(p=p):
                fetch(p).wait()

            if p + 1 < n_units:
                @pl.when((n == p) & (m == prep_m))
                def _(p=p):
                    if sched[p + 1][0] != "own":
                        arrival(p + 1).wait_recv()
                    for cp in after.get(p + 1, []):
                        cp.start()
                    fetch(p + 1).start()
                    if p + 1 == later_ici_pos:
                        later_ici()

        proj_ref[...] = jnp.dot(h_ref[...], bbuf[n % 2], preferred_element_type=F32).astype(proj_ref.dtype)

        @pl.when((n == n_units - 1) & (m == m_tiles - 1))
        def _():
            for cp in copies:
                cp.wait_send()
            for cp in locals_:
                cp.wait()
            later_finish()

    n_out = len(sched) - nh
    return pl.pallas_call(
        body, name="mm_proj_allgather",
        grid_spec=pltpu.PrefetchScalarGridSpec(
            num_scalar_prefetch=1, grid=grid,
            in_specs=[pl.BlockSpec((tm, d), lambda n, m, order: (m, 0)), ANY_SPEC, ANY_SPEC],
            out_specs=(pl.BlockSpec((tm, u), lambda n, m, order: (m, order[n])), ANY_SPEC, ANY_SPEC),
            scratch_shapes=[pltpu.VMEM((2, d, u), h.dtype), pltpu.SemaphoreType.DMA((2,)),
                            pltpu.SemaphoreType.DMA((n_out,)), pltpu.SemaphoreType.DMA((n_out,)),
                            pltpu.SemaphoreType.DMA((nh,)),
                            pltpu.SemaphoreType.DMA((N_DEV - 1,)), pltpu.SemaphoreType.DMA((N_DEV - 1,)),
                            pltpu.SemaphoreType.DMA]),
        out_shape=(jax.ShapeDtypeStruct((t, n_units * u), h.dtype), jax.ShapeDtypeStruct((n_units, d, u), h.dtype),
                   jax.ShapeDtypeStruct((N_DEV,) + w_later.shape, w_later.dtype)),
        compiler_params=_params("arbitrary", "arbitrary"),
    )(order, h, w_own, w_later)


ANY_SPEC = pl.BlockSpec(memory_space=pl.ANY)


def _grid_first_last(grid):
    ids = [pl.program_id(a) for a in range(len(grid))]
    first = functools.reduce(lambda p, q: p & q, [i == 0 for i in ids])
    last = functools.reduce(lambda p, q: p & q, [i == n - 1 for i, n in zip(ids, grid)])
    return first, last


def _all_to_all_copies(src, dst, send_sems, recv_sems, local_sem):
    x, y, c = _mesh_pos()
    my = 4 * x + 2 * y + c
    copies = [pltpu.make_async_copy(src.at[my], dst.at[my], local_sem)]
    for d in range(1, N_DEV):
        px = 1 - x if d & 4 else x
        py = 1 - y if d & 2 else y
        pc = 1 - c if d & 1 else c
        copies.append(pltpu.make_async_remote_copy(
            src_ref=src.at[4 * px + 2 * py + pc], dst_ref=dst.at[my],
            send_sem=send_sems.at[d - 1], recv_sem=recv_sems.at[d - 1],
            device_id=(px, py, pc), device_id_type=MESH))
    return copies


def _other_chips():
    x, y, _ = _mesh_pos()
    return [(1 - x, y), (x, 1 - y), (1 - x, 1 - y)]


def _same_core_copies(src, dst, send_sems, recv_sems):
    c = lax.axis_index("c")
    return [pltpu.make_async_remote_copy(
        src_ref=src.at[j], dst_ref=dst.at[j], send_sem=send_sems.at[j], recv_sem=recv_sems.at[j],
        device_id=(*chip, c), device_id_type=MESH) for j, chip in enumerate(_other_chips())]


def _sibling_copies(src, dst, send_sems, recv_sems):
    x, y, c = _mesh_pos()
    return [pltpu.make_async_remote_copy(
        src_ref=src.at[j], dst_ref=dst.at[j], send_sem=send_sems.at[j], recv_sem=recv_sems.at[j],
        device_id=(x, y, 1 - c), device_id_type=MESH) for j in range(src.shape[0])]


def _chip_presum(mine, from_sibling):
    _, r, cdim = mine.shape
    tr = _tile(r, 256)

    def body(p_ref, s_ref, o_ref):
        o_ref[...] = (p_ref[...].astype(F32) + s_ref[...].astype(F32)).astype(o_ref.dtype)

    blk = pl.BlockSpec((None, tr, cdim), lambda j, i: (1 + j, i, 0))
    return pl.pallas_call(
        body, name="chip_presum", grid=(3, r // tr), in_specs=[blk, blk],
        out_specs=pl.BlockSpec((None, tr, cdim), lambda j, i: (j, i, 0)),
        out_shape=jax.ShapeDtypeStruct((3, r, cdim), mine.dtype),
        compiler_params=_params("parallel", "parallel"),
    )(mine, from_sibling)


def _mm_call(a, b, dims, nk, grid, a_spec, b_spec, o_spec, out_shape, acc_shape, name):
    def body(a_ref, b_ref, o_ref, acc_ref):
        k = pl.program_id(2)

        @pl.when(k == 0)
        def _():
            acc_ref[...] = jnp.zeros_like(acc_ref)

        acc_ref[...] += lax.dot_general(a_ref[...], b_ref[...], dims, preferred_element_type=F32)

        @pl.when(k == nk - 1)
        def _():
            o_ref[...] = acc_ref[...].astype(o_ref.dtype)

    return pl.pallas_call(
        body, name=name, grid=grid, in_specs=[a_spec, b_spec], out_specs=o_spec, out_shape=out_shape,
        scratch_shapes=[pltpu.VMEM(acc_shape, F32)],
        compiler_params=_params("parallel", "parallel", "arbitrary"),
    )(a, b)


MM_TM, MM_TN, MM_TK = 1024, 2048, 1024


def _mm_nn(a, b3, out_dtype, name):
    m, kk = a.shape
    g, _, nb = b3.shape
    tm, tn, tk = _tile(m, MM_TM), _tile(nb, MM_TN), _tile(kk, MM_TK)
    npb = nb // tn
    return _mm_call(
        a, b3, (((1,), (0,)), ((), ())), kk // tk, (m // tm, g * npb, kk // tk),
        pl.BlockSpec((tm, tk), lambda i, j, k: (i, k)),
        pl.BlockSpec((None, tk, tn), lambda i, j, k: (j // npb, k, j % npb)),
        pl.BlockSpec((tm, tn), lambda i, j, k: (i, j)),
        jax.ShapeDtypeStruct((m, g * nb), out_dtype), (tm, tn), name)


def _mm_nt(a, b3, out_dtype, name):
    m, kk = a.shape
    g, n, kb = b3.shape
    tm, tn, tk = _tile(m, MM_TM), _tile(n, MM_TN), _tile(kb, MM_TK)
    kpb = kb // tk
    return _mm_call(
        a, b3, NT_DIMS, kk // tk, (m // tm, n // tn, kk // tk),
        pl.BlockSpec((tm, tk), lambda i, j, k: (i, k)),
        pl.BlockSpec((None, tn, tk), lambda i, j, k: (k // kpb, j, k % kpb)),
        pl.BlockSpec((tm, tn), lambda i, j, k: (i, j)),
        jax.ShapeDtypeStruct((m, n), out_dtype), (tm, tn), name)


def _mm_nt_with_chip_exchange(a, b3, out_dtype, send3, name):
    m, kk = a.shape
    g, n, kb = b3.shape
    tm, tn, tk = _tile(m, MM_TM), _tile(n, MM_TN), _tile(kb, MM_TK)
    kpb = kb // tk
    grid = (m // tm, n // tn, kk // tk)

    def body(a_ref, b_ref, s_ref, o_ref, r_ref, acc_ref, send_sems, recv_sems):
        first, last = _grid_first_last(grid)
        k = pl.program_id(2)

        @pl.when(first)
        def _():
            for cp in _same_core_copies(s_ref, r_ref, send_sems, recv_sems):
                cp.start()

        @pl.when(k == 0)
        def _():
            acc_ref[...] = jnp.zeros_like(acc_ref)

        acc_ref[...] += lax.dot_general(a_ref[...], b_ref[...], NT_DIMS, preferred_element_type=F32)

        @pl.when(k == grid[2] - 1)
        def _():
            o_ref[...] = acc_ref[...].astype(o_ref.dtype)

        @pl.when(last)
        def _():
            for cp in _same_core_copies(s_ref, r_ref, send_sems, recv_sems):
                cp.wait()

    return pl.pallas_call(
        body, name=name, grid=grid,
        in_specs=[pl.BlockSpec((tm, tk), lambda i, j, k: (i, k)),
                  pl.BlockSpec((None, tn, tk), lambda i, j, k: (k // kpb, j, k % kpb)), ANY_SPEC],
        out_specs=(pl.BlockSpec((tm, tn), lambda i, j, k: (i, j)), ANY_SPEC),
        out_shape=(jax.ShapeDtypeStruct((m, n), out_dtype), jax.ShapeDtypeStruct(send3.shape, send3.dtype)),
        scratch_shapes=[pltpu.VMEM((tm, tn), F32), pltpu.SemaphoreType.DMA((3,)), pltpu.SemaphoreType.DMA((3,))],
        compiler_params=_params("arbitrary", "arbitrary", "arbitrary"),
    )(a, b3, send3)


def _mm_tn_groups(groups, a, b, n_groups, name, to_sibling=None):
    t, m = a.shape
    nb = b.shape[1] // n_groups
    ng = groups.shape[0]
    tm, tk = _tile(m, MM_TM), _tile(t, MM_TK)
    grid = (m // tm, ng, t // tk)
    carry = to_sibling is not None

    def body(groups_ref, a_ref, b_ref, *rest):
        if carry:
            s_ref, o_ref, r_ref, acc_ref, send_sems, recv_sems = rest
        else:
            o_ref, acc_ref = rest
        k = pl.program_id(2)
        first, last = _grid_first_last(grid)

        if carry:
            @pl.when(first)
            def _():
                for cp in _sibling_copies(s_ref, r_ref, send_sems, recv_sems):
                    cp.start()

        @pl.when(k == 0)
        def _():
            acc_ref[...] = jnp.zeros_like(acc_ref)

        acc_ref[...] += lax.dot_general(a_ref[...], b_ref[...], TN_DIMS, preferred_element_type=F32)

        @pl.when(k == grid[2] - 1)
        def _():
            o_ref[...] = acc_ref[...].astype(o_ref.dtype)

        if carry:
            @pl.when(last)
            def _():
                for cp in _sibling_copies(s_ref, r_ref, send_sems, recv_sems):
                    cp.wait()

    shp = jax.ShapeDtypeStruct((ng, m, nb), b.dtype)
    o_spec = pl.BlockSpec((None, tm, nb), lambda i, j, k, grp: (j, i, 0))
    return pl.pallas_call(
        body, name=name,
        grid_spec=pltpu.PrefetchScalarGridSpec(
            num_scalar_prefetch=1, grid=grid,
            in_specs=[pl.BlockSpec((tk, tm), lambda i, j, k, grp: (k, i)),
                      pl.BlockSpec((tk, nb), lambda i, j, k, grp: (k, grp[j]))] + ([ANY_SPEC] if carry else []),
            out_specs=(o_spec, ANY_SPEC) if carry else o_spec,
            scratch_shapes=[pltpu.VMEM((tm, nb), F32)] + (
                [pltpu.SemaphoreType.DMA((ng,)), pltpu.SemaphoreType.DMA((ng,))] if carry else [])),
        out_shape=(shp, shp) if carry else shp,
        compiler_params=_params("arbitrary", "arbitrary", "arbitrary"),
    )(*((groups, a, b, to_sibling) if carry else (groups, a, b)))


def _mm_tn(a, b, g, out_dtype, name):
    t, m = a.shape
    nb = b.shape[1] // g
    tm, tn, tk = _tile(m, MM_TM), _tile(nb, MM_TN), _tile(t, MM_TK)
    npb = nb // tn
    return _mm_call(
        a, b, TN_DIMS, t // tk, (m // tm, g * npb, t // tk),
        pl.BlockSpec((tk, tm), lambda i, j, k: (k, i)),
        pl.BlockSpec((tk, tn), lambda i, j, k: (k, j)),
        pl.BlockSpec((None, tm, tn), lambda i, j, k: (j // npb, i, j % npb)),
        jax.ShapeDtypeStruct((g, m, nb), out_dtype), (tm, tn), name)


def _silu(z):
    return z * jax.nn.sigmoid(z)


def _ada_fwd(c_all, w_shard, b_own):
    r, d = c_all.shape
    na = w_shard.shape[1]
    tk = _tile(d, 512)
    nk = d // tk

    def body(c_ref, w_ref, b_ref, o_ref):
        k = pl.program_id(0)

        @pl.when(k == 0)
        def _():
            o_ref[...] = jnp.zeros_like(o_ref) + b_ref[...]

        cs = _silu(c_ref[...]).astype(BF16)
        o_ref[...] += jnp.dot(cs, w_ref[...].astype(BF16), preferred_element_type=F32)

    return pl.pallas_call(
        body, name="ada_fwd", grid=(nk,),
        in_specs=[pl.BlockSpec((r, tk), lambda k: (0, k)), pl.BlockSpec((tk, na), lambda k: (k, 0)),
                  pl.BlockSpec((1, na), lambda k: (0, 0))],
        out_specs=pl.BlockSpec((r, na), lambda k: (0, 0)),
        out_shape=jax.ShapeDtypeStruct((r, na), F32),
        compiler_params=_params("arbitrary"),
    )(c_all, w_shard, b_own)


def _adam(w, g, m, v):
    nm = ADAM_B1 * m + (1.0 - ADAM_B1) * g
    nv = ADAM_B2 * v + (1.0 - ADAM_B2) * (g * g)
    m_hat = nm / (1.0 - ADAM_B1 ** ADAM_STEP)
    v_hat = nv / (1.0 - ADAM_B2 ** ADAM_STEP)
    delta = -ADAM_LR * (m_hat / (jnp.sqrt(v_hat) + ADAM_EPS) + ADAM_WD * w)
    return delta, nm, nv


def _ada_bwd_adam(c_rows, dmod_cols, w, m, v):
    bg, d = c_rows.shape
    na = w.shape[1]
    tr = _tile(d, 256)

    def body(c_ref, dm_ref, w_ref, m_ref, v_ref, g_ref, d_ref, nm_ref, nv_ref):
        cs = _silu(c_ref[...]).astype(BF16)
        g = lax.dot_general(cs, dm_ref[...].astype(BF16), TN_DIMS, preferred_element_type=F32)
        delta, nm, nv = _adam(w_ref[...], g, m_ref[...], v_ref[...])
        g_ref[...] = g
        d_ref[...] = delta
        nm_ref[...] = nm
        nv_ref[...] = nv

    blk = pl.BlockSpec((tr, na), lambda i: (i, 0))
    shp = jax.ShapeDtypeStruct((d, na), F32)
    return pl.pallas_call(
        body, name="ada_bwd_adam", grid=(d // tr,),
        in_specs=[pl.BlockSpec((bg, tr), lambda i: (0, i)), pl.BlockSpec((bg, na), lambda i: (0, 0)), blk, blk, blk],
        out_specs=(blk, blk, blk, blk), out_shape=(shp, shp, shp, shp),
        compiler_params=_params("parallel"),
    )(c_rows, dmod_cols, w, m, v)


def _small_adam(pkg_all, n_batch_rows, w, m, v):
    d = w.shape[1]

    def body(p_ref, w_ref, m_ref, v_ref, g_ref, d_ref, nm_ref, nv_ref):
        for part in range(3):
            acc = jnp.zeros((1, d), F32)
            for dev in range(N_DEV):
                for b in range(n_batch_rows // 3):
                    acc = acc + p_ref[dev, 3 * b + part:3 * b + part + 1, :]
            g_ref[part:part + 1, :] = acc
        for rrow in range(3):
            acc = jnp.zeros((1, d), F32)
            for dev in range(N_DEV):
                acc = acc + p_ref[dev, n_batch_rows + rrow:n_batch_rows + rrow + 1, :]
            g_ref[3 + rrow:4 + rrow, :] = acc
        g_ref[6:8, :] = jnp.zeros((2, d), F32)
        g = g_ref[...]
        delta, nm, nv = _adam(w_ref[...], g, m_ref[...], v_ref[...])
        d_ref[...] = delta
        nm_ref[...] = nm
        nv_ref[...] = nv

    vm = pl.BlockSpec(memory_space=pltpu.VMEM)
    shp = jax.ShapeDtypeStruct((SMALL_ROWS, d), F32)
    return pl.pallas_call(
        body, name="small_adam", in_specs=[vm, vm, vm, vm], out_specs=(vm, vm, vm, vm),
        out_shape=(shp, shp, shp, shp),
    )(pkg_all, w, m, v)


def _adam_from_chip_sums(mine, from_sibling, from_chips, w, m, v, name):
    _, r, c = mine.shape
    tr = _tile(r, 128)

    def body(p_ref, s_ref, f_ref, w_ref, m_ref, v_ref, g_ref, d_ref, nm_ref, nv_ref):
        g = p_ref[...].astype(F32) + s_ref[...].astype(F32)
        for j in range(3):
            g = g + f_ref[j].astype(F32)
        delta, nm, nv = _adam(w_ref[...], g, m_ref[...], v_ref[...])
        g_ref[...] = g
        d_ref[...] = delta
        nm_ref[...] = nm
        nv_ref[...] = nv

    blk = pl.BlockSpec((tr, c), lambda i: (i, 0))
    slot0 = pl.BlockSpec((None, tr, c), lambda i: (0, i, 0))
    shp = jax.ShapeDtypeStruct((r, c), F32)
    return pl.pallas_call(
        body, name=name, grid=(r // tr,),
        in_specs=[slot0, slot0, pl.BlockSpec((3, tr, c), lambda i: (0, i, 0)), blk, blk, blk],
        out_specs=(blk, blk, blk, blk), out_shape=(shp, shp, shp, shp),
        compiler_params=_params("parallel"),
    )(mine, from_sibling, from_chips, w, m, v)


def _adam_from_partials(recv, w, m, v, name):
    _, r, c = recv.shape
    tr = _tile(r, 128)

    def body(p_ref, w_ref, m_ref, v_ref, g_ref, d_ref, nm_ref, nv_ref):
        g = p_ref[0].astype(F32)
        for dev in range(1, N_DEV):
            g = g + p_ref[dev].astype(F32)
        delta, nm, nv = _adam(w_ref[...], g, m_ref[...], v_ref[...])
        g_ref[...] = g
        d_ref[...] = delta
        nm_ref[...] = nm
        nv_ref[...] = nv

    blk = pl.BlockSpec((tr, c), lambda i: (i, 0))
    shp = jax.ShapeDtypeStruct((r, c), F32)
    return pl.pallas_call(
        body, name=name, grid=(r // tr,),
        in_specs=[pl.BlockSpec((N_DEV, tr, c), lambda i: (0, i, 0)), blk, blk, blk],
        out_specs=(blk, blk, blk, blk), out_shape=(shp, shp, shp, shp),
        compiler_params=_params("parallel"),
    )(recv, w, m, v)


def _norm_mod(x, g_norm, scale, shift):
    b, s, d = x.shape
    ts = _tile(s, 256)

    def body(x_ref, g_ref, sc_ref, sh_ref, h_ref):
        xv = x_ref[...]
        r = lax.rsqrt(jnp.mean(xv * xv, axis=-1, keepdims=True) + EPS)
        xn = (xv * r) * g_ref[...]
        h_ref[...] = (xn * (1.0 + sc_ref[...]) + sh_ref[...]).astype(BF16)

    tok = pl.BlockSpec((None, ts, d), lambda i, j: (i, j, 0))
    per_b = pl.BlockSpec((None, 1, d), lambda i, j: (i, 0, 0))
    return pl.pallas_call(
        body, name="norm_mod", grid=(b, s // ts),
        in_specs=[tok, pl.BlockSpec((1, d), lambda i, j: (0, 0)), per_b, per_b],
        out_specs=tok, out_shape=jax.ShapeDtypeStruct((b, s, d), BF16),
        compiler_params=_params("parallel", "parallel"),
    )(x, g_norm, scale, shift)


def _final_fwd_bwd(x, out, gate, g_final, target):
    b, s, d = x.shape
    ts = _tile(s, 256)

    def body(x_ref, o_ref, gt_ref, g_ref, t_ref, loss_ref, dx2_ref, dout_ref, dgate_ref, gg_ref):
        i, j = pl.program_id(0), pl.program_id(1)

        @pl.when((i == 0) & (j == 0))
        def _():
            loss_ref[...] = jnp.zeros_like(loss_ref)
            gg_ref[...] = jnp.zeros_like(gg_ref)

        @pl.when(j == 0)
        def _():
            dgate_ref[...] = jnp.zeros_like(dgate_ref)

        ov = o_ref[...]
        gt = gt_ref[...]
        x2 = x_ref[...] + gt * ov
        r = lax.rsqrt(jnp.mean(x2 * x2, axis=-1, keepdims=True) + EPS)
        xh = x2 * r
        err = xh * g_ref[...] - t_ref[...]
        loss_ref[...] += 0.5 * jnp.sum(jnp.mean(err * err, axis=-1, keepdims=True), axis=0, keepdims=True)
        dfin = err * (1.0 / d)
        gg_ref[...] += jnp.sum(dfin * xh, axis=0, keepdims=True)
        dxh = dfin * g_ref[...]
        dx2 = r * (dxh - xh * jnp.mean(dxh * xh, axis=-1, keepdims=True))
        dx2_ref[...] = dx2
        dout_ref[...] = (gt * dx2).astype(BF16)
        dgate_ref[...] += jnp.sum(dx2 * ov, axis=0, keepdims=True)

    tok = pl.BlockSpec((None, ts, d), lambda i, j: (i, j, 0))
    per_b = pl.BlockSpec((None, 1, d), lambda i, j: (i, 0, 0))
    vec = pl.BlockSpec((1, d), lambda i, j: (0, 0))
    return pl.pallas_call(
        body, name="final_fwd_bwd", grid=(b, s // ts),
        in_specs=[tok, tok, per_b, vec, tok],
        out_specs=(pl.BlockSpec((8, 128), lambda i, j: (0, 0)), tok, tok, per_b, vec),
        out_shape=(jax.ShapeDtypeStruct((8, 128), F32), jax.ShapeDtypeStruct((b, s, d), F32),
                   jax.ShapeDtypeStruct((b, s, d), BF16), jax.ShapeDtypeStruct((b, 1, d), F32),
                   jax.ShapeDtypeStruct((1, d), F32)),
        compiler_params=_params("arbitrary", "arbitrary"),
    )(x, out, gate, g_final, target)


def _norm_bwd(x, dh, dx2, scale, g_norm):
    b, s, d = x.shape
    ts = _tile(s, 256)

    def body(x_ref, dh_ref, dx2_ref, sc_ref, g_ref, gx_ref, dsh_ref, dsc_ref, gg_ref):
        i, j = pl.program_id(0), pl.program_id(1)

        @pl.when((i == 0) & (j == 0))
        def _():
            gg_ref[...] = jnp.zeros_like(gg_ref)

        @pl.when(j == 0)
        def _():
            dsh_ref[...] = jnp.zeros_like(dsh_ref)
            dsc_ref[...] = jnp.zeros_like(dsc_ref)

        xv = x_ref[...]
        dhv = dh_ref[...]
        r = lax.rsqrt(jnp.mean(xv * xv, axis=-1, keepdims=True) + EPS)
        xh = xv * r
        xn = xh * g_ref[...]
        dsh_ref[...] += jnp.sum(dhv, axis=0, keepdims=True)
        dsc_ref[...] += jnp.sum(dhv * xn, axis=0, keepdims=True)
        dxn = dhv * (1.0 + sc_ref[...])
        gg_ref[...] += jnp.sum(dxn * xh, axis=0, keepdims=True)
        dxh = dxn * g_ref[...]
        gx_ref[...] = dx2_ref[...] + r * (dxh - xh * jnp.mean(dxh * xh, axis=-1, keepdims=True))

    tok = pl.BlockSpec((None, ts, d), lambda i, j: (i, j, 0))
    per_b = pl.BlockSpec((None, 1, d), lambda i, j: (i, 0, 0))
    vec = pl.BlockSpec((1, d), lambda i, j: (0, 0))
    return pl.pallas_call(
        body, name="norm_bwd", grid=(b, s // ts),
        in_specs=[tok, tok, tok, per_b, vec],
        out_specs=(tok, per_b, per_b, vec),
        out_shape=(jax.ShapeDtypeStruct((b, s, d), F32), jax.ShapeDtypeStruct((b, 1, d), F32),
                   jax.ShapeDtypeStruct((b, 1, d), F32), jax.ShapeDtypeStruct((1, d), F32)),
        compiler_params=_params("arbitrary", "arbitrary"),
    )(x, dh, dx2, scale, g_norm)


def _gate_fwd(y_sb, y_dl, proj, g_sb, g_dl):
    t, e = y_sb.shape
    n_heads = e // HEAD_DIM
    tt = _tile(t, 256)

    def body(ys_ref, yd_ref, zs_ref, zd_ref, gs_ref, gd_ref, o_ref):
        for grp, (y_ref, z_ref, g_ref) in enumerate(((ys_ref, zs_ref, gs_ref), (yd_ref, zd_ref, gd_ref))):
            for h in range(n_heads):
                sl = slice(h * HEAD_DIM, (h + 1) * HEAD_DIM)
                y = y_ref[:, sl]
                r = lax.rsqrt(jnp.mean(y * y, axis=-1, keepdims=True) + EPS)
                yn = (y * r) * g_ref[:, sl]
                z = z_ref[:, sl].astype(F32)
                o_ref[:, grp * e + h * HEAD_DIM:grp * e + (h + 1) * HEAD_DIM] = (yn * _silu(z)).astype(BF16)

    yblk = pl.BlockSpec((tt, e), lambda i: (i, 0))
    gblk = pl.BlockSpec((1, e), lambda i: (0, 0))
    return pl.pallas_call(
        body, name="gate_fwd", grid=(t // tt,),
        in_specs=[yblk, yblk, pl.BlockSpec((tt, e), lambda i: (i, 3)), pl.BlockSpec((tt, e), lambda i: (i, 7)),
                  gblk, gblk],
        out_specs=pl.BlockSpec((tt, 2 * e), lambda i: (i, 0)),
        out_shape=jax.ShapeDtypeStruct((t, 2 * e), BF16),
        compiler_params=_params("parallel"),
    )(y_sb, y_dl, proj, proj, g_sb, g_dl)


def _gate_bwd(dyg, y_sb, y_dl, proj, g_sb, g_dl):
    t, e = y_sb.shape
    n_heads = e // HEAD_DIM
    tt = _tile(t, 256)

    def body(dg_ref, ys_ref, yd_ref, zs_ref, zd_ref, gs_ref, gd_ref,
             dys_ref, dyd_ref, dzs_ref, dzd_ref, ggs_ref, ggd_ref):
        @pl.when(pl.program_id(0) == 0)
        def _():
            ggs_ref[...] = jnp.zeros_like(ggs_ref)
            ggd_ref[...] = jnp.zeros_like(ggd_ref)

        groups = ((ys_ref, zs_ref, gs_ref, dys_ref, dzs_ref, ggs_ref), (yd_ref, zd_ref, gd_ref, dyd_ref, dzd_ref, ggd_ref))
        for grp, (y_ref, z_ref, g_ref, dy_ref, dz_ref, gg_ref) in enumerate(groups):
            for h in range(n_heads):
                sl = slice(h * HEAD_DIM, (h + 1) * HEAD_DIM)
                dg = dg_ref[:, grp * e + h * HEAD_DIM:grp * e + (h + 1) * HEAD_DIM].astype(F32)
                y = y_ref[:, sl]
                z = z_ref[:, sl].astype(F32)
                g = g_ref[:, sl]
                r = lax.rsqrt(jnp.mean(y * y, axis=-1, keepdims=True) + EPS)
                yh = y * r
                sig = jax.nn.sigmoid(z)
                dyn = dg * (z * sig)
                dz_ref[:, sl] = (dg * (yh * g) * (sig * (1.0 + z * (1.0 - sig)))).astype(BF16)
                gg_ref[:, sl] += jnp.sum(dyn * yh, axis=0, keepdims=True)
                dyh = dyn * g
                dy_ref[:, sl] = (r * (dyh - yh * jnp.mean(dyh * yh, axis=-1, keepdims=True))).astype(BF16)

    yblk = pl.BlockSpec((tt, e), lambda i: (i, 0))
    gblk = pl.BlockSpec((1, e), lambda i: (0, 0))
    act = jax.ShapeDtypeStruct((t, e), BF16)
    vec = jax.ShapeDtypeStruct((1, e), F32)
    return pl.pallas_call(
        body, name="gate_bwd", grid=(t // tt,),
        in_specs=[pl.BlockSpec((tt, 2 * e), lambda i: (i, 0)), yblk, yblk,
                  pl.BlockSpec((tt, e), lambda i: (i, 3)), pl.BlockSpec((tt, e), lambda i: (i, 7)), gblk, gblk],
        out_specs=(yblk, yblk, yblk, yblk, gblk, gblk),
        out_shape=(act, act, act, act, vec, vec),
        compiler_params=_params("arbitrary"),
    )(dyg, y_sb, y_dl, proj, proj, g_sb, g_dl)


ATT_TQ = 256
HEADS_PER_STEP = 4
SOFTPLUS_CLAMP = 30.0


def _split2_dot(x, u):
    hi = x.astype(BF16)
    lo = (x - hi.astype(F32)).astype(BF16)
    n = x.shape[0]
    both = jnp.dot(jnp.concatenate([hi, lo], axis=0), u, preferred_element_type=F32)
    return both[:n] + both[n:]


def _iota2(n):
    return lax.broadcasted_iota(jnp.int32, (n, n), 0), lax.broadcasted_iota(jnp.int32, (n, n), 1)


def _head_slices():
    return [slice(hh * HEAD_DIM, (hh + 1) * HEAD_DIM) for hh in range(HEADS_PER_STEP)]


def _att_specs(s, e, tq, col0):
    n_heads = e // HEAD_DIM
    hp = HEADS_PER_STEP
    assert n_heads % hp == 0 and col0 % hp == 0
    w = hp * HEAD_DIM
    q_spec = pl.BlockSpec((None, tq, w), lambda i, h, j: (i, j, col0 // hp + h))
    k_spec = pl.BlockSpec((None, s, w), lambda i, h, j: (i, 0, (col0 + n_heads) // hp + h))
    v_spec = pl.BlockSpec((None, s, w), lambda i, h, j: (i, 0, (col0 + 2 * n_heads) // hp + h))
    return q_spec, k_spec, v_spec


def _sb_fwd(proj3, e):
    b, s, _ = proj3.shape
    n_heads = e // HEAD_DIM
    hp = HEADS_PER_STEP
    tq = _tile(s, ATT_TQ)
    nq = s // tq
    inv = 1.0 / math.sqrt(HEAD_DIM)

    def body(q_ref, k_ref, v_ref, y_ref, tot_ref, acc_ref, car_ref):
        i = pl.program_id(2)
        row, col = _iota2(tq)
        before = row > col
        u_after = before.astype(BF16)
        acc_ref[...] = jnp.zeros_like(acc_ref)
        car_ref[...] = jnp.zeros_like(car_ref)

        def block(j, diagonal):
            keys = pl.ds(pl.multiple_of(j * tq, tq), tq)
            heads = list(enumerate(_head_slices()))
            zs = [lax.dot_general(q_ref[:, hs], k_ref[keys, hs], NT_DIMS, preferred_element_type=F32) * inv
                  for _, hs in heads]
            sps = [jnp.maximum(jnp.log(1.0 + jnp.exp(jnp.minimum(z, SOFTPLUS_CLAMP))), z) for z in zs]
            loms = [jnp.where(before, -sp, 0.0) if diagonal else -sp for sp in sps]
            sufs = [_split2_dot(loms[hh], u_after) + car_ref[hh] for hh, _ in heads]
            avs = [jnp.exp((zs[hh] - sps[hh]) + sufs[hh]) for hh, _ in heads]
            if diagonal:
                avs = [jnp.where(before, a, 0.0) for a in avs]
            pvs = [jnp.dot(avs[hh].astype(BF16), v_ref[keys, hs], preferred_element_type=F32) for hh, hs in heads]
            for hh, _ in heads:
                acc_ref[hh] += pvs[hh]
                car_ref[hh] += jnp.sum(loms[hh], axis=1, keepdims=True)

        block(i, True)

        def step(it, carry):
            block(i - it, False)
            return carry

        lax.fori_loop(1, i + 1, step, 0)
        for hh, hs in enumerate(_head_slices()):
            y_ref[:, hs] = acc_ref[hh]
            tot_ref[:, hs] = jnp.broadcast_to(car_ref[hh], (tq, HEAD_DIM))

    q_spec, k_spec, v_spec = _att_specs(s, e, tq, 0)
    blk_q = pl.BlockSpec((None, tq, hp * HEAD_DIM), lambda i, h, j: (i, j, h))
    shp = jax.ShapeDtypeStruct((b, s, e), F32)
    return pl.pallas_call(
        body, name="sb_fwd", grid=(b, n_heads // hp, nq),
        in_specs=[q_spec, k_spec, v_spec],
        out_specs=(blk_q, blk_q), out_shape=(shp, shp),
        scratch_shapes=[pltpu.VMEM((hp, tq, HEAD_DIM), F32), pltpu.VMEM((hp, tq, 1), F32)],
        compiler_params=_params("parallel", "parallel", "arbitrary"),
    )(proj3, proj3, proj3)


def _sb_bwd(proj3, lom_total, dy, partials):
    b, s, e = dy.shape
    n_heads = e // HEAD_DIM
    hp = HEADS_PER_STEP
    tq = _tile(s, ATT_TQ)
    nq = s // tq
    inv = 1.0 / math.sqrt(HEAD_DIM)
    grid = (b, n_heads // hp, nq)

    def body(q_ref, k_ref, v_ref, tot_ref, dy_ref, p_ref, dq_ref, dk_ref, dv_ref, r_ref,
             dqa, dka, dva, car, car2, send_sems, recv_sems, local_sem):
        i = pl.program_id(2)
        first, last = _grid_first_last(grid)

        @pl.when(first)
        def _():
            for cp in _all_to_all_copies(p_ref, r_ref, send_sems, recv_sems, local_sem):
                cp.start()

        @pl.when(i == 0)
        def _():
            dka[...] = jnp.zeros_like(dka)
            dva[...] = jnp.zeros_like(dva)

        row, col = _iota2(tq)
        before = row > col
        u_upto = (row <= col).astype(BF16)
        u_before = (row < col).astype(BF16)
        dqa[...] = jnp.zeros_like(dqa)
        car[...] = jnp.zeros_like(car)
        car2[...] = jnp.zeros_like(car2)

        def block(j, diagonal):
            keys = pl.ds(pl.multiple_of(j * tq, tq), tq)
            heads = list(enumerate(_head_slices()))
            zs = [lax.dot_general(q_ref[:, hs], k_ref[keys, hs], NT_DIMS, preferred_element_type=F32) * inv
                  for _, hs in heads]
            das = [lax.dot_general(dy_ref[:, hs], v_ref[keys, hs], NT_DIMS, preferred_element_type=F32) for _, hs in heads]
            ezs = [jnp.exp(jnp.minimum(z, SOFTPLUS_CLAMP)) for z in zs]
            sps = [jnp.maximum(jnp.log(1.0 + ezs[hh]), zs[hh]) for hh, _ in heads]
            loms = [jnp.where(before, -sp, 0.0) if diagonal else -sp for sp in sps]
            sufs = [tot_ref[:, hh * HEAD_DIM:hh * HEAD_DIM + 1] - (_split2_dot(loms[hh], u_upto) + car[hh])
                    for hh, _ in heads]
            avs = [jnp.exp((zs[hh] - sps[hh]) + sufs[hh]) for hh, _ in heads]
            if diagonal:
                avs = [jnp.where(before, a, 0.0) for a in avs]
            dls = [avs[hh] * das[hh] for hh, _ in heads]
            prefixes = [_split2_dot(dls[hh], u_before) + car2[hh] for hh, _ in heads]
            dzs = []
            for hh, _ in heads:
                one_minus_beta = 1.0 / (1.0 + ezs[hh])
                dz = (dls[hh] * one_minus_beta - prefixes[hh] * (ezs[hh] * one_minus_beta)) * inv
                if diagonal:
                    dz = jnp.where(before, dz, 0.0)
                dzs.append(dz.astype(BF16))
            dqs = [jnp.dot(dzs[hh], k_ref[keys, hs], preferred_element_type=F32) for hh, hs in heads]
            dks = [lax.dot_general(dzs[hh], q_ref[:, hs], TN_DIMS, preferred_element_type=F32) for hh, hs in heads]
            dvs = [lax.dot_general(avs[hh].astype(BF16), dy_ref[:, hs], TN_DIMS, preferred_element_type=F32)
                   for hh, hs in heads]
            for hh, hs in heads:
                dqa[hh] += dqs[hh]
                dka[keys, hs] += dks[hh]
                dva[keys, hs] += dvs[hh]
                car[hh] += jnp.sum(loms[hh], axis=1, keepdims=True)
                car2[hh] += jnp.sum(dls[hh], axis=1, keepdims=True)

        def step(j, carry):
            block(j, False)
            return carry

        lax.fori_loop(0, i, step, 0)
        block(i, True)
        for hh, hs in enumerate(_head_slices()):
            dq_ref[:, hs] = dqa[hh].astype(BF16)

        @pl.when(i == nq - 1)
        def _():
            dk_ref[...] = dka[...].astype(BF16)
            dv_ref[...] = dva[...].astype(BF16)

        @pl.when(last)
        def _():
            for cp in _all_to_all_copies(p_ref, r_ref, send_sems, recv_sems, local_sem):
                cp.wait()

    q_spec, k_spec, v_spec = _att_specs(s, e, tq, 0)
    w = hp * HEAD_DIM
    blk_q = pl.BlockSpec((None, tq, w), lambda i, h, j: (i, j, h))
    blk_kv = pl.BlockSpec((None, s, w), lambda i, h, j: (i, 0, h))
    shp = jax.ShapeDtypeStruct((b, s, e), BF16)
    return pl.pallas_call(
        body, name="sb_bwd", grid=grid,
        in_specs=[q_spec, k_spec, v_spec, blk_q, blk_q, ANY_SPEC],
        out_specs=(blk_q, blk_kv, blk_kv, ANY_SPEC),
        out_shape=(shp, shp, shp, jax.ShapeDtypeStruct(partials.shape, partials.dtype)),
        scratch_shapes=[pltpu.VMEM((hp, tq, HEAD_DIM), F32), pltpu.VMEM((s, w), F32), pltpu.VMEM((s, w), F32),
                        pltpu.VMEM((hp, tq, 1), F32), pltpu.VMEM((hp, tq, 1), F32),
                        pltpu.SemaphoreType.DMA((N_DEV - 1,)), pltpu.SemaphoreType.DMA((N_DEV - 1,)),
                        pltpu.SemaphoreType.DMA],
        compiler_params=_params("arbitrary", "arbitrary", "arbitrary"),
    )(proj3, proj3, proj3, lom_total, dy, partials)


def _dil_near_tiles(tq):
    return (DIL_PAIRS[1][0] + tq - 1) // tq + 1


def _dil_fill_bias(bias_ref, sl_ref, tq):
    row, col = _iota2(tq)
    for hh in range(HEADS_PER_STEP):
        slope = sl_ref[hh, 0:1, 0:1]
        for d in range(_dil_near_tiles(tq) + 1):
            dist = d * tq + row - col
            cnt = jnp.zeros(dist.shape, jnp.int32)
            for window, dilation in DIL_PAIRS:
                cnt = cnt + (((dist & (dilation - 1)) == 0) & (dist <= window)).astype(jnp.int32)
            bias = jnp.where(cnt == 3, math.log(3.0), jnp.where(cnt == 2, math.log(2.0), 0.0))
            bias_ref[hh, d] = jnp.where((dist >= 0) & (cnt > 0), bias - slope * dist.astype(F32), NEG)


def _dil_scores(q, k, bias_ref, hh, slope, it, tq, inv):
    near = _dil_near_tiles(tq)
    beyond = jnp.maximum(it - near, 0).astype(F32) * float(tq)
    sc = lax.dot_general(q, k, NT_DIMS, preferred_element_type=F32) * inv
    return (sc + bias_ref[hh, jnp.minimum(it, near)]) - slope * beyond


def _dil_fwd(proj3, e, slopes):
    b, s, _ = proj3.shape
    n_heads = e // HEAD_DIM
    hp = HEADS_PER_STEP
    tq = _tile(s, ATT_TQ)
    nq = s // tq
    inv = 1.0 / math.sqrt(HEAD_DIM)
    assert s <= DIL_PAIRS[2][0]

    def body(q_ref, k_ref, v_ref, sl_ref, y_ref, lse_ref, acc_ref, m_ref, l_ref, bias_ref):
        i = pl.program_id(2)

        @pl.when(i == 0)
        def _():
            _dil_fill_bias(bias_ref, sl_ref, tq)

        acc_ref[...] = jnp.zeros_like(acc_ref)
        m_ref[...] = jnp.full_like(m_ref, NEG)
        l_ref[...] = jnp.zeros_like(l_ref)

        def step(it, carry):
            keys = pl.ds(pl.multiple_of((i - it) * tq, tq), tq)
            heads = list(enumerate(_head_slices()))
            scs = [_dil_scores(q_ref[:, hs], k_ref[keys, hs], bias_ref, hh, sl_ref[hh, 0:1, 0:1], it, tq, inv)
                   for hh, hs in heads]
            m_old = [m_ref[hh] for hh, _ in heads]
            m_new = [jnp.maximum(m_old[hh], jnp.max(scs[hh], axis=1, keepdims=True)) for hh, _ in heads]
            ps = [jnp.exp(scs[hh] - m_new[hh]) for hh, _ in heads]
            pvs = [jnp.dot(ps[hh].astype(BF16), v_ref[keys, hs], preferred_element_type=F32) for hh, hs in heads]
            for hh, _ in heads:
                alpha = jnp.exp(m_old[hh] - m_new[hh])
                l_ref[hh] = alpha * l_ref[hh] + (ps[hh][:, :tq // 2] + ps[hh][:, tq // 2:])
                acc_ref[hh] = alpha * acc_ref[hh] + pvs[hh]
                m_ref[hh] = m_new[hh]
            return carry

        lax.fori_loop(0, i + 1, step, 0)
        for hh, hs in enumerate(_head_slices()):
            l = jnp.sum(l_ref[hh], axis=1, keepdims=True)
            y_ref[:, hs] = acc_ref[hh] / l
            lse_ref[:, hs] = jnp.broadcast_to(m_ref[hh] + jnp.log(l), (tq, HEAD_DIM))

    q_spec, k_spec, v_spec = _att_specs(s, e, tq, 4 * n_heads)
    blk_q = pl.BlockSpec((None, tq, hp * HEAD_DIM), lambda i, h, j: (i, j, h))
    shp = jax.ShapeDtypeStruct((b, s, e), F32)
    return pl.pallas_call(
        body, name="dil_fwd", grid=(b, n_heads // hp, nq),
        in_specs=[q_spec, k_spec, v_spec, pl.BlockSpec((hp, 8, HEAD_DIM), lambda i, h, j: (h, 0, 0))],
        out_specs=(blk_q, blk_q), out_shape=(shp, shp),
        scratch_shapes=[pltpu.VMEM((hp, tq, HEAD_DIM), F32), pltpu.VMEM((hp, tq, 1), F32), pltpu.VMEM((hp, tq, tq // 2), F32),
                        pltpu.VMEM((hp, _dil_near_tiles(tq) + 1, tq, tq), F32)],
        compiler_params=_params("parallel", "parallel", "arbitrary"),
    )(proj3, proj3, proj3, slopes)


def _dil_bwd(proj3, y, lse, dy, slopes):
    b, s, e = y.shape
    n_heads = e // HEAD_DIM
    hp = HEADS_PER_STEP
    tq = _tile(s, ATT_TQ)
    nq = s // tq
    inv = 1.0 / math.sqrt(HEAD_DIM)

    def body(q_ref, k_ref, v_ref, sl_ref, y_ref, lse_ref, dy_ref, dq_ref, dk_ref, dv_ref, dqa, dka, dva, bias_ref):
        i = pl.program_id(2)

        @pl.when(i == 0)
        def _():
            dka[...] = jnp.zeros_like(dka)
            dva[...] = jnp.zeros_like(dva)
            _dil_fill_bias(bias_ref, sl_ref, tq)

        delta = [jnp.sum(dy_ref[:, hs].astype(F32) * y_ref[:, hs], axis=1, keepdims=True) for hs in _head_slices()]
        dqa[...] = jnp.zeros_like(dqa)

        def step(it, carry):
            keys = pl.ds(pl.multiple_of((i - it) * tq, tq), tq)
            heads = list(enumerate(_head_slices()))
            scs = [_dil_scores(q_ref[:, hs], k_ref[keys, hs], bias_ref, hh, sl_ref[hh, 0:1, 0:1], it, tq, inv)
                   for hh, hs in heads]
            dps = [lax.dot_general(dy_ref[:, hs], v_ref[keys, hs], NT_DIMS, preferred_element_type=F32) for _, hs in heads]
            ps = [jnp.exp(scs[hh] - lse_ref[:, hh * HEAD_DIM:hh * HEAD_DIM + 1]) for hh, _ in heads]
            dss = [((ps[hh] * (dps[hh] - delta[hh])) * inv).astype(BF16) for hh, _ in heads]
            dqs = [jnp.dot(dss[hh], k_ref[keys, hs], preferred_element_type=F32) for hh, hs in heads]
            dks = [lax.dot_general(dss[hh], q_ref[:, hs], TN_DIMS, preferred_element_type=F32) for hh, hs in heads]
            dvs = [lax.dot_general(ps[hh].astype(BF16), dy_ref[:, hs], TN_DIMS, preferred_element_type=F32)
                   for hh, hs in heads]
            for hh, hs in heads:
                dqa[hh] += dqs[hh]
                dka[keys, hs] += dks[hh]
                dva[keys, hs] += dvs[hh]
            return carry

        lax.fori_loop(0, i + 1, step, 0)
        for hh, hs in enumerate(_head_slices()):
            dq_ref[:, hs] = dqa[hh].astype(BF16)

        @pl.when(i == nq - 1)
        def _():
            dk_ref[...] = dka[...].astype(BF16)
            dv_ref[...] = dva[...].astype(BF16)

    q_spec, k_spec, v_spec = _att_specs(s, e, tq, 4 * n_heads)
    w = hp * HEAD_DIM
    blk_q = pl.BlockSpec((None, tq, w), lambda i, h, j: (i, j, h))
    blk_kv = pl.BlockSpec((None, s, w), lambda i, h, j: (i, 0, h))
    shp = jax.ShapeDtypeStruct((b, s, e), BF16)
    return pl.pallas_call(
        body, name="dil_bwd", grid=(b, n_heads // hp, nq),
        in_specs=[q_spec, k_spec, v_spec, pl.BlockSpec((hp, 8, HEAD_DIM), lambda i, h, j: (h, 0, 0)),
                  blk_q, blk_q, blk_q],
        out_specs=(blk_q, blk_kv, blk_kv), out_shape=(shp, shp, shp),
        scratch_shapes=[pltpu.VMEM((hp, tq, HEAD_DIM), F32), pltpu.VMEM((s, w), F32), pltpu.VMEM((s, w), F32),
                        pltpu.VMEM((hp, _dil_near_tiles(tq) + 1, tq, tq), F32)],
        compiler_params=_params("parallel", "parallel", "arbitrary"),
    )(proj3, proj3, proj3, slopes, y, lse, dy)


def kernel(x, c, w_ada, b_ada, g_norm, w_in, g_sb, g_dil, w_out, g_final, loss_target, m_w_ada, m_b_ada, m_g_norm, m_w_in, m_g_sb, m_g_dil, m_w_out, m_g_final, v_w_ada, v_b_ada, v_g_norm, v_w_in, v_g_sb, v_g_dil, v_w_out, v_g_final):
    b, s, d = x.shape
    t = b * s
    e = w_in.shape[2]
    n_heads = e // HEAD_DIM
    na = w_ada.shape[2]
    r_out = w_out.shape[1]
    assert g_sb.shape[1] == e and g_dil.shape[1] == e and N_DEV * r_out == 2 * e and N_DEV * na == 3 * d
    assert b <= SMALL_ROWS and 3 * b + 3 <= 2 * SMALL_ROWS
    ix, iy, ic = _mesh_pos()
    me = 4 * ix + 2 * iy + ic

    c_all = _allgather_rows(jnp.pad(c, ((0, SMALL_ROWS - b), (0, 0))), "ag_c")
    b_own = lax.dynamic_slice(b_ada, (0, me * na), (1, na))
    mod_cols = _ada_fwd(c_all, w_ada[0], b_own)
    mod_all = _allgather_rows(mod_cols, "ag_mod").reshape(N_DEV, N_DEV, SMALL_ROWS, na)
    mod_own = lax.dynamic_slice(mod_all, (0, me, 0, 0), (N_DEV, 1, b, na))[:, 0]
    mod = mod_own.transpose(1, 0, 2).reshape(b, 1, 3 * d)
    shift, scale, gate = mod[:, :, :d], mod[:, :, d:2 * d], mod[:, :, 2 * d:]

    w_own =w_in[0].astype(BF16).reshape(d, PROJ_HALVES, e // PROJ_HALVES).transpose(1, 0, 2)

    h = _norm_mod(x, g_norm, scale, shift).reshape(t, d)
    proj, w_in3, w_out3 = _proj_with_allgather(
        h, w_own, jnp.stack(_unit_ids()).astype(jnp.int32), w_out[0].astype(BF16))
    w_out1 = w_out3.reshape(1, N_DEV * r_out, d)
    proj3 = proj.reshape(b, s, N_DEV * e)
    slopes = jnp.exp2(-ALIBI_MAX_BIAS * jnp.arange(1, n_heads + 1, dtype=F32) / n_heads)
    slopes = jnp.broadcast_to(slopes[:, None, None], (n_heads, 8, HEAD_DIM))
    y_sb, lom_total = _sb_fwd(proj3, e)
    y_dl, lse = _dil_fwd(proj3, e, slopes)
    yg = _gate_fwd(y_sb.reshape(t, e), y_dl.reshape(t, e), proj, g_sb, g_dil)
    out = _mm_nn(yg, w_out1, F32, "mm_out").reshape(b, s, d)
    loss_p, dx2, d_out, dgate, gg_final = _final_fwd_bwd(x, out, gate, g_final.reshape(1, d), loss_target)

    d_out2 = d_out.reshape(t, d)
    dyg = _mm_nt(d_out2, w_out1, BF16, "mm_dy")
    gw_out_p = _mm_tn(yg, d_out2, 1, BF16, "mm_gw_out").reshape(N_DEV, r_out, d)
    dy_sb, dy_dl, dz_sb, dz_dl, gg_sb, gg_dl = _gate_bwd(dyg, y_sb.reshape(t, e), y_dl.reshape(t, e), proj, g_sb, g_dil)
    dq_sb, dk_sb, dv_sb, recv_out = _sb_bwd(proj3, lom_total, dy_sb.reshape(b, s, e), gw_out_p)
    dq_dl, dk_dl, dv_dl = _dil_bwd(proj3, y_dl, lse, dy_dl.reshape(b, s, e), slopes)
    dproj = jnp.concatenate(
        [a.reshape(t, e) for a in (dq_sb, dk_sb, dv_sb, dz_sb, dq_dl, dk_dl, dv_dl, dz_dl)], axis=1)
    chips4 = [(ix, iy)] + _other_chips()
    to_sibling_core = jnp.stack([4 * px + 2 * py + (1 - ic) for px, py in chips4]).astype(jnp.int32)
    to_my_core = jnp.stack([4 * px + 2 * py + ic for px, py in chips4]).astype(jnp.int32)
    gw_in_sibs = _mm_tn_groups(to_sibling_core, h, dproj, N_DEV, "mm_gw_in_sibling")
    gw_in_mine, gw_in_sib = _mm_tn_groups(to_my_core, h, dproj, N_DEV, "mm_gw_in_mine", to_sibling=gw_in_sibs)
    gw_in_send = _chip_presum(gw_in_mine, gw_in_sib)
    dh, gw_in_recv = _mm_nt_with_chip_exchange(dproj, w_in3, F32, gw_in_send, "mm_dh")
    dh = dh.reshape(b, s, d)
    grad_x, dshift, dscale, gg_norm = _norm_bwd(x, dh, dx2, scale, g_norm)

    dmod = jnp.concatenate([dshift, dscale, dgate], axis=1).reshape(3 * b, d)
    pkg = jnp.concatenate([dmod, gg_norm, gg_final, jnp.concatenate([gg_sb, gg_dl], axis=1),
                           jnp.zeros((2 * SMALL_ROWS - 3 * b - 3, d), F32)], axis=0)
    pkg_all = _allgather_rows(pkg, "ag_small_grads").reshape(N_DEV, 2 * SMALL_ROWS, d)
    dmod_all = pkg_all[:, :3 * b].reshape(N_DEV * b, 3 * d)
    dmod_cols = lax.dynamic_slice(dmod_all, (0, me * na), (N_DEV * b, na))
    c_rows = c_all.reshape(N_DEV, SMALL_ROWS, d)[:, :b].reshape(N_DEV * b, d)
    g_w_ada, d_w_ada, nm_w_ada, nv_w_ada = _ada_bwd_adam(c_rows, dmod_cols, w_ada[0], m_w_ada[0], v_w_ada[0])

    def pack(b_ada_like, g_norm_like, g_sb_like, g_dil_like, g_final_like):
        return jnp.concatenate([b_ada_like.reshape(3, d), g_norm_like.reshape(1, d), g_final_like.reshape(1, d),
                                jnp.concatenate([g_sb_like, g_dil_like], axis=1).reshape(1, d),
                                jnp.zeros((2, d), F32)], axis=0)

    small = _small_adam(pkg_all, 3 * b, pack(b_ada, g_norm, g_sb, g_dil, g_final),
                        pack(m_b_ada, m_g_norm, m_g_sb, m_g_dil, m_g_final),
                        pack(v_b_ada, v_g_norm, v_g_sb, v_g_dil, v_g_final))

    def unpack(p):
        return (p[0:3].reshape(1, 3 * d), p[3:4], p[5:6, :e], p[5:6, e:], p[4])

    sm_g, sm_d, sm_m, sm_v = (unpack(p) for p in small)

    g_w_in, d_w_in, nm_w_in, nv_w_in = _adam_from_chip_sums(
        gw_in_mine, gw_in_sib, gw_in_recv, w_in[0], m_w_in[0], v_w_in[0], "adam_w_in")
    g_w_out, d_w_out, nm_w_out, nv_w_out = _adam_from_partials(recv_out, w_out[0], m_w_out[0], v_w_out[0], "adam_w_out")

    loss = lax.psum(loss_p[0, 0], ("x", "y", "c"))

    def weights(ada, small_parts, w_in_part, w_out_part):
        b_ada_p, g_norm_p, g_sb_p, g_dil_p, g_final_p = small_parts
        return (ada[None], b_ada_p, g_norm_p, w_in_part[None], g_sb_p, g_dil_p, w_out_part[None], g_final_p)

    return (loss, grad_x,
            *weights(g_w_ada, sm_g, g_w_in, g_w_out),
            *weights(d_w_ada, sm_d, d_w_in, d_w_out),
            *weights(nm_w_ada, sm_m, nm_w_in, nm_w_out),
            *weights(nv_w_ada, sm_v, nv_w_in, nv_w_out))
```

```python
import functools
import math

import jax
import jax.numpy as jnp
from jax import lax
from jax.experimental import pallas as pl
from jax.experimental.pallas import tpu as pltpu

F32 = jnp.float32
BF16 = jnp.bfloat16
MESH = pl.DeviceIdType.MESH

N_DEV = 8
HEAD_DIM = 128
EPS = 1e-6
ALIBI_MAX_BIAS = 8.0
DIL_PAIRS = ((128, 1), (512, 4), (2048, 16))
DIL_STEPS = 128
NEG = -1e30

ADAM_LR = 0.001
ADAM_B1 = 0.9
ADAM_B2 = 0.999
ADAM_EPS = 1e-08
ADAM_WD = 0.01
ADAM_STEP = 10

VMEM_LIMIT_BYTES = 56 * 1024 * 1024
SMALL_ROWS = 8

NT_DIMS = (((1,), (1,)), ((), ()))
TN_DIMS = (((0,), (0,)), ((), ()))


def _params(*semantics):
    return pltpu.CompilerParams(dimension_semantics=semantics, vmem_limit_bytes=VMEM_LIMIT_BYTES)


def _tile(n, want):
    t = min(n, want)
    assert n % t == 0, (n, want)
    return t


def _mesh_pos():
    return lax.axis_index("x"), lax.axis_index("y"), lax.axis_index("c")


def _allgather_rows(x_shard, name):
    m_per, n = x_shard.shape

    def body(x_ref, out_ref, send_sems, recv_sems, local_sem):
        x, y, c = _mesh_pos()
        me, sibling = (x, y, c), (x, y, 1 - c)
        chips = [(1 - x, y), (x, 1 - y), (1 - x, 1 - y)]

        def rows(px, py, pc):
            return out_ref.at[pl.ds((4 * px + 2 * py + pc) * m_per, m_per), :]

        def copy(k, block, to, src=None):
            return pltpu.make_async_remote_copy(
                src_ref=rows(*block) if src is None else src, dst_ref=rows(*block),
                send_sem=send_sems.at[k], recv_sem=recv_sems.at[k], device_id=to, device_id_type=MESH)

        mine = pltpu.make_async_copy(x_ref, rows(*me), local_sem)
        mine.start()
        first = [copy(0, me, sibling, src=x_ref)]
        first += [copy(1 + j, me, (*chip, c), src=x_ref) for j, chip in enumerate(chips)]
        for cp in first:
            cp.start()
        passed = [copy(4 + j, (*chip, c), sibling) for j, chip in enumerate(chips)]
        for j, chip in enumerate(chips):
            copy(1 + j, (*chip, c), me).wait_recv()
            passed[j].start()
        copy(0, sibling, me).wait_recv()
        for j, chip in enumerate(chips):
            copy(4 + j, (*chip, 1 - c), me).wait_recv()
        for cp in first + passed:
            cp.wait_send()
        mine.wait()

    return pl.pallas_call(
        body, name=name,
        out_shape=jax.ShapeDtypeStruct((N_DEV * m_per, n), x_shard.dtype),
        in_specs=[pl.BlockSpec(memory_space=pltpu.VMEM)],
        out_specs=pl.BlockSpec(memory_space=pltpu.VMEM),
        scratch_shapes=[pltpu.SemaphoreType.DMA((7,)), pltpu.SemaphoreType.DMA((7,)), pltpu.SemaphoreType.DMA],
    )(x_shard)


def _two_level_gather(w_ref, out_ref, send_sems, recv_sems, local_sem):
    x, y, c = _mesh_pos()
    me, sibling = (x, y, c), (x, y, 1 - c)
    chips = [(1 - x, y), (x, 1 - y), (1 - x, 1 - y)]

    def copy(k, block, to, src=None):
        dst = out_ref.at[4 * block[0] + 2 * block[1] + block[2]]
        return pltpu.make_async_remote_copy(
            src_ref=dst if src is None else src, dst_ref=dst,
            send_sem=send_sems.at[k], recv_sem=recv_sems.at[k], device_id=to, device_id_type=MESH)

    mine = pltpu.make_async_copy(w_ref, out_ref.at[4 * x + 2 * y + c], local_sem)
    to_sibling = copy(0, me, sibling, src=w_ref)
    to_chips = [copy(1 + j, me, (*chip, c), src=w_ref) for j, chip in enumerate(chips)]
    passed = [copy(4 + j, (*chip, c), sibling) for j, chip in enumerate(chips)]

    def early():
        mine.start()
        to_sibling.start()

    def ici():
        for cp in to_chips:
            cp.start()

    def finish():
        for j, chip in enumerate(chips):
            copy(1 + j, (*chip, c), me).wait_recv()
            passed[j].start()
        copy(0, sibling, me).wait_recv()
        for j, chip in enumerate(chips):
            copy(4 + j, (*chip, 1 - c), me).wait_recv()
        for cp in [to_sibling] + to_chips + passed:
            cp.wait_send()
        mine.wait()

    return early, ici, finish


PROJ_HALVES = 2


def _unit_schedule():
    assert PROJ_HALVES == 2
    sched = [("own", None, hf) for hf in range(PROJ_HALVES)] + [("sib", None, hf) for hf in range(PROJ_HALVES)]
    for rnd in ([(0, 0), (1, 1)], [(2, 0), (2, 1)], [(0, 1), (1, 0)]):
        sched += [("direct", j, hf) for j, hf in rnd] + [("fwd", j, hf) for j, hf in rnd]
    return sched


def _unit_ids():
    x, y, c = _mesh_pos()
    chips = _other_chips()
    ids = []
    for kind, j, hf in _unit_schedule():
        px, py = (x, y) if j is None else chips[j]
        pc = c if kind in ("own", "direct") else 1 - c
        ids.append(PROJ_HALVES * (4 * px + 2 * py + pc) + hf)
    return ids


def _proj_with_allgather(h, w_own, order, w_later):
    t, d = h.shape
    nh, _, u = w_own.shape
    assert nh == PROJ_HALVES
    sched = _unit_schedule()
    n_units = len(sched)
    pos = {entry: p for p, entry in enumerate(sched)}
    tm = _tile(t, MM_TM)
    m_tiles = t // tm
    prep_m = max(m_tiles - 2, 0)
    grid = (n_units, m_tiles)

    later_ici_pos = pos[("direct", 2, 0)]

    def body(order_ref, h_ref, wown_ref, later_ref, proj_ref, w3_ref, later3_ref, bbuf, bsems, send_sems, recv_sems,
             local_sems, later_send_sems, later_recv_sems, later_local_sem):
        n, m = pl.program_id(0), pl.program_id(1)
        x, y, c = _mesh_pos()
        chips = _other_chips()
        sibling = (x, y, 1 - c)
        later_early, later_ici, later_finish = _two_level_gather(
            later_ref, later3_ref, later_send_sems, later_recv_sems, later_local_sem)

        def unit(p):
            return w3_ref.at[order_ref[p]]

        def arrival(p):
            return pltpu.make_async_remote_copy(
                src_ref=unit(p), dst_ref=unit(p), send_sem=send_sems.at[0], recv_sem=recv_sems.at[p - nh],
                device_id=sibling, device_id_type=MESH)

        def send(k, src, p_here, p_there, to):
            return pltpu.make_async_remote_copy(
                src_ref=src, dst_ref=unit(p_here), send_sem=send_sems.at[k], recv_sem=recv_sems.at[p_there - nh],
                device_id=to, device_id_type=MESH)

        copies = []

        def out(src, p_here, p_there, to):
            copies.append(send(len(copies), src, p_here, p_there, to))
            return copies[-1]

        def own(hf, p_there, to):
            return out(wown_ref.at[hf], pos[("own", None, hf)], p_there, to)

        x_nbr, y_nbr = (*chips[0], c), (*chips[1], c)
        first = [own(hf, pos[("sib", None, hf)], sibling) for hf in range(nh)]
        first += [own(0, pos[("direct", 0, 0)], x_nbr), own(1, pos[("direct", 1, 1)], y_nbr)]
        after = {}
        for kind, j, hf in sched:
            if kind == "direct":
                p = pos[(kind, j, hf)]
                after[p] = [out(unit(p), p, pos[("fwd", j, hf)], sibling)]
        p = pos[("direct", 0, 0)]
        after[p] += [out(unit(p), p, pos[("direct", 2, 0)], y_nbr), own(0, pos[("direct", 1, 0)], y_nbr)]
        p = pos[("direct", 1, 1)]
        after[p] += [out(unit(p), p, pos[("direct", 2, 1)], x_nbr), own(1, pos[("direct", 0, 1)], x_nbr)]
        locals_ = [pltpu.make_async_copy(wown_ref.at[hf], unit(pos[("own", None, hf)]), local_sems.at[hf])
                   for hf in range(nh)]

        def fetch(p):
            src = wown_ref.at[sched[p][2]] if sched[p][0] == "own" else unit(p)
            return pltpu.make_async_copy(src, bbuf.at[p % 2], bsems.at[p % 2])

        @pl.when((n == 0) & (m == 0))
        def _():
            for cp in locals_ + first:
                cp.start()
            fetch(0).start()
            later_early()

        for p in range(n_units):
            @pl.when((n == p) & (m == 0))
            def _(p=p):
                fetch(p).wait()

            if p + 1 < n_units:
                @pl.when((n == p) & (m == prep_m))
                def _(p=p):
                    if sched[p + 1][0] != "own":
                        arrival(p + 1).wait_recv()
                    for cp in after.get(p + 1, []):
                        cp.start()
                    fetch(p + 1).start()
                    if p + 1 == later_ici_pos:
                        later_ici()

        proj_ref[...] = jnp.dot(h_ref[...], bbuf[n % 2], preferred_element_type=F32).astype(proj_ref.dtype)

        @pl.when((n == n_units - 1) & (m == m_tiles - 1))
        def _():
            for cp in copies:
                cp.wait_send()
            for cp in locals_:
                cp.wait()
            later_finish()

    n_out = len(sched) - nh
    return pl.pallas_call(
        body, name="mm_proj_allgather",
        grid_spec=pltpu.PrefetchScalarGridSpec(
            num_scalar_prefetch=1, grid=grid,
            in_specs=[pl.BlockSpec((tm, d), lambda n, m, order: (m, 0)), ANY_SPEC, ANY_SPEC],
            out_specs=(pl.BlockSpec((tm, u), lambda n, m, order: (m, order[n])), ANY_SPEC, ANY_SPEC),
            scratch_shapes=[pltpu.VMEM((2, d, u), h.dtype), pltpu.SemaphoreType.DMA((2,)),
                            pltpu.SemaphoreType.DMA((n_out,)), pltpu.SemaphoreType.DMA((n_out,)),
                            pltpu.SemaphoreType.DMA((nh,)),
                            pltpu.SemaphoreType.DMA((N_DEV - 1,)), pltpu.SemaphoreType.DMA((N_DEV - 1,)),
                            pltpu.SemaphoreType.DMA]),
        out_shape=(jax.ShapeDtypeStruct((t, n_units * u), h.dtype), jax.ShapeDtypeStruct((n_units, d, u), h.dtype),
                   jax.ShapeDtypeStruct((N_DEV,) + w_later.shape, w_later.dtype)),
        compiler_params=_params("arbitrary", "arbitrary"),
    )(order, h, w_own, w_later)


ANY_SPEC = pl.BlockSpec(memory_space=pl.ANY)


def _grid_first_last(grid):
    ids = [pl.program_id(a) for a in range(len(grid))]
    first = functools.reduce(lambda p, q: p & q, [i == 0 for i in ids])
    last = functools.reduce(lambda p, q: p & q, [i == n - 1 for i, n in zip(ids, grid)])
    return first, last


def _all_to_all_copies(src, dst, send_sems, recv_sems, local_sem):
    x, y, c = _mesh_pos()
    my = 4 * x + 2 * y + c
    copies = [pltpu.make_async_copy(src.at[my], dst.at[my], local_sem)]
    for d in range(1, N_DEV):
        px = 1 - x if d & 4 else x
        py = 1 - y if d & 2 else y
        pc = 1 - c if d & 1 else c
        copies.append(pltpu.make_async_remote_copy(
            src_ref=src.at[4 * px + 2 * py + pc], dst_ref=dst.at[my],
            send_sem=send_sems.at[d - 1], recv_sem=recv_sems.at[d - 1],
            device_id=(px, py, pc), device_id_type=MESH))
    return copies


def _other_chips():
    x, y, _ = _mesh_pos()
    return [(1 - x, y), (x, 1 - y), (1 - x, 1 - y)]


def _same_core_copies(src, dst, send_sems, recv_sems):
    c = lax.axis_index("c")
    return [pltpu.make_async_remote_copy(
        src_ref=src.at[j], dst_ref=dst.at[j], send_sem=send_sems.at[j], recv_sem=recv_sems.at[j],
        device_id=(*chip, c), device_id_type=MESH) for j, chip in enumerate(_other_chips())]


def _sibling_copies(src, dst, send_sems, recv_sems):
    x, y, c = _mesh_pos()
    return [pltpu.make_async_remote_copy(
        src_ref=src.at[j], dst_ref=dst.at[j], send_sem=send_sems.at[j], recv_sem=recv_sems.at[j],
        device_id=(x, y, 1 - c), device_id_type=MESH) for j in range(src.shape[0])]


def _chip_presum(mine, from_sibling):
    _, r, cdim = mine.shape
    tr = _tile(r, 256)

    def body(p_ref, s_ref, o_ref):
        o_ref[...] = (p_ref[...].astype(F32) + s_ref[...].astype(F32)).astype(o_ref.dtype)

    blk = pl.BlockSpec((None, tr, cdim), lambda j, i: (1 + j, i, 0))
    return pl.pallas_call(
        body, name="chip_presum", grid=(3, r // tr), in_specs=[blk, blk],
        out_specs=pl.BlockSpec((None, tr, cdim), lambda j, i: (j, i, 0)),
        out_shape=jax.ShapeDtypeStruct((3, r, cdim), mine.dtype),
        compiler_params=_params("parallel", "parallel"),
    )(mine, from_sibling)


def _mm_call(a, b, dims, nk, grid, a_spec, b_spec, o_spec, out_shape, acc_shape, name):
    def body(a_ref, b_ref, o_ref, acc_ref):
        k = pl.program_id(2)

        @pl.when(k == 0)
        def _():
            acc_ref[...] = jnp.zeros_like(acc_ref)

        acc_ref[...] += lax.dot_general(a_ref[...], b_ref[...], dims, preferred_element_type=F32)

        @pl.when(k == nk - 1)
        def _():
            o_ref[...] = acc_ref[...].astype(o_ref.dtype)

    return pl.pallas_call(
        body, name=name, grid=grid, in_specs=[a_spec, b_spec], out_specs=o_spec, out_shape=out_shape,
        scratch_shapes=[pltpu.VMEM(acc_shape, F32)],
        compiler_params=_params("parallel", "parallel", "arbitrary"),
    )(a, b)


MM_TM, MM_TN, MM_TK = 1024, 2048, 1024


def _mm_nn(a, b3, out_dtype, name):
    m, kk = a.shape
    g, _, nb = b3.shape
    tm, tn, tk = _tile(m, MM_TM), _tile(nb, MM_TN), _tile(kk, MM_TK)
    npb = nb // tn
    return _mm_call(
        a, b3, (((1,), (0,)), ((), ())), kk // tk, (m // tm, g * npb, kk // tk),
        pl.BlockSpec((tm, tk), lambda i, j, k: (i, k)),
        pl.BlockSpec((None, tk, tn), lambda i, j, k: (j // npb, k, j % npb)),
        pl.BlockSpec((tm, tn), lambda i, j, k: (i, j)),
        jax.ShapeDtypeStruct((m, g * nb), out_dtype), (tm, tn), name)


def _mm_nt(a, b3, out_dtype, name):
    m, kk = a.shape
    g, n, kb = b3.shape
    tm, tn, tk = _tile(m, MM_TM), _tile(n, MM_TN), _tile(kb, MM_TK)
    kpb = kb // tk
    return _mm_call(
        a, b3, NT_DIMS, kk // tk, (m // tm, n // tn, kk // tk),
        pl.BlockSpec((tm, tk), lambda i, j, k: (i, k)),
        pl.BlockSpec((None, tn, tk), lambda i, j, k: (k // kpb, j, k % kpb)),
        pl.BlockSpec((tm, tn), lambda i, j, k: (i, j)),
        jax.ShapeDtypeStruct((m, n), out_dtype), (tm, tn), name)


def _mm_nt_with_chip_exchange(a, b3, out_dtype, send3, name):
    m, kk = a.shape
    g, n, kb = b3.shape
    tm, tn, tk = _tile(m, MM_TM), _tile(n, MM_TN), _tile(kb, MM_TK)
    kpb = kb // tk
    grid = (m // tm, n // tn, kk // tk)

    def body(a_ref, b_ref, s_ref, o_ref, r_ref, acc_ref, send_sems, recv_sems):
        first, last = _grid_first_last(grid)
        k = pl.program_id(2)

        @pl.when(first)
        def _():
            for cp in _same_core_copies(s_ref, r_ref, send_sems, recv_sems):
                cp.start()

        @pl.when(k == 0)
        def _():
            acc_ref[...] = jnp.zeros_like(acc_ref)

        acc_ref[...] += lax.dot_general(a_ref[...], b_ref[...], NT_DIMS, preferred_element_type=F32)

        @pl.when(k == grid[2] - 1)
        def _():
            o_ref[...] = acc_ref[...].astype(o_ref.dtype)

        @pl.when(last)
        def _():
            for cp in _same_core_copies(s_ref, r_ref, send_sems, recv_sems):
                cp.wait()

    return pl.pallas_call(
        body, name=name, grid=grid,
        in_specs=[pl.BlockSpec((tm, tk), lambda i, j, k: (i, k)),
                  pl.BlockSpec((None, tn, tk), lambda i, j, k: (k // kpb, j, k % kpb)), ANY_SPEC],
        out_specs=(pl.BlockSpec((tm, tn), lambda i, j, k: (i, j)), ANY_SPEC),
        out_shape=(jax.ShapeDtypeStruct((m, n), out_dtype), jax.ShapeDtypeStruct(send3.shape, send3.dtype)),
        scratch_shapes=[pltpu.VMEM((tm, tn), F32), pltpu.SemaphoreType.DMA((3,)), pltpu.SemaphoreType.DMA((3,))],
        compiler_params=_params("arbitrary", "arbitrary", "arbitrary"),
    )(a, b3, send3)


def _mm_tn_groups(groups, a, b, n_groups, name, to_sibling=None):
    t, m = a.shape
    nb = b.shape[1] // n_groups
    ng = groups.shape[0]
    tm, tk = _tile(m, MM_TM), _tile(t, MM_TK)
    grid = (m // tm, ng, t // tk)
    carry = to_sibling is not None

    def body(groups_ref, a_ref, b_ref, *rest):
        if carry:
            s_ref, o_ref, r_ref, acc_ref, send_sems, recv_sems = rest
        else:
            o_ref, acc_ref = rest
        k = pl.program_id(2)
        first, last = _grid_first_last(grid)

        if carry:
            @pl.when(first)
            def _():
                for cp in _sibling_copies(s_ref, r_ref, send_sems, recv_sems):
                    cp.start()

        @pl.when(k == 0)
        def _():
            acc_ref[...] = jnp.zeros_like(acc_ref)

        acc_ref[...] += lax.dot_general(a_ref[...], b_ref[...], TN_DIMS, preferred_element_type=F32)

        @pl.when(k == grid[2] - 1)
        def _():
            o_ref[...] = acc_ref[...].astype(o_ref.dtype)

        if carry:
            @pl.when(last)
            def _():
                for cp in _sibling_copies(s_ref, r_ref, send_sems, recv_sems):
                    cp.wait()

    shp = jax.ShapeDtypeStruct((ng, m, nb), b.dtype)
    o_spec = pl.BlockSpec((None, tm, nb), lambda i, j, k, grp: (j, i, 0))
    return pl.pallas_call(
        body, name=name,
        grid_spec=pltpu.PrefetchScalarGridSpec(
            num_scalar_prefetch=1, grid=grid,
            in_specs=[pl.BlockSpec((tk, tm), lambda i, j, k, grp: (k, i)),
                      pl.BlockSpec((tk, nb), lambda i, j, k, grp: (k, grp[j]))] + ([ANY_SPEC] if carry else []),
            out_specs=(o_spec, ANY_SPEC) if carry else o_spec,
            scratch_shapes=[pltpu.VMEM((tm, nb), F32)] + (
                [pltpu.SemaphoreType.DMA((ng,)), pltpu.SemaphoreType.DMA((ng,))] if carry else [])),
        out_shape=(shp, shp) if carry else shp,
        compiler_params=_params("arbitrary", "arbitrary", "arbitrary"),
    )(*((groups, a, b, to_sibling) if carry else (groups, a, b)))


def _mm_tn(a, b, g, out_dtype, name):
    t, m = a.shape
    nb = b.shape[1] // g
    tm, tn, tk = _tile(m, MM_TM), _tile(nb, MM_TN), _tile(t, MM_TK)
    npb = nb // tn
    return _mm_call(
        a, b, TN_DIMS, t // tk, (m // tm, g * npb, t // tk),
        pl.BlockSpec((tk, tm), lambda i, j, k: (k, i)),
        pl.BlockSpec((tk, tn), lambda i, j, k: (k, j)),
        pl.BlockSpec((None, tm, tn), lambda i, j, k: (j // npb, i, j % npb)),
        jax.ShapeDtypeStruct((g, m, nb), out_dtype), (tm, tn), name)


def _silu(z):
    return z * jax.nn.sigmoid(z)


def _ada_fwd(c_all, w_shard, b_own):
    r, d = c_all.shape
    na = w_shard.shape[1]
    tk = _tile(d, 512)
    nk = d // tk

    def body(c_ref, w_ref, b_ref, o_ref):
        k = pl.program_id(0)

        @pl.when(k == 0)
        def _():
            o_ref[...] = jnp.zeros_like(o_ref) + b_ref[...]

        cs = _silu(c_ref[...]).astype(BF16)
        o_ref[...] += jnp.dot(cs, w_ref[...].astype(BF16), preferred_element_type=F32)

    return pl.pallas_call(
        body, name="ada_fwd", grid=(nk,),
        in_specs=[pl.BlockSpec((r, tk), lambda k: (0, k)), pl.BlockSpec((tk, na), lambda k: (k, 0)),
                  pl.BlockSpec((1, na), lambda k: (0, 0))],
        out_specs=pl.BlockSpec((r, na), lambda k: (0, 0)),
        out_shape=jax.ShapeDtypeStruct((r, na), F32),
        compiler_params=_params("arbitrary"),
    )(c_all, w_shard, b_own)


def _adam(w, g, m, v):
    nm = ADAM_B1 * m + (1.0 - ADAM_B1) * g
    nv = ADAM_B2 * v + (1.0 - ADAM_B2) * (g * g)
    m_hat = nm / (1.0 - ADAM_B1 ** ADAM_STEP)
    v_hat = nv / (1.0 - ADAM_B2 ** ADAM_STEP)
    delta = -ADAM_LR * (m_hat / (jnp.sqrt(v_hat) + ADAM_EPS) + ADAM_WD * w)
    return delta, nm, nv


def _ada_bwd_adam(c_rows, dmod_cols, w, m, v):
    bg, d = c_rows.shape
    na = w.shape[1]
    tr = _tile(d, 256)

    def body(c_ref, dm_ref, w_ref, m_ref, v_ref, g_ref, d_ref, nm_ref, nv_ref):
        cs = _silu(c_ref[...]).astype(BF16)
        g = lax.dot_general(cs, dm_ref[...].astype(BF16), TN_DIMS, preferred_element_type=F32)
        delta, nm, nv = _adam(w_ref[...], g, m_ref[...], v_ref[...])
        g_ref[...] = g
        d_ref[...] = delta
        nm_ref[...] = nm
        nv_ref[...] = nv

    blk = pl.BlockSpec((tr, na), lambda i: (i, 0))
    shp = jax.ShapeDtypeStruct((d, na), F32)
    return pl.pallas_call(
        body, name="ada_bwd_adam", grid=(d // tr,),
        in_specs=[pl.BlockSpec((bg, tr), lambda i: (0, i)), pl.BlockSpec((bg, na), lambda i: (0, 0)), blk, blk, blk],
        out_specs=(blk, blk, blk, blk), out_shape=(shp, shp, shp, shp),
        compiler_params=_params("parallel"),
    )(c_rows, dmod_cols, w, m, v)


def _small_adam(pkg_all, n_batch_rows, w, m, v):
    d = w.shape[1]

    def body(p_ref, w_ref, m_ref, v_ref, g_ref, d_ref, nm_ref, nv_ref):
        for part in range(3):
            acc = jnp.zeros((1, d), F32)
            for dev in range(N_DEV):
                for b in range(n_batch_rows // 3):
                    acc = acc + p_ref[dev, 3 * b + part:3 * b + part + 1, :]
            g_ref[part:part + 1, :] = acc
        for rrow in range(3):
            acc = jnp.zeros((1, d), F32)
            for dev in range(N_DEV):
                acc = acc + p_ref[dev, n_batch_rows + rrow:n_batch_rows + rrow + 1, :]
            g_ref[3 + rrow:4 + rrow, :] = acc
        g_ref[6:8, :] = jnp.zeros((2, d), F32)
        g = g_ref[...]
        delta, nm, nv = _adam(w_ref[...], g, m_ref[...], v_ref[...])
        d_ref[...] = delta
        nm_ref[...] = nm
        nv_ref[...] = nv

    vm = pl.BlockSpec(memory_space=pltpu.VMEM)
    shp = jax.ShapeDtypeStruct((SMALL_ROWS, d), F32)
    return pl.pallas_call(
        body, name="small_adam", in_specs=[vm, vm, vm, vm], out_specs=(vm, vm, vm, vm),
        out_shape=(shp, shp, shp, shp),
    )(pkg_all, w, m, v)


def _adam_from_chip_sums(mine, from_sibling, from_chips, w, m, v, name):
    _, r, c = mine.shape
    tr = _tile(r, 128)

    def body(p_ref, s_ref, f_ref, w_ref, m_ref, v_ref, g_ref, d_ref, nm_ref, nv_ref):
        g = p_ref[...].astype(F32) + s_ref[...].astype(F32)
        for j in range(3):
            g = g + f_ref[j].astype(F32)
        delta, nm, nv = _adam(w_ref[...], g, m_ref[...], v_ref[...])
        g_ref[...] = g
        d_ref[...] = delta
        nm_ref[...] = nm
        nv_ref[...] = nv

    blk = pl.BlockSpec((tr, c), lambda i: (i, 0))
    slot0 = pl.BlockSpec((None, tr, c), lambda i: (0, i, 0))
    shp = jax.ShapeDtypeStruct((r, c), F32)
    return pl.pallas_call(
        body, name=name, grid=(r // tr,),
        in_specs=[slot0, slot0, pl.BlockSpec((3, tr, c), lambda i: (0, i, 0)), blk, blk, blk],
        out_specs=(blk, blk, blk, blk), out_shape=(shp, shp, shp, shp),
        compiler_params=_params("parallel"),
    )(mine, from_sibling, from_chips, w, m, v)


def _adam_from_partials(recv, w, m, v, name):
    _, r, c = recv.shape
    tr = _tile(r, 128)

    def body(p_ref, w_ref, m_ref, v_ref, g_ref, d_ref, nm_ref, nv_ref):
        g = p_ref[0].astype(F32)
        for dev in range(1, N_DEV):
            g = g + p_ref[dev].astype(F32)
        delta, nm, nv = _adam(w_ref[...], g, m_ref[...], v_ref[...])
        g_ref[...] = g
        d_ref[...] = delta
        nm_ref[...] = nm
        nv_ref[...] = nv

    blk = pl.BlockSpec((tr, c), lambda i: (i, 0))
    shp = jax.ShapeDtypeStruct((r, c), F32)
    return pl.pallas_call(
        body, name=name, grid=(r // tr,),
        in_specs=[pl.BlockSpec((N_DEV, tr, c), lambda i: (0, i, 0)), blk, blk, blk],
        out_specs=(blk, blk, blk, blk), out_shape=(shp, shp, shp, shp),
        compiler_params=_params("parallel"),
    )(recv, w, m, v)


def _norm_mod(x, g_norm, scale, shift):
    b, s, d = x.shape
    ts = _tile(s, 256)

    def body(x_ref, g_ref, sc_ref, sh_ref, h_ref):
        xv = x_ref[...]
        r = lax.rsqrt(jnp.mean(xv * xv, axis=-1, keepdims=True) + EPS)
        xn = (xv * r) * g_ref[...]
        h_ref[...] = (xn * (1.0 + sc_ref[...]) + sh_ref[...]).astype(BF16)

    tok = pl.BlockSpec((None, ts, d), lambda i, j: (i, j, 0))
    per_b = pl.BlockSpec((None, 1, d), lambda i, j: (i, 0, 0))
    return pl.pallas_call(
        body, name="norm_mod", grid=(b, s // ts),
        in_specs=[tok, pl.BlockSpec((1, d), lambda i, j: (0, 0)), per_b, per_b],
        out_specs=tok, out_shape=jax.ShapeDtypeStruct((b, s, d), BF16),
        compiler_params=_params("parallel", "parallel"),
    )(x, g_norm, scale, shift)


def _final_fwd_bwd(x, out, gate, g_final, target):
    b, s, d = x.shape
    ts = _tile(s, 256)

    def body(x_ref, o_ref, gt_ref, g_ref, t_ref, loss_ref, dx2_ref, dout_ref, dgate_ref, gg_ref):
        i, j = pl.program_id(0), pl.program_id(1)

        @pl.when((i == 0) & (j == 0))
        def _():
            loss_ref[...] = jnp.zeros_like(loss_ref)
            gg_ref[...] = jnp.zeros_like(gg_ref)

        @pl.when(j == 0)
        def _():
            dgate_ref[...] = jnp.zeros_like(dgate_ref)

        ov = o_ref[...]
        gt = gt_ref[...]
        x2 = x_ref[...] + gt * ov
        r = lax.rsqrt(jnp.mean(x2 * x2, axis=-1, keepdims=True) + EPS)
        xh = x2 * r
        err = xh * g_ref[...] - t_ref[...]
        loss_ref[...] += 0.5 * jnp.sum(jnp.mean(err * err, axis=-1, keepdims=True), axis=0, keepdims=True)
        dfin = err * (1.0 / d)
        gg_ref[...] += jnp.sum(dfin * xh, axis=0, keepdims=True)
        dxh = dfin * g_ref[...]
        dx2 = r * (dxh - xh * jnp.mean(dxh * xh, axis=-1, keepdims=True))
        dx2_ref[...] = dx2
        dout_ref[...] = (gt * dx2).astype(BF16)
        dgate_ref[...] += jnp.sum(dx2 * ov, axis=0, keepdims=True)

    tok = pl.BlockSpec((None, ts, d), lambda i, j: (i, j, 0))
    per_b = pl.BlockSpec((None, 1, d), lambda i, j: (i, 0, 0))
    vec = pl.BlockSpec((1, d), lambda i, j: (0, 0))
    return pl.pallas_call(
        body, name="final_fwd_bwd", grid=(b, s // ts),
        in_specs=[tok, tok, per_b, vec, tok],
        out_specs=(pl.BlockSpec((8, 128), lambda i, j: (0, 0)), tok, tok, per_b, vec),
        out_shape=(jax.ShapeDtypeStruct((8, 128), F32), jax.ShapeDtypeStruct((b, s, d), F32),
                   jax.ShapeDtypeStruct((b, s, d), BF16), jax.ShapeDtypeStruct((b, 1, d), F32),
                   jax.ShapeDtypeStruct((1, d), F32)),
        compiler_params=_params("arbitrary", "arbitrary"),
    )(x, out, gate, g_final, target)


def _norm_bwd(x, dh, dx2, scale, g_norm):
    b, s, d = x.shape
    ts = _tile(s, 256)

    def body(x_ref, dh_ref, dx2_ref, sc_ref, g_ref, gx_ref, dsh_ref, dsc_ref, gg_ref):
        i, j = pl.program_id(0), pl.program_id(1)

        @pl.when((i == 0) & (j == 0))
        def _():
            gg_ref[...] = jnp.zeros_like(gg_ref)

        @pl.when(j == 0)
        def _():
            dsh_ref[...] = jnp.zeros_like(dsh_ref)
            dsc_ref[...] = jnp.zeros_like(dsc_ref)

        xv = x_ref[...]
        dhv = dh_ref[...]
        r = lax.rsqrt(jnp.mean(xv * xv, axis=-1, keepdims=True) + EPS)
        xh = xv * r
        xn = xh * g_ref[...]
        dsh_ref[...] += jnp.sum(dhv, axis=0, keepdims=True)
        dsc_ref[...] += jnp.sum(dhv * xn, axis=0, keepdims=True)
        dxn = dhv * (1.0 + sc_ref[...])
        gg_ref[...] += jnp.sum(dxn * xh, axis=0, keepdims=True)
        dxh = dxn * g_ref[...]
        gx_ref[...] = dx2_ref[...] + r * (dxh - xh * jnp.mean(dxh * xh, axis=-1, keepdims=True))

    tok = pl.BlockSpec((None, ts, d), lambda i, j: (i, j, 0))
    per_b = pl.BlockSpec((None, 1, d), lambda i, j: (i, 0, 0))
    vec = pl.BlockSpec((1, d), lambda i, j: (0, 0))
    return pl.pallas_call(
        body, name="norm_bwd", grid=(b, s // ts),
        in_specs=[tok, tok, tok, per_b, vec],
        out_specs=(tok, per_b, per_b, vec),
        out_shape=(jax.ShapeDtypeStruct((b, s, d), F32), jax.ShapeDtypeStruct((b, 1, d), F32),
                   jax.ShapeDtypeStruct((b, 1, d), F32), jax.ShapeDtypeStruct((1, d), F32)),
        compiler_params=_params("arbitrary", "arbitrary"),
    )(x, dh, dx2, scale, g_norm)


def _gate_fwd(y_sb, y_dl, proj, g_sb, g_dl):
    t, e = y_sb.shape
    n_heads = e // HEAD_DIM
    tt = _tile(t, 256)

    def body(ys_ref, yd_ref, zs_ref, zd_ref, gs_ref, gd_ref, o_ref):
        for grp, (y_ref, z_ref, g_ref) in enumerate(((ys_ref, zs_ref, gs_ref), (yd_ref, zd_ref, gd_ref))):
            for h in range(n_heads):
                sl = slice(h * HEAD_DIM, (h + 1) * HEAD_DIM)
                y = y_ref[:, sl]
                r = lax.rsqrt(jnp.mean(y * y, axis=-1, keepdims=True) + EPS)
                yn = (y * r) * g_ref[:, sl]
                z = z_ref[:, sl].astype(F32)
                o_ref[:, grp * e + h * HEAD_DIM:grp * e + (h + 1) * HEAD_DIM] = (yn * _silu(z)).astype(BF16)

    yblk = pl.BlockSpec((tt, e), lambda i: (i, 0))
    gblk = pl.BlockSpec((1, e), lambda i: (0, 0))
    return pl.pallas_call(
        body, name="gate_fwd", grid=(t // tt,),
        in_specs=[yblk, yblk, pl.BlockSpec((tt, e), lambda i: (i, 3)), pl.BlockSpec((tt, e), lambda i: (i, 7)),
                  gblk, gblk],
        out_specs=pl.BlockSpec((tt, 2 * e), lambda i: (i, 0)),
        out_shape=jax.ShapeDtypeStruct((t, 2 * e), BF16),
        compiler_params=_params("parallel"),
    )(y_sb, y_dl, proj, proj, g_sb, g_dl)


def _gate_bwd(dyg, y_sb, y_dl, proj, g_sb, g_dl):
    t, e = y_sb.shape
    n_heads = e // HEAD_DIM
    tt = _tile(t, 256)

    def body(dg_ref, ys_ref, yd_ref, zs_ref, zd_ref, gs_ref, gd_ref,
             dys_ref, dyd_ref, dzs_ref, dzd_ref, ggs_ref, ggd_ref):
        @pl.when(pl.program_id(0) == 0)
        def _():
            ggs_ref[...] = jnp.zeros_like(ggs_ref)
            ggd_ref[...] = jnp.zeros_like(ggd_ref)

        groups = ((ys_ref, zs_ref, gs_ref, dys_ref, dzs_ref, ggs_ref), (yd_ref, zd_ref, gd_ref, dyd_ref, dzd_ref, ggd_ref))
        for grp, (y_ref, z_ref, g_ref, dy_ref, dz_ref, gg_ref) in enumerate(groups):
            for h in range(n_heads):
                sl = slice(h * HEAD_DIM, (h + 1) * HEAD_DIM)
                dg = dg_ref[:, grp * e + h * HEAD_DIM:grp * e + (h + 1) * HEAD_DIM].astype(F32)
                y = y_ref[:, sl]
                z = z_ref[:, sl].astype(F32)
                g = g_ref[:, sl]
                r = lax.rsqrt(jnp.mean(y * y, axis=-1, keepdims=True) + EPS)
                yh = y * r
                sig = jax.nn.sigmoid(z)
                dyn = dg * (z * sig)
                dz_ref[:, sl] = (dg * (yh * g) * (sig * (1.0 + z * (1.0 - sig)))).astype(BF16)
                gg_ref[:, sl] += jnp.sum(dyn * yh, axis=0, keepdims=True)
                dyh = dyn * g
                dy_ref[:, sl] = (r * (dyh - yh * jnp.mean(dyh * yh, axis=-1, keepdims=True))).astype(BF16)

    yblk = pl.BlockSpec((tt, e), lambda i: (i, 0))
    gblk = pl.BlockSpec((1, e), lambda i: (0, 0))
    act = jax.ShapeDtypeStruct((t, e), BF16)
    vec = jax.ShapeDtypeStruct((1, e), F32)
    return pl.pallas_call(
        body, name="gate_bwd", grid=(t // tt,),
        in_specs=[pl.BlockSpec((tt, 2 * e), lambda i: (i, 0)), yblk, yblk,
                  pl.BlockSpec((tt, e), lambda i: (i, 3)), pl.BlockSpec((tt, e), lambda i: (i, 7)), gblk, gblk],
        out_specs=(yblk, yblk, yblk, yblk, gblk, gblk),
        out_shape=(act, act, act, act, vec, vec),
        compiler_params=_params("arbitrary"),
    )(dyg, y_sb, y_dl, proj, proj, g_sb, g_dl)


ATT_TQ = 256
HEADS_PER_STEP = 4
SOFTPLUS_CLAMP = 30.0
ATT_STRIP = 32


def _split2_dot(x, u):
    hi = x.astype(BF16)
    lo = (x - hi.astype(F32)).astype(BF16)
    n = x.shape[0]
    both = jnp.dot(jnp.concatenate([hi, lo], axis=0), u, preferred_element_type=F32)
    return both[:n] + both[n:]


def _iota2(n):
    return lax.broadcasted_iota(jnp.int32, (n, n), 0), lax.broadcasted_iota(jnp.int32, (n, n), 1)


def _head_slices():
    return [slice(hh * HEAD_DIM, (hh + 1) * HEAD_DIM) for hh in range(HEADS_PER_STEP)]


def _att_specs(s, e, tq, col0):
    n_heads = e // HEAD_DIM
    hp = HEADS_PER_STEP
    assert n_heads % hp == 0 and col0 % hp == 0
    w = hp * HEAD_DIM
    q_spec = pl.BlockSpec((None, tq, w), lambda i, h, j: (i, j, col0 // hp + h))
    k_spec = pl.BlockSpec((None, s, w), lambda i, h, j: (i, 0, (col0 + n_heads) // hp + h))
    v_spec = pl.BlockSpec((None, s, w), lambda i, h, j: (i, 0, (col0 + 2 * n_heads) // hp + h))
    return q_spec, k_spec, v_spec


def _sb_fwd(proj3, e):
    b, s, _ = proj3.shape
    n_heads = e // HEAD_DIM
    hp = HEADS_PER_STEP
    tq = _tile(s, ATT_TQ)
    nq = s // tq
    inv = 1.0 / math.sqrt(HEAD_DIM)

    def body(q_ref, k_ref, v_ref, y_ref, tot_ref, acc_ref, car_ref, lb_ref, lhs_ref, a_ref, rs_ref):
        i = pl.program_id(2)
        row, col = _iota2(tq)
        before = row > col
        u_after = before.astype(BF16)
        acc_ref[...] = jnp.zeros_like(acc_ref)
        car_ref[...] = jnp.zeros_like(car_ref)
        strips = [slice(r0, r0 + ATT_STRIP) for r0 in range(0, tq, ATT_STRIP)]

        def block(j, diagonal):
            keys = pl.ds(pl.multiple_of(j * tq, tq), tq)
            heads = list(enumerate(_head_slices()))
            raws = [lax.dot_general(q_ref[:, hs], k_ref[keys, hs], NT_DIMS, preferred_element_type=F32)
                    for _, hs in heads]
            for hh, _ in heads:
                for rows in strips:
                    z = raws[hh][rows] * inv
                    sp = jnp.maximum(jnp.log(1.0 + jnp.exp(jnp.minimum(z, SOFTPLUS_CLAMP))), z)
                    lom = jnp.where(before[rows], -sp, 0.0) if diagonal else -sp
                    lb_ref[hh, rows] = z - sp
                    hi = lom.astype(BF16)
                    lhs_ref[hh, rows] = hi
                    lhs_ref[hh, slice(tq + rows.start, tq + rows.stop)] = (lom - hi.astype(F32)).astype(BF16)
                    rs_ref[hh, rows] = jnp.sum(lom, axis=1, keepdims=True)
            sums = [jnp.dot(lhs_ref[hh], u_after, preferred_element_type=F32) for hh, _ in heads]
            for hh, _ in heads:
                for rows in strips:
                    suffix = (sums[hh][rows] + sums[hh][tq + rows.start:tq + rows.stop]) + car_ref[hh, rows]
                    a = jnp.exp(lb_ref[hh, rows] + suffix)
                    if diagonal:
                        a = jnp.where(before[rows], a, 0.0)
                    a_ref[hh, rows] = a.astype(BF16)
            pvs = [jnp.dot(a_ref[hh], v_ref[keys, hs], preferred_element_type=F32) for hh, hs in heads]
            for hh, _ in heads:
                acc_ref[hh] += pvs[hh]
                car_ref[hh] += rs_ref[hh]

        block(i, True)

        def step(it, carry):
            block(i - it, False)
            return carry

        lax.fori_loop(1, i + 1, step, 0)
        for hh, hs in enumerate(_head_slices()):
            y_ref[:, hs] = acc_ref[hh]
            tot_ref[:, hs] = jnp.broadcast_to(car_ref[hh], (tq, HEAD_DIM))

    q_spec, k_spec, v_spec = _att_specs(s, e, tq, 0)
    blk_q = pl.BlockSpec((None, tq, hp * HEAD_DIM), lambda i, h, j: (i, j, h))
    shp = jax.ShapeDtypeStruct((b, s, e), F32)
    return pl.pallas_call(
        body, name="sb_fwd", grid=(b, n_heads // hp, nq),
        in_specs=[q_spec, k_spec, v_spec],
        out_specs=(blk_q, blk_q), out_shape=(shp, shp),
        scratch_shapes=[pltpu.VMEM((hp, tq, HEAD_DIM), F32), pltpu.VMEM((hp, tq, 1), F32),
                        pltpu.VMEM((hp, tq, tq), F32), pltpu.VMEM((hp, 2 * tq, tq), BF16),
                        pltpu.VMEM((hp, tq, tq), BF16), pltpu.VMEM((hp, tq, 1), F32)],
        compiler_params=_params("parallel", "parallel", "arbitrary"),
    )(proj3, proj3, proj3)


def _sb_bwd(proj3, lom_total, dy, partials):
    b, s, e = dy.shape
    n_heads = e // HEAD_DIM
    hp = HEADS_PER_STEP
    tq = _tile(s, ATT_TQ)
    nq = s // tq
    inv = 1.0 / math.sqrt(HEAD_DIM)
    grid = (b, n_heads // hp, nq)

    def body(q_ref, k_ref, v_ref, tot_ref, dy_ref, p_ref, dq_ref, dk_ref, dv_ref, r_ref,
             dqa, dka, dva, car, car2, send_sems, recv_sems, local_sem):
        i = pl.program_id(2)
        first, last = _grid_first_last(grid)

        @pl.when(first)
        def _():
            for cp in _all_to_all_copies(p_ref, r_ref, send_sems, recv_sems, local_sem):
                cp.start()

        @pl.when(i == 0)
        def _():
            dka[...] = jnp.zeros_like(dka)
            dva[...] = jnp.zeros_like(dva)

        row, col = _iota2(tq)
        before = row > col
        u_upto = (row <= col).astype(BF16)
        u_before = (row < col).astype(BF16)
        dqa[...] = jnp.zeros_like(dqa)
        car[...] = jnp.zeros_like(car)
        car2[...] = jnp.zeros_like(car2)

        def block(j, diagonal):
            keys = pl.ds(pl.multiple_of(j * tq, tq), tq)
            heads = list(enumerate(_head_slices()))
            zs = [lax.dot_general(q_ref[:, hs], k_ref[keys, hs], NT_DIMS, preferred_element_type=F32) * inv
                  for _, hs in heads]
            das = [lax.dot_general(dy_ref[:, hs], v_ref[keys, hs], NT_DIMS, preferred_element_type=F32) for _, hs in heads]
            ezs = [jnp.exp(jnp.minimum(z, SOFTPLUS_CLAMP)) for z in zs]
            sps = [jnp.maximum(jnp.log(1.0 + ezs[hh]), zs[hh]) for hh, _ in heads]
            loms = [jnp.where(before, -sp, 0.0) if diagonal else -sp for sp in sps]
            sufs = [tot_ref[:, hh * HEAD_DIM:hh * HEAD_DIM + 1] - (_split2_dot(loms[hh], u_upto) + car[hh])
                    for hh, _ in heads]
            avs = [jnp.exp((zs[hh] - sps[hh]) + sufs[hh]) for hh, _ in heads]
            if diagonal:
                avs = [jnp.where(before, a, 0.0) for a in avs]
            dls = [avs[hh] * das[hh] for hh, _ in heads]
            prefixes = [_split2_dot(dls[hh], u_before) + car2[hh] for hh, _ in heads]
            dzs = []
            for hh, _ in heads:
                one_minus_beta = 1.0 / (1.0 + ezs[hh])
                dz = (dls[hh] * one_minus_beta - prefixes[hh] * (ezs[hh] * one_minus_beta)) * inv
                if diagonal:
                    dz = jnp.where(before, dz, 0.0)
                dzs.append(dz.astype(BF16))
            dqs = [jnp.dot(dzs[hh], k_ref[keys, hs], preferred_element_type=F32) for hh, hs in heads]
            dks = [lax.dot_general(dzs[hh], q_ref[:, hs], TN_DIMS, preferred_element_type=F32) for hh, hs in heads]
            dvs = [lax.dot_general(avs[hh].astype(BF16), dy_ref[:, hs], TN_DIMS, preferred_element_type=F32)
                   for hh, hs in heads]
            for hh, hs in heads:
                dqa[hh] += dqs[hh]
                dka[keys, hs] += dks[hh]
                dva[keys, hs] += dvs[hh]
                car[hh] += jnp.sum(loms[hh], axis=1, keepdims=True)
                car2[hh] += jnp.sum(dls[hh], axis=1, keepdims=True)

        def step(j, carry):
            block(j, False)
            return carry

        lax.fori_loop(0, i, step, 0)
        block(i, True)
        for hh, hs in enumerate(_head_slices()):
            dq_ref[:, hs] = dqa[hh].astype(BF16)

        @pl.when(i == nq - 1)
        def _():
            dk_ref[...] = dka[...].astype(BF16)
            dv_ref[...] = dva[...].astype(BF16)

        @pl.when(last)
        def _():
            for cp in _all_to_all_copies(p_ref, r_ref, send_sems, recv_sems, local_sem):
                cp.wait()

    q_spec, k_spec, v_spec = _att_specs(s, e, tq, 0)
    w = hp * HEAD_DIM
    blk_q = pl.BlockSpec((None, tq, w), lambda i, h, j: (i, j, h))
    blk_kv = pl.BlockSpec((None, s, w), lambda i, h, j: (i, 0, h))
    shp = jax.ShapeDtypeStruct((b, s, e), BF16)
    return pl.pallas_call(
        body, name="sb_bwd", grid=grid,
        in_specs=[q_spec, k_spec, v_spec, blk_q, blk_q, ANY_SPEC],
        out_specs=(blk_q, blk_kv, blk_kv, ANY_SPEC),
        out_shape=(shp, shp, shp, jax.ShapeDtypeStruct(partials.shape, partials.dtype)),
        scratch_shapes=[pltpu.VMEM((hp, tq, HEAD_DIM), F32), pltpu.VMEM((s, w), F32), pltpu.VMEM((s, w), F32),
                        pltpu.VMEM((hp, tq, 1), F32), pltpu.VMEM((hp, tq, 1), F32),
                        pltpu.SemaphoreType.DMA((N_DEV - 1,)), pltpu.SemaphoreType.DMA((N_DEV - 1,)),
                        pltpu.SemaphoreType.DMA],
        compiler_params=_params("arbitrary", "arbitrary", "arbitrary"),
    )(proj3, proj3, proj3, lom_total, dy, partials)


def _dil_near_tiles(tq):
    return (DIL_PAIRS[1][0] + tq - 1) // tq + 1


def _dil_fill_bias(bias_ref, sl_ref, tq):
    row, col = _iota2(tq)
    for hh in range(HEADS_PER_STEP):
        slope = sl_ref[hh, 0:1, 0:1]
        for d in range(_dil_near_tiles(tq) + 1):
            dist = d * tq + row - col
            cnt = jnp.zeros(dist.shape, jnp.int32)
            for window, dilation in DIL_PAIRS:
                cnt = cnt + (((dist & (dilation - 1)) == 0) & (dist <= window)).astype(jnp.int32)
            bias = jnp.where(cnt == 3, math.log(3.0), jnp.where(cnt == 2, math.log(2.0), 0.0))
            bias_ref[hh, d] = jnp.where((dist >= 0) & (cnt > 0), bias - slope * dist.astype(F32), NEG)


def _dil_fwd(proj3, e, slopes):
    b, s, _ = proj3.shape
    n_heads = e // HEAD_DIM
    hp = HEADS_PER_STEP
    tq = _tile(s, ATT_TQ)
    nq = s // tq
    inv = 1.0 / math.sqrt(HEAD_DIM)
    assert s <= DIL_PAIRS[2][0]

    def body(q_ref, k_ref, v_ref, sl_ref, y_ref, lse_ref, acc_ref, m_ref, l_ref, bias_ref, p_ref):
        i = pl.program_id(2)

        @pl.when(i == 0)
        def _():
            _dil_fill_bias(bias_ref, sl_ref, tq)

        acc_ref[...] = jnp.zeros_like(acc_ref)
        m_ref[...] = jnp.full_like(m_ref, NEG)
        l_ref[...] = jnp.zeros_like(l_ref)

        def step(it, carry):
            keys = pl.ds(pl.multiple_of((i - it) * tq, tq), tq)
            heads = list(enumerate(_head_slices()))
            near = _dil_near_tiles(tq)
            tile = jnp.minimum(it, near)
            beyond = jnp.maximum(it - near, 0).astype(F32) * float(tq)
            raws = [lax.dot_general(q_ref[:, hs], k_ref[keys, hs], NT_DIMS, preferred_element_type=F32)
                    for _, hs in heads]
            for hh, _ in heads:
                shift = sl_ref[hh, 0:1, 0:1] * beyond
                for r0 in range(0, tq, ATT_STRIP):
                    rows = slice(r0, r0 + ATT_STRIP)
                    sc = (raws[hh][rows] * inv + bias_ref[hh, tile, rows, :]) - shift
                    m_old = m_ref[hh, rows]
                    m_new = jnp.maximum(m_old, jnp.max(sc, axis=1, keepdims=True))
                    p = jnp.exp(sc - m_new)
                    alpha = jnp.exp(m_old - m_new)
                    l_ref[hh, rows] = alpha * l_ref[hh, rows] + (p[:, :tq // 2] + p[:, tq // 2:])
                    acc_ref[hh, rows] = alpha * acc_ref[hh, rows]
                    p_ref[hh, rows] = p.astype(BF16)
                    m_ref[hh, rows] = m_new
            pvs = [jnp.dot(p_ref[hh], v_ref[keys, hs], preferred_element_type=F32) for hh, hs in heads]
            for hh, _ in heads:
                acc_ref[hh] += pvs[hh]
            return carry

        lax.fori_loop(0, i + 1, step, 0)
        for hh, hs in enumerate(_head_slices()):
            l = jnp.sum(l_ref[hh], axis=1, keepdims=True)
            y_ref[:, hs] = acc_ref[hh] / l
            lse_ref[:, hs] = jnp.broadcast_to(m_ref[hh] + jnp.log(l), (tq, HEAD_DIM))

    q_spec, k_spec, v_spec = _att_specs(s, e, tq, 4 * n_heads)
    blk_q = pl.BlockSpec((None, tq, hp * HEAD_DIM), lambda i, h, j: (i, j, h))
    shp = jax.ShapeDtypeStruct((b, s, e), F32)
    return pl.pallas_call(
        body, name="dil_fwd", grid=(b, n_heads // hp, nq),
        in_specs=[q_spec, k_spec, v_spec, pl.BlockSpec((hp, 8, HEAD_DIM), lambda i, h, j: (h, 0, 0))],
        out_specs=(blk_q, blk_q), out_shape=(shp, shp),
        scratch_shapes=[pltpu.VMEM((hp, tq, HEAD_DIM), F32), pltpu.VMEM((hp, tq, 1), F32), pltpu.VMEM((hp, tq, tq // 2), F32),
                        pltpu.VMEM((hp, _dil_near_tiles(tq) + 1, tq, tq), F32), pltpu.VMEM((hp, tq, tq), BF16)],
        compiler_params=_params("parallel", "parallel", "arbitrary"),
    )(proj3, proj3, proj3, slopes)


def _dil_bwd(proj3, y, lse, dy, slopes):
    b, s, e = y.shape
    n_heads = e // HEAD_DIM
    hp = HEADS_PER_STEP
    tq = _tile(s, ATT_TQ)
    nq = s // tq
    inv = 1.0 / math.sqrt(HEAD_DIM)

    def body(q_ref, k_ref, v_ref, sl_ref, y_ref, lse_ref, dy_ref, dq_ref, dk_ref, dv_ref, dqa, dka, dva, bias_ref,
             p_ref, ds_ref):
        i = pl.program_id(2)

        @pl.when(i == 0)
        def _():
            dka[...] = jnp.zeros_like(dka)
            dva[...] = jnp.zeros_like(dva)
            _dil_fill_bias(bias_ref, sl_ref, tq)

        delta = [jnp.sum(dy_ref[:, hs].astype(F32) * y_ref[:, hs], axis=1, keepdims=True) for hs in _head_slices()]
        dqa[...] = jnp.zeros_like(dqa)

        def step(it, carry):
            keys = pl.ds(pl.multiple_of((i - it) * tq, tq), tq)
            heads = list(enumerate(_head_slices()))
            near = _dil_near_tiles(tq)
            tile = jnp.minimum(it, near)
            beyond = jnp.maximum(it - near, 0).astype(F32) * float(tq)
            raws = [lax.dot_general(q_ref[:, hs], k_ref[keys, hs], NT_DIMS, preferred_element_type=F32)
                    for _, hs in heads]
            dps = [lax.dot_general(dy_ref[:, hs], v_ref[keys, hs], NT_DIMS, preferred_element_type=F32) for _, hs in heads]
            for hh, _ in heads:
                shift = sl_ref[hh, 0:1, 0:1] * beyond
                for r0 in range(0, tq, ATT_STRIP):
                    rows = slice(r0, r0 + ATT_STRIP)
                    sc = (raws[hh][rows] * inv + bias_ref[hh, tile, rows, :]) - shift
                    p = jnp.exp(sc - lse_ref[rows, hh * HEAD_DIM:hh * HEAD_DIM + 1])
                    p_ref[hh, rows] = p.astype(BF16)
                    ds_ref[hh, rows] = ((p * (dps[hh][rows] - delta[hh][rows])) * inv).astype(BF16)
            dqs = [jnp.dot(ds_ref[hh], k_ref[keys, hs], preferred_element_type=F32) for hh, hs in heads]
            dks = [lax.dot_general(ds_ref[hh], q_ref[:, hs], TN_DIMS, preferred_element_type=F32) for hh, hs in heads]
            dvs = [lax.dot_general(p_ref[hh], dy_ref[:, hs], TN_DIMS, preferred_element_type=F32) for hh, hs in heads]
            for hh, hs in heads:
                dqa[hh] += dqs[hh]
                dka[keys, hs] += dks[hh]
                dva[keys, hs] += dvs[hh]
            return carry

        lax.fori_loop(0, i + 1, step, 0)
        for hh, hs in enumerate(_head_slices()):
            dq_ref[:, hs] = dqa[hh].astype(BF16)

        @pl.when(i == nq - 1)
        def _():
            dk_ref[...] = dka[...].astype(BF16)
            dv_ref[...] = dva[...].astype(BF16)

    q_spec, k_spec, v_spec = _att_specs(s, e, tq, 4 * n_heads)
    w = hp * HEAD_DIM
    blk_q = pl.BlockSpec((None, tq, w), lambda i, h, j: (i, j, h))
    blk_kv = pl.BlockSpec((None, s, w), lambda i, h, j: (i, 0, h))
    shp = jax.ShapeDtypeStruct((b, s, e), BF16)
    return pl.pallas_call(
        body, name="dil_bwd", grid=(b, n_heads // hp, nq),
        in_specs=[q_spec, k_spec, v_spec, pl.BlockSpec((hp, 8, HEAD_DIM), lambda i, h, j: (h, 0, 0)),
                  blk_q, blk_q, blk_q],
        out_specs=(blk_q, blk_kv, blk_kv), out_shape=(shp, shp, shp),
        scratch_shapes=[pltpu.VMEM((hp, tq, HEAD_DIM), F32), pltpu.VMEM((s, w), F32), pltpu.VMEM((s, w), F32),
                        pltpu.VMEM((hp, _dil_near_tiles(tq) + 1, tq, tq), F32),
                        pltpu.VMEM((hp, tq, tq), BF16), pltpu.VMEM((hp, tq, tq), BF16)],
        compiler_params=_params("parallel", "parallel", "arbitrary"),
    )(proj3, proj3, proj3, slopes, y, lse, dy)


def kernel(x, c, w_ada, b_ada, g_norm, w_in, g_sb, g_dil, w_out, g_final, loss_target, m_w_ada, m_b_ada, m_g_norm, m_w_in, m_g_sb, m_g_dil, m_w_out, m_g_final, v_w_ada, v_b_ada, v_g_norm, v_w_in, v_g_sb, v_g_dil, v_w_out, v_g_final):
    b, s, d = x.shape
    t = b * s
    e = w_in.shape[2]
    n_heads = e // HEAD_DIM
    na = w_ada.shape[2]
    r_out = w_out.shape[1]
    assert g_sb.shape[1] == e and g_dil.shape[1] == e and N_DEV * r_out == 2 * e and N_DEV * na == 3 * d
    assert b <= SMALL_ROWS and 3 * b + 3 <= 2 * SMALL_ROWS
    ix, iy, ic = _mesh_pos()
    me = 4 * ix + 2 * iy + ic

    c_all = _allgather_rows(jnp.pad(c, ((0, SMALL_ROWS - b), (0, 0))), "ag_c")
    b_own = lax.dynamic_slice(b_ada, (0, me * na), (1, na))
    mod_cols = _ada_fwd(c_all, w_ada[0], b_own)
    mod_all = _allgather_rows(mod_cols, "ag_mod").reshape(N_DEV, N_DEV, SMALL_ROWS, na)
    mod_own = lax.dynamic_slice(mod_all, (0, me, 0, 0), (N_DEV, 1, b, na))[:, 0]
    mod = mod_own.transpose(1, 0, 2).reshape(b, 1, 3 * d)
    shift, scale, gate = mod[:, :, :d], mod[:, :, d:2 * d], mod[:, :, 2 * d:]

    w_own =w_in[0].astype(BF16).reshape(d, PROJ_HALVES, e // PROJ_HALVES).transpose(1, 0, 2)

    h = _norm_mod(x, g_norm, scale, shift).reshape(t, d)
    proj, w_in3, w_out3 = _proj_with_allgather(
        h, w_own, jnp.stack(_unit_ids()).astype(jnp.int32), w_out[0].astype(BF16))
    w_out1 = w_out3.reshape(1, N_DEV * r_out, d)
    proj3 = proj.reshape(b, s, N_DEV * e)
    slopes = jnp.exp2(-ALIBI_MAX_BIAS * jnp.arange(1, n_heads + 1, dtype=F32) / n_heads)
    slopes = jnp.broadcast_to(slopes[:, None, None], (n_heads, 8, HEAD_DIM))
    y_sb, lom_total = _sb_fwd(proj3, e)
    y_dl, lse = _dil_fwd(proj3, e, slopes)
    yg = _gate_fwd(y_sb.reshape(t, e), y_dl.reshape(t, e), proj, g_sb, g_dil)
    out = _mm_nn(yg, w_out1, F32, "mm_out").reshape(b, s, d)
    loss_p, dx2, d_out, dgate, gg_final = _final_fwd_bwd(x, out, gate, g_final.reshape(1, d), loss_target)

    d_out2 = d_out.reshape(t, d)
    dyg = _mm_nt(d_out2, w_out1, BF16, "mm_dy")
    gw_out_p = _mm_tn(yg, d_out2, 1, BF16, "mm_gw_out").reshape(N_DEV, r_out, d)
    dy_sb, dy_dl, dz_sb, dz_dl, gg_sb, gg_dl = _gate_bwd(dyg, y_sb.reshape(t, e), y_dl.reshape(t, e), proj, g_sb, g_dil)
    dq_sb, dk_sb, dv_sb, recv_out = _sb_bwd(proj3, lom_total, dy_sb.reshape(b, s, e), gw_out_p)
    dq_dl, dk_dl, dv_dl = _dil_bwd(proj3, y_dl, lse, dy_dl.reshape(b, s, e), slopes)
    dproj = jnp.concatenate(
        [a.reshape(t, e) for a in (dq_sb, dk_sb, dv_sb, dz_sb, dq_dl, dk_dl, dv_dl, dz_dl)], axis=1)
    chips4 = [(ix, iy)] + _other_chips()
    to_sibling_core = jnp.stack([4 * px + 2 * py + (1 - ic) for px, py in chips4]).astype(jnp.int32)
    to_my_core = jnp.stack([4 * px + 2 * py + ic for px, py in chips4]).astype(jnp.int32)
    gw_in_sibs = _mm_tn_groups(to_sibling_core, h, dproj, N_DEV, "mm_gw_in_sibling")
    gw_in_mine, gw_in_sib = _mm_tn_groups(to_my_core, h, dproj, N_DEV, "mm_gw_in_mine", to_sibling=gw_in_sibs)
    gw_in_send = _chip_presum(gw_in_mine, gw_in_sib)
    dh, gw_in_recv = _mm_nt_with_chip_exchange(dproj, w_in3, F32, gw_in_send, "mm_dh")
    dh = dh.reshape(b, s, d)
    grad_x, dshift, dscale, gg_norm = _norm_bwd(x, dh, dx2, scale, g_norm)

    dmod = jnp.concatenate([dshift, dscale, dgate], axis=1).reshape(3 * b, d)
    pkg = jnp.concatenate([dmod, gg_norm, gg_final, jnp.concatenate([gg_sb, gg_dl], axis=1),
                           jnp.zeros((2 * SMALL_ROWS - 3 * b - 3, d), F32)], axis=0)
    pkg_all = _allgather_rows(pkg, "ag_small_grads").reshape(N_DEV, 2 * SMALL_ROWS, d)
    dmod_all = pkg_all[:, :3 * b].reshape(N_DEV * b, 3 * d)
    dmod_cols = lax.dynamic_slice(dmod_all, (0, me * na), (N_DEV * b, na))
    c_rows = c_all.reshape(N_DEV, SMALL_ROWS, d)[:, :b].reshape(N_DEV * b, d)
    g_w_ada, d_w_ada, nm_w_ada, nv_w_ada = _ada_bwd_adam(c_rows, dmod_cols, w_ada[0], m_w_ada[0], v_w_ada[0])

    def pack(b_ada_like, g_norm_like, g_sb_like, g_dil_like, g_final_like):
        return jnp.concatenate([b_ada_like.reshape(3, d), g_norm_like.reshape(1, d), g_final_like.reshape(1, d),
                                jnp.concatenate([g_sb_like, g_dil_like], axis=1).reshape(1, d),
                                jnp.zeros((2, d), F32)], axis=0)

    small = _small_adam(pkg_all, 3 * b, pack(b_ada, g_norm, g_sb, g_dil, g_final),
                        pack(m_b_ada, m_g_norm, m_g_sb, m_g_dil, m_g_final),
                        pack(v_b_ada, v_g_norm, v_g_sb, v_g_dil, v_g_final))

    def unpack(p):
        return (p[0:3].reshape(1, 3 * d), p[3:4], p[5:6, :e], p[5:6, e:], p[4])

    sm_g, sm_d, sm_m, sm_v = (unpack(p) for p in small)

    g_w_in, d_w_in, nm_w_in, nv_w_in = _adam_from_chip_sums(
        gw_in_mine, gw_in_sib, gw_in_recv, w_in[0], m_w_in[0], v_w_in[0], "adam_w_in")
    g_w_out, d_w_out, nm_w_out, nv_w_out = _adam_from_partials(recv_out, w_out[0], m_w_out[0], v_w_out[0], "adam_w_out")

    loss = lax.psum(loss_p[0, 0], ("x", "y", "c"))

    def weights(ada, small_parts, w_in_part, w_out_part):
        b_ada_p, g_norm_p, g_sb_p, g_dil_p, g_final_p = small_parts
        return (ada[None], b_ada_p, g_norm_p, w_in_part[None], g_sb_p, g_dil_p, w_out_part[None], g_final_p)

    return (loss, grad_x,
            *weights(g_w_ada, sm_g, g_w_in, g_w_out),
            *weights(d_w_ada, sm_d, d_w_in, d_w_out),
            *weights(nm_w_ada, sm_m, nm_w_in, nm_w_out),
            *weights(nv_w_ada, sm_v, nv_w_in, nv_w_out))
```

```python
import functools
import math

import jax
import jax.numpy as jnp
from jax import lax
from jax.experimental import pallas as pl
from jax.experimental.pallas import tpu as pltpu

F32 = jnp.float32
BF16 = jnp.bfloat16
MESH = pl.DeviceIdType.MESH

N_DEV = 8
HEAD_DIM = 128
EPS = 1e-6
ALIBI_MAX_BIAS = 8.0
DIL_PAIRS = ((128, 1), (512, 4), (2048, 16))
DIL_STEPS = 128
NEG = -1e30

ADAM_LR = 0.001
ADAM_B1 = 0.9
ADAM_B2 = 0.999
ADAM_EPS = 1e-08
ADAM_WD = 0.01
ADAM_STEP = 10

VMEM_LIMIT_BYTES = 56 * 1024 * 1024
SMALL_ROWS = 8

NT_DIMS = (((1,), (1,)), ((), ()))
TN_DIMS = (((0,), (0,)), ((), ()))


def _params(*semantics):
    return pltpu.CompilerParams(dimension_semantics=semantics, vmem_limit_bytes=VMEM_LIMIT_BYTES)


def _tile(n, want):
    t = min(n, want)
    assert n % t == 0, (n, want)
    return t


def _mesh_pos():
    return lax.axis_index("x"), lax.axis_index("y"), lax.axis_index("c")


def _allgather_rows(x_shard, name):
    m_per, n = x_shard.shape

    def body(x_ref, out_ref, send_sems, recv_sems, local_sem):
        x, y, c = _mesh_pos()
        me, sibling = (x, y, c), (x, y, 1 - c)
        chips = [(1 - x, y), (x, 1 - y), (1 - x, 1 - y)]

        def rows(px, py, pc):
            return out_ref.at[pl.ds((4 * px + 2 * py + pc) * m_per, m_per), :]

        def copy(k, block, to, src=None):
            return pltpu.make_async_remote_copy(
                src_ref=rows(*block) if src is None else src, dst_ref=rows(*block),
                send_sem=send_sems.at[k], recv_sem=recv_sems.at[k], device_id=to, device_id_type=MESH)

        mine = pltpu.make_async_copy(x_ref, rows(*me), local_sem)
        mine.start()
        first = [copy(0, me, sibling, src=x_ref)]
        first += [copy(1 + j, me, (*chip, c), src=x_ref) for j, chip in enumerate(chips)]
        for cp in first:
            cp.start()
        passed = [copy(4 + j, (*chip, c), sibling) for j, chip in enumerate(chips)]
        for j, chip in enumerate(chips):
            copy(1 + j, (*chip, c), me).wait_recv()
            passed[j].start()
        copy(0, sibling, me).wait_recv()
        for j, chip in enumerate(chips):
            copy(4 + j, (*chip, 1 - c), me).wait_recv()
        for cp in first + passed:
            cp.wait_send()
        mine.wait()

    return pl.pallas_call(
        body, name=name,
        out_shape=jax.ShapeDtypeStruct((N_DEV * m_per, n), x_shard.dtype),
        in_specs=[pl.BlockSpec(memory_space=pltpu.VMEM)],
        out_specs=pl.BlockSpec(memory_space=pltpu.VMEM),
        scratch_shapes=[pltpu.SemaphoreType.DMA((7,)), pltpu.SemaphoreType.DMA((7,)), pltpu.SemaphoreType.DMA],
    )(x_shard)


def _two_level_gather(w_ref, out_ref, send_sems, recv_sems, local_sem):
    x, y, c = _mesh_pos()
    me, sibling = (x, y, c), (x, y, 1 - c)
    chips = [(1 - x, y), (x, 1 - y), (1 - x, 1 - y)]

    def copy(k, block, to, src=None):
        dst = out_ref.at[4 * block[0] + 2 * block[1] + block[2]]
        return pltpu.make_async_remote_copy(
            src_ref=dst if src is None else src, dst_ref=dst,
            send_sem=send_sems.at[k], recv_sem=recv_sems.at[k], device_id=to, device_id_type=MESH)

    mine = pltpu.make_async_copy(w_ref, out_ref.at[4 * x + 2 * y + c], local_sem)
    to_sibling = copy(0, me, sibling, src=w_ref)
    to_chips = [copy(1 + j, me, (*chip, c), src=w_ref) for j, chip in enumerate(chips)]
    passed = [copy(4 + j, (*chip, c), sibling) for j, chip in enumerate(chips)]

    def early():
        mine.start()
        to_sibling.start()

    def ici():
        for cp in to_chips:
            cp.start()

    def finish():
        for j, chip in enumerate(chips):
            copy(1 + j, (*chip, c), me).wait_recv()
            passed[j].start()
        copy(0, sibling, me).wait_recv()
        for j, chip in enumerate(chips):
            copy(4 + j, (*chip, 1 - c), me).wait_recv()
        for cp in [to_sibling] + to_chips + passed:
            cp.wait_send()
        mine.wait()

    return early, ici, finish


PROJ_HALVES = 2


def _unit_schedule():
    assert PROJ_HALVES == 2
    sched = [("own", None, hf) for hf in range(PROJ_HALVES)] + [("sib", None, hf) for hf in range(PROJ_HALVES)]
    for rnd in ([(0, 0), (1, 1)], [(2, 0), (2, 1)], [(0, 1), (1, 0)]):
        sched += [("direct", j, hf) for j, hf in rnd] + [("fwd", j, hf) for j, hf in rnd]
    return sched


def _unit_ids():
    x, y, c = _mesh_pos()
    chips = _other_chips()
    ids = []
    for kind, j, hf in _unit_schedule():
        px, py = (x, y) if j is None else chips[j]
        pc = c if kind in ("own", "direct") else 1 - c
        ids.append(PROJ_HALVES * (4 * px + 2 * py + pc) + hf)
    return ids


def _proj_with_allgather(h, w_own, order, w_later):
    t, d = h.shape
    nh = PROJ_HALVES
    u = w_own.shape[1] // nh
    sched = _unit_schedule()
    n_units = len(sched)
    pos = {entry: p for p, entry in enumerate(sched)}
    tm = _tile(t, MM_TM)
    m_tiles = t // tm
    prep_m = max(m_tiles - 2, 0)
    grid = (n_units, m_tiles)

    later_ici_pos = pos[("direct", 2, 0)]

    def body(order_ref, h_ref, wown_ref, later_ref, proj_ref, w3_ref, later3_ref, bbuf, bsems, send_sems, recv_sems,
             local_sems, later_send_sems, later_recv_sems, later_local_sem):
        n, m = pl.program_id(0), pl.program_id(1)
        x, y, c = _mesh_pos()
        chips = _other_chips()
        sibling = (x, y, 1 - c)
        later_early, later_ici, later_finish = _two_level_gather(
            later_ref, later3_ref, later_send_sems, later_recv_sems, later_local_sem)

        def unit(p):
            return w3_ref.at[order_ref[p]]

        def arrival(p):
            return pltpu.make_async_remote_copy(
                src_ref=unit(p), dst_ref=unit(p), send_sem=send_sems.at[0], recv_sem=recv_sems.at[p - nh],
                device_id=sibling, device_id_type=MESH)

        def send(k, src, p_here, p_there, to):
            return pltpu.make_async_remote_copy(
                src_ref=src, dst_ref=unit(p_here), send_sem=send_sems.at[k], recv_sem=recv_sems.at[p_there - nh],
                device_id=to, device_id_type=MESH)

        copies = []

        def out(src, p_here, p_there, to):
            copies.append(send(len(copies), src, p_here, p_there, to))
            return copies[-1]

        def own_unit(hf):
            return wown_ref.at[:, pl.ds(hf * u, u)]

        def own(hf, p_there, to):
            return out(own_unit(hf), pos[("own", None, hf)], p_there, to)

        x_nbr, y_nbr = (*chips[0], c), (*chips[1], c)
        first = [own(hf, pos[("sib", None, hf)], sibling) for hf in range(nh)]
        first += [own(0, pos[("direct", 0, 0)], x_nbr), own(1, pos[("direct", 1, 1)], y_nbr)]
        after = {}
        for kind, j, hf in sched:
            if kind == "direct":
                p = pos[(kind, j, hf)]
                after[p] = [out(unit(p), p, pos[("fwd", j, hf)], sibling)]
        p = pos[("direct", 0, 0)]
        after[p] += [out(unit(p), p, pos[("direct", 2, 0)], y_nbr), own(0, pos[("direct", 1, 0)], y_nbr)]
        p = pos[("direct", 1, 1)]
        after[p] += [out(unit(p), p, pos[("direct", 2, 1)], x_nbr), own(1, pos[("direct", 0, 1)], x_nbr)]
        locals_ = [pltpu.make_async_copy(own_unit(hf), unit(pos[("own", None, hf)]), local_sems.at[hf])
                   for hf in range(nh)]

        def fetch(p):
            src = own_unit(sched[p][2]) if sched[p][0] == "own" else unit(p)
            return pltpu.make_async_copy(src, bbuf.at[p % 2], bsems.at[p % 2])

        @pl.when((n == 0) & (m == 0))
        def _():
            for cp in locals_ + first:
                cp.start()
            fetch(0).start()
            later_early()

        for p in range(n_units):
            @pl.when((n == p) & (m == 0))
            def _(p=p):
                fetch(p).wait()

            if p + 1 < n_units:
                @pl.when((n == p) & (m == prep_m))
                def _(p=p):
                    if sched[p + 1][0] != "own":
                        arrival(p + 1).wait_recv()
                    for cp in after.get(p + 1, []):
                        cp.start()
                    fetch(p + 1).start()
                    if p + 1 == later_ici_pos:
                        later_ici()

        proj_ref[...] = jnp.dot(h_ref[...], bbuf[n % 2], preferred_element_type=F32).astype(proj_ref.dtype)

        @pl.when((n == n_units - 1) & (m == m_tiles - 1))
        def _():
            for cp in copies:
                cp.wait_send()
            for cp in locals_:
                cp.wait()
            later_finish()

    n_out = len(sched) - nh
    return pl.pallas_call(
        body, name="mm_proj_allgather",
        grid_spec=pltpu.PrefetchScalarGridSpec(
            num_scalar_prefetch=1, grid=grid,
            in_specs=[pl.BlockSpec((tm, d), lambda n, m, order: (m, 0)), ANY_SPEC, ANY_SPEC],
            out_specs=(pl.BlockSpec((tm, u), lambda n, m, order: (m, order[n])), ANY_SPEC, ANY_SPEC),
            scratch_shapes=[pltpu.VMEM((2, d, u), h.dtype), pltpu.SemaphoreType.DMA((2,)),
                            pltpu.SemaphoreType.DMA((n_out,)), pltpu.SemaphoreType.DMA((n_out,)),
                            pltpu.SemaphoreType.DMA((nh,)),
                            pltpu.SemaphoreType.DMA((N_DEV - 1,)), pltpu.SemaphoreType.DMA((N_DEV - 1,)),
                            pltpu.SemaphoreType.DMA]),
        out_shape=(jax.ShapeDtypeStruct((t, n_units * u), h.dtype), jax.ShapeDtypeStruct((n_units, d, u), h.dtype),
                   jax.ShapeDtypeStruct((N_DEV,) + w_later.shape, w_later.dtype)),
        compiler_params=_params("arbitrary", "arbitrary"),
    )(order, h, w_own, w_later)


ANY_SPEC = pl.BlockSpec(memory_space=pl.ANY)


def _grid_first_last(grid):
    ids = [pl.program_id(a) for a in range(len(grid))]
    first = functools.reduce(lambda p, q: p & q, [i == 0 for i in ids])
    last = functools.reduce(lambda p, q: p & q, [i == n - 1 for i, n in zip(ids, grid)])
    return first, last


def _all_to_all_copies(src, dst, send_sems, recv_sems, local_sem):
    x, y, c = _mesh_pos()
    my = 4 * x + 2 * y + c
    copies = [pltpu.make_async_copy(src.at[my], dst.at[my], local_sem)]
    for d in range(1, N_DEV):
        px = 1 - x if d & 4 else x
        py = 1 - y if d & 2 else y
        pc = 1 - c if d & 1 else c
        copies.append(pltpu.make_async_remote_copy(
            src_ref=src.at[4 * px + 2 * py + pc], dst_ref=dst.at[my],
            send_sem=send_sems.at[d - 1], recv_sem=recv_sems.at[d - 1],
            device_id=(px, py, pc), device_id_type=MESH))
    return copies


def _other_chips():
    x, y, _ = _mesh_pos()
    return [(1 - x, y), (x, 1 - y), (1 - x, 1 - y)]


def _same_core_copies(src, dst, send_sems, recv_sems):
    c = lax.axis_index("c")
    return [pltpu.make_async_remote_copy(
        src_ref=src.at[j], dst_ref=dst.at[j], send_sem=send_sems.at[j], recv_sem=recv_sems.at[j],
        device_id=(*chip, c), device_id_type=MESH) for j, chip in enumerate(_other_chips())]


def _sibling_copies(src, dst, send_sems, recv_sems):
    x, y, c = _mesh_pos()
    return [pltpu.make_async_remote_copy(
        src_ref=src.at[j], dst_ref=dst.at[j], send_sem=send_sems.at[j], recv_sem=recv_sems.at[j],
        device_id=(x, y, 1 - c), device_id_type=MESH) for j in range(src.shape[0])]


def _chip_presum(mine, from_sibling):
    _, r, cdim = mine.shape
    tr = _tile(r, 256)

    def body(p_ref, s_ref, o_ref):
        o_ref[...] = (p_ref[...].astype(F32) + s_ref[...].astype(F32)).astype(o_ref.dtype)

    blk = pl.BlockSpec((None, tr, cdim), lambda j, i: (1 + j, i, 0))
    return pl.pallas_call(
        body, name="chip_presum", grid=(3, r // tr), in_specs=[blk, blk],
        out_specs=pl.BlockSpec((None, tr, cdim), lambda j, i: (j, i, 0)),
        out_shape=jax.ShapeDtypeStruct((3, r, cdim), mine.dtype),
        compiler_params=_params("parallel", "parallel"),
    )(mine, from_sibling)


def _mm_call(a, b, dims, nk, grid, a_spec, b_spec, o_spec, out_shape, acc_shape, name):
    def body(a_ref, b_ref, o_ref, acc_ref):
        k = pl.program_id(2)

        @pl.when(k == 0)
        def _():
            acc_ref[...] = jnp.zeros_like(acc_ref)

        acc_ref[...] += lax.dot_general(a_ref[...], b_ref[...], dims, preferred_element_type=F32)

        @pl.when(k == nk - 1)
        def _():
            o_ref[...] = acc_ref[...].astype(o_ref.dtype)

    return pl.pallas_call(
        body, name=name, grid=grid, in_specs=[a_spec, b_spec], out_specs=o_spec, out_shape=out_shape,
        scratch_shapes=[pltpu.VMEM(acc_shape, F32)],
        compiler_params=_params("parallel", "parallel", "arbitrary"),
    )(a, b)


MM_TM, MM_TN, MM_TK = 1024, 2048, 1024


def _mm_nn(a, b3, out_dtype, name):
    m, kk = a.shape
    g, _, nb = b3.shape
    tm, tn, tk = _tile(m, MM_TM), _tile(nb, MM_TN), _tile(kk, MM_TK)
    npb = nb // tn
    return _mm_call(
        a, b3, (((1,), (0,)), ((), ())), kk // tk, (m // tm, g * npb, kk // tk),
        pl.BlockSpec((tm, tk), lambda i, j, k: (i, k)),
        pl.BlockSpec((None, tk, tn), lambda i, j, k: (j // npb, k, j % npb)),
        pl.BlockSpec((tm, tn), lambda i, j, k: (i, j)),
        jax.ShapeDtypeStruct((m, g * nb), out_dtype), (tm, tn), name)


def _mm_nt(a, b3, out_dtype, name):
    m, kk = a.shape
    g, n, kb = b3.shape
    tm, tn, tk = _tile(m, MM_TM), _tile(n, MM_TN), _tile(kb, MM_TK)
    kpb = kb // tk
    return _mm_call(
        a, b3, NT_DIMS, kk // tk, (m // tm, n // tn, kk // tk),
        pl.BlockSpec((tm, tk), lambda i, j, k: (i, k)),
        pl.BlockSpec((None, tn, tk), lambda i, j, k: (k // kpb, j, k % kpb)),
        pl.BlockSpec((tm, tn), lambda i, j, k: (i, j)),
        jax.ShapeDtypeStruct((m, n), out_dtype), (tm, tn), name)


def _mm_nt_with_chip_exchange(a, b3, out_dtype, send3, name):
    m, kk = a.shape
    g, n, kb = b3.shape
    tm, tn, tk = _tile(m, MM_TM), _tile(n, MM_TN), _tile(kb, MM_TK)
    kpb = kb // tk
    grid = (m // tm, n // tn, kk // tk)

    def body(a_ref, b_ref, s_ref, o_ref, r_ref, acc_ref, send_sems, recv_sems):
        first, last = _grid_first_last(grid)
        k = pl.program_id(2)

        @pl.when(first)
        def _():
            for cp in _same_core_copies(s_ref, r_ref, send_sems, recv_sems):
                cp.start()

        @pl.when(k == 0)
        def _():
            acc_ref[...] = jnp.zeros_like(acc_ref)

        acc_ref[...] += lax.dot_general(a_ref[...], b_ref[...], NT_DIMS, preferred_element_type=F32)

        @pl.when(k == grid[2] - 1)
        def _():
            o_ref[...] = acc_ref[...].astype(o_ref.dtype)

        @pl.when(last)
        def _():
            for cp in _same_core_copies(s_ref, r_ref, send_sems, recv_sems):
                cp.wait()

    return pl.pallas_call(
        body, name=name, grid=grid,
        in_specs=[pl.BlockSpec((tm, tk), lambda i, j, k: (i, k)),
                  pl.BlockSpec((None, tn, tk), lambda i, j, k: (k // kpb, j, k % kpb)), ANY_SPEC],
        out_specs=(pl.BlockSpec((tm, tn), lambda i, j, k: (i, j)), ANY_SPEC),
        out_shape=(jax.ShapeDtypeStruct((m, n), out_dtype), jax.ShapeDtypeStruct(send3.shape, send3.dtype)),
        scratch_shapes=[pltpu.VMEM((tm, tn), F32), pltpu.SemaphoreType.DMA((3,)), pltpu.SemaphoreType.DMA((3,))],
        compiler_params=_params("arbitrary", "arbitrary", "arbitrary"),
    )(a, b3, send3)


def _mm_tn_groups(groups, a, b, n_groups, name, to_sibling=None):
    t, m = a.shape
    nb = b.shape[1] // n_groups
    ng = groups.shape[0]
    tm, tk = _tile(m, MM_TM), _tile(t, MM_TK)
    grid = (m // tm, ng, t // tk)
    carry = to_sibling is not None

    def body(groups_ref, a_ref, b_ref, *rest):
        if carry:
            s_ref, o_ref, r_ref, acc_ref, send_sems, recv_sems = rest
        else:
            o_ref, acc_ref = rest
        k = pl.program_id(2)
        first, last = _grid_first_last(grid)

        if carry:
            @pl.when(first)
            def _():
                for cp in _sibling_copies(s_ref, r_ref, send_sems, recv_sems):
                    cp.start()

        @pl.when(k == 0)
        def _():
            acc_ref[...] = jnp.zeros_like(acc_ref)

        acc_ref[...] += lax.dot_general(a_ref[...], b_ref[...], TN_DIMS, preferred_element_type=F32)

        @pl.when(k == grid[2] - 1)
        def _():
            o_ref[...] = acc_ref[...].astype(o_ref.dtype)

        if carry:
            @pl.when(last)
            def _():
                for cp in _sibling_copies(s_ref, r_ref, send_sems, recv_sems):
                    cp.wait()

    shp = jax.ShapeDtypeStruct((ng, m, nb), b.dtype)
    o_spec = pl.BlockSpec((None, tm, nb), lambda i, j, k, grp: (j, i, 0))
    return pl.pallas_call(
        body, name=name,
        grid_spec=pltpu.PrefetchScalarGridSpec(
            num_scalar_prefetch=1, grid=grid,
            in_specs=[pl.BlockSpec((tk, tm), lambda i, j, k, grp: (k, i)),
                      pl.BlockSpec((tk, nb), lambda i, j, k, grp: (k, grp[j]))] + ([ANY_SPEC] if carry else []),
            out_specs=(o_spec, ANY_SPEC) if carry else o_spec,
            scratch_shapes=[pltpu.VMEM((tm, nb), F32)] + (
                [pltpu.SemaphoreType.DMA((ng,)), pltpu.SemaphoreType.DMA((ng,))] if carry else [])),
        out_shape=(shp, shp) if carry else shp,
        compiler_params=_params("arbitrary", "arbitrary", "arbitrary"),
    )(*((groups, a, b, to_sibling) if carry else (groups, a, b)))


def _mm_tn(a, b, g, out_dtype, name):
    t, m = a.shape
    nb = b.shape[1] // g
    tm, tn, tk = _tile(m, MM_TM), _tile(nb, MM_TN), _tile(t, MM_TK)
    npb = nb // tn
    return _mm_call(
        a, b, TN_DIMS, t // tk, (m // tm, g * npb, t // tk),
        pl.BlockSpec((tk, tm), lambda i, j, k: (k, i)),
        pl.BlockSpec((tk, tn), lambda i, j, k: (k, j)),
        pl.BlockSpec((None, tm, tn), lambda i, j, k: (j // npb, i, j % npb)),
        jax.ShapeDtypeStruct((g, m, nb), out_dtype), (tm, tn), name)


def _silu(z):
    return z * jax.nn.sigmoid(z)


def _ada_fwd(c_all, w_shard, b_own):
    r, d = c_all.shape
    na = w_shard.shape[1]
    tk = _tile(d, 512)
    nk = d // tk

    def body(c_ref, w_ref, b_ref, o_ref):
        k = pl.program_id(0)

        @pl.when(k == 0)
        def _():
            o_ref[...] = jnp.zeros_like(o_ref) + b_ref[...]

        cs = _silu(c_ref[...]).astype(BF16)
        o_ref[...] += jnp.dot(cs, w_ref[...].astype(BF16), preferred_element_type=F32)

    return pl.pallas_call(
        body, name="ada_fwd", grid=(nk,),
        in_specs=[pl.BlockSpec((r, tk), lambda k: (0, k)), pl.BlockSpec((tk, na), lambda k: (k, 0)),
                  pl.BlockSpec((1, na), lambda k: (0, 0))],
        out_specs=pl.BlockSpec((r, na), lambda k: (0, 0)),
        out_shape=jax.ShapeDtypeStruct((r, na), F32),
        compiler_params=_params("arbitrary"),
    )(c_all, w_shard, b_own)


def _adam(w, g, m, v):
    nm = ADAM_B1 * m + (1.0 - ADAM_B1) * g
    nv = ADAM_B2 * v + (1.0 - ADAM_B2) * (g * g)
    m_hat = nm / (1.0 - ADAM_B1 ** ADAM_STEP)
    v_hat = nv / (1.0 - ADAM_B2 ** ADAM_STEP)
    delta = -ADAM_LR * (m_hat / (jnp.sqrt(v_hat) + ADAM_EPS) + ADAM_WD * w)
    return delta, nm, nv


def _ada_bwd_adam(c_rows, dmod_cols, w, m, v):
    bg, d = c_rows.shape
    na = w.shape[1]
    tr = _tile(d, 256)

    def body(c_ref, dm_ref, w_ref, m_ref, v_ref, g_ref, d_ref, nm_ref, nv_ref):
        cs = _silu(c_ref[...]).astype(BF16)
        g = lax.dot_general(cs, dm_ref[...].astype(BF16), TN_DIMS, preferred_element_type=F32)
        delta, nm, nv = _adam(w_ref[...], g, m_ref[...], v_ref[...])
        g_ref[...] = g
        d_ref[...] = delta
        nm_ref[...] = nm
        nv_ref[...] = nv

    blk = pl.BlockSpec((tr, na), lambda i: (i, 0))
    shp = jax.ShapeDtypeStruct((d, na), F32)
    return pl.pallas_call(
        body, name="ada_bwd_adam", grid=(d // tr,),
        in_specs=[pl.BlockSpec((bg, tr), lambda i: (0, i)), pl.BlockSpec((bg, na), lambda i: (0, 0)), blk, blk, blk],
        out_specs=(blk, blk, blk, blk), out_shape=(shp, shp, shp, shp),
        compiler_params=_params("parallel"),
    )(c_rows, dmod_cols, w, m, v)


def _small_adam(pkg_all, n_batch_rows, w, m, v):
    d = w.shape[1]

    def body(p_ref, w_ref, m_ref, v_ref, g_ref, d_ref, nm_ref, nv_ref):
        for part in range(3):
            acc = jnp.zeros((1, d), F32)
            for dev in range(N_DEV):
                for b in range(n_batch_rows // 3):
                    acc = acc + p_ref[dev, 3 * b + part:3 * b + part + 1, :]
            g_ref[part:part + 1, :] = acc
        for rrow in range(3):
            acc = jnp.zeros((1, d), F32)
            for dev in range(N_DEV):
                acc = acc + p_ref[dev, n_batch_rows + rrow:n_batch_rows + rrow + 1, :]
            g_ref[3 + rrow:4 + rrow, :] = acc
        g_ref[6:8, :] = jnp.zeros((2, d), F32)
        g = g_ref[...]
        delta, nm, nv = _adam(w_ref[...], g, m_ref[...], v_ref[...])
        d_ref[...] = delta
        nm_ref[...] = nm
        nv_ref[...] = nv

    vm = pl.BlockSpec(memory_space=pltpu.VMEM)
    shp = jax.ShapeDtypeStruct((SMALL_ROWS, d), F32)
    return pl.pallas_call(
        body, name="small_adam", in_specs=[vm, vm, vm, vm], out_specs=(vm, vm, vm, vm),
        out_shape=(shp, shp, shp, shp),
    )(pkg_all, w, m, v)


def _adam_from_chip_sums(mine, from_sibling, from_chips, w, m, v, name):
    _, r, c = mine.shape
    tr = _tile(r, 128)

    def body(p_ref, s_ref, f_ref, w_ref, m_ref, v_ref, g_ref, d_ref, nm_ref, nv_ref):
        g = p_ref[...].astype(F32) + s_ref[...].astype(F32)
        for j in range(3):
            g = g + f_ref[j].astype(F32)
        delta, nm, nv = _adam(w_ref[...], g, m_ref[...], v_ref[...])
        g_ref[...] = g
        d_ref[...] = delta
        nm_ref[...] = nm
        nv_ref[...] = nv

    blk = pl.BlockSpec((tr, c), lambda i: (i, 0))
    slot0 = pl.BlockSpec((None, tr, c), lambda i: (0, i, 0))
    shp = jax.ShapeDtypeStruct((r, c), F32)
    return pl.pallas_call(
        body, name=name, grid=(r // tr,),
        in_specs=[slot0, slot0, pl.BlockSpec((3, tr, c), lambda i: (0, i, 0)), blk, blk, blk],
        out_specs=(blk, blk, blk, blk), out_shape=(shp, shp, shp, shp),
        compiler_params=_params("parallel"),
    )(mine, from_sibling, from_chips, w, m, v)


def _adam_from_partials(recv, w, m, v, name):
    _, r, c = recv.shape
    tr = _tile(r, 128)

    def body(p_ref, w_ref, m_ref, v_ref, g_ref, d_ref, nm_ref, nv_ref):
        g = p_ref[0].astype(F32)
        for dev in range(1, N_DEV):
            g = g + p_ref[dev].astype(F32)
        delta, nm, nv = _adam(w_ref[...], g, m_ref[...], v_ref[...])
        g_ref[...] = g
        d_ref[...] = delta
        nm_ref[...] = nm
        nv_ref[...] = nv

    blk = pl.BlockSpec((tr, c), lambda i: (i, 0))
    shp = jax.ShapeDtypeStruct((r, c), F32)
    return pl.pallas_call(
        body, name=name, grid=(r // tr,),
        in_specs=[pl.BlockSpec((N_DEV, tr, c), lambda i: (0, i, 0)), blk, blk, blk],
        out_specs=(blk, blk, blk, blk), out_shape=(shp, shp, shp, shp),
        compiler_params=_params("parallel"),
    )(recv, w, m, v)


def _norm_mod(x, g_norm, scale, shift):
    b, s, d = x.shape
    ts = _tile(s, 256)

    def body(x_ref, g_ref, sc_ref, sh_ref, h_ref):
        xv = x_ref[...]
        r = lax.rsqrt(jnp.mean(xv * xv, axis=-1, keepdims=True) + EPS)
        xn = (xv * r) * g_ref[...]
        h_ref[...] = (xn * (1.0 + sc_ref[...]) + sh_ref[...]).astype(BF16)

    tok = pl.BlockSpec((None, ts, d), lambda i, j: (i, j, 0))
    per_b = pl.BlockSpec((None, 1, d), lambda i, j: (i, 0, 0))
    return pl.pallas_call(
        body, name="norm_mod", grid=(b, s // ts),
        in_specs=[tok, pl.BlockSpec((1, d), lambda i, j: (0, 0)), per_b, per_b],
        out_specs=tok, out_shape=jax.ShapeDtypeStruct((b, s, d), BF16),
        compiler_params=_params("parallel", "parallel"),
    )(x, g_norm, scale, shift)


def _final_fwd_bwd(x, out, gate, g_final, target):
    b, s, d = x.shape
    ts = _tile(s, 256)

    def body(x_ref, o_ref, gt_ref, g_ref, t_ref, loss_ref, dx2_ref, dout_ref, dgate_ref, gg_ref):
        i, j = pl.program_id(0), pl.program_id(1)

        @pl.when((i == 0) & (j == 0))
        def _():
            loss_ref[...] = jnp.zeros_like(loss_ref)
            gg_ref[...] = jnp.zeros_like(gg_ref)

        @pl.when(j == 0)
        def _():
            dgate_ref[...] = jnp.zeros_like(dgate_ref)

        ov = o_ref[...]
        gt = gt_ref[...]
        x2 = x_ref[...] + gt * ov
        r = lax.rsqrt(jnp.mean(x2 * x2, axis=-1, keepdims=True) + EPS)
        xh = x2 * r
        err = xh * g_ref[...] - t_ref[...]
        loss_ref[...] += 0.5 * jnp.sum(jnp.mean(err * err, axis=-1, keepdims=True), axis=0, keepdims=True)
        dfin = err * (1.0 / d)
        gg_ref[...] += jnp.sum(dfin * xh, axis=0, keepdims=True)
        dxh = dfin * g_ref[...]
        dx2 = r * (dxh - xh * jnp.mean(dxh * xh, axis=-1, keepdims=True))
        dx2_ref[...] = dx2
        dout_ref[...] = (gt * dx2).astype(BF16)
        dgate_ref[...] += jnp.sum(dx2 * ov, axis=0, keepdims=True)

    tok = pl.BlockSpec((None, ts, d), lambda i, j: (i, j, 0))
    per_b = pl.BlockSpec((None, 1, d), lambda i, j: (i, 0, 0))
    vec = pl.BlockSpec((1, d), lambda i, j: (0, 0))
    return pl.pallas_call(
        body, name="final_fwd_bwd", grid=(b, s // ts),
        in_specs=[tok, tok, per_b, vec, tok],
        out_specs=(pl.BlockSpec((8, 128), lambda i, j: (0, 0)), tok, tok, per_b, vec),
        out_shape=(jax.ShapeDtypeStruct((8, 128), F32), jax.ShapeDtypeStruct((b, s, d), F32),
                   jax.ShapeDtypeStruct((b, s, d), BF16), jax.ShapeDtypeStruct((b, 1, d), F32),
                   jax.ShapeDtypeStruct((1, d), F32)),
        compiler_params=_params("arbitrary", "arbitrary"),
    )(x, out, gate, g_final, target)


def _norm_bwd(x, dh, dx2, scale, g_norm):
    b, s, d = x.shape
    ts = _tile(s, 256)

    def body(x_ref, dh_ref, dx2_ref, sc_ref, g_ref, gx_ref, dsh_ref, dsc_ref, gg_ref):
        i, j = pl.program_id(0), pl.program_id(1)

        @pl.when((i == 0) & (j == 0))
        def _():
            gg_ref[...] = jnp.zeros_like(gg_ref)

        @pl.when(j == 0)
        def _():
            dsh_ref[...] = jnp.zeros_like(dsh_ref)
            dsc_ref[...] = jnp.zeros_like(dsc_ref)

        xv = x_ref[...]
        dhv = dh_ref[...]
        r = lax.rsqrt(jnp.mean(xv * xv, axis=-1, keepdims=True) + EPS)
        xh = xv * r
        xn = xh * g_ref[...]
        dsh_ref[...] += jnp.sum(dhv, axis=0, keepdims=True)
        dsc_ref[...] += jnp.sum(dhv * xn, axis=0, keepdims=True)
        dxn = dhv * (1.0 + sc_ref[...])
        gg_ref[...] += jnp.sum(dxn * xh, axis=0, keepdims=True)
        dxh = dxn * g_ref[...]
        gx_ref[...] = dx2_ref[...] + r * (dxh - xh * jnp.mean(dxh * xh, axis=-1, keepdims=True))

    tok = pl.BlockSpec((None, ts, d), lambda i, j: (i, j, 0))
    per_b = pl.BlockSpec((None, 1, d), lambda i, j: (i, 0, 0))
    vec = pl.BlockSpec((1, d), lambda i, j: (0, 0))
    return pl.pallas_call(
        body, name="norm_bwd", grid=(b, s // ts),
        in_specs=[tok, tok, tok, per_b, vec],
        out_specs=(tok, per_b, per_b, vec),
        out_shape=(jax.ShapeDtypeStruct((b, s, d), F32), jax.ShapeDtypeStruct((b, 1, d), F32),
                   jax.ShapeDtypeStruct((b, 1, d), F32), jax.ShapeDtypeStruct((1, d), F32)),
        compiler_params=_params("arbitrary", "arbitrary"),
    )(x, dh, dx2, scale, g_norm)


def _gate_fwd(y_sb, y_dl, proj, g_sb, g_dl):
    t, e = y_sb.shape
    n_heads = e // HEAD_DIM
    tt = _tile(t, 256)

    def body(ys_ref, yd_ref, zs_ref, zd_ref, gs_ref, gd_ref, o_ref):
        for grp, (y_ref, z_ref, g_ref) in enumerate(((ys_ref, zs_ref, gs_ref), (yd_ref, zd_ref, gd_ref))):
            for h in range(n_heads):
                sl = slice(h * HEAD_DIM, (h + 1) * HEAD_DIM)
                y = y_ref[:, sl]
                r = lax.rsqrt(jnp.mean(y * y, axis=-1, keepdims=True) + EPS)
                yn = (y * r) * g_ref[:, sl]
                z = z_ref[:, sl].astype(F32)
                o_ref[:, grp * e + h * HEAD_DIM:grp * e + (h + 1) * HEAD_DIM] = (yn * _silu(z)).astype(BF16)

    yblk = pl.BlockSpec((tt, e), lambda i: (i, 0))
    gblk = pl.BlockSpec((1, e), lambda i: (0, 0))
    return pl.pallas_call(
        body, name="gate_fwd", grid=(t // tt,),
        in_specs=[yblk, yblk, pl.BlockSpec((tt, e), lambda i: (i, 3)), pl.BlockSpec((tt, e), lambda i: (i, 7)),
                  gblk, gblk],
        out_specs=pl.BlockSpec((tt, 2 * e), lambda i: (i, 0)),
        out_shape=jax.ShapeDtypeStruct((t, 2 * e), BF16),
        compiler_params=_params("parallel"),
    )(y_sb, y_dl, proj, proj, g_sb, g_dl)


def _gate_bwd(dyg, y_sb, y_dl, proj, g_sb, g_dl):
    t, e = y_sb.shape
    n_heads = e // HEAD_DIM
    tt = _tile(t, 256)

    def body(dg_ref, ys_ref, yd_ref, zs_ref, zd_ref, gs_ref, gd_ref,
             dys_ref, dyd_ref, dzs_ref, dzd_ref, ggs_ref, ggd_ref):
        @pl.when(pl.program_id(0) == 0)
        def _():
            ggs_ref[...] = jnp.zeros_like(ggs_ref)
            ggd_ref[...] = jnp.zeros_like(ggd_ref)

        groups = ((ys_ref, zs_ref, gs_ref, dys_ref, dzs_ref, ggs_ref), (yd_ref, zd_ref, gd_ref, dyd_ref, dzd_ref, ggd_ref))
        for grp, (y_ref, z_ref, g_ref, dy_ref, dz_ref, gg_ref) in enumerate(groups):
            for h in range(n_heads):
                sl = slice(h * HEAD_DIM, (h + 1) * HEAD_DIM)
                dg = dg_ref[:, grp * e + h * HEAD_DIM:grp * e + (h + 1) * HEAD_DIM].astype(F32)
                y = y_ref[:, sl]
                z = z_ref[:, sl].astype(F32)
                g = g_ref[:, sl]
                r = lax.rsqrt(jnp.mean(y * y, axis=-1, keepdims=True) + EPS)
                yh = y * r
                sig = jax.nn.sigmoid(z)
                dyn = dg * (z * sig)
                dz_ref[:, sl] = (dg * (yh * g) * (sig * (1.0 + z * (1.0 - sig)))).astype(BF16)
                gg_ref[:, sl] += jnp.sum(dyn * yh, axis=0, keepdims=True)
                dyh = dyn * g
                dy_ref[:, sl] = (r * (dyh - yh * jnp.mean(dyh * yh, axis=-1, keepdims=True))).astype(BF16)

    yblk = pl.BlockSpec((tt, e), lambda i: (i, 0))
    gblk = pl.BlockSpec((1, e), lambda i: (0, 0))
    act = jax.ShapeDtypeStruct((t, e), BF16)
    vec = jax.ShapeDtypeStruct((1, e), F32)
    return pl.pallas_call(
        body, name="gate_bwd", grid=(t // tt,),
        in_specs=[pl.BlockSpec((tt, 2 * e), lambda i: (i, 0)), yblk, yblk,
                  pl.BlockSpec((tt, e), lambda i: (i, 3)), pl.BlockSpec((tt, e), lambda i: (i, 7)), gblk, gblk],
        out_specs=(yblk, yblk, yblk, yblk, gblk, gblk),
        out_shape=(act, act, act, act, vec, vec),
        compiler_params=_params("arbitrary"),
    )(dyg, y_sb, y_dl, proj, proj, g_sb, g_dl)


ATT_TQ = 256
HEADS_PER_STEP = 4
SOFTPLUS_CLAMP = 30.0
ATT_STRIP = 32


def _split2_dot(x, u):
    hi = x.astype(BF16)
    lo = (x - hi.astype(F32)).astype(BF16)
    n = x.shape[0]
    both = jnp.dot(jnp.concatenate([hi, lo], axis=0), u, preferred_element_type=F32)
    return both[:n] + both[n:]


def _iota2(n):
    return lax.broadcasted_iota(jnp.int32, (n, n), 0), lax.broadcasted_iota(jnp.int32, (n, n), 1)


def _head_slices():
    return [slice(hh * HEAD_DIM, (hh + 1) * HEAD_DIM) for hh in range(HEADS_PER_STEP)]


def _att_specs(s, e, tq, col0):
    n_heads = e // HEAD_DIM
    hp = HEADS_PER_STEP
    assert n_heads % hp == 0 and col0 % hp == 0
    w = hp * HEAD_DIM
    q_spec = pl.BlockSpec((None, tq, w), lambda i, h, j: (i, j, col0 // hp + h))
    k_spec = pl.BlockSpec((None, s, w), lambda i, h, j: (i, 0, (col0 + n_heads) // hp + h))
    v_spec = pl.BlockSpec((None, s, w), lambda i, h, j: (i, 0, (col0 + 2 * n_heads) // hp + h))
    return q_spec, k_spec, v_spec


def _sb_fwd(proj3, e):
    b, s, _ = proj3.shape
    n_heads = e // HEAD_DIM
    hp = HEADS_PER_STEP
    tq = _tile(s, ATT_TQ)
    nq = s // tq
    inv = 1.0 / math.sqrt(HEAD_DIM)

    def body(q_ref, k_ref, v_ref, y_ref, tot_ref, acc_ref, car_ref):
        i = pl.program_id(2)
        row, col = _iota2(tq)
        before = row > col
        u_after = before.astype(BF16)
        acc_ref[...] = jnp.zeros_like(acc_ref)
        car_ref[...] = jnp.zeros_like(car_ref)

        def block(j, diagonal):
            keys = pl.ds(pl.multiple_of(j * tq, tq), tq)
            heads = list(enumerate(_head_slices()))
            zs = [lax.dot_general(q_ref[:, hs], k_ref[keys, hs], NT_DIMS, preferred_element_type=F32) * inv
                  for _, hs in heads]
            sps = [jnp.maximum(jnp.log(1.0 + jnp.exp(jnp.minimum(z, SOFTPLUS_CLAMP))), z) for z in zs]
            loms = [jnp.where(before, -sp, 0.0) if diagonal else -sp for sp in sps]
            sufs = [_split2_dot(loms[hh], u_after) + car_ref[hh] for hh, _ in heads]
            avs = [jnp.exp((zs[hh] - sps[hh]) + sufs[hh]) for hh, _ in heads]
            if diagonal:
                avs = [jnp.where(before, a, 0.0) for a in avs]
            pvs = [jnp.dot(avs[hh].astype(BF16), v_ref[keys, hs], preferred_element_type=F32) for hh, hs in heads]
            for hh, _ in heads:
                acc_ref[hh] += pvs[hh]
                car_ref[hh] += jnp.sum(loms[hh], axis=1, keepdims=True)

        block(i, True)

        def step(it, carry):
            block(i - it, False)
            return carry

        lax.fori_loop(1, i + 1, step, 0)
        for hh, hs in enumerate(_head_slices()):
            y_ref[:, hs] = acc_ref[hh]
            tot_ref[:, hs] = jnp.broadcast_to(car_ref[hh], (tq, HEAD_DIM))

    q_spec, k_spec, v_spec = _att_specs(s, e, tq, 0)
    blk_q = pl.BlockSpec((None, tq, hp * HEAD_DIM), lambda i, h, j: (i, j, h))
    shp = jax.ShapeDtypeStruct((b, s, e), F32)
    return pl.pallas_call(
        body, name="sb_fwd", grid=(b, n_heads // hp, nq),
        in_specs=[q_spec, k_spec, v_spec],
        out_specs=(blk_q, blk_q), out_shape=(shp, shp),
        scratch_shapes=[pltpu.VMEM((hp, tq, HEAD_DIM), F32), pltpu.VMEM((hp, tq, 1), F32)],
        compiler_params=_params("parallel", "parallel", "arbitrary"),
    )(proj3, proj3, proj3)


def _sb_bwd(proj3, lom_total, dy, partials):
    b, s, e = dy.shape
    n_heads = e // HEAD_DIM
    hp = HEADS_PER_STEP
    tq = _tile(s, ATT_TQ)
    nq = s // tq
    inv = 1.0 / math.sqrt(HEAD_DIM)
    grid = (b, n_heads // hp, nq)

    def body(q_ref, k_ref, v_ref, tot_ref, dy_ref, p_ref, dq_ref, dk_ref, dv_ref, r_ref,
             dqa, dka, dva, car, car2, send_sems, recv_sems, local_sem):
        i = pl.program_id(2)
        first, last = _grid_first_last(grid)

        @pl.when(first)
        def _():
            for cp in _all_to_all_copies(p_ref, r_ref, send_sems, recv_sems, local_sem):
                cp.start()

        @pl.when(i == 0)
        def _():
            dka[...] = jnp.zeros_like(dka)
            dva[...] = jnp.zeros_like(dva)

        row, col = _iota2(tq)
        before = row > col
        u_upto = (row <= col).astype(BF16)
        u_before = (row < col).astype(BF16)
        dqa[...] = jnp.zeros_like(dqa)
        car[...] = jnp.zeros_like(car)
        car2[...] = jnp.zeros_like(car2)

        def block(j, diagonal):
            keys = pl.ds(pl.multiple_of(j * tq, tq), tq)
            heads = list(enumerate(_head_slices()))
            zs = [lax.dot_general(q_ref[:, hs], k_ref[keys, hs], NT_DIMS, preferred_element_type=F32) * inv
                  for _, hs in heads]
            das = [lax.dot_general(dy_ref[:, hs], v_ref[keys, hs], NT_DIMS, preferred_element_type=F32) for _, hs in heads]
            ezs = [jnp.exp(jnp.minimum(z, SOFTPLUS_CLAMP)) for z in zs]
            sps = [jnp.maximum(jnp.log(1.0 + ezs[hh]), zs[hh]) for hh, _ in heads]
            loms = [jnp.where(before, -sp, 0.0) if diagonal else -sp for sp in sps]
            sufs = [tot_ref[:, hh * HEAD_DIM:hh * HEAD_DIM + 1] - (_split2_dot(loms[hh], u_upto) + car[hh])
                    for hh, _ in heads]
            avs = [jnp.exp((zs[hh] - sps[hh]) + sufs[hh]) for hh, _ in heads]
            if diagonal:
                avs = [jnp.where(before, a, 0.0) for a in avs]
            dls = [avs[hh] * das[hh] for hh, _ in heads]
            prefixes = [_split2_dot(dls[hh], u_before) + car2[hh] for hh, _ in heads]
            dzs = []
            for hh, _ in heads:
                one_minus_beta = 1.0 / (1.0 + ezs[hh])
                dz = (dls[hh] * one_minus_beta - prefixes[hh] * (ezs[hh] * one_minus_beta)) * inv
                if diagonal:
                    dz = jnp.where(before, dz, 0.0)
                dzs.append(dz.astype(BF16))
            dqs = [jnp.dot(dzs[hh], k_ref[keys, hs], preferred_element_type=F32) for hh, hs in heads]
            dks = [lax.dot_general(dzs[hh], q_ref[:, hs], TN_DIMS, preferred_element_type=F32) for hh, hs in heads]
            dvs = [lax.dot_general(avs[hh].astype(BF16), dy_ref[:, hs], TN_DIMS, preferred_element_type=F32)
                   for hh, hs in heads]
            for hh, hs in heads:
                dqa[hh] += dqs[hh]
                dka[keys, hs] += dks[hh]
                dva[keys, hs] += dvs[hh]
                car[hh] += jnp.sum(loms[hh], axis=1, keepdims=True)
                car2[hh] += jnp.sum(dls[hh], axis=1, keepdims=True)

        def step(j, carry):
            block(j, False)
            return carry

        lax.fori_loop(0, i, step, 0)
        block(i, True)
        for hh, hs in enumerate(_head_slices()):
            dq_ref[:, hs] = dqa[hh].astype(BF16)

        @pl.when(i == nq - 1)
        def _():
            dk_ref[...] = dka[...].astype(BF16)
            dv_ref[...] = dva[...].astype(BF16)

        @pl.when(last)
        def _():
            for cp in _all_to_all_copies(p_ref, r_ref, send_sems, recv_sems, local_sem):
                cp.wait()

    q_spec, k_spec, v_spec = _att_specs(s, e, tq, 0)
    w = hp * HEAD_DIM
    blk_q = pl.BlockSpec((None, tq, w), lambda i, h, j: (i, j, h))
    blk_kv = pl.BlockSpec((None, s, w), lambda i, h, j: (i, 0, h))
    shp = jax.ShapeDtypeStruct((b, s, e), BF16)
    return pl.pallas_call(
        body, name="sb_bwd", grid=grid,
        in_specs=[q_spec, k_spec, v_spec, blk_q, blk_q, ANY_SPEC],
        out_specs=(blk_q, blk_kv, blk_kv, ANY_SPEC),
        out_shape=(shp, shp, shp, jax.ShapeDtypeStruct(partials.shape, partials.dtype)),
        scratch_shapes=[pltpu.VMEM((hp, tq, HEAD_DIM), F32), pltpu.VMEM((s, w), F32), pltpu.VMEM((s, w), F32),
                        pltpu.VMEM((hp, tq, 1), F32), pltpu.VMEM((hp, tq, 1), F32),
                        pltpu.SemaphoreType.DMA((N_DEV - 1,)), pltpu.SemaphoreType.DMA((N_DEV - 1,)),
                        pltpu.SemaphoreType.DMA],
        compiler_params=_params("arbitrary", "arbitrary", "arbitrary"),
    )(proj3, proj3, proj3, lom_total, dy, partials)


def _dil_near_tiles(tq):
    return (DIL_PAIRS[1][0] + tq - 1) // tq + 1


def _dil_fill_bias(bias_ref, sl_ref, tq):
    row, col = _iota2(tq)
    for hh in range(HEADS_PER_STEP):
        slope = sl_ref[hh, 0:1, 0:1]
        for d in range(_dil_near_tiles(tq) + 1):
            dist = d * tq + row - col
            cnt = jnp.zeros(dist.shape, jnp.int32)
            for window, dilation in DIL_PAIRS:
                cnt = cnt + (((dist & (dilation - 1)) == 0) & (dist <= window)).astype(jnp.int32)
            bias = jnp.where(cnt == 3, math.log(3.0), jnp.where(cnt == 2, math.log(2.0), 0.0))
            bias_ref[hh, d] = jnp.where((dist >= 0) & (cnt > 0), bias - slope * dist.astype(F32), NEG)


def _dil_fwd(proj3, e, slopes):
    b, s, _ = proj3.shape
    n_heads = e // HEAD_DIM
    hp = HEADS_PER_STEP
    tq = _tile(s, ATT_TQ)
    nq = s // tq
    inv = 1.0 / math.sqrt(HEAD_DIM)
    assert s <= DIL_PAIRS[2][0]

    def body(q_ref, k_ref, v_ref, sl_ref, y_ref, lse_ref, acc_ref, m_ref, l_ref, bias_ref, p_ref):
        i = pl.program_id(2)

        @pl.when(i == 0)
        def _():
            _dil_fill_bias(bias_ref, sl_ref, tq)

        acc_ref[...] = jnp.zeros_like(acc_ref)
        m_ref[...] = jnp.full_like(m_ref, NEG)
        l_ref[...] = jnp.zeros_like(l_ref)

        def step(it, carry):
            keys = pl.ds(pl.multiple_of((i - it) * tq, tq), tq)
            heads = list(enumerate(_head_slices()))
            near = _dil_near_tiles(tq)
            tile = jnp.minimum(it, near)
            beyond = jnp.maximum(it - near, 0).astype(F32) * float(tq)
            raws = [lax.dot_general(q_ref[:, hs], k_ref[keys, hs], NT_DIMS, preferred_element_type=F32)
                    for _, hs in heads]
            for hh, _ in heads:
                shift = sl_ref[hh, 0:1, 0:1] * beyond
                for r0 in range(0, tq, ATT_STRIP):
                    rows = slice(r0, r0 + ATT_STRIP)
                    sc = (raws[hh][rows] * inv + bias_ref[hh, tile, rows, :]) - shift
                    m_old = m_ref[hh, rows]
                    m_new = jnp.maximum(m_old, jnp.max(sc, axis=1, keepdims=True))
                    p = jnp.exp(sc - m_new)
                    alpha = jnp.exp(m_old - m_new)
                    l_ref[hh, rows] = alpha * l_ref[hh, rows] + (p[:, :tq // 2] + p[:, tq // 2:])
                    acc_ref[hh, rows] = alpha * acc_ref[hh, rows]
                    p_ref[hh, rows] = p.astype(BF16)
                    m_ref[hh, rows] = m_new
            pvs = [jnp.dot(p_ref[hh], v_ref[keys, hs], preferred_element_type=F32) for hh, hs in heads]
            for hh, _ in heads:
                acc_ref[hh] += pvs[hh]
            return carry

        lax.fori_loop(0, i + 1, step, 0)
        for hh, hs in enumerate(_head_slices()):
            l = jnp.sum(l_ref[hh], axis=1, keepdims=True)
            y_ref[:, hs] = acc_ref[hh] / l
            lse_ref[:, hs] = jnp.broadcast_to(m_ref[hh] + jnp.log(l), (tq, HEAD_DIM))

    q_spec, k_spec, v_spec = _att_specs(s, e, tq, 4 * n_heads)
    blk_q = pl.BlockSpec((None, tq, hp * HEAD_DIM), lambda i, h, j: (i, j, h))
    shp = jax.ShapeDtypeStruct((b, s, e), F32)
    return pl.pallas_call(
        body, name="dil_fwd", grid=(b, n_heads // hp, nq),
        in_specs=[q_spec, k_spec, v_spec, pl.BlockSpec((hp, 8, HEAD_DIM), lambda i, h, j: (h, 0, 0))],
        out_specs=(blk_q, blk_q), out_shape=(shp, shp),
        scratch_shapes=[pltpu.VMEM((hp, tq, HEAD_DIM), F32), pltpu.VMEM((hp, tq, 1), F32), pltpu.VMEM((hp, tq, tq // 2), F32),
                        pltpu.VMEM((hp, _dil_near_tiles(tq) + 1, tq, tq), F32), pltpu.VMEM((hp, tq, tq), BF16)],
        compiler_params=_params("parallel", "parallel", "arbitrary"),
    )(proj3, proj3, proj3, slopes)


def _dil_bwd(proj3, y, lse, dy, slopes):
    b, s, e = y.shape
    n_heads = e // HEAD_DIM
    hp = HEADS_PER_STEP
    tq = _tile(s, ATT_TQ)
    nq = s // tq
    inv = 1.0 / math.sqrt(HEAD_DIM)

    def body(q_ref, k_ref, v_ref, sl_ref, y_ref, lse_ref, dy_ref, dq_ref, dk_ref, dv_ref, dqa, dka, dva, bias_ref,
             p_ref, ds_ref):
        i = pl.program_id(2)

        @pl.when(i == 0)
        def _():
            dka[...] = jnp.zeros_like(dka)
            dva[...] = jnp.zeros_like(dva)
            _dil_fill_bias(bias_ref, sl_ref, tq)

        delta = [jnp.sum(dy_ref[:, hs].astype(F32) * y_ref[:, hs], axis=1, keepdims=True) for hs in _head_slices()]
        dqa[...] = jnp.zeros_like(dqa)

        def step(it, carry):
            keys = pl.ds(pl.multiple_of((i - it) * tq, tq), tq)
            heads = list(enumerate(_head_slices()))
            near = _dil_near_tiles(tq)
            tile = jnp.minimum(it, near)
            beyond = jnp.maximum(it - near, 0).astype(F32) * float(tq)
            raws = [lax.dot_general(q_ref[:, hs], k_ref[keys, hs], NT_DIMS, preferred_element_type=F32)
                    for _, hs in heads]
            dps = [lax.dot_general(dy_ref[:, hs], v_ref[keys, hs], NT_DIMS, preferred_element_type=F32) for _, hs in heads]
            for hh, _ in heads:
                shift = sl_ref[hh, 0:1, 0:1] * beyond
                for r0 in range(0, tq, ATT_STRIP):
                    rows = slice(r0, r0 + ATT_STRIP)
                    sc = (raws[hh][rows] * inv + bias_ref[hh, tile, rows, :]) - shift
                    p = jnp.exp(sc - lse_ref[rows, hh * HEAD_DIM:hh * HEAD_DIM + 1])
                    p_ref[hh, rows] = p.astype(BF16)
                    ds_ref[hh, rows] = ((p * (dps[hh][rows] - delta[hh][rows])) * inv).astype(BF16)
            dqs = [jnp.dot(ds_ref[hh], k_ref[keys, hs], preferred_element_type=F32) for hh, hs in heads]
            dks = [lax.dot_general(ds_ref[hh], q_ref[:, hs], TN_DIMS, preferred_element_type=F32) for hh, hs in heads]
            dvs = [lax.dot_general(p_ref[hh], dy_ref[:, hs], TN_DIMS, preferred_element_type=F32) for hh, hs in heads]
            for hh, hs in heads:
                dqa[hh] += dqs[hh]
                dka[keys, hs] += dks[hh]
                dva[keys, hs] += dvs[hh]
            return carry

        lax.fori_loop(0, i + 1, step, 0)
        for hh, hs in enumerate(_head_slices()):
            dq_ref[:, hs] = dqa[hh].astype(BF16)

        @pl.when(i == nq - 1)
        def _():
            dk_ref[...] = dka[...].astype(BF16)
            dv_ref[...] = dva[...].astype(BF16)

    q_spec, k_spec, v_spec = _att_specs(s, e, tq, 4 * n_heads)
    w = hp * HEAD_DIM
    blk_q = pl.BlockSpec((None, tq, w), lambda i, h, j: (i, j, h))
    blk_kv = pl.BlockSpec((None, s, w), lambda i, h, j: (i, 0, h))
    shp = jax.ShapeDtypeStruct((b, s, e), BF16)
    return pl.pallas_call(
        body, name="dil_bwd", grid=(b, n_heads // hp, nq),
        in_specs=[q_spec, k_spec, v_spec, pl.BlockSpec((hp, 8, HEAD_DIM), lambda i, h, j: (h, 0, 0)),
                  blk_q, blk_q, blk_q],
        out_specs=(blk_q, blk_kv, blk_kv), out_shape=(shp, shp, shp),
        scratch_shapes=[pltpu.VMEM((hp, tq, HEAD_DIM), F32), pltpu.VMEM((s, w), F32), pltpu.VMEM((s, w), F32),
                        pltpu.VMEM((hp, _dil_near_tiles(tq) + 1, tq, tq), F32),
                        pltpu.VMEM((hp, tq, tq), BF16), pltpu.VMEM((hp, tq, tq), BF16)],
        compiler_params=_params("parallel", "parallel", "arbitrary"),
    )(proj3, proj3, proj3, slopes, y, lse, dy)


def kernel(x, c, w_ada, b_ada, g_norm, w_in, g_sb, g_dil, w_out, g_final, loss_target, m_w_ada, m_b_ada, m_g_norm, m_w_in, m_g_sb, m_g_dil, m_w_out, m_g_final, v_w_ada, v_b_ada, v_g_norm, v_w_in, v_g_sb, v_g_dil, v_w_out, v_g_final):
    b, s, d = x.shape
    t = b * s
    e = w_in.shape[2]
    n_heads = e // HEAD_DIM
    na = w_ada.shape[2]
    r_out = w_out.shape[1]
    assert g_sb.shape[1] == e and g_dil.shape[1] == e and N_DEV * r_out == 2 * e and N_DEV * na == 3 * d
    assert b <= SMALL_ROWS and 3 * b + 3 <= 2 * SMALL_ROWS
    ix, iy, ic = _mesh_pos()
    me = 4 * ix + 2 * iy + ic

    c_all = _allgather_rows(jnp.pad(c, ((0, SMALL_ROWS - b), (0, 0))), "ag_c")
    b_own = lax.dynamic_slice(b_ada, (0, me * na), (1, na))
    mod_cols = _ada_fwd(c_all, w_ada[0], b_own)
    mod_all = _allgather_rows(mod_cols, "ag_mod").reshape(N_DEV, N_DEV, SMALL_ROWS, na)
    mod_own = lax.dynamic_slice(mod_all, (0, me, 0, 0), (N_DEV, 1, b, na))[:, 0]
    mod = mod_own.transpose(1, 0, 2).reshape(b, 1, 3 * d)
    shift, scale, gate = mod[:, :, :d], mod[:, :, d:2 * d], mod[:, :, 2 * d:]

    w_own = w_in[0].astype(BF16)

    h = _norm_mod(x, g_norm, scale, shift).reshape(t, d)
    proj, w_in3, w_out3 = _proj_with_allgather(
        h, w_own, jnp.stack(_unit_ids()).astype(jnp.int32), w_out[0].astype(BF16))
    w_out1 = w_out3.reshape(1, N_DEV * r_out, d)
    proj3 = proj.reshape(b, s, N_DEV * e)
    slopes = jnp.exp2(-ALIBI_MAX_BIAS * jnp.arange(1, n_heads + 1, dtype=F32) / n_heads)
    slopes = jnp.broadcast_to(slopes[:, None, None], (n_heads, 8, HEAD_DIM))
    y_sb, lom_total = _sb_fwd(proj3, e)
    y_dl, lse = _dil_fwd(proj3, e, slopes)
    yg = _gate_fwd(y_sb.reshape(t, e), y_dl.reshape(t, e), proj, g_sb, g_dil)
    out = _mm_nn(yg, w_out1, F32, "mm_out").reshape(b, s, d)
    loss_p, dx2, d_out, dgate, gg_final = _final_fwd_bwd(x, out, gate, g_final.reshape(1, d), loss_target)

    d_out2 = d_out.reshape(t, d)
    dyg = _mm_nt(d_out2, w_out1, BF16, "mm_dy")
    gw_out_p = _mm_tn(yg, d_out2, 1, BF16, "mm_gw_out").reshape(N_DEV, r_out, d)
    dy_sb, dy_dl, dz_sb, dz_dl, gg_sb, gg_dl = _gate_bwd(dyg, y_sb.reshape(t, e), y_dl.reshape(t, e), proj, g_sb, g_dil)
    dq_sb, dk_sb, dv_sb, recv_out = _sb_bwd(proj3, lom_total, dy_sb.reshape(b, s, e), gw_out_p)
    dq_dl, dk_dl, dv_dl = _dil_bwd(proj3, y_dl, lse, dy_dl.reshape(b, s, e), slopes)
    dproj = jnp.concatenate(
        [a.reshape(t, e) for a in (dq_sb, dk_sb, dv_sb, dz_sb, dq_dl, dk_dl, dv_dl, dz_dl)], axis=1)
    chips4 = [(ix, iy)] + _other_chips()
    to_sibling_core = jnp.stack([4 * px + 2 * py + (1 - ic) for px, py in chips4]).astype(jnp.int32)
    to_my_core = jnp.stack([4 * px + 2 * py + ic for px, py in chips4]).astype(jnp.int32)
    gw_in_sibs = _mm_tn_groups(to_sibling_core, h, dproj, N_DEV, "mm_gw_in_sibling")
    gw_in_mine, gw_in_sib = _mm_tn_groups(to_my_core, h, dproj, N_DEV, "mm_gw_in_mine", to_sibling=gw_in_sibs)
    gw_in_send = _chip_presum(gw_in_mine, gw_in_sib)
    dh, gw_in_recv = _mm_nt_with_chip_exchange(dproj, w_in3, F32, gw_in_send, "mm_dh")
    dh = dh.reshape(b, s, d)
    grad_x, dshift, dscale, gg_norm = _norm_bwd(x, dh, dx2, scale, g_norm)

    dmod = jnp.concatenate([dshift, dscale, dgate], axis=1).reshape(3 * b, d)
    pkg = jnp.concatenate([dmod, gg_norm, gg_final, jnp.concatenate([gg_sb, gg_dl], axis=1),
                           jnp.zeros((2 * SMALL_ROWS - 3 * b - 3, d), F32)], axis=0)
    pkg_all = _allgather_rows(pkg, "ag_small_grads").reshape(N_DEV, 2 * SMALL_ROWS, d)
    dmod_all = pkg_all[:, :3 * b].reshape(N_DEV * b, 3 * d)
    dmod_cols = lax.dynamic_slice(dmod_all, (0, me * na), (N_DEV * b, na))
    c_rows = c_all.reshape(N_DEV, SMALL_ROWS, d)[:, :b].reshape(N_DEV * b, d)
    g_w_ada, d_w_ada, nm_w_ada, nv_w_ada = _ada_bwd_adam(c_rows, dmod_cols, w_ada[0], m_w_ada[0], v_w_ada[0])

    def pack(b_ada_like, g_norm_like, g_sb_like, g_dil_like, g_final_like):
        return jnp.concatenate([b_ada_like.reshape(3, d), g_norm_like.reshape(1, d), g_final_like.reshape(1, d),
                                jnp.concatenate([g_sb_like, g_dil_like], axis=1).reshape(1, d),
                                jnp.zeros((2, d), F32)], axis=0)

    small = _small_adam(pkg_all, 3 * b, pack(b_ada, g_norm, g_sb, g_dil, g_final),
                        pack(m_b_ada, m_g_norm, m_g_sb, m_g_dil, m_g_final),
                        pack(v_b_ada, v_g_norm, v_g_sb, v_g_dil, v_g_final))

    def unpack(p):
        return (p[0:3].reshape(1, 3 * d), p[3:4], p[5:6, :e], p[5:6, e:], p[4])

    sm_g, sm_d, sm_m, sm_v = (unpack(p) for p in small)

    g_w_in, d_w_in, nm_w_in, nv_w_in = _adam_from_chip_sums(
        gw_in_mine, gw_in_sib, gw_in_recv, w_in[0], m_w_in[0], v_w_in[0], "adam_w_in")
    g_w_out, d_w_out, nm_w_out, nv_w_out = _adam_from_partials(recv_out, w_out[0], m_w_out[0], v_w_out[0], "adam_w_out")

    loss = lax.psum(loss_p[0, 0], ("x", "y", "c"))

    def weights(ada, small_parts, w_in_part, w_out_part):
        b_ada_p, g_norm_p, g_sb_p, g_dil_p, g_final_p = small_parts
        return (ada[None], b_ada_p, g_norm_p, w_in_part[None], g_sb_p, g_dil_p, w_out_part[None], g_final_p)

    return (loss, grad_x,
            *weights(g_w_ada, sm_g, g_w_in, g_w_out),
            *weights(d_w_ada, sm_d, d_w_in, d_w_out),
            *weights(nm_w_ada, sm_m, nm_w_in, nm_w_out),
            *weights(nv_w_ada, sm_v, nv_w_in, nv_w_out))
```

```python
import functools
import math

import jax
import jax.numpy as jnp
from jax import lax
from jax.experimental import pallas as pl
from jax.experimental.pallas import tpu as pltpu

F32 = jnp.float32
BF16 = jnp.bfloat16
MESH = pl.DeviceIdType.MESH

N_DEV = 8
HEAD_DIM = 128
EPS = 1e-6
ALIBI_MAX_BIAS = 8.0
DIL_PAIRS = ((128, 1), (512, 4), (2048, 16))
DIL_STEPS = 128
NEG = -1e30

ADAM_LR = 0.001
ADAM_B1 = 0.9
ADAM_B2 = 0.999
ADAM_EPS = 1e-08
ADAM_WD = 0.01
ADAM_STEP = 10

VMEM_LIMIT_BYTES = 56 * 1024 * 1024
SMALL_ROWS = 8

NT_DIMS = (((1,), (1,)), ((), ()))
TN_DIMS = (((0,), (0,)), ((), ()))


def _params(*semantics):
    return pltpu.CompilerParams(dimension_semantics=semantics, vmem_limit_bytes=VMEM_LIMIT_BYTES)


def _tile(n, want):
    t = min(n, want)
    assert n % t == 0, (n, want)
    return t


def _mesh_pos():
    return lax.axis_index("x"), lax.axis_index("y"), lax.axis_index("c")


def _allgather_rows(x_shard, name):
    m_per, n = x_shard.shape

    def body(x_ref, out_ref, send_sems, recv_sems, local_sem):
        x, y, c = _mesh_pos()
        me, sibling = (x, y, c), (x, y, 1 - c)
        chips = [(1 - x, y), (x, 1 - y), (1 - x, 1 - y)]

        def rows(px, py, pc):
            return out_ref.at[pl.ds((4 * px + 2 * py + pc) * m_per, m_per), :]

        def copy(k, block, to, src=None):
            return pltpu.make_async_remote_copy(
                src_ref=rows(*block) if src is None else src, dst_ref=rows(*block),
                send_sem=send_sems.at[k], recv_sem=recv_sems.at[k], device_id=to, device_id_type=MESH)

        mine = pltpu.make_async_copy(x_ref, rows(*me), local_sem)
        mine.start()
        first = [copy(0, me, sibling, src=x_ref)]
        first += [copy(1 + j, me, (*chip, c), src=x_ref) for j, chip in enumerate(chips)]
        for cp in first:
            cp.start()
        passed = [copy(4 + j, (*chip, c), sibling) for j, chip in enumerate(chips)]
        for j, chip in enumerate(chips):
            copy(1 + j, (*chip, c), me).wait_recv()
            passed[j].start()
        copy(0, sibling, me).wait_recv()
        for j, chip in enumerate(chips):
            copy(4 + j, (*chip, 1 - c), me).wait_recv()
        for cp in first + passed:
            cp.wait_send()
        mine.wait()

    return pl.pallas_call(
        body, name=name,
        out_shape=jax.ShapeDtypeStruct((N_DEV * m_per, n), x_shard.dtype),
        in_specs=[pl.BlockSpec(memory_space=pltpu.VMEM)],
        out_specs=pl.BlockSpec(memory_space=pltpu.VMEM),
        scratch_shapes=[pltpu.SemaphoreType.DMA((7,)), pltpu.SemaphoreType.DMA((7,)), pltpu.SemaphoreType.DMA],
    )(x_shard)


def _two_level_gather(w_ref, out_ref, send_sems, recv_sems, local_sem):
    x, y, c = _mesh_pos()
    me, sibling = (x, y, c), (x, y, 1 - c)
    chips = [(1 - x, y), (x, 1 - y), (1 - x, 1 - y)]

    def copy(k, block, to, src=None):
        dst = out_ref.at[4 * block[0] + 2 * block[1] + block[2]]
        return pltpu.make_async_remote_copy(
            src_ref=dst if src is None else src, dst_ref=dst,
            send_sem=send_sems.at[k], recv_sem=recv_sems.at[k], device_id=to, device_id_type=MESH)

    mine = pltpu.make_async_copy(w_ref, out_ref.at[4 * x + 2 * y + c], local_sem)
    to_sibling = copy(0, me, sibling, src=w_ref)
    to_chips = [copy(1 + j, me, (*chip, c), src=w_ref) for j, chip in enumerate(chips)]
    passed = [copy(4 + j, (*chip, c), sibling) for j, chip in enumerate(chips)]

    def early():
        mine.start()
        to_sibling.start()

    def ici():
        for cp in to_chips:
            cp.start()

    def finish():
        for j, chip in enumerate(chips):
            copy(1 + j, (*chip, c), me).wait_recv()
            passed[j].start()
        copy(0, sibling, me).wait_recv()
        for j, chip in enumerate(chips):
            copy(4 + j, (*chip, 1 - c), me).wait_recv()
        for cp in [to_sibling] + to_chips + passed:
            cp.wait_send()
        mine.wait()

    return early, ici, finish


PROJ_HALVES = 2


def _unit_schedule():
    assert PROJ_HALVES == 2
    sched = [("own", None, hf) for hf in range(PROJ_HALVES)] + [("sib", None, hf) for hf in range(PROJ_HALVES)]
    for rnd in ([(0, 0), (1, 1)], [(2, 0), (2, 1)], [(0, 1), (1, 0)]):
        sched += [("direct", j, hf) for j, hf in rnd] + [("fwd", j, hf) for j, hf in rnd]
    return sched


def _unit_ids():
    x, y, c = _mesh_pos()
    chips = _other_chips()
    ids = []
    for kind, j, hf in _unit_schedule():
        px, py = (x, y) if j is None else chips[j]
        pc = c if kind in ("own", "direct") else 1 - c
        ids.append(PROJ_HALVES * (4 * px + 2 * py + pc) + hf)
    return ids


def _proj_with_allgather(h, w_own, order, w_later):
    t, d = h.shape
    nh = PROJ_HALVES
    u = w_own.shape[1] // nh
    sched = _unit_schedule()
    n_units = len(sched)
    pos = {entry: p for p, entry in enumerate(sched)}
    tm = _tile(t, MM_TM)
    m_tiles = t // tm
    prep_m = max(m_tiles - 2, 0)
    grid = (n_units, m_tiles)

    later_ici_pos = pos[("direct", 2, 0)]

    def body(order_ref, h_ref, wown_ref, later_ref, proj_ref, w3_ref, later3_ref, bbuf, bsems, send_sems, recv_sems,
             local_sems, later_send_sems, later_recv_sems, later_local_sem):
        n, m = pl.program_id(0), pl.program_id(1)
        x, y, c = _mesh_pos()
        chips = _other_chips()
        sibling = (x, y, 1 - c)
        later_early, later_ici, later_finish = _two_level_gather(
            later_ref, later3_ref, later_send_sems, later_recv_sems, later_local_sem)

        def unit(p):
            return w3_ref.at[order_ref[p]]

        def arrival(p):
            return pltpu.make_async_remote_copy(
                src_ref=unit(p), dst_ref=unit(p), send_sem=send_sems.at[0], recv_sem=recv_sems.at[p - nh],
                device_id=sibling, device_id_type=MESH)

        def send(k, src, p_here, p_there, to):
            return pltpu.make_async_remote_copy(
                src_ref=src, dst_ref=unit(p_here), send_sem=send_sems.at[k], recv_sem=recv_sems.at[p_there - nh],
                device_id=to, device_id_type=MESH)

        copies = []

        def out(src, p_here, p_there, to):
            copies.append(send(len(copies), src, p_here, p_there, to))
            return copies[-1]

        def own_unit(hf):
            return wown_ref.at[:, pl.ds(hf * u, u)]

        def own(hf, p_there, to):
            return out(own_unit(hf), pos[("own", None, hf)], p_there, to)

        x_nbr, y_nbr = (*chips[0], c), (*chips[1], c)
        first = [own(hf, pos[("sib", None, hf)], sibling) for hf in range(nh)]
        first += [own(0, pos[("direct", 0, 0)], x_nbr), own(1, pos[("direct", 1, 1)], y_nbr)]
        after = {}
        for kind, j, hf in sched:
            if kind == "direct":
                p = pos[(kind, j, hf)]
                after[p] = [out(unit(p), p, pos[("fwd", j, hf)], sibling)]
        p = pos[("direct", 0, 0)]
        after[p] += [out(unit(p), p, pos[("direct", 2, 0)], y_nbr), own(0, pos[("direct", 1, 0)], y_nbr)]
        p = pos[("direct", 1, 1)]
        after[p] += [out(unit(p), p, pos[("direct", 2, 1)], x_nbr), own(1, pos[("direct", 0, 1)], x_nbr)]
        locals_ = [pltpu.make_async_copy(own_unit(hf), unit(pos[("own", None, hf)]), local_sems.at[hf])
                   for hf in range(nh)]

        def fetch(p):
            src = own_unit(sched[p][2]) if sched[p][0] == "own" else unit(p)
            return pltpu.make_async_copy(src, bbuf.at[p % 2], bsems.at[p % 2])

        @pl.when((n == 0) & (m == 0))
        def _():
            for cp in locals_ + first:
                cp.start()
            fetch(0).start()
            later_early()

        for p in range(n_units):
            @pl.when((n == p) & (m == 0))
            def _(p=p):
                fetch(p).wait()

            if p + 1 < n_units:
                @pl.when((n == p) & (m == prep_m))
                def _(p=p):
                    if sched[p + 1][0] != "own":
                        arrival(p + 1).wait_recv()
                    for cp in after.get(p + 1, []):
                        cp.start()
                    fetch(p + 1).start()
                    if p + 1 == later_ici_pos:
                        later_ici()

        proj_ref[...] = jnp.dot(h_ref[...], bbuf[n % 2], preferred_element_type=F32).astype(proj_ref.dtype)

        @pl.when((n == n_units - 1) & (m == m_tiles - 1))
        def _():
            for cp in copies:
                cp.wait_send()
            for cp in locals_:
                cp.wait()
            later_finish()

    n_out = len(sched) - nh
    return pl.pallas_call(
        body, name="mm_proj_allgather",
        grid_spec=pltpu.PrefetchScalarGridSpec(
            num_scalar_prefetch=1, grid=grid,
            in_specs=[pl.BlockSpec((tm, d), lambda n, m, order: (m, 0)), ANY_SPEC, ANY_SPEC],
            out_specs=(pl.BlockSpec((tm, u), lambda n, m, order: (m, order[n])), ANY_SPEC, ANY_SPEC),
            scratch_shapes=[pltpu.VMEM((2, d, u), h.dtype), pltpu.SemaphoreType.DMA((2,)),
                            pltpu.SemaphoreType.DMA((n_out,)), pltpu.SemaphoreType.DMA((n_out,)),
                            pltpu.SemaphoreType.DMA((nh,)),
                            pltpu.SemaphoreType.DMA((N_DEV - 1,)), pltpu.SemaphoreType.DMA((N_DEV - 1,)),
                            pltpu.SemaphoreType.DMA]),
        out_shape=(jax.ShapeDtypeStruct((t, n_units * u), h.dtype), jax.ShapeDtypeStruct((n_units, d, u), h.dtype),
                   jax.ShapeDtypeStruct((N_DEV,) + w_later.shape, w_later.dtype)),
        compiler_params=_params("arbitrary", "arbitrary"),
    )(order, h, w_own, w_later)


ANY_SPEC = pl.BlockSpec(memory_space=pl.ANY)


def _grid_first_last(grid):
    ids = [pl.program_id(a) for a in range(len(grid))]
    first = functools.reduce(lambda p, q: p & q, [i == 0 for i in ids])
    last = functools.reduce(lambda p, q: p & q, [i == n - 1 for i, n in zip(ids, grid)])
    return first, last


def _all_to_all_copies(src, dst, send_sems, recv_sems, local_sem):
    x, y, c = _mesh_pos()
    my = 4 * x + 2 * y + c
    copies = [pltpu.make_async_copy(src.at[my], dst.at[my], local_sem)]
    for d in range(1, N_DEV):
        px = 1 - x if d & 4 else x
        py = 1 - y if d & 2 else y
        pc = 1 - c if d & 1 else c
        copies.append(pltpu.make_async_remote_copy(
            src_ref=src.at[4 * px + 2 * py + pc], dst_ref=dst.at[my],
            send_sem=send_sems.at[d - 1], recv_sem=recv_sems.at[d - 1],
            device_id=(px, py, pc), device_id_type=MESH))
    return copies


def _other_chips():
    x, y, _ = _mesh_pos()
    return [(1 - x, y), (x, 1 - y), (1 - x, 1 - y)]


def _same_core_copies(src, dst, send_sems, recv_sems):
    c = lax.axis_index("c")
    return [pltpu.make_async_remote_copy(
        src_ref=src.at[j], dst_ref=dst.at[j], send_sem=send_sems.at[j], recv_sem=recv_sems.at[j],
        device_id=(*chip, c), device_id_type=MESH) for j, chip in enumerate(_other_chips())]


def _sibling_copies(src, dst, send_sems, recv_sems):
    x, y, c = _mesh_pos()
    return [pltpu.make_async_remote_copy(
        src_ref=src.at[j], dst_ref=dst.at[j], send_sem=send_sems.at[j], recv_sem=recv_sems.at[j],
        device_id=(x, y, 1 - c), device_id_type=MESH) for j in range(src.shape[0])]


def _chip_presum(mine, from_sibling):
    _, r, cdim = mine.shape
    tr = _tile(r, 256)

    def body(p_ref, s_ref, o_ref):
        o_ref[...] = (p_ref[...].astype(F32) + s_ref[...].astype(F32)).astype(o_ref.dtype)

    blk = pl.BlockSpec((None, tr, cdim), lambda j, i: (1 + j, i, 0))
    return pl.pallas_call(
        body, name="chip_presum", grid=(3, r // tr), in_specs=[blk, blk],
        out_specs=pl.BlockSpec((None, tr, cdim), lambda j, i: (j, i, 0)),
        out_shape=jax.ShapeDtypeStruct((3, r, cdim), mine.dtype),
        compiler_params=_params("parallel", "parallel"),
    )(mine, from_sibling)


def _mm_call(a, b, dims, nk, grid, a_spec, b_spec, o_spec, out_shape, acc_shape, name):
    def body(a_ref, b_ref, o_ref, acc_ref):
        k = pl.program_id(2)

        @pl.when(k == 0)
        def _():
            acc_ref[...] = jnp.zeros_like(acc_ref)

        acc_ref[...] += lax.dot_general(a_ref[...], b_ref[...], dims, preferred_element_type=F32)

        @pl.when(k == nk - 1)
        def _():
            o_ref[...] = acc_ref[...].astype(o_ref.dtype)

    return pl.pallas_call(
        body, name=name, grid=grid, in_specs=[a_spec, b_spec], out_specs=o_spec, out_shape=out_shape,
        scratch_shapes=[pltpu.VMEM(acc_shape, F32)],
        compiler_params=_params("parallel", "parallel", "arbitrary"),
    )(a, b)


MM_TM, MM_TN, MM_TK = 1024, 2048, 1024


def _mm_nn(a, b3, out_dtype, name):
    m, kk = a.shape
    g, _, nb = b3.shape
    tm, tn, tk = _tile(m, MM_TM), _tile(nb, MM_TN), _tile(kk, MM_TK)
    npb = nb // tn
    return _mm_call(
        a, b3, (((1,), (0,)), ((), ())), kk // tk, (m // tm, g * npb, kk // tk),
        pl.BlockSpec((tm, tk), lambda i, j, k: (i, k)),
        pl.BlockSpec((None, tk, tn), lambda i, j, k: (j // npb, k, j % npb)),
        pl.BlockSpec((tm, tn), lambda i, j, k: (i, j)),
        jax.ShapeDtypeStruct((m, g * nb), out_dtype), (tm, tn), name)


def _mm_nt(a, b3, out_dtype, name):
    m, kk = a.shape
    g, n, kb = b3.shape
    tm, tn, tk = _tile(m, MM_TM), _tile(n, MM_TN), _tile(kb, MM_TK)
    kpb = kb // tk
    return _mm_call(
        a, b3, NT_DIMS, kk // tk, (m // tm, n // tn, kk // tk),
        pl.BlockSpec((tm, tk), lambda i, j, k: (i, k)),
        pl.BlockSpec((None, tn, tk), lambda i, j, k: (k // kpb, j, k % kpb)),
        pl.BlockSpec((tm, tn), lambda i, j, k: (i, j)),
        jax.ShapeDtypeStruct((m, n), out_dtype), (tm, tn), name)


def _mm_nt_with_chip_exchange(a, b3, out_dtype, send3, name):
    m, kk = a.shape
    g, n, kb = b3.shape
    tm, tn, tk = _tile(m, MM_TM), _tile(n, MM_TN), _tile(kb, MM_TK)
    kpb = kb // tk
    grid = (m // tm, n // tn, kk // tk)

    def body(a_ref, b_ref, s_ref, o_ref, r_ref, acc_ref, send_sems, recv_sems):
        first, last = _grid_first_last(grid)
        k = pl.program_id(2)

        @pl.when(first)
        def _():
            for cp in _same_core_copies(s_ref, r_ref, send_sems, recv_sems):
                cp.start()

        @pl.when(k == 0)
        def _():
            acc_ref[...] = jnp.zeros_like(acc_ref)

        acc_ref[...] += lax.dot_general(a_ref[...], b_ref[...], NT_DIMS, preferred_element_type=F32)

        @pl.when(k == grid[2] - 1)
        def _():
            o_ref[...] = acc_ref[...].astype(o_ref.dtype)

        @pl.when(last)
        def _():
            for cp in _same_core_copies(s_ref, r_ref, send_sems, recv_sems):
                cp.wait()

    return pl.pallas_call(
        body, name=name, grid=grid,
        in_specs=[pl.BlockSpec((tm, tk), lambda i, j, k: (i, k)),
                  pl.BlockSpec((None, tn, tk), lambda i, j, k: (k // kpb, j, k % kpb)), ANY_SPEC],
        out_specs=(pl.BlockSpec((tm, tn), lambda i, j, k: (i, j)), ANY_SPEC),
        out_shape=(jax.ShapeDtypeStruct((m, n), out_dtype), jax.ShapeDtypeStruct(send3.shape, send3.dtype)),
        scratch_shapes=[pltpu.VMEM((tm, tn), F32), pltpu.SemaphoreType.DMA((3,)), pltpu.SemaphoreType.DMA((3,))],
        compiler_params=_params("arbitrary", "arbitrary", "arbitrary"),
    )(a, b3, send3)


def _mm_tn_groups(groups, a, b, n_groups, name, to_sibling=None):
    t, m = a.shape
    nb = b.shape[1] // n_groups
    ng = groups.shape[0]
    tm, tk = _tile(m, MM_TM), _tile(t, MM_TK)
    grid = (m // tm, ng, t // tk)
    carry = to_sibling is not None

    def body(groups_ref, a_ref, b_ref, *rest):
        if carry:
            s_ref, o_ref, r_ref, acc_ref, send_sems, recv_sems = rest
        else:
            o_ref, acc_ref = rest
        k = pl.program_id(2)
        first, last = _grid_first_last(grid)

        if carry:
            @pl.when(first)
            def _():
                for cp in _sibling_copies(s_ref, r_ref, send_sems, recv_sems):
                    cp.start()

        @pl.when(k == 0)
        def _():
            acc_ref[...] = jnp.zeros_like(acc_ref)

        acc_ref[...] += lax.dot_general(a_ref[...], b_ref[...], TN_DIMS, preferred_element_type=F32)

        @pl.when(k == grid[2] - 1)
        def _():
            o_ref[...] = acc_ref[...].astype(o_ref.dtype)

        if carry:
            @pl.when(last)
            def _():
                for cp in _sibling_copies(s_ref, r_ref, send_sems, recv_sems):
                    cp.wait()

    shp = jax.ShapeDtypeStruct((ng, m, nb), b.dtype)
    o_spec = pl.BlockSpec((None, tm, nb), lambda i, j, k, grp: (j, i, 0))
    return pl.pallas_call(
        body, name=name,
        grid_spec=pltpu.PrefetchScalarGridSpec(
            num_scalar_prefetch=1, grid=grid,
            in_specs=[pl.BlockSpec((tk, tm), lambda i, j, k, grp: (k, i)),
                      pl.BlockSpec((tk, nb), lambda i, j, k, grp: (k, grp[j]))] + ([ANY_SPEC] if carry else []),
            out_specs=(o_spec, ANY_SPEC) if carry else o_spec,
            scratch_shapes=[pltpu.VMEM((tm, nb), F32)] + (
                [pltpu.SemaphoreType.DMA((ng,)), pltpu.SemaphoreType.DMA((ng,))] if carry else [])),
        out_shape=(shp, shp) if carry else shp,
        compiler_params=_params("arbitrary", "arbitrary", "arbitrary"),
    )(*((groups, a, b, to_sibling) if carry else (groups, a, b)))


def _mm_tn(a, b, g, out_dtype, name):
    t, m = a.shape
    nb = b.shape[1] // g
    tm, tn, tk = _tile(m, MM_TM), _tile(nb, MM_TN), _tile(t, MM_TK)
    npb = nb // tn
    return _mm_call(
        a, b, TN_DIMS, t // tk, (m // tm, g * npb, t // tk),
        pl.BlockSpec((tk, tm), lambda i, j, k: (k, i)),
        pl.BlockSpec((tk, tn), lambda i, j, k: (k, j)),
        pl.BlockSpec((None, tm, tn), lambda i, j, k: (j // npb, i, j % npb)),
        jax.ShapeDtypeStruct((g, m, nb), out_dtype), (tm, tn), name)


def _silu(z):
    return z * jax.nn.sigmoid(z)


def _ada_fwd(c_all, w_shard, b_own):
    r, d = c_all.shape
    na = w_shard.shape[1]
    tk = _tile(d, 512)
    nk = d // tk

    def body(c_ref, w_ref, b_ref, o_ref):
        k = pl.program_id(0)

        @pl.when(k == 0)
        def _():
            o_ref[...] = jnp.zeros_like(o_ref) + b_ref[...]

        cs = _silu(c_ref[...]).astype(BF16)
        o_ref[...] += jnp.dot(cs, w_ref[...].astype(BF16), preferred_element_type=F32)

    return pl.pallas_call(
        body, name="ada_fwd", grid=(nk,),
        in_specs=[pl.BlockSpec((r, tk), lambda k: (0, k)), pl.BlockSpec((tk, na), lambda k: (k, 0)),
                  pl.BlockSpec((1, na), lambda k: (0, 0))],
        out_specs=pl.BlockSpec((r, na), lambda k: (0, 0)),
        out_shape=jax.ShapeDtypeStruct((r, na), F32),
        compiler_params=_params("arbitrary"),
    )(c_all, w_shard, b_own)


def _adam(w, g, m, v):
    nm = ADAM_B1 * m + (1.0 - ADAM_B1) * g
    nv = ADAM_B2 * v + (1.0 - ADAM_B2) * (g * g)
    m_hat = nm / (1.0 - ADAM_B1 ** ADAM_STEP)
    v_hat = nv / (1.0 - ADAM_B2 ** ADAM_STEP)
    delta = -ADAM_LR * (m_hat / (jnp.sqrt(v_hat) + ADAM_EPS) + ADAM_WD * w)
    return delta, nm, nv


def _ada_bwd_adam(c_rows, dmod_cols, w, m, v):
    bg, d = c_rows.shape
    na = w.shape[1]
    tr = _tile(d, 256)

    def body(c_ref, dm_ref, w_ref, m_ref, v_ref, g_ref, d_ref, nm_ref, nv_ref):
        cs = _silu(c_ref[...]).astype(BF16)
        g = lax.dot_general(cs, dm_ref[...].astype(BF16), TN_DIMS, preferred_element_type=F32)
        delta, nm, nv = _adam(w_ref[...], g, m_ref[...], v_ref[...])
        g_ref[...] = g
        d_ref[...] = delta
        nm_ref[...] = nm
        nv_ref[...] = nv

    blk = pl.BlockSpec((tr, na), lambda i: (i, 0))
    shp = jax.ShapeDtypeStruct((d, na), F32)
    return pl.pallas_call(
        body, name="ada_bwd_adam", grid=(d // tr,),
        in_specs=[pl.BlockSpec((bg, tr), lambda i: (0, i)), pl.BlockSpec((bg, na), lambda i: (0, 0)), blk, blk, blk],
        out_specs=(blk, blk, blk, blk), out_shape=(shp, shp, shp, shp),
        compiler_params=_params("parallel"),
    )(c_rows, dmod_cols, w, m, v)


def _small_adam(pkg_all, n_batch_rows, w, m, v):
    d = w.shape[1]

    def body(p_ref, w_ref, m_ref, v_ref, g_ref, d_ref, nm_ref, nv_ref):
        for part in range(3):
            acc = jnp.zeros((1, d), F32)
            for dev in range(N_DEV):
                for b in range(n_batch_rows // 3):
                    acc = acc + p_ref[dev, 3 * b + part:3 * b + part + 1, :]
            g_ref[part:part + 1, :] = acc
        for rrow in range(3):
            acc = jnp.zeros((1, d), F32)
            for dev in range(N_DEV):
                acc = acc + p_ref[dev, n_batch_rows + rrow:n_batch_rows + rrow + 1, :]
            g_ref[3 + rrow:4 + rrow, :] = acc
        g_ref[6:8, :] = jnp.zeros((2, d), F32)
        g = g_ref[...]
        delta, nm, nv = _adam(w_ref[...], g, m_ref[...], v_ref[...])
        d_ref[...] = delta
        nm_ref[...] = nm
        nv_ref[...] = nv

    vm = pl.BlockSpec(memory_space=pltpu.VMEM)
    shp = jax.ShapeDtypeStruct((SMALL_ROWS, d), F32)
    return pl.pallas_call(
        body, name="small_adam", in_specs=[vm, vm, vm, vm], out_specs=(vm, vm, vm, vm),
        out_shape=(shp, shp, shp, shp),
    )(pkg_all, w, m, v)


def _adam_from_chip_sums(mine, from_sibling, from_chips, w, m, v, name):
    _, r, c = mine.shape
    tr = _tile(r, 128)

    def body(p_ref, s_ref, f_ref, w_ref, m_ref, v_ref, g_ref, d_ref, nm_ref, nv_ref):
        g = p_ref[...].astype(F32) + s_ref[...].astype(F32)
        for j in range(3):
            g = g + f_ref[j].astype(F32)
        delta, nm, nv = _adam(w_ref[...], g, m_ref[...], v_ref[...])
        g_ref[...] = g
        d_ref[...] = delta
        nm_ref[...] = nm
        nv_ref[...] = nv

    blk = pl.BlockSpec((tr, c), lambda i: (i, 0))
    slot0 = pl.BlockSpec((None, tr, c), lambda i: (0, i, 0))
    shp = jax.ShapeDtypeStruct((r, c), F32)
    return pl.pallas_call(
        body, name=name, grid=(r // tr,),
        in_specs=[slot0, slot0, pl.BlockSpec((3, tr, c), lambda i: (0, i, 0)), blk, blk, blk],
        out_specs=(blk, blk, blk, blk), out_shape=(shp, shp, shp, shp),
        compiler_params=_params("parallel"),
    )(mine, from_sibling, from_chips, w, m, v)


def _adam_from_partials(recv, w, m, v, name):
    _, r, c = recv.shape
    tr = _tile(r, 128)

    def body(p_ref, w_ref, m_ref, v_ref, g_ref, d_ref, nm_ref, nv_ref):
        g = p_ref[0].astype(F32)
        for dev in range(1, N_DEV):
            g = g + p_ref[dev].astype(F32)
        delta, nm, nv = _adam(w_ref[...], g, m_ref[...], v_ref[...])
        g_ref[...] = g
        d_ref[...] = delta
        nm_ref[...] = nm
        nv_ref[...] = nv

    blk = pl.BlockSpec((tr, c), lambda i: (i, 0))
    shp = jax.ShapeDtypeStruct((r, c), F32)
    return pl.pallas_call(
        body, name=name, grid=(r // tr,),
        in_specs=[pl.BlockSpec((N_DEV, tr, c), lambda i: (0, i, 0)), blk, blk, blk],
        out_specs=(blk, blk, blk, blk), out_shape=(shp, shp, shp, shp),
        compiler_params=_params("parallel"),
    )(recv, w, m, v)


def _norm_mod(x, g_norm, scale, shift):
    b, s, d = x.shape
    ts = _tile(s, 256)

    def body(x_ref, g_ref, sc_ref, sh_ref, h_ref):
        xv = x_ref[...]
        r = lax.rsqrt(jnp.mean(xv * xv, axis=-1, keepdims=True) + EPS)
        xn = (xv * r) * g_ref[...]
        h_ref[...] = (xn * (1.0 + sc_ref[...]) + sh_ref[...]).astype(BF16)

    tok = pl.BlockSpec((None, ts, d), lambda i, j: (i, j, 0))
    per_b = pl.BlockSpec((None, 1, d), lambda i, j: (i, 0, 0))
    return pl.pallas_call(
        body, name="norm_mod", grid=(b, s // ts),
        in_specs=[tok, pl.BlockSpec((1, d), lambda i, j: (0, 0)), per_b, per_b],
        out_specs=tok, out_shape=jax.ShapeDtypeStruct((b, s, d), BF16),
        compiler_params=_params("parallel", "parallel"),
    )(x, g_norm, scale, shift)


def _final_fwd_bwd(x, out, gate, g_final, target):
    b, s, d = x.shape
    ts = _tile(s, 256)

    def body(x_ref, o_ref, gt_ref, g_ref, t_ref, loss_ref, dx2_ref, dout_ref, dgate_ref, gg_ref):
        i, j = pl.program_id(0), pl.program_id(1)

        @pl.when((i == 0) & (j == 0))
        def _():
            loss_ref[...] = jnp.zeros_like(loss_ref)
            gg_ref[...] = jnp.zeros_like(gg_ref)

        @pl.when(j == 0)
        def _():
            dgate_ref[...] = jnp.zeros_like(dgate_ref)

        ov = o_ref[...]
        gt = gt_ref[...]
        x2 = x_ref[...] + gt * ov
        r = lax.rsqrt(jnp.mean(x2 * x2, axis=-1, keepdims=True) + EPS)
        xh = x2 * r
        err = xh * g_ref[...] - t_ref[...]
        loss_ref[...] += 0.5 * jnp.sum(jnp.mean(err * err, axis=-1, keepdims=True), axis=0, keepdims=True)
        dfin = err * (1.0 / d)
        gg_ref[...] += jnp.sum(dfin * xh, axis=0, keepdims=True)
        dxh = dfin * g_ref[...]
        dx2 = r * (dxh - xh * jnp.mean(dxh * xh, axis=-1, keepdims=True))
        dx2_ref[...] = dx2
        dout_ref[...] = (gt * dx2).astype(BF16)
        dgate_ref[...] += jnp.sum(dx2 * ov, axis=0, keepdims=True)

    tok = pl.BlockSpec((None, ts, d), lambda i, j: (i, j, 0))
    per_b = pl.BlockSpec((None, 1, d), lambda i, j: (i, 0, 0))
    vec = pl.BlockSpec((1, d), lambda i, j: (0, 0))
    return pl.pallas_call(
        body, name="final_fwd_bwd", grid=(b, s // ts),
        in_specs=[tok, tok, per_b, vec, tok],
        out_specs=(pl.BlockSpec((8, 128), lambda i, j: (0, 0)), tok, tok, per_b, vec),
        out_shape=(jax.ShapeDtypeStruct((8, 128), F32), jax.ShapeDtypeStruct((b, s, d), F32),
                   jax.ShapeDtypeStruct((b, s, d), BF16), jax.ShapeDtypeStruct((b, 1, d), F32),
                   jax.ShapeDtypeStruct((1, d), F32)),
        compiler_params=_params("arbitrary", "arbitrary"),
    )(x, out, gate, g_final, target)


def _norm_bwd(x, dh, dx2, scale, g_norm):
    b, s, d = x.shape
    ts = _tile(s, 256)

    def body(x_ref, dh_ref, dx2_ref, sc_ref, g_ref, gx_ref, dsh_ref, dsc_ref, gg_ref):
        i, j = pl.program_id(0), pl.program_id(1)

        @pl.when((i == 0) & (j == 0))
        def _():
            gg_ref[...] = jnp.zeros_like(gg_ref)

        @pl.when(j == 0)
        def _():
            dsh_ref[...] = jnp.zeros_like(dsh_ref)
            dsc_ref[...] = jnp.zeros_like(dsc_ref)

        xv = x_ref[...]
        dhv = dh_ref[...]
        r = lax.rsqrt(jnp.mean(xv * xv, axis=-1, keepdims=True) + EPS)
        xh = xv * r
        xn = xh * g_ref[...]
        dsh_ref[...] += jnp.sum(dhv, axis=0, keepdims=True)
        dsc_ref[...] += jnp.sum(dhv * xn, axis=0, keepdims=True)
        dxn = dhv * (1.0 + sc_ref[...])
        gg_ref[...] += jnp.sum(dxn * xh, axis=0, keepdims=True)
        dxh = dxn * g_ref[...]
        gx_ref[...] = dx2_ref[...] + r * (dxh - xh * jnp.mean(dxh * xh, axis=-1, keepdims=True))

    tok = pl.BlockSpec((None, ts, d), lambda i, j: (i, j, 0))
    per_b = pl.BlockSpec((None, 1, d), lambda i, j: (i, 0, 0))
    vec = pl.BlockSpec((1, d), lambda i, j: (0, 0))
    return pl.pallas_call(
        body, name="norm_bwd", grid=(b, s // ts),
        in_specs=[tok, tok, tok, per_b, vec],
        out_specs=(tok, per_b, per_b, vec),
        out_shape=(jax.ShapeDtypeStruct((b, s, d), F32), jax.ShapeDtypeStruct((b, 1, d), F32),
                   jax.ShapeDtypeStruct((b, 1, d), F32), jax.ShapeDtypeStruct((1, d), F32)),
        compiler_params=_params("arbitrary", "arbitrary"),
    )(x, dh, dx2, scale, g_norm)


def _gate_fwd(y_sb, y_dl, proj, g_sb, g_dl):
    t, e = y_sb.shape
    n_heads = e // HEAD_DIM
    tt = _tile(t, 256)

    def body(ys_ref, yd_ref, zs_ref, zd_ref, gs_ref, gd_ref, o_ref):
        for grp, (y_ref, z_ref, g_ref) in enumerate(((ys_ref, zs_ref, gs_ref), (yd_ref, zd_ref, gd_ref))):
            for h in range(n_heads):
                sl = slice(h * HEAD_DIM, (h + 1) * HEAD_DIM)
                y = y_ref[:, sl]
                r = lax.rsqrt(jnp.mean(y * y, axis=-1, keepdims=True) + EPS)
                yn = (y * r) * g_ref[:, sl]
                z = z_ref[:, sl].astype(F32)
                o_ref[:, grp * e + h * HEAD_DIM:grp * e + (h + 1) * HEAD_DIM] = (yn * _silu(z)).astype(BF16)

    yblk = pl.BlockSpec((tt, e), lambda i: (i, 0))
    gblk = pl.BlockSpec((1, e), lambda i: (0, 0))
    return pl.pallas_call(
        body, name="gate_fwd", grid=(t // tt,),
        in_specs=[yblk, yblk, pl.BlockSpec((tt, e), lambda i: (i, 3)), pl.BlockSpec((tt, e), lambda i: (i, 7)),
                  gblk, gblk],
        out_specs=pl.BlockSpec((tt, 2 * e), lambda i: (i, 0)),
        out_shape=jax.ShapeDtypeStruct((t, 2 * e), BF16),
        compiler_params=_params("parallel"),
    )(y_sb, y_dl, proj, proj, g_sb, g_dl)


def _gate_bwd(dyg, y_sb, y_dl, proj, g_sb, g_dl):
    t, e = y_sb.shape
    n_heads = e // HEAD_DIM
    tt = _tile(t, 256)

    def body(dg_ref, ys_ref, yd_ref, zs_ref, zd_ref, gs_ref, gd_ref, dys_ref, dyd_ref, dz_ref, ggs_ref, ggd_ref):
        grp = pl.program_id(1)

        @pl.when((pl.program_id(0) == 0) & (grp == 0))
        def _():
            ggs_ref[...] = jnp.zeros_like(ggs_ref)
            ggd_ref[...] = jnp.zeros_like(ggd_ref)

        def one_group(y_ref, z_ref, g_ref, dy_ref, gg_ref):
            for h in range(n_heads):
                sl = slice(h * HEAD_DIM, (h + 1) * HEAD_DIM)
                dg = dg_ref[:, sl].astype(F32)
                y = y_ref[:, sl]
                z = z_ref[:, sl].astype(F32)
                g = g_ref[:, sl]
                r = lax.rsqrt(jnp.mean(y * y, axis=-1, keepdims=True) + EPS)
                yh = y * r
                sig = jax.nn.sigmoid(z)
                dyn = dg * (z * sig)
                dz_ref[:, sl] = (dg * (yh * g) * (sig * (1.0 + z * (1.0 - sig)))).astype(BF16)
                gg_ref[:, sl] += jnp.sum(dyn * yh, axis=0, keepdims=True)
                dyh = dyn * g
                dy_ref[:, sl] = (r * (dyh - yh * jnp.mean(dyh * yh, axis=-1, keepdims=True))).astype(BF16)

        @pl.when(grp == 0)
        def _():
            one_group(ys_ref, zs_ref, gs_ref, dys_ref, ggs_ref)

        @pl.when(grp == 1)
        def _():
            one_group(yd_ref, zd_ref, gd_ref, dyd_ref, ggd_ref)

    yblk = pl.BlockSpec((tt, e), lambda i, g: (i, 0))
    gblk = pl.BlockSpec((1, e), lambda i, g: (0, 0))
    act = jax.ShapeDtypeStruct((t, e), BF16)
    vec = jax.ShapeDtypeStruct((1, e), F32)
    return pl.pallas_call(
        body, name="gate_bwd", grid=(t // tt, 2),
        in_specs=[pl.BlockSpec((tt, e), lambda i, g: (i, g)), yblk, yblk,
                  pl.BlockSpec((tt, e), lambda i, g: (i, 3)), pl.BlockSpec((tt, e), lambda i, g: (i, 7)), gblk, gblk],
        out_specs=(yblk, yblk, pl.BlockSpec((tt, e), lambda i, g: (i, 3 + 4 * g)), gblk, gblk),
        out_shape=(act, act, jax.ShapeDtypeStruct((t, N_DEV * e), BF16), vec, vec),
        compiler_params=_params("arbitrary", "arbitrary"),
    )(dyg, y_sb, y_dl, proj, proj, g_sb, g_dl)


ATT_TQ = 256
HEADS_PER_STEP = 4
SOFTPLUS_CLAMP = 30.0
ATT_STRIP = 32


def _split2_dot(x, u):
    hi = x.astype(BF16)
    lo = (x - hi.astype(F32)).astype(BF16)
    n = x.shape[0]
    both = jnp.dot(jnp.concatenate([hi, lo], axis=0), u, preferred_element_type=F32)
    return both[:n] + both[n:]


def _iota2(n):
    return lax.broadcasted_iota(jnp.int32, (n, n), 0), lax.broadcasted_iota(jnp.int32, (n, n), 1)


def _head_slices():
    return [slice(hh * HEAD_DIM, (hh + 1) * HEAD_DIM) for hh in range(HEADS_PER_STEP)]


def _bwd_out_spec(s, e, tq, group0):
    w = HEADS_PER_STEP * HEAD_DIM
    per_group = e // w
    nq = s // tq
    return pl.BlockSpec((None, s, w), lambda i, h, j: (i, 0, (group0 + jnp.maximum(j - (nq - 1), 0)) * per_group + h))


def _att_specs(s, e, tq, col0):
    n_heads = e // HEAD_DIM
    hp = HEADS_PER_STEP
    assert n_heads % hp == 0 and col0 % hp == 0
    w = hp * HEAD_DIM
    nq = s // tq
    q_spec = pl.BlockSpec((None, tq, w), lambda i, h, j: (i, jnp.minimum(j, nq - 1), col0 // hp + h))
    k_spec = pl.BlockSpec((None, s, w), lambda i, h, j: (i, 0, (col0 + n_heads) // hp + h))
    v_spec = pl.BlockSpec((None, s, w), lambda i, h, j: (i, 0, (col0 + 2 * n_heads) // hp + h))
    return q_spec, k_spec, v_spec


def _sb_fwd(proj3, e):
    b, s, _ = proj3.shape
    n_heads = e // HEAD_DIM
    hp = HEADS_PER_STEP
    tq = _tile(s, ATT_TQ)
    nq = s // tq
    inv = 1.0 / math.sqrt(HEAD_DIM)

    def body(q_ref, k_ref, v_ref, y_ref, tot_ref, acc_ref, car_ref):
        i = pl.program_id(2)
        row, col = _iota2(tq)
        before = row > col
        u_after = before.astype(BF16)
        acc_ref[...] = jnp.zeros_like(acc_ref)
        car_ref[...] = jnp.zeros_like(car_ref)

        def block(j, diagonal):
            keys = pl.ds(pl.multiple_of(j * tq, tq), tq)
            heads = list(enumerate(_head_slices()))
            zs = [lax.dot_general(q_ref[:, hs], k_ref[keys, hs], NT_DIMS, preferred_element_type=F32) * inv
                  for _, hs in heads]
            sps = [jnp.maximum(jnp.log(1.0 + jnp.exp(jnp.minimum(z, SOFTPLUS_CLAMP))), z) for z in zs]
            loms = [jnp.where(before, -sp, 0.0) if diagonal else -sp for sp in sps]
            sufs = [_split2_dot(loms[hh], u_after) + car_ref[hh] for hh, _ in heads]
            avs = [jnp.exp((zs[hh] - sps[hh]) + sufs[hh]) for hh, _ in heads]
            if diagonal:
                avs = [jnp.where(before, a, 0.0) for a in avs]
            pvs = [jnp.dot(avs[hh].astype(BF16), v_ref[keys, hs], preferred_element_type=F32) for hh, hs in heads]
            for hh, _ in heads:
                acc_ref[hh] += pvs[hh]
                car_ref[hh] += jnp.sum(loms[hh], axis=1, keepdims=True)

        block(i, True)

        def step(it, carry):
            block(i - it, False)
            return carry

        lax.fori_loop(1, i + 1, step, 0)
        for hh, hs in enumerate(_head_slices()):
            y_ref[:, hs] = acc_ref[hh]
            tot_ref[:, hs] = jnp.broadcast_to(car_ref[hh], (tq, HEAD_DIM))

    q_spec, k_spec, v_spec = _att_specs(s, e, tq, 0)
    blk_q = pl.BlockSpec((None, tq, hp * HEAD_DIM), lambda i, h, j: (i, j, h))
    shp = jax.ShapeDtypeStruct((b, s, e), F32)
    return pl.pallas_call(
        body, name="sb_fwd", grid=(b, n_heads // hp, nq),
        in_specs=[q_spec, k_spec, v_spec],
        out_specs=(blk_q, blk_q), out_shape=(shp, shp),
        scratch_shapes=[pltpu.VMEM((hp, tq, HEAD_DIM), F32), pltpu.VMEM((hp, tq, 1), F32)],
        compiler_params=_params("parallel", "parallel", "arbitrary"),
    )(proj3, proj3, proj3)


def _sb_bwd(proj3, lom_total, dy, partials, dproj):
    b, s, e = dy.shape
    n_heads = e // HEAD_DIM
    hp = HEADS_PER_STEP
    tq = _tile(s, ATT_TQ)
    nq = s // tq
    inv = 1.0 / math.sqrt(HEAD_DIM)
    grid = (b, n_heads // hp, nq + 2)

    def body(q_ref, k_ref, v_ref, tot_ref, dy_ref, p_ref, dproj_ref, out_ref, r_ref,
             dqa, dka, dva, car, car2, send_sems, recv_sems, local_sem):
        i = pl.program_id(2)
        first, last = _grid_first_last(grid)

        @pl.when(first)
        def _():
            for cp in _all_to_all_copies(p_ref, r_ref, send_sems, recv_sems, local_sem):
                cp.start()

        @pl.when(i == 0)
        def _():
            dka[...] = jnp.zeros_like(dka)
            dva[...] = jnp.zeros_like(dva)

        row, col = _iota2(tq)
        before = row > col
        u_upto = (row <= col).astype(BF16)
        u_before = (row < col).astype(BF16)

        def block(j, diagonal):
            keys = pl.ds(pl.multiple_of(j * tq, tq), tq)
            heads = list(enumerate(_head_slices()))
            zs = [lax.dot_general(q_ref[:, hs], k_ref[keys, hs], NT_DIMS, preferred_element_type=F32) * inv
                  for _, hs in heads]
            das = [lax.dot_general(dy_ref[:, hs], v_ref[keys, hs], NT_DIMS, preferred_element_type=F32) for _, hs in heads]
            ezs = [jnp.exp(jnp.minimum(z, SOFTPLUS_CLAMP)) for z in zs]
            sps = [jnp.maximum(jnp.log(1.0 + ezs[hh]), zs[hh]) for hh, _ in heads]
            loms = [jnp.where(before, -sp, 0.0) if diagonal else -sp for sp in sps]
            sufs = [tot_ref[:, hh * HEAD_DIM:hh * HEAD_DIM + 1] - (_split2_dot(loms[hh], u_upto) + car[hh])
                    for hh, _ in heads]
            avs = [jnp.exp((zs[hh] - sps[hh]) + sufs[hh]) for hh, _ in heads]
            if diagonal:
                avs = [jnp.where(before, a, 0.0) for a in avs]
            dls = [avs[hh] * das[hh] for hh, _ in heads]
            prefixes = [_split2_dot(dls[hh], u_before) + car2[hh] for hh, _ in heads]
            dzs = []
            for hh, _ in heads:
                one_minus_beta = 1.0 / (1.0 + ezs[hh])
                dz = (dls[hh] * one_minus_beta - prefixes[hh] * (ezs[hh] * one_minus_beta)) * inv
                if diagonal:
                    dz = jnp.where(before, dz, 0.0)
                dzs.append(dz.astype(BF16))
            dqs = [jnp.dot(dzs[hh], k_ref[keys, hs], preferred_element_type=F32) for hh, hs in heads]
            dks = [lax.dot_general(dzs[hh], q_ref[:, hs], TN_DIMS, preferred_element_type=F32) for hh, hs in heads]
            dvs = [lax.dot_general(avs[hh].astype(BF16), dy_ref[:, hs], TN_DIMS, preferred_element_type=F32)
                   for hh, hs in heads]
            for hh, hs in heads:
                dqa[hh] += dqs[hh]
                dka[keys, hs] += dks[hh]
                dva[keys, hs] += dvs[hh]
                car[hh] += jnp.sum(loms[hh], axis=1, keepdims=True)
                car2[hh] += jnp.sum(dls[hh], axis=1, keepdims=True)

        def step(j, carry):
            block(j, False)
            return carry

        @pl.when(i < nq)
        def _():
            dqa[...] = jnp.zeros_like(dqa)
            car[...] = jnp.zeros_like(car)
            car2[...] = jnp.zeros_like(car2)
            lax.fori_loop(0, i, step, 0)
            block(i, True)
            q_rows = pl.ds(pl.multiple_of(i * tq, tq), tq)
            for hh, hs in enumerate(_head_slices()):
                out_ref[q_rows, hs] = dqa[hh].astype(BF16)

        @pl.when(i == nq)
        def _():
            out_ref[...] = dka[...].astype(BF16)

        @pl.when(i == nq + 1)
        def _():
            out_ref[...] = dva[...].astype(BF16)

        @pl.when(last)
        def _():
            for cp in _all_to_all_copies(p_ref, r_ref, send_sems, recv_sems, local_sem):
                cp.wait()

    q_spec, k_spec, v_spec = _att_specs(s, e, tq, 0)
    w = hp * HEAD_DIM
    blk_q = pl.BlockSpec((None, tq, w), lambda i, h, j: (i, jnp.minimum(j, nq - 1), h))
    return pl.pallas_call(
        body, name="sb_bwd", grid=grid,
        in_specs=[q_spec, k_spec, v_spec, blk_q, blk_q, ANY_SPEC, ANY_SPEC],
        out_specs=(_bwd_out_spec(s, e, tq, 0), ANY_SPEC),
        out_shape=(jax.ShapeDtypeStruct(dproj.shape, dproj.dtype), jax.ShapeDtypeStruct(partials.shape, partials.dtype)),
        input_output_aliases={6: 0},
        scratch_shapes=[pltpu.VMEM((hp, tq, HEAD_DIM), F32), pltpu.VMEM((s, w), F32), pltpu.VMEM((s, w), F32),
                        pltpu.VMEM((hp, tq, 1), F32), pltpu.VMEM((hp, tq, 1), F32),
                        pltpu.SemaphoreType.DMA((N_DEV - 1,)), pltpu.SemaphoreType.DMA((N_DEV - 1,)),
                        pltpu.SemaphoreType.DMA],
        compiler_params=_params("arbitrary", "arbitrary", "arbitrary"),
    )(proj3, proj3, proj3, lom_total, dy, partials, dproj)


def _dil_near_tiles(tq):
    return (DIL_PAIRS[1][0] + tq - 1) // tq + 1


def _dil_fill_bias(bias_ref, sl_ref, tq):
    row, col = _iota2(tq)
    for hh in range(HEADS_PER_STEP):
        slope = sl_ref[hh, 0:1, 0:1]
        for d in range(_dil_near_tiles(tq) + 1):
            dist = d * tq + row - col
            cnt = jnp.zeros(dist.shape, jnp.int32)
            for window, dilation in DIL_PAIRS:
                cnt = cnt + (((dist & (dilation - 1)) == 0) & (dist <= window)).astype(jnp.int32)
            bias = jnp.where(cnt == 3, math.log(3.0), jnp.where(cnt == 2, math.log(2.0), 0.0))
            bias_ref[hh, d] = jnp.where((dist >= 0) & (cnt > 0), bias - slope * dist.astype(F32), NEG)


def _dil_fwd(proj3, e, slopes):
    b, s, _ = proj3.shape
    n_heads = e // HEAD_DIM
    hp = HEADS_PER_STEP
    tq = _tile(s, ATT_TQ)
    nq = s // tq
    inv = 1.0 / math.sqrt(HEAD_DIM)
    assert s <= DIL_PAIRS[2][0]

    def body(q_ref, k_ref, v_ref, sl_ref, y_ref, lse_ref, acc_ref, m_ref, l_ref, bias_ref, p_ref):
        i = pl.program_id(2)

        @pl.when(i == 0)
        def _():
            _dil_fill_bias(bias_ref, sl_ref, tq)

        acc_ref[...] = jnp.zeros_like(acc_ref)
        m_ref[...] = jnp.full_like(m_ref, NEG)
        l_ref[...] = jnp.zeros_like(l_ref)

        def step(it, carry):
            keys = pl.ds(pl.multiple_of((i - it) * tq, tq), tq)
            heads = list(enumerate(_head_slices()))
            near = _dil_near_tiles(tq)
            tile = jnp.minimum(it, near)
            beyond = jnp.maximum(it - near, 0).astype(F32) * float(tq)
            raws = [lax.dot_general(q_ref[:, hs], k_ref[keys, hs], NT_DIMS, preferred_element_type=F32)
                    for _, hs in heads]
            for hh, _ in heads:
                shift = sl_ref[hh, 0:1, 0:1] * beyond
                for r0 in range(0, tq, ATT_STRIP):
                    rows = slice(r0, r0 + ATT_STRIP)
                    sc = (raws[hh][rows] * inv + bias_ref[hh, tile, rows, :]) - shift
                    m_old = m_ref[hh, rows]
                    m_new = jnp.maximum(m_old, jnp.max(sc, axis=1, keepdims=True))
                    p = jnp.exp(sc - m_new)
                    alpha = jnp.exp(m_old - m_new)
                    l_ref[hh, rows] = alpha * l_ref[hh, rows] + (p[:, :tq // 2] + p[:, tq // 2:])
                    acc_ref[hh, rows] = alpha * acc_ref[hh, rows]
                    p_ref[hh, rows] = p.astype(BF16)
                    m_ref[hh, rows] = m_new
            pvs = [jnp.dot(p_ref[hh], v_ref[keys, hs], preferred_element_type=F32) for hh, hs in heads]
            for hh, _ in heads:
                acc_ref[hh] += pvs[hh]
            return carry

        lax.fori_loop(0, i + 1, step, 0)
        for hh, hs in enumerate(_head_slices()):
            l = jnp.sum(l_ref[hh], axis=1, keepdims=True)
            y_ref[:, hs] = acc_ref[hh] / l
            lse_ref[:, hs] = jnp.broadcast_to(m_ref[hh] + jnp.log(l), (tq, HEAD_DIM))

    q_spec, k_spec, v_spec = _att_specs(s, e, tq, 4 * n_heads)
    blk_q = pl.BlockSpec((None, tq, hp * HEAD_DIM), lambda i, h, j: (i, j, h))
    shp = jax.ShapeDtypeStruct((b, s, e), F32)
    return pl.pallas_call(
        body, name="dil_fwd", grid=(b, n_heads // hp, nq),
        in_specs=[q_spec, k_spec, v_spec, pl.BlockSpec((hp, 8, HEAD_DIM), lambda i, h, j: (h, 0, 0))],
        out_specs=(blk_q, blk_q), out_shape=(shp, shp),
        scratch_shapes=[pltpu.VMEM((hp, tq, HEAD_DIM), F32), pltpu.VMEM((hp, tq, 1), F32), pltpu.VMEM((hp, tq, tq // 2), F32),
                        pltpu.VMEM((hp, _dil_near_tiles(tq) + 1, tq, tq), F32), pltpu.VMEM((hp, tq, tq), BF16)],
        compiler_params=_params("parallel", "parallel", "arbitrary"),
    )(proj3, proj3, proj3, slopes)


def _dil_bwd(proj3, y, lse, dy, slopes, dproj):
    b, s, e = y.shape
    n_heads = e // HEAD_DIM
    hp = HEADS_PER_STEP
    tq = _tile(s, ATT_TQ)
    nq = s // tq
    inv = 1.0 / math.sqrt(HEAD_DIM)

    def body(q_ref, k_ref, v_ref, sl_ref, y_ref, lse_ref, dy_ref, dproj_ref, out_ref, dqa, dka, dva, bias_ref,
             p_ref, ds_ref):
        i = pl.program_id(2)

        @pl.when(i == 0)
        def _():
            dka[...] = jnp.zeros_like(dka)
            dva[...] = jnp.zeros_like(dva)
            _dil_fill_bias(bias_ref, sl_ref, tq)

        delta = [jnp.sum(dy_ref[:, hs].astype(F32) * y_ref[:, hs], axis=1, keepdims=True) for hs in _head_slices()]

        def step(it, carry):
            keys = pl.ds(pl.multiple_of((i - it) * tq, tq), tq)
            heads = list(enumerate(_head_slices()))
            near = _dil_near_tiles(tq)
            tile = jnp.minimum(it, near)
            beyond = jnp.maximum(it - near, 0).astype(F32) * float(tq)
            raws = [lax.dot_general(q_ref[:, hs], k_ref[keys, hs], NT_DIMS, preferred_element_type=F32)
                    for _, hs in heads]
            dps = [lax.dot_general(dy_ref[:, hs], v_ref[keys, hs], NT_DIMS, preferred_element_type=F32) for _, hs in heads]
            for hh, _ in heads:
                shift = sl_ref[hh, 0:1, 0:1] * beyond
                for r0 in range(0, tq, ATT_STRIP):
                    rows = slice(r0, r0 + ATT_STRIP)
                    sc = (raws[hh][rows] * inv + bias_ref[hh, tile, rows, :]) - shift
                    p = jnp.exp(sc - lse_ref[rows, hh * HEAD_DIM:hh * HEAD_DIM + 1])
                    p_ref[hh, rows] = p.astype(BF16)
                    ds_ref[hh, rows] = ((p * (dps[hh][rows] - delta[hh][rows])) * inv).astype(BF16)
            dqs = [jnp.dot(ds_ref[hh], k_ref[keys, hs], preferred_element_type=F32) for hh, hs in heads]
            dks = [lax.dot_general(ds_ref[hh], q_ref[:, hs], TN_DIMS, preferred_element_type=F32) for hh, hs in heads]
            dvs = [lax.dot_general(p_ref[hh], dy_ref[:, hs], TN_DIMS, preferred_element_type=F32) for hh, hs in heads]
            for hh, hs in heads:
                dqa[hh] += dqs[hh]
                dka[keys, hs] += dks[hh]
                dva[keys, hs] += dvs[hh]
            return carry

        @pl.when(i < nq)
        def _():
            dqa[...] = jnp.zeros_like(dqa)
            lax.fori_loop(0, i + 1, step, 0)
            q_rows = pl.ds(pl.multiple_of(i * tq, tq), tq)
            for hh, hs in enumerate(_head_slices()):
                out_ref[q_rows, hs] = dqa[hh].astype(BF16)

        @pl.when(i == nq)
        def _():
            out_ref[...] = dka[...].astype(BF16)

        @pl.when(i == nq + 1)
        def _():
            out_ref[...] = dva[...].astype(BF16)

    q_spec, k_spec, v_spec = _att_specs(s, e, tq, 4 * n_heads)
    w = hp * HEAD_DIM
    blk_q = pl.BlockSpec((None, tq, w), lambda i, h, j: (i, jnp.minimum(j, nq - 1), h))
    return pl.pallas_call(
        body, name="dil_bwd", grid=(b, n_heads // hp, nq + 2),
        in_specs=[q_spec, k_spec, v_spec, pl.BlockSpec((hp, 8, HEAD_DIM), lambda i, h, j: (h, 0, 0)),
                  blk_q, blk_q, blk_q, ANY_SPEC],
        out_specs=_bwd_out_spec(s, e, tq, 4), out_shape=jax.ShapeDtypeStruct(dproj.shape, dproj.dtype),
        input_output_aliases={7: 0},
        scratch_shapes=[pltpu.VMEM((hp, tq, HEAD_DIM), F32), pltpu.VMEM((s, w), F32), pltpu.VMEM((s, w), F32),
                        pltpu.VMEM((hp, _dil_near_tiles(tq) + 1, tq, tq), F32),
                        pltpu.VMEM((hp, tq, tq), BF16), pltpu.VMEM((hp, tq, tq), BF16)],
        compiler_params=_params("parallel", "parallel", "arbitrary"),
    )(proj3, proj3, proj3, slopes, y, lse, dy, dproj)


def kernel(x, c, w_ada, b_ada, g_norm, w_in, g_sb, g_dil, w_out, g_final, loss_target, m_w_ada, m_b_ada, m_g_norm, m_w_in, m_g_sb, m_g_dil, m_w_out, m_g_final, v_w_ada, v_b_ada, v_g_norm, v_w_in, v_g_sb, v_g_dil, v_w_out, v_g_final):
    b, s, d = x.shape
    t = b * s
    e = w_in.shape[2]
    n_heads = e // HEAD_DIM
    na = w_ada.shape[2]
    r_out = w_out.shape[1]
    assert g_sb.shape[1] == e and g_dil.shape[1] == e and N_DEV * r_out == 2 * e and N_DEV * na == 3 * d
    assert b <= SMALL_ROWS and 3 * b + 3 <= 2 * SMALL_ROWS
    ix, iy, ic = _mesh_pos()
    me = 4 * ix + 2 * iy + ic

    c_all = _allgather_rows(jnp.pad(c, ((0, SMALL_ROWS - b), (0, 0))), "ag_c")
    b_own = lax.dynamic_slice(b_ada, (0, me * na), (1, na))
    mod_cols = _ada_fwd(c_all, w_ada[0], b_own)
    mod_all = _allgather_rows(mod_cols, "ag_mod").reshape(N_DEV, N_DEV, SMALL_ROWS, na)
    mod_own = lax.dynamic_slice(mod_all, (0, me, 0, 0), (N_DEV, 1, b, na))[:, 0]
    mod = mod_own.transpose(1, 0, 2).reshape(b, 1, 3 * d)
    shift, scale, gate = mod[:, :, :d], mod[:, :, d:2 * d], mod[:, :, 2 * d:]

    w_own = w_in[0].astype(BF16)

    h = _norm_mod(x, g_norm, scale, shift).reshape(t, d)
    proj, w_in3, w_out3 = _proj_with_allgather(
        h, w_own, jnp.stack(_unit_ids()).astype(jnp.int32), w_out[0].astype(BF16))
    w_out1 = w_out3.reshape(1, N_DEV * r_out, d)
    proj3 = proj.reshape(b, s, N_DEV * e)
    slopes = jnp.exp2(-ALIBI_MAX_BIAS * jnp.arange(1, n_heads + 1, dtype=F32) / n_heads)
    slopes = jnp.broadcast_to(slopes[:, None, None], (n_heads, 8, HEAD_DIM))
    y_sb, lom_total = _sb_fwd(proj3, e)
    y_dl, lse = _dil_fwd(proj3, e, slopes)
    yg = _gate_fwd(y_sb.reshape(t, e), y_dl.reshape(t, e), proj, g_sb, g_dil)
    out = _mm_nn(yg, w_out1, F32, "mm_out").reshape(b, s, d)
    loss_p, dx2, d_out, dgate, gg_final = _final_fwd_bwd(x, out, gate, g_final.reshape(1, d), loss_target)

    d_out2 = d_out.reshape(t, d)
    dyg = _mm_nt(d_out2, w_out1, BF16, "mm_dy")
    gw_out_p = _mm_tn(yg, d_out2, 1, BF16, "mm_gw_out").reshape(N_DEV, r_out, d)
    dy_sb, dy_dl, dproj, gg_sb, gg_dl = _gate_bwd(dyg, y_sb.reshape(t, e), y_dl.reshape(t, e), proj, g_sb, g_dil)
    dproj, recv_out = _sb_bwd(proj3, lom_total, dy_sb.reshape(b, s, e), gw_out_p, dproj.reshape(b, s, N_DEV * e))
    dproj = _dil_bwd(proj3, y_dl, lse, dy_dl.reshape(b, s, e), slopes, dproj).reshape(t, N_DEV * e)
    chips4 = [(ix, iy)] + _other_chips()
    to_sibling_core = jnp.stack([4 * px + 2 * py + (1 - ic) for px, py in chips4]).astype(jnp.int32)
    to_my_core = jnp.stack([4 * px + 2 * py + ic for px, py in chips4]).astype(jnp.int32)
    gw_in_sibs = _mm_tn_groups(to_sibling_core, h, dproj, N_DEV, "mm_gw_in_sibling")
    gw_in_mine, gw_in_sib = _mm_tn_groups(to_my_core, h, dproj, N_DEV, "mm_gw_in_mine", to_sibling=gw_in_sibs)
    gw_in_send = _chip_presum(gw_in_mine, gw_in_sib)
    dh, gw_in_recv = _mm_nt_with_chip_exchange(dproj, w_in3, F32, gw_in_send, "mm_dh")
    dh = dh.reshape(b, s, d)
    grad_x, dshift, dscale, gg_norm = _norm_bwd(x, dh, dx2, scale, g_norm)

    dmod = jnp.concatenate([dshift, dscale, dgate], axis=1).reshape(3 * b, d)
    pkg = jnp.concatenate([dmod, gg_norm, gg_final, jnp.concatenate([gg_sb, gg_dl], axis=1),
                           jnp.zeros((2 * SMALL_ROWS - 3 * b - 3, d), F32)], axis=0)
    pkg_all = _allgather_rows(pkg, "ag_small_grads").reshape(N_DEV, 2 * SMALL_ROWS, d)
    dmod_all = pkg_all[:, :3 * b].reshape(N_DEV * b, 3 * d)
    dmod_cols = lax.dynamic_slice(dmod_all, (0, me * na), (N_DEV * b, na))
    c_rows = c_all.reshape(N_DEV, SMALL_ROWS, d)[:, :b].reshape(N_DEV * b, d)
    g_w_ada, d_w_ada, nm_w_ada, nv_w_ada = _ada_bwd_adam(c_rows, dmod_cols, w_ada[0], m_w_ada[0], v_w_ada[0])

    def pack(b_ada_like, g_norm_like, g_sb_like, g_dil_like, g_final_like):
        return jnp.concatenate([b_ada_like.reshape(3, d), g_norm_like.reshape(1, d), g_final_like.reshape(1, d),
                                jnp.concatenate([g_sb_like, g_dil_like], axis=1).reshape(1, d),
                                jnp.zeros((2, d), F32)], axis=0)

    small = _small_adam(pkg_all, 3 * b, pack(b_ada, g_norm, g_sb, g_dil, g_final),
                        pack(m_b_ada, m_g_norm, m_g_sb, m_g_dil, m_g_final),
                        pack(v_b_ada, v_g_norm, v_g_sb, v_g_dil, v_g_final))

    def unpack(p):
        return (p[0:3].reshape(1, 3 * d), p[3:4], p[5:6, :e], p[5:6, e:], p[4])

    sm_g, sm_d, sm_m, sm_v = (unpack(p) for p in small)

    g_w_in, d_w_in, nm_w_in, nv_w_in = _adam_from_chip_sums(
        gw_in_mine, gw_in_sib, gw_in_recv, w_in[0], m_w_in[0], v_w_in[0], "adam_w_in")
    g_w_out, d_w_out, nm_w_out, nv_w_out = _adam_from_partials(recv_out, w_out[0], m_w_out[0], v_w_out[0], "adam_w_out")

    loss = lax.psum(loss_p[0, 0], ("x", "y", "c"))

    def weights(ada, small_parts, w_in_part, w_out_part):
        b_ada_p, g_norm_p, g_sb_p, g_dil_p, g_final_p = small_parts
        return (ada[None], b_ada_p, g_norm_p, w_in_part[None], g_sb_p, g_dil_p, w_out_part[None], g_final_p)

    return (loss, grad_x,
            *weights(g_w_ada, sm_g, g_w_in, g_w_out),
            *weights(d_w_ada, sm_d, d_w_in, d_w_out),
            *weights(nm_w_ada, sm_m, nm_w_in, nm_w_out),
            *weights(nv_w_ada, sm_v, nv_w_in, nv_w_out))
```

```python
import functools
import math

import jax
import jax.numpy as jnp
from jax import lax
from jax.experimental import pallas as pl
from jax.experimental.pallas import tpu as pltpu

F32 = jnp.float32
BF16 = jnp.bfloat16
MESH = pl.DeviceIdType.MESH

N_DEV = 8
HEAD_DIM = 128
EPS = 1e-6
ALIBI_MAX_BIAS = 8.0
DIL_PAIRS = ((128, 1), (512, 4), (2048, 16))
DIL_STEPS = 128
NEG = -1e30

ADAM_LR = 0.001
ADAM_B1 = 0.9
ADAM_B2 = 0.999
ADAM_EPS = 1e-08
ADAM_WD = 0.01
ADAM_STEP = 10

VMEM_LIMIT_BYTES = 56 * 1024 * 1024
SMALL_ROWS = 8

NT_DIMS = (((1,), (1,)), ((), ()))
TN_DIMS = (((0,), (0,)), ((), ()))


def _params(*semantics):
    return pltpu.CompilerParams(dimension_semantics=semantics, vmem_limit_bytes=VMEM_LIMIT_BYTES)


def _tile(n, want):
    t = min(n, want)
    assert n % t == 0, (n, want)
    return t


def _mesh_pos():
    return lax.axis_index("x"), lax.axis_index("y"), lax.axis_index("c")


def _allgather_rows(x_shard, name):
    m_per, n = x_shard.shape

    def body(x_ref, out_ref, send_sems, recv_sems, local_sem):
        x, y, c = _mesh_pos()
        me, sibling = (x, y, c), (x, y, 1 - c)
        chips = [(1 - x, y), (x, 1 - y), (1 - x, 1 - y)]

        def rows(px, py, pc):
            return out_ref.at[pl.ds((4 * px + 2 * py + pc) * m_per, m_per), :]

        def copy(k, block, to, src=None):
            return pltpu.make_async_remote_copy(
                src_ref=rows(*block) if src is None else src, dst_ref=rows(*block),
                send_sem=send_sems.at[k], recv_sem=recv_sems.at[k], device_id=to, device_id_type=MESH)

        mine = pltpu.make_async_copy(x_ref, rows(*me), local_sem)
        mine.start()
        first = [copy(0, me, sibling, src=x_ref)]
        first += [copy(1 + j, me, (*chip, c), src=x_ref) for j, chip in enumerate(chips)]
        for cp in first:
            cp.start()
        passed = [copy(4 + j, (*chip, c), sibling) for j, chip in enumerate(chips)]
        for j, chip in enumerate(chips):
            copy(1 + j, (*chip, c), me).wait_recv()
            passed[j].start()
        copy(0, sibling, me).wait_recv()
        for j, chip in enumerate(chips):
            copy(4 + j, (*chip, 1 - c), me).wait_recv()
        for cp in first + passed:
            cp.wait_send()
        mine.wait()

    return pl.pallas_call(
        body, name=name,
        out_shape=jax.ShapeDtypeStruct((N_DEV * m_per, n), x_shard.dtype),
        in_specs=[pl.BlockSpec(memory_space=pltpu.VMEM)],
        out_specs=pl.BlockSpec(memory_space=pltpu.VMEM),
        scratch_shapes=[pltpu.SemaphoreType.DMA((7,)), pltpu.SemaphoreType.DMA((7,)), pltpu.SemaphoreType.DMA],
    )(x_shard)


def _two_level_gather(w_ref, out_ref, send_sems, recv_sems, local_sem):
    x, y, c = _mesh_pos()
    me, sibling = (x, y, c), (x, y, 1 - c)
    chips = [(1 - x, y), (x, 1 - y), (1 - x, 1 - y)]

    def copy(k, block, to, src=None):
        dst = out_ref.at[4 * block[0] + 2 * block[1] + block[2]]
        return pltpu.make_async_remote_copy(
            src_ref=dst if src is None else src, dst_ref=dst,
            send_sem=send_sems.at[k], recv_sem=recv_sems.at[k], device_id=to, device_id_type=MESH)

    mine = pltpu.make_async_copy(w_ref, out_ref.at[4 * x + 2 * y + c], local_sem)
    to_sibling = copy(0, me, sibling, src=w_ref)
    to_chips = [copy(1 + j, me, (*chip, c), src=w_ref) for j, chip in enumerate(chips)]
    passed = [copy(4 + j, (*chip, c), sibling) for j, chip in enumerate(chips)]

    def early():
        mine.start()
        to_sibling.start()

    def ici():
        for cp in to_chips:
            cp.start()

    def finish():
        for j, chip in enumerate(chips):
            copy(1 + j, (*chip, c), me).wait_recv()
            passed[j].start()
        copy(0, sibling, me).wait_recv()
        for j, chip in enumerate(chips):
            copy(4 + j, (*chip, 1 - c), me).wait_recv()
        for cp in [to_sibling] + to_chips + passed:
            cp.wait_send()
        mine.wait()

    return early, ici, finish


PROJ_HALVES = 2


def _unit_schedule():
    assert PROJ_HALVES == 2
    sched = [("own", None, hf) for hf in range(PROJ_HALVES)] + [("sib", None, hf) for hf in range(PROJ_HALVES)]
    for rnd in ([(0, 0), (1, 1)], [(2, 0), (2, 1)], [(0, 1), (1, 0)]):
        sched += [("direct", j, hf) for j, hf in rnd] + [("fwd", j, hf) for j, hf in rnd]
    return sched


def _unit_ids():
    x, y, c = _mesh_pos()
    chips = _other_chips()
    ids = []
    for kind, j, hf in _unit_schedule():
        px, py = (x, y) if j is None else chips[j]
        pc = c if kind in ("own", "direct") else 1 - c
        ids.append(PROJ_HALVES * (4 * px + 2 * py + pc) + hf)
    return ids


def _proj_with_allgather(h, w_own, order, w_later):
    t, d = h.shape
    nh = PROJ_HALVES
    u = w_own.shape[1] // nh
    sched = _unit_schedule()
    n_units = len(sched)
    pos = {entry: p for p, entry in enumerate(sched)}
    tm = _tile(t, MM_TM)
    m_tiles = t // tm
    prep_m = max(m_tiles - 2, 0)
    grid = (n_units, m_tiles)

    later_ici_pos = pos[("direct", 2, 0)]

    def body(order_ref, h_ref, wown_ref, later_ref, proj_ref, w3_ref, later3_ref, bbuf, bsems, send_sems, recv_sems,
             local_sems, later_send_sems, later_recv_sems, later_local_sem):
        n, m = pl.program_id(0), pl.program_id(1)
        x, y, c = _mesh_pos()
        chips = _other_chips()
        sibling = (x, y, 1 - c)
        later_early, later_ici, later_finish = _two_level_gather(
            later_ref, later3_ref, later_send_sems, later_recv_sems, later_local_sem)

        def unit(p):
            return w3_ref.at[order_ref[p]]

        def arrival(p):
            return pltpu.make_async_remote_copy(
                src_ref=unit(p), dst_ref=unit(p), send_sem=send_sems.at[0], recv_sem=recv_sems.at[p - nh],
                device_id=sibling, device_id_type=MESH)

        def send(k, src, p_here, p_there, to):
            return pltpu.make_async_remote_copy(
                src_ref=src, dst_ref=unit(p_here), send_sem=send_sems.at[k], recv_sem=recv_sems.at[p_there - nh],
                device_id=to, device_id_type=MESH)

        copies = []

        def out(src, p_here, p_there, to):
            copies.append(send(len(copies), src, p_here, p_there, to))
            return copies[-1]

        def own_unit(hf):
            return wown_ref.at[:, pl.ds(hf * u, u)]

        def own(hf, p_there, to):
            return out(own_unit(hf), pos[("own", None, hf)], p_there, to)

        x_nbr, y_nbr = (*chips[0], c), (*chips[1], c)
        first = [own(hf, pos[("sib", None, hf)], sibling) for hf in range(nh)]
        first += [own(0, pos[("direct", 0, 0)], x_nbr), own(1, pos[("direct", 1, 1)], y_nbr)]
        after = {}
        for kind, j, hf in sched:
            if kind == "direct":
                p = pos[(kind, j, hf)]
                after[p] = [out(unit(p), p, pos[("fwd", j, hf)], sibling)]
        p = pos[("direct", 0, 0)]
        after[p] += [out(unit(p), p, pos[("direct", 2, 0)], y_nbr), own(0, pos[("direct", 1, 0)], y_nbr)]
        p = pos[("direct", 1, 1)]
        after[p] += [out(unit(p), p, pos[("direct", 2, 1)], x_nbr), own(1, pos[("direct", 0, 1)], x_nbr)]
        locals_ = [pltpu.make_async_copy(own_unit(hf), unit(pos[("own", None, hf)]), local_sems.at[hf])
                   for hf in range(nh)]

        def fetch(p):
            src = own_unit(sched[p][2]) if sched[p][0] == "own" else unit(p)
            return pltpu.make_async_copy(src, bbuf.at[p % 2], bsems.at[p % 2])

        @pl.when((n == 0) & (m == 0))
        def _():
            for cp in locals_ + first:
                cp.start()
            fetch(0).start()
            later_early()

        for p in range(n_units):
            @pl.when((n == p) & (m == 0))
            def _(p=p):
                fetch(p).wait()

            if p + 1 < n_units:
                @pl.when((n == p) & (m == prep_m))
                def _(p=p):
                    if sched[p + 1][0] != "own":
                        arrival(p + 1).wait_recv()
                    for cp in after.get(p + 1, []):
                        cp.start()
                    fetch(p + 1).start()
                    if p + 1 == later_ici_pos:
                        later_ici()

        proj_ref[...] = jnp.dot(h_ref[...], bbuf[n % 2], preferred_element_type=F32).astype(proj_ref.dtype)

        @pl.when((n == n_units - 1) & (m == m_tiles - 1))
        def _():
            for cp in copies:
                cp.wait_send()
            for cp in locals_:
                cp.wait()
            later_finish()

    n_out = len(sched) - nh
    return pl.pallas_call(
        body, name="mm_proj_allgather",
        grid_spec=pltpu.PrefetchScalarGridSpec(
            num_scalar_prefetch=1, grid=grid,
            in_specs=[pl.BlockSpec((tm, d), lambda n, m, order: (m, 0)), ANY_SPEC, ANY_SPEC],
            out_specs=(pl.BlockSpec((tm, u), lambda n, m, order: (m, order[n])), ANY_SPEC, ANY_SPEC),
            scratch_shapes=[pltpu.VMEM((2, d, u), h.dtype), pltpu.SemaphoreType.DMA((2,)),
                            pltpu.SemaphoreType.DMA((n_out,)), pltpu.SemaphoreType.DMA((n_out,)),
                            pltpu.SemaphoreType.DMA((nh,)),
                            pltpu.SemaphoreType.DMA((N_DEV - 1,)), pltpu.SemaphoreType.DMA((N_DEV - 1,)),
                            pltpu.SemaphoreType.DMA]),
        out_shape=(jax.ShapeDtypeStruct((t, n_units * u), h.dtype), jax.ShapeDtypeStruct((n_units, d, u), h.dtype),
                   jax.ShapeDtypeStruct((N_DEV,) + w_later.shape, w_later.dtype)),
        compiler_params=_params("arbitrary", "arbitrary"),
    )(order, h, w_own, w_later)


ANY_SPEC = pl.BlockSpec(memory_space=pl.ANY)


def _grid_first_last(grid):
    ids = [pl.program_id(a) for a in range(len(grid))]
    first = functools.reduce(lambda p, q: p & q, [i == 0 for i in ids])
    last = functools.reduce(lambda p, q: p & q, [i == n - 1 for i, n in zip(ids, grid)])
    return first, last


def _all_to_all_copies(src, dst, send_sems, recv_sems, local_sem):
    x, y, c = _mesh_pos()
    my = 4 * x + 2 * y + c
    copies = [pltpu.make_async_copy(src.at[my], dst.at[my], local_sem)]
    for d in range(1, N_DEV):
        px = 1 - x if d & 4 else x
        py = 1 - y if d & 2 else y
        pc = 1 - c if d & 1 else c
        copies.append(pltpu.make_async_remote_copy(
            src_ref=src.at[4 * px + 2 * py + pc], dst_ref=dst.at[my],
            send_sem=send_sems.at[d - 1], recv_sem=recv_sems.at[d - 1],
            device_id=(px, py, pc), device_id_type=MESH))
    return copies


def _other_chips():
    x, y, _ = _mesh_pos()
    return [(1 - x, y), (x, 1 - y), (1 - x, 1 - y)]


def _same_core_copies(src, dst, send_sems, recv_sems):
    c = lax.axis_index("c")
    return [pltpu.make_async_remote_copy(
        src_ref=src.at[j], dst_ref=dst.at[j], send_sem=send_sems.at[j], recv_sem=recv_sems.at[j],
        device_id=(*chip, c), device_id_type=MESH) for j, chip in enumerate(_other_chips())]


def _sibling_copies(src, dst, send_sems, recv_sems):
    x, y, c = _mesh_pos()
    return [pltpu.make_async_remote_copy(
        src_ref=src.at[j], dst_ref=dst.at[j], send_sem=send_sems.at[j], recv_sem=recv_sems.at[j],
        device_id=(x, y, 1 - c), device_id_type=MESH) for j in range(src.shape[0])]


def _chip_presum(mine, from_sibling):
    _, r, cdim = mine.shape
    tr = _tile(r, 256)

    def body(p_ref, s_ref, o_ref):
        o_ref[...] = (p_ref[...].astype(F32) + s_ref[...].astype(F32)).astype(o_ref.dtype)

    blk = pl.BlockSpec((None, tr, cdim), lambda j, i: (1 + j, i, 0))
    return pl.pallas_call(
        body, name="chip_presum", grid=(3, r // tr), in_specs=[blk, blk],
        out_specs=pl.BlockSpec((None, tr, cdim), lambda j, i: (j, i, 0)),
        out_shape=jax.ShapeDtypeStruct((3, r, cdim), mine.dtype),
        compiler_params=_params("parallel", "parallel"),
    )(mine, from_sibling)


def _mm_call(a, b, dims, nk, grid, a_spec, b_spec, o_spec, out_shape, acc_shape, name):
    def body(a_ref, b_ref, o_ref, acc_ref):
        k = pl.program_id(2)

        @pl.when(k == 0)
        def _():
            acc_ref[...] = jnp.zeros_like(acc_ref)

        acc_ref[...] += lax.dot_general(a_ref[...], b_ref[...], dims, preferred_element_type=F32)

        @pl.when(k == nk - 1)
        def _():
            o_ref[...] = acc_ref[...].astype(o_ref.dtype)

    return pl.pallas_call(
        body, name=name, grid=grid, in_specs=[a_spec, b_spec], out_specs=o_spec, out_shape=out_shape,
        scratch_shapes=[pltpu.VMEM(acc_shape, F32)],
        compiler_params=_params("parallel", "parallel", "arbitrary"),
    )(a, b)


MM_TM, MM_TN, MM_TK = 1024, 2048, 1024


def _mm_nn(a, b3, out_dtype, name):
    m, kk = a.shape
    g, _, nb = b3.shape
    tm, tn, tk = _tile(m, MM_TM), _tile(nb, MM_TN), _tile(kk, MM_TK)
    npb = nb // tn
    return _mm_call(
        a, b3, (((1,), (0,)), ((), ())), kk // tk, (m // tm, g * npb, kk // tk),
        pl.BlockSpec((tm, tk), lambda i, j, k: (i, k)),
        pl.BlockSpec((None, tk, tn), lambda i, j, k: (j // npb, k, j % npb)),
        pl.BlockSpec((tm, tn), lambda i, j, k: (i, j)),
        jax.ShapeDtypeStruct((m, g * nb), out_dtype), (tm, tn), name)


def _mm_nt(a, b3, out_dtype, name):
    m, kk = a.shape
    g, n, kb = b3.shape
    tm, tn, tk = _tile(m, MM_TM), _tile(n, MM_TN), _tile(kb, MM_TK)
    kpb = kb // tk
    return _mm_call(
        a, b3, NT_DIMS, kk // tk, (m // tm, n // tn, kk // tk),
        pl.BlockSpec((tm, tk), lambda i, j, k: (i, k)),
        pl.BlockSpec((None, tn, tk), lambda i, j, k: (k // kpb, j, k % kpb)),
        pl.BlockSpec((tm, tn), lambda i, j, k: (i, j)),
        jax.ShapeDtypeStruct((m, n), out_dtype), (tm, tn), name)


def _mm_nt_with_chip_exchange(a, b3, out_dtype, send3, name):
    m, kk = a.shape
    g, n, kb = b3.shape
    tm, tn, tk = _tile(m, MM_TM), _tile(n, MM_TN), _tile(kb, MM_TK)
    kpb = kb // tk
    grid = (m // tm, n // tn, kk // tk)

    def body(a_ref, b_ref, s_ref, o_ref, r_ref, acc_ref, send_sems, recv_sems):
        first, last = _grid_first_last(grid)
        k = pl.program_id(2)

        @pl.when(first)
        def _():
            for cp in _same_core_copies(s_ref, r_ref, send_sems, recv_sems):
                cp.start()

        @pl.when(k == 0)
        def _():
            acc_ref[...] = jnp.zeros_like(acc_ref)

        acc_ref[...] += lax.dot_general(a_ref[...], b_ref[...], NT_DIMS, preferred_element_type=F32)

        @pl.when(k == grid[2] - 1)
        def _():
            o_ref[...] = acc_ref[...].astype(o_ref.dtype)

        @pl.when(last)
        def _():
            for cp in _same_core_copies(s_ref, r_ref, send_sems, recv_sems):
                cp.wait()

    return pl.pallas_call(
        body, name=name, grid=grid,
        in_specs=[pl.BlockSpec((tm, tk), lambda i, j, k: (i, k)),
                  pl.BlockSpec((None, tn, tk), lambda i, j, k: (k // kpb, j, k % kpb)), ANY_SPEC],
        out_specs=(pl.BlockSpec((tm, tn), lambda i, j, k: (i, j)), ANY_SPEC),
        out_shape=(jax.ShapeDtypeStruct((m, n), out_dtype), jax.ShapeDtypeStruct(send3.shape, send3.dtype)),
        scratch_shapes=[pltpu.VMEM((tm, tn), F32), pltpu.SemaphoreType.DMA((3,)), pltpu.SemaphoreType.DMA((3,))],
        compiler_params=_params("arbitrary", "arbitrary", "arbitrary"),
    )(a, b3, send3)


def _mm_tn_groups(groups, a, b, n_groups, name, to_sibling=None):
    t, m = a.shape
    nb = b.shape[1] // n_groups
    ng = groups.shape[0]
    tm, tk = _tile(m, MM_TM), _tile(t, MM_TK)
    grid = (m // tm, ng, t // tk)
    carry = to_sibling is not None

    def body(groups_ref, a_ref, b_ref, *rest):
        if carry:
            s_ref, o_ref, r_ref, acc_ref, send_sems, recv_sems = rest
        else:
            o_ref, acc_ref = rest
        k = pl.program_id(2)
        first, last = _grid_first_last(grid)

        if carry:
            @pl.when(first)
            def _():
                for cp in _sibling_copies(s_ref, r_ref, send_sems, recv_sems):
                    cp.start()

        @pl.when(k == 0)
        def _():
            acc_ref[...] = jnp.zeros_like(acc_ref)

        acc_ref[...] += lax.dot_general(a_ref[...], b_ref[...], TN_DIMS, preferred_element_type=F32)

        @pl.when(k == grid[2] - 1)
        def _():
            o_ref[...] = acc_ref[...].astype(o_ref.dtype)

        if carry:
            @pl.when(last)
            def _():
                for cp in _sibling_copies(s_ref, r_ref, send_sems, recv_sems):
                    cp.wait()

    shp = jax.ShapeDtypeStruct((ng, m, nb), b.dtype)
    o_spec = pl.BlockSpec((None, tm, nb), lambda i, j, k, grp: (j, i, 0))
    return pl.pallas_call(
        body, name=name,
        grid_spec=pltpu.PrefetchScalarGridSpec(
            num_scalar_prefetch=1, grid=grid,
            in_specs=[pl.BlockSpec((tk, tm), lambda i, j, k, grp: (k, i)),
                      pl.BlockSpec((tk, nb), lambda i, j, k, grp: (k, grp[j]))] + ([ANY_SPEC] if carry else []),
            out_specs=(o_spec, ANY_SPEC) if carry else o_spec,
            scratch_shapes=[pltpu.VMEM((tm, nb), F32)] + (
                [pltpu.SemaphoreType.DMA((ng,)), pltpu.SemaphoreType.DMA((ng,))] if carry else [])),
        out_shape=(shp, shp) if carry else shp,
        compiler_params=_params("arbitrary", "arbitrary", "arbitrary"),
    )(*((groups, a, b, to_sibling) if carry else (groups, a, b)))


def _mm_tn(a, b, g, out_dtype, name):
    t, m = a.shape
    nb = b.shape[1] // g
    tm, tn, tk = _tile(m, MM_TM), _tile(nb, MM_TN), _tile(t, MM_TK)
    npb = nb // tn
    return _mm_call(
        a, b, TN_DIMS, t // tk, (m // tm, g * npb, t // tk),
        pl.BlockSpec((tk, tm), lambda i, j, k: (k, i)),
        pl.BlockSpec((tk, tn), lambda i, j, k: (k, j)),
        pl.BlockSpec((None, tm, tn), lambda i, j, k: (j // npb, i, j % npb)),
        jax.ShapeDtypeStruct((g, m, nb), out_dtype), (tm, tn), name)


def _silu(z):
    return z * jax.nn.sigmoid(z)


def _ada_fwd(c_all, w_shard, b_own):
    r, d = c_all.shape
    na = w_shard.shape[1]
    tk = _tile(d, 512)
    nk = d // tk

    def body(c_ref, w_ref, b_ref, o_ref):
        k = pl.program_id(0)

        @pl.when(k == 0)
        def _():
            o_ref[...] = jnp.zeros_like(o_ref) + b_ref[...]

        cs = _silu(c_ref[...]).astype(BF16)
        o_ref[...] += jnp.dot(cs, w_ref[...].astype(BF16), preferred_element_type=F32)

    return pl.pallas_call(
        body, name="ada_fwd", grid=(nk,),
        in_specs=[pl.BlockSpec((r, tk), lambda k: (0, k)), pl.BlockSpec((tk, na), lambda k: (k, 0)),
                  pl.BlockSpec((1, na), lambda k: (0, 0))],
        out_specs=pl.BlockSpec((r, na), lambda k: (0, 0)),
        out_shape=jax.ShapeDtypeStruct((r, na), F32),
        compiler_params=_params("arbitrary"),
    )(c_all, w_shard, b_own)


def _adam(w, g, m, v):
    nm = ADAM_B1 * m + (1.0 - ADAM_B1) * g
    nv = ADAM_B2 * v + (1.0 - ADAM_B2) * (g * g)
    m_hat = nm / (1.0 - ADAM_B1 ** ADAM_STEP)
    v_hat = nv / (1.0 - ADAM_B2 ** ADAM_STEP)
    delta = -ADAM_LR * (m_hat / (jnp.sqrt(v_hat) + ADAM_EPS) + ADAM_WD * w)
    return delta, nm, nv


def _ada_bwd_adam(c_rows, dmod_cols, w, m, v):
    bg, d = c_rows.shape
    na = w.shape[1]
    tr = _tile(d, 256)

    def body(c_ref, dm_ref, w_ref, m_ref, v_ref, g_ref, d_ref, nm_ref, nv_ref):
        cs = _silu(c_ref[...]).astype(BF16)
        g = lax.dot_general(cs, dm_ref[...].astype(BF16), TN_DIMS, preferred_element_type=F32)
        delta, nm, nv = _adam(w_ref[...], g, m_ref[...], v_ref[...])
        g_ref[...] = g
        d_ref[...] = delta
        nm_ref[...] = nm
        nv_ref[...] = nv

    blk = pl.BlockSpec((tr, na), lambda i: (i, 0))
    shp = jax.ShapeDtypeStruct((d, na), F32)
    return pl.pallas_call(
        body, name="ada_bwd_adam", grid=(d // tr,),
        in_specs=[pl.BlockSpec((bg, tr), lambda i: (0, i)), pl.BlockSpec((bg, na), lambda i: (0, 0)), blk, blk, blk],
        out_specs=(blk, blk, blk, blk), out_shape=(shp, shp, shp, shp),
        compiler_params=_params("parallel"),
    )(c_rows, dmod_cols, w, m, v)


def _small_adam(pkg_all, n_batch_rows, w, m, v):
    d = w.shape[1]

    def body(p_ref, w_ref, m_ref, v_ref, g_ref, d_ref, nm_ref, nv_ref):
        for part in range(3):
            acc = jnp.zeros((1, d), F32)
            for dev in range(N_DEV):
                for b in range(n_batch_rows // 3):
                    acc = acc + p_ref[dev, 3 * b + part:3 * b + part + 1, :]
            g_ref[part:part + 1, :] = acc
        for rrow in range(3):
            acc = jnp.zeros((1, d), F32)
            for dev in range(N_DEV):
                acc = acc + p_ref[dev, n_batch_rows + rrow:n_batch_rows + rrow + 1, :]
            g_ref[3 + rrow:4 + rrow, :] = acc
        g_ref[6:8, :] = jnp.zeros((2, d), F32)
        g = g_ref[...]
        delta, nm, nv = _adam(w_ref[...], g, m_ref[...], v_ref[...])
        d_ref[...] = delta
        nm_ref[...] = nm
        nv_ref[...] = nv

    vm = pl.BlockSpec(memory_space=pltpu.VMEM)
    shp = jax.ShapeDtypeStruct((SMALL_ROWS, d), F32)
    return pl.pallas_call(
        body, name="small_adam", in_specs=[vm, vm, vm, vm], out_specs=(vm, vm, vm, vm),
        out_shape=(shp, shp, shp, shp),
    )(pkg_all, w, m, v)


def _adam_from_chip_sums(mine, from_sibling, from_chips, w, m, v, name):
    _, r, c = mine.shape
    tr = _tile(r, 128)

    def body(p_ref, s_ref, f_ref, w_ref, m_ref, v_ref, g_ref, d_ref, nm_ref, nv_ref):
        g = p_ref[...].astype(F32) + s_ref[...].astype(F32)
        for j in range(3):
            g = g + f_ref[j].astype(F32)
        delta, nm, nv = _adam(w_ref[...], g, m_ref[...], v_ref[...])
        g_ref[...] = g
        d_ref[...] = delta
        nm_ref[...] = nm
        nv_ref[...] = nv

    blk = pl.BlockSpec((tr, c), lambda i: (i, 0))
    slot0 = pl.BlockSpec((None, tr, c), lambda i: (0, i, 0))
    shp = jax.ShapeDtypeStruct((r, c), F32)
    return pl.pallas_call(
        body, name=name, grid=(r // tr,),
        in_specs=[slot0, slot0, pl.BlockSpec((3, tr, c), lambda i: (0, i, 0)), blk, blk, blk],
        out_specs=(blk, blk, blk, blk), out_shape=(shp, shp, shp, shp),
        compiler_params=_params("parallel"),
    )(mine, from_sibling, from_chips, w, m, v)


def _adam_from_partials(recv, w, m, v, name):
    _, r, c = recv.shape
    tr = _tile(r, 128)

    def body(p_ref, w_ref, m_ref, v_ref, g_ref, d_ref, nm_ref, nv_ref):
        g = p_ref[0].astype(F32)
        for dev in range(1, N_DEV):
            g = g + p_ref[dev].astype(F32)
        delta, nm, nv = _adam(w_ref[...], g, m_ref[...], v_ref[...])
        g_ref[...] = g
        d_ref[...] = delta
        nm_ref[...] = nm
        nv_ref[...] = nv

    blk = pl.BlockSpec((tr, c), lambda i: (i, 0))
    shp = jax.ShapeDtypeStruct((r, c), F32)
    return pl.pallas_call(
        body, name=name, grid=(r // tr,),
        in_specs=[pl.BlockSpec((N_DEV, tr, c), lambda i: (0, i, 0)), blk, blk, blk],
        out_specs=(blk, blk, blk, blk), out_shape=(shp, shp, shp, shp),
        compiler_params=_params("parallel"),
    )(recv, w, m, v)


def _norm_mod(x, g_norm, scale, shift):
    b, s, d = x.shape
    ts = _tile(s, 256)

    def body(x_ref, g_ref, sc_ref, sh_ref, h_ref):
        xv = x_ref[...]
        r = lax.rsqrt(jnp.mean(xv * xv, axis=-1, keepdims=True) + EPS)
        xn = (xv * r) * g_ref[...]
        h_ref[...] = (xn * (1.0 + sc_ref[...]) + sh_ref[...]).astype(BF16)

    tok = pl.BlockSpec((None, ts, d), lambda i, j: (i, j, 0))
    per_b = pl.BlockSpec((None, 1, d), lambda i, j: (i, 0, 0))
    return pl.pallas_call(
        body, name="norm_mod", grid=(b, s // ts),
        in_specs=[tok, pl.BlockSpec((1, d), lambda i, j: (0, 0)), per_b, per_b],
        out_specs=tok, out_shape=jax.ShapeDtypeStruct((b, s, d), BF16),
        compiler_params=_params("parallel", "parallel"),
    )(x, g_norm, scale, shift)


def _final_fwd_bwd(x, out, gate, g_final, target):
    b, s, d = x.shape
    ts = _tile(s, 256)

    def body(x_ref, o_ref, gt_ref, g_ref, t_ref, loss_ref, dx2_ref, dout_ref, dgate_ref, gg_ref):
        i, j = pl.program_id(0), pl.program_id(1)

        @pl.when((i == 0) & (j == 0))
        def _():
            loss_ref[...] = jnp.zeros_like(loss_ref)
            gg_ref[...] = jnp.zeros_like(gg_ref)

        @pl.when(j == 0)
        def _():
            dgate_ref[...] = jnp.zeros_like(dgate_ref)

        ov = o_ref[...].astype(F32)
        gt = gt_ref[...]
        x2 = x_ref[...] + gt * ov
        r = lax.rsqrt(jnp.mean(x2 * x2, axis=-1, keepdims=True) + EPS)
        xh = x2 * r
        err = xh * g_ref[...] - t_ref[...]
        loss_ref[...] += 0.5 * jnp.sum(jnp.mean(err * err, axis=-1, keepdims=True), axis=0, keepdims=True)
        dfin = err * (1.0 / d)
        gg_ref[...] += jnp.sum(dfin * xh, axis=0, keepdims=True)
        dxh = dfin * g_ref[...]
        dx2 = r * (dxh - xh * jnp.mean(dxh * xh, axis=-1, keepdims=True))
        dx2_ref[...] = dx2
        dout_ref[...] = (gt * dx2).astype(BF16)
        dgate_ref[...] += jnp.sum(dx2 * ov, axis=0, keepdims=True)

    tok = pl.BlockSpec((None, ts, d), lambda i, j: (i, j, 0))
    per_b = pl.BlockSpec((None, 1, d), lambda i, j: (i, 0, 0))
    vec = pl.BlockSpec((1, d), lambda i, j: (0, 0))
    return pl.pallas_call(
        body, name="final_fwd_bwd", grid=(b, s // ts),
        in_specs=[tok, tok, per_b, vec, tok],
        out_specs=(pl.BlockSpec((8, 128), lambda i, j: (0, 0)), tok, tok, per_b, vec),
        out_shape=(jax.ShapeDtypeStruct((8, 128), F32), jax.ShapeDtypeStruct((b, s, d), F32),
                   jax.ShapeDtypeStruct((b, s, d), BF16), jax.ShapeDtypeStruct((b, 1, d), F32),
                   jax.ShapeDtypeStruct((1, d), F32)),
        compiler_params=_params("arbitrary", "arbitrary"),
    )(x, out, gate, g_final, target)


def _norm_bwd(x, dh, dx2, scale, g_norm):
    b, s, d = x.shape
    ts = _tile(s, 256)

    def body(x_ref, dh_ref, dx2_ref, sc_ref, g_ref, gx_ref, dsh_ref, dsc_ref, gg_ref):
        i, j = pl.program_id(0), pl.program_id(1)

        @pl.when((i == 0) & (j == 0))
        def _():
            gg_ref[...] = jnp.zeros_like(gg_ref)

        @pl.when(j == 0)
        def _():
            dsh_ref[...] = jnp.zeros_like(dsh_ref)
            dsc_ref[...] = jnp.zeros_like(dsc_ref)

        xv = x_ref[...]
        dhv = dh_ref[...].astype(F32)
        r = lax.rsqrt(jnp.mean(xv * xv, axis=-1, keepdims=True) + EPS)
        xh = xv * r
        xn = xh * g_ref[...]
        dsh_ref[...] += jnp.sum(dhv, axis=0, keepdims=True)
        dsc_ref[...] += jnp.sum(dhv * xn, axis=0, keepdims=True)
        dxn = dhv * (1.0 + sc_ref[...])
        gg_ref[...] += jnp.sum(dxn * xh, axis=0, keepdims=True)
        dxh = dxn * g_ref[...]
        gx_ref[...] = dx2_ref[...] + r * (dxh - xh * jnp.mean(dxh * xh, axis=-1, keepdims=True))

    tok = pl.BlockSpec((None, ts, d), lambda i, j: (i, j, 0))
    per_b = pl.BlockSpec((None, 1, d), lambda i, j: (i, 0, 0))
    vec = pl.BlockSpec((1, d), lambda i, j: (0, 0))
    return pl.pallas_call(
        body, name="norm_bwd", grid=(b, s // ts),
        in_specs=[tok, tok, tok, per_b, vec],
        out_specs=(tok, per_b, per_b, vec),
        out_shape=(jax.ShapeDtypeStruct((b, s, d), F32), jax.ShapeDtypeStruct((b, 1, d), F32),
                   jax.ShapeDtypeStruct((b, 1, d), F32), jax.ShapeDtypeStruct((1, d), F32)),
        compiler_params=_params("arbitrary", "arbitrary"),
    )(x, dh, dx2, scale, g_norm)


def _gate_fwd(y_sb, y_dl, proj, g_sb, g_dl):
    t, e = y_sb.shape
    n_heads = e // HEAD_DIM
    tt = _tile(t, 256)

    def body(ys_ref, yd_ref, zs_ref, zd_ref, gs_ref, gd_ref, o_ref):
        for grp, (y_ref, z_ref, g_ref) in enumerate(((ys_ref, zs_ref, gs_ref), (yd_ref, zd_ref, gd_ref))):
            for h in range(n_heads):
                sl = slice(h * HEAD_DIM, (h + 1) * HEAD_DIM)
                y = y_ref[:, sl]
                r = lax.rsqrt(jnp.mean(y * y, axis=-1, keepdims=True) + EPS)
                yn = (y * r) * g_ref[:, sl]
                z = z_ref[:, sl].astype(F32)
                o_ref[:, grp * e + h * HEAD_DIM:grp * e + (h + 1) * HEAD_DIM] = (yn * _silu(z)).astype(BF16)

    yblk = pl.BlockSpec((tt, e), lambda i: (i, 0))
    gblk = pl.BlockSpec((1, e), lambda i: (0, 0))
    return pl.pallas_call(
        body, name="gate_fwd", grid=(t // tt,),
        in_specs=[yblk, yblk, pl.BlockSpec((tt, e), lambda i: (i, 3)), pl.BlockSpec((tt, e), lambda i: (i, 7)),
                  gblk, gblk],
        out_specs=pl.BlockSpec((tt, 2 * e), lambda i: (i, 0)),
        out_shape=jax.ShapeDtypeStruct((t, 2 * e), BF16),
        compiler_params=_params("parallel"),
    )(y_sb, y_dl, proj, proj, g_sb, g_dl)


def _gate_bwd(dyg, y_sb, y_dl, proj, g_sb, g_dl):
    t, e = y_sb.shape
    n_heads = e // HEAD_DIM
    tt = _tile(t, 256)

    def body(dg_ref, ys_ref, yd_ref, zs_ref, zd_ref, gs_ref, gd_ref,
             dys_ref, dyd_ref, dzs_ref, dzd_ref, ggs_ref, ggd_ref):
        @pl.when(pl.program_id(0) == 0)
        def _():
            ggs_ref[...] = jnp.zeros_like(ggs_ref)
            ggd_ref[...] = jnp.zeros_like(ggd_ref)

        groups = ((ys_ref, zs_ref, gs_ref, dys_ref, dzs_ref, ggs_ref), (yd_ref, zd_ref, gd_ref, dyd_ref, dzd_ref, ggd_ref))
        for grp, (y_ref, z_ref, g_ref, dy_ref, dz_ref, gg_ref) in enumerate(groups):
            for h in range(n_heads):
                sl = slice(h * HEAD_DIM, (h + 1) * HEAD_DIM)
                dg = dg_ref[:, grp * e + h * HEAD_DIM:grp * e + (h + 1) * HEAD_DIM].astype(F32)
                y = y_ref[:, sl]
                z = z_ref[:, sl].astype(F32)
                g = g_ref[:, sl]
                r = lax.rsqrt(jnp.mean(y * y, axis=-1, keepdims=True) + EPS)
                yh = y * r
                sig = jax.nn.sigmoid(z)
                dyn = dg * (z * sig)
                dz_ref[:, sl] = (dg * (yh * g) * (sig * (1.0 + z * (1.0 - sig)))).astype(BF16)
                gg_ref[:, sl] += jnp.sum(dyn * yh, axis=0, keepdims=True)
                dyh = dyn * g
                dy_ref[:, sl] = (r * (dyh - yh * jnp.mean(dyh * yh, axis=-1, keepdims=True))).astype(BF16)

    yblk = pl.BlockSpec((tt, e), lambda i: (i, 0))
    gblk = pl.BlockSpec((1, e), lambda i: (0, 0))
    act = jax.ShapeDtypeStruct((t, e), BF16)
    vec = jax.ShapeDtypeStruct((1, e), F32)
    return pl.pallas_call(
        body, name="gate_bwd", grid=(t // tt,),
        in_specs=[pl.BlockSpec((tt, 2 * e), lambda i: (i, 0)), yblk, yblk,
                  pl.BlockSpec((tt, e), lambda i: (i, 3)), pl.BlockSpec((tt, e), lambda i: (i, 7)), gblk, gblk],
        out_specs=(yblk, yblk, yblk, yblk, gblk, gblk),
        out_shape=(act, act, act, act, vec, vec),
        compiler_params=_params("arbitrary"),
    )(dyg, y_sb, y_dl, proj, proj, g_sb, g_dl)


ATT_TQ = 256
HEADS_PER_STEP = 4
SOFTPLUS_CLAMP = 30.0
ATT_STRIP = 32


def _split2_dot(x, u):
    hi = x.astype(BF16)
    lo = (x - hi.astype(F32)).astype(BF16)
    n = x.shape[0]
    both = jnp.dot(jnp.concatenate([hi, lo], axis=0), u, preferred_element_type=F32)
    return both[:n] + both[n:]


def _iota2(n):
    return lax.broadcasted_iota(jnp.int32, (n, n), 0), lax.broadcasted_iota(jnp.int32, (n, n), 1)


def _head_slices():
    return [slice(hh * HEAD_DIM, (hh + 1) * HEAD_DIM) for hh in range(HEADS_PER_STEP)]


def _att_specs(s, e, tq, col0):
    n_heads = e // HEAD_DIM
    hp = HEADS_PER_STEP
    assert n_heads % hp == 0 and col0 % hp == 0
    w = hp * HEAD_DIM
    q_spec = pl.BlockSpec((None, tq, w), lambda i, h, j: (i, j, col0 // hp + h))
    k_spec = pl.BlockSpec((None, s, w), lambda i, h, j: (i, 0, (col0 + n_heads) // hp + h))
    v_spec = pl.BlockSpec((None, s, w), lambda i, h, j: (i, 0, (col0 + 2 * n_heads) // hp + h))
    return q_spec, k_spec, v_spec


def _sb_fwd(proj3, e):
    b, s, _ = proj3.shape
    n_heads = e // HEAD_DIM
    hp = HEADS_PER_STEP
    tq = _tile(s, ATT_TQ)
    nq = s // tq
    inv = 1.0 / math.sqrt(HEAD_DIM)

    def body(q_ref, k_ref, v_ref, y_ref, tot_ref, acc_ref, car_ref):
        i = pl.program_id(2)
        row, col = _iota2(tq)
        before = row > col
        u_after = before.astype(BF16)
        acc_ref[...] = jnp.zeros_like(acc_ref)
        car_ref[...] = jnp.zeros_like(car_ref)

        def block(j, diagonal):
            keys = pl.ds(pl.multiple_of(j * tq, tq), tq)
            heads = list(enumerate(_head_slices()))
            zs = [lax.dot_general(q_ref[:, hs], k_ref[keys, hs], NT_DIMS, preferred_element_type=F32) * inv
                  for _, hs in heads]
            sps = [jnp.maximum(jnp.log(1.0 + jnp.exp(jnp.minimum(z, SOFTPLUS_CLAMP))), z) for z in zs]
            loms = [jnp.where(before, -sp, 0.0) if diagonal else -sp for sp in sps]
            sufs = [_split2_dot(loms[hh], u_after) + car_ref[hh] for hh, _ in heads]
            avs = [jnp.exp((zs[hh] - sps[hh]) + sufs[hh]) for hh, _ in heads]
            if diagonal:
                avs = [jnp.where(before, a, 0.0) for a in avs]
            pvs = [jnp.dot(avs[hh].astype(BF16), v_ref[keys, hs], preferred_element_type=F32) for hh, hs in heads]
            for hh, _ in heads:
                acc_ref[hh] += pvs[hh]
                car_ref[hh] += jnp.sum(loms[hh], axis=1, keepdims=True)

        block(i, True)

        def step(it, carry):
            block(i - it, False)
            return carry

        lax.fori_loop(1, i + 1, step, 0)
        for hh, hs in enumerate(_head_slices()):
            y_ref[:, hs] = acc_ref[hh]
            tot_ref[:, hs] = jnp.broadcast_to(car_ref[hh], (tq, HEAD_DIM))

    q_spec, k_spec, v_spec = _att_specs(s, e, tq, 0)
    blk_q = pl.BlockSpec((None, tq, hp * HEAD_DIM), lambda i, h, j: (i, j, h))
    shp = jax.ShapeDtypeStruct((b, s, e), F32)
    return pl.pallas_call(
        body, name="sb_fwd", grid=(b, n_heads // hp, nq),
        in_specs=[q_spec, k_spec, v_spec],
        out_specs=(blk_q, blk_q), out_shape=(shp, shp),
        scratch_shapes=[pltpu.VMEM((hp, tq, HEAD_DIM), F32), pltpu.VMEM((hp, tq, 1), F32)],
        compiler_params=_params("parallel", "parallel", "arbitrary"),
    )(proj3, proj3, proj3)


def _sb_bwd(proj3, lom_total, dy, partials):
    b, s, e = dy.shape
    n_heads = e // HEAD_DIM
    hp = HEADS_PER_STEP
    tq = _tile(s, ATT_TQ)
    nq = s // tq
    inv = 1.0 / math.sqrt(HEAD_DIM)
    grid = (b, n_heads // hp, nq)

    def body(q_ref, k_ref, v_ref, tot_ref, dy_ref, p_ref, dq_ref, dk_ref, dv_ref, r_ref,
             dqa, dka, dva, car, car2, send_sems, recv_sems, local_sem):
        i = pl.program_id(2)
        first, last = _grid_first_last(grid)

        @pl.when(first)
        def _():
            for cp in _all_to_all_copies(p_ref, r_ref, send_sems, recv_sems, local_sem):
                cp.start()

        @pl.when(i == 0)
        def _():
            dka[...] = jnp.zeros_like(dka)
            dva[...] = jnp.zeros_like(dva)

        row, col = _iota2(tq)
        before = row > col
        u_upto = (row <= col).astype(BF16)
        u_before = (row < col).astype(BF16)
        dqa[...] = jnp.zeros_like(dqa)
        car[...] = jnp.zeros_like(car)
        car2[...] = jnp.zeros_like(car2)

        def block(j, diagonal):
            keys = pl.ds(pl.multiple_of(j * tq, tq), tq)
            heads = list(enumerate(_head_slices()))
            zs = [lax.dot_general(q_ref[:, hs], k_ref[keys, hs], NT_DIMS, preferred_element_type=F32) * inv
                  for _, hs in heads]
            das = [lax.dot_general(dy_ref[:, hs], v_ref[keys, hs], NT_DIMS, preferred_element_type=F32) for _, hs in heads]
            ezs = [jnp.exp(jnp.minimum(z, SOFTPLUS_CLAMP)) for z in zs]
            sps = [jnp.maximum(jnp.log(1.0 + ezs[hh]), zs[hh]) for hh, _ in heads]
            loms = [jnp.where(before, -sp, 0.0) if diagonal else -sp for sp in sps]
            sufs = [tot_ref[:, hh * HEAD_DIM:hh * HEAD_DIM + 1] - (_split2_dot(loms[hh], u_upto) + car[hh])
                    for hh, _ in heads]
            avs = [jnp.exp((zs[hh] - sps[hh]) + sufs[hh]) for hh, _ in heads]
            if diagonal:
                avs = [jnp.where(before, a, 0.0) for a in avs]
            dls = [avs[hh] * das[hh] for hh, _ in heads]
            prefixes = [_split2_dot(dls[hh], u_before) + car2[hh] for hh, _ in heads]
            dzs = []
            for hh, _ in heads:
                one_minus_beta = 1.0 / (1.0 + ezs[hh])
                dz = (dls[hh] * one_minus_beta - prefixes[hh] * (ezs[hh] * one_minus_beta)) * inv
                if diagonal:
                    dz = jnp.where(before, dz, 0.0)
                dzs.append(dz.astype(BF16))
            dqs = [jnp.dot(dzs[hh], k_ref[keys, hs], preferred_element_type=F32) for hh, hs in heads]
            dks = [lax.dot_general(dzs[hh], q_ref[:, hs], TN_DIMS, preferred_element_type=F32) for hh, hs in heads]
            dvs = [lax.dot_general(avs[hh].astype(BF16), dy_ref[:, hs], TN_DIMS, preferred_element_type=F32)
                   for hh, hs in heads]
            for hh, hs in heads:
                dqa[hh] += dqs[hh]
                dka[keys, hs] += dks[hh]
                dva[keys, hs] += dvs[hh]
                car[hh] += jnp.sum(loms[hh], axis=1, keepdims=True)
                car2[hh] += jnp.sum(dls[hh], axis=1, keepdims=True)

        def step(j, carry):
            block(j, False)
            return carry

        lax.fori_loop(0, i, step, 0)
        block(i, True)
        for hh, hs in enumerate(_head_slices()):
            dq_ref[:, hs] = dqa[hh].astype(BF16)

        @pl.when(i == nq - 1)
        def _():
            dk_ref[...] = dka[...].astype(BF16)
            dv_ref[...] = dva[...].astype(BF16)

        @pl.when(last)
        def _():
            for cp in _all_to_all_copies(p_ref, r_ref, send_sems, recv_sems, local_sem):
                cp.wait()

    q_spec, k_spec, v_spec = _att_specs(s, e, tq, 0)
    w = hp * HEAD_DIM
    blk_q = pl.BlockSpec((None, tq, w), lambda i, h, j: (i, j, h))
    blk_kv = pl.BlockSpec((None, s, w), lambda i, h, j: (i, 0, h))
    shp = jax.ShapeDtypeStruct((b, s, e), BF16)
    return pl.pallas_call(
        body, name="sb_bwd", grid=grid,
        in_specs=[q_spec, k_spec, v_spec, blk_q, blk_q, ANY_SPEC],
        out_specs=(blk_q, blk_kv, blk_kv, ANY_SPEC),
        out_shape=(shp, shp, shp, jax.ShapeDtypeStruct(partials.shape, partials.dtype)),
        scratch_shapes=[pltpu.VMEM((hp, tq, HEAD_DIM), F32), pltpu.VMEM((s, w), F32), pltpu.VMEM((s, w), F32),
                        pltpu.VMEM((hp, tq, 1), F32), pltpu.VMEM((hp, tq, 1), F32),
                        pltpu.SemaphoreType.DMA((N_DEV - 1,)), pltpu.SemaphoreType.DMA((N_DEV - 1,)),
                        pltpu.SemaphoreType.DMA],
        compiler_params=_params("arbitrary", "arbitrary", "arbitrary"),
    )(proj3, proj3, proj3, lom_total, dy, partials)


def _dil_near_tiles(tq):
    return (DIL_PAIRS[1][0] + tq - 1) // tq + 1


def _dil_fill_bias(bias_ref, sl_ref, tq):
    row, col = _iota2(tq)
    for hh in range(HEADS_PER_STEP):
        slope = sl_ref[hh, 0:1, 0:1]
        for d in range(_dil_near_tiles(tq) + 1):
            dist = d * tq + row - col
            cnt = jnp.zeros(dist.shape, jnp.int32)
            for window, dilation in DIL_PAIRS:
                cnt = cnt + (((dist & (dilation - 1)) == 0) & (dist <= window)).astype(jnp.int32)
            bias = jnp.where(cnt == 3, math.log(3.0), jnp.where(cnt == 2, math.log(2.0), 0.0))
            bias_ref[hh, d] = jnp.where((dist >= 0) & (cnt > 0), bias - slope * dist.astype(F32), NEG)


def _dil_fwd(proj3, e, slopes):
    b, s, _ = proj3.shape
    n_heads = e // HEAD_DIM
    hp = HEADS_PER_STEP
    tq = _tile(s, ATT_TQ)
    nq = s // tq
    inv = 1.0 / math.sqrt(HEAD_DIM)
    assert s <= DIL_PAIRS[2][0]

    def body(q_ref, k_ref, v_ref, sl_ref, y_ref, lse_ref, acc_ref, m_ref, l_ref, bias_ref, p_ref):
        i = pl.program_id(2)

        @pl.when(i == 0)
        def _():
            _dil_fill_bias(bias_ref, sl_ref, tq)

        acc_ref[...] = jnp.zeros_like(acc_ref)
        m_ref[...] = jnp.full_like(m_ref, NEG)
        l_ref[...] = jnp.zeros_like(l_ref)

        def step(it, carry):
            keys = pl.ds(pl.multiple_of((i - it) * tq, tq), tq)
            heads = list(enumerate(_head_slices()))
            near = _dil_near_tiles(tq)
            tile = jnp.minimum(it, near)
            beyond = jnp.maximum(it - near, 0).astype(F32) * float(tq)
            raws = [lax.dot_general(q_ref[:, hs], k_ref[keys, hs], NT_DIMS, preferred_element_type=F32)
                    for _, hs in heads]
            for hh, _ in heads:
                shift = sl_ref[hh, 0:1, 0:1] * beyond
                for r0 in range(0, tq, ATT_STRIP):
                    rows = slice(r0, r0 + ATT_STRIP)
                    sc = (raws[hh][rows] * inv + bias_ref[hh, tile, rows, :]) - shift
                    m_old = m_ref[hh, rows]
                    m_new = jnp.maximum(m_old, jnp.max(sc, axis=1, keepdims=True))
                    p = jnp.exp(sc - m_new)
                    alpha = jnp.exp(m_old - m_new)
                    l_ref[hh, rows] = alpha * l_ref[hh, rows] + (p[:, :tq // 2] + p[:, tq // 2:])
                    acc_ref[hh, rows] = alpha * acc_ref[hh, rows]
                    p_ref[hh, rows] = p.astype(BF16)
                    m_ref[hh, rows] = m_new
            pvs = [jnp.dot(p_ref[hh], v_ref[keys, hs], preferred_element_type=F32) for hh, hs in heads]
            for hh, _ in heads:
                acc_ref[hh] += pvs[hh]
            return carry

        lax.fori_loop(0, i + 1, step, 0)
        for hh, hs in enumerate(_head_slices()):
            l = jnp.sum(l_ref[hh], axis=1, keepdims=True)
            y_ref[:, hs] = acc_ref[hh] / l
            lse_ref[:, hs] = jnp.broadcast_to(m_ref[hh] + jnp.log(l), (tq, HEAD_DIM))

    q_spec, k_spec, v_spec = _att_specs(s, e, tq, 4 * n_heads)
    blk_q = pl.BlockSpec((None, tq, hp * HEAD_DIM), lambda i, h, j: (i, j, h))
    shp = jax.ShapeDtypeStruct((b, s, e), F32)
    return pl.pallas_call(
        body, name="dil_fwd", grid=(b, n_heads // hp, nq),
        in_specs=[q_spec, k_spec, v_spec, pl.BlockSpec((hp, 8, HEAD_DIM), lambda i, h, j: (h, 0, 0))],
        out_specs=(blk_q, blk_q), out_shape=(shp, shp),
        scratch_shapes=[pltpu.VMEM((hp, tq, HEAD_DIM), F32), pltpu.VMEM((hp, tq, 1), F32), pltpu.VMEM((hp, tq, tq // 2), F32),
                        pltpu.VMEM((hp, _dil_near_tiles(tq) + 1, tq, tq), F32), pltpu.VMEM((hp, tq, tq), BF16)],
        compiler_params=_params("parallel", "parallel", "arbitrary"),
    )(proj3, proj3, proj3, slopes)


def _dil_bwd(proj3, y, lse, dy, slopes):
    b, s, e = y.shape
    n_heads = e // HEAD_DIM
    hp = HEADS_PER_STEP
    tq = _tile(s, ATT_TQ)
    nq = s // tq
    inv = 1.0 / math.sqrt(HEAD_DIM)

    def body(q_ref, k_ref, v_ref, sl_ref, y_ref, lse_ref, dy_ref, dq_ref, dk_ref, dv_ref, dqa, dka, dva, bias_ref,
             p_ref, ds_ref):
        i = pl.program_id(2)

        @pl.when(i == 0)
        def _():
            dka[...] = jnp.zeros_like(dka)
            dva[...] = jnp.zeros_like(dva)
            _dil_fill_bias(bias_ref, sl_ref, tq)

        delta = [jnp.sum(dy_ref[:, hs].astype(F32) * y_ref[:, hs], axis=1, keepdims=True) for hs in _head_slices()]
        dqa[...] = jnp.zeros_like(dqa)

        def step(it, carry):
            keys = pl.ds(pl.multiple_of((i - it) * tq, tq), tq)
            heads = list(enumerate(_head_slices()))
            near = _dil_near_tiles(tq)
            tile = jnp.minimum(it, near)
            beyond = jnp.maximum(it - near, 0).astype(F32) * float(tq)
            raws = [lax.dot_general(q_ref[:, hs], k_ref[keys, hs], NT_DIMS, preferred_element_type=F32)
                    for _, hs in heads]
            dps = [lax.dot_general(dy_ref[:, hs], v_ref[keys, hs], NT_DIMS, preferred_element_type=F32) for _, hs in heads]
            for hh, _ in heads:
                shift = sl_ref[hh, 0:1, 0:1] * beyond
                for r0 in range(0, tq, ATT_STRIP):
                    rows = slice(r0, r0 + ATT_STRIP)
                    sc = (raws[hh][rows] * inv + bias_ref[hh, tile, rows, :]) - shift
                    p = jnp.exp(sc - lse_ref[rows, hh * HEAD_DIM:hh * HEAD_DIM + 1])
                    p_ref[hh, rows] = p.astype(BF16)
                    ds_ref[hh, rows] = ((p * (dps[hh][rows] - delta[hh][rows])) * inv).astype(BF16)
            dqs = [jnp.dot(ds_ref[hh], k_ref[keys, hs], preferred_element_type=F32) for hh, hs in heads]
            dks = [lax.dot_general(ds_ref[hh], q_ref[:, hs], TN_DIMS, preferred_element_type=F32) for hh, hs in heads]
            dvs = [lax.dot_general(p_ref[hh], dy_ref[:, hs], TN_DIMS, preferred_element_type=F32) for hh, hs in heads]
            for hh, hs in heads:
                dqa[hh] += dqs[hh]
                dka[keys, hs] += dks[hh]
                dva[keys, hs] += dvs[hh]
            return carry

        lax.fori_loop(0, i + 1, step, 0)
        for hh, hs in enumerate(_head_slices()):
            dq_ref[:, hs] = dqa[hh].astype(BF16)

        @pl.when(i == nq - 1)
        def _():
            dk_ref[...] = dka[...].astype(BF16)
            dv_ref[...] = dva[...].astype(BF16)

    q_spec, k_spec, v_spec = _att_specs(s, e, tq, 4 * n_heads)
    w = hp * HEAD_DIM
    blk_q = pl.BlockSpec((None, tq, w), lambda i, h, j: (i, j, h))
    blk_kv = pl.BlockSpec((None, s, w), lambda i, h, j: (i, 0, h))
    shp = jax.ShapeDtypeStruct((b, s, e), BF16)
    return pl.pallas_call(
        body, name="dil_bwd", grid=(b, n_heads // hp, nq),
        in_specs=[q_spec, k_spec, v_spec, pl.BlockSpec((hp, 8, HEAD_DIM), lambda i, h, j: (h, 0, 0)),
                  blk_q, blk_q, blk_q],
        out_specs=(blk_q, blk_kv, blk_kv), out_shape=(shp, shp, shp),
        scratch_shapes=[pltpu.VMEM((hp, tq, HEAD_DIM), F32), pltpu.VMEM((s, w), F32), pltpu.VMEM((s, w), F32),
                        pltpu.VMEM((hp, _dil_near_tiles(tq) + 1, tq, tq), F32),
                        pltpu.VMEM((hp, tq, tq), BF16), pltpu.VMEM((hp, tq, tq), BF16)],
        compiler_params=_params("parallel", "parallel", "arbitrary"),
    )(proj3, proj3, proj3, slopes, y, lse, dy)


def kernel(x, c, w_ada, b_ada, g_norm, w_in, g_sb, g_dil, w_out, g_final, loss_target, m_w_ada, m_b_ada, m_g_norm, m_w_in, m_g_sb, m_g_dil, m_w_out, m_g_final, v_w_ada, v_b_ada, v_g_norm, v_w_in, v_g_sb, v_g_dil, v_w_out, v_g_final):
    b, s, d = x.shape
    t = b * s
    e = w_in.shape[2]
    n_heads = e // HEAD_DIM
    na = w_ada.shape[2]
    r_out = w_out.shape[1]
    assert g_sb.shape[1] == e and g_dil.shape[1] == e and N_DEV * r_out == 2 * e and N_DEV * na == 3 * d
    assert b <= SMALL_ROWS and 3 * b + 3 <= 2 * SMALL_ROWS
    ix, iy, ic = _mesh_pos()
    me = 4 * ix + 2 * iy + ic

    c_all = _allgather_rows(jnp.pad(c, ((0, SMALL_ROWS - b), (0, 0))), "ag_c")
    b_own = lax.dynamic_slice(b_ada, (0, me * na), (1, na))
    mod_cols = _ada_fwd(c_all, w_ada[0], b_own)
    mod_all = _allgather_rows(mod_cols, "ag_mod").reshape(N_DEV, N_DEV, SMALL_ROWS, na)
    mod_own = lax.dynamic_slice(mod_all, (0, me, 0, 0), (N_DEV, 1, b, na))[:, 0]
    mod = mod_own.transpose(1, 0, 2).reshape(b, 1, 3 * d)
    shift, scale, gate = mod[:, :, :d], mod[:, :, d:2 * d], mod[:, :, 2 * d:]

    w_own = w_in[0].astype(BF16)

    h = _norm_mod(x, g_norm, scale, shift).reshape(t, d)
    proj, w_in3, w_out3 = _proj_with_allgather(
        h, w_own, jnp.stack(_unit_ids()).astype(jnp.int32), w_out[0].astype(BF16))
    w_out1 = w_out3.reshape(1, N_DEV * r_out, d)
    proj3 = proj.reshape(b, s, N_DEV * e)
    slopes = jnp.exp2(-ALIBI_MAX_BIAS * jnp.arange(1, n_heads + 1, dtype=F32) / n_heads)
    slopes = jnp.broadcast_to(slopes[:, None, None], (n_heads, 8, HEAD_DIM))
    y_sb, lom_total = _sb_fwd(proj3, e)
    y_dl, lse = _dil_fwd(proj3, e, slopes)
    yg = _gate_fwd(y_sb.reshape(t, e), y_dl.reshape(t, e), proj, g_sb, g_dil)
    out = _mm_nn(yg, w_out1, BF16, "mm_out").reshape(b, s, d)
    loss_p, dx2, d_out, dgate, gg_final = _final_fwd_bwd(x, out, gate, g_final.reshape(1, d), loss_target)

    d_out2 = d_out.reshape(t, d)
    dyg = _mm_nt(d_out2, w_out1, BF16, "mm_dy")
    gw_out_p = _mm_tn(yg, d_out2, 1, BF16, "mm_gw_out").reshape(N_DEV, r_out, d)
    dy_sb, dy_dl, dz_sb, dz_dl, gg_sb, gg_dl = _gate_bwd(dyg, y_sb.reshape(t, e), y_dl.reshape(t, e), proj, g_sb, g_dil)
    dq_sb, dk_sb, dv_sb, recv_out = _sb_bwd(proj3, lom_total, dy_sb.reshape(b, s, e), gw_out_p)
    dq_dl, dk_dl, dv_dl = _dil_bwd(proj3, y_dl, lse, dy_dl.reshape(b, s, e), slopes)
    dproj = jnp.concatenate(
        [a.reshape(t, e) for a in (dq_sb, dk_sb, dv_sb, dz_sb, dq_dl, dk_dl, dv_dl, dz_dl)], axis=1)
    chips4 = [(ix, iy)] + _other_chips()
    to_sibling_core = jnp.stack([4 * px + 2 * py + (1 - ic) for px, py in chips4]).astype(jnp.int32)
    to_my_core = jnp.stack([4 * px + 2 * py + ic for px, py in chips4]).astype(jnp.int32)
    gw_in_sibs = _mm_tn_groups(to_sibling_core, h, dproj, N_DEV, "mm_gw_in_sibling")
    gw_in_mine, gw_in_sib = _mm_tn_groups(to_my_core, h, dproj, N_DEV, "mm_gw_in_mine", to_sibling=gw_in_sibs)
    gw_in_send = _chip_presum(gw_in_mine, gw_in_sib)
    dh, gw_in_recv = _mm_nt_with_chip_exchange(dproj, w_in3, BF16, gw_in_send, "mm_dh")
    dh = dh.reshape(b, s, d)
    grad_x, dshift, dscale, gg_norm = _norm_bwd(x, dh, dx2, scale, g_norm)

    dmod = jnp.concatenate([dshift, dscale, dgate], axis=1).reshape(3 * b, d)
    pkg = jnp.concatenate([dmod, gg_norm, gg_final, jnp.concatenate([gg_sb, gg_dl], axis=1),
                           jnp.zeros((2 * SMALL_ROWS - 3 * b - 3, d), F32)], axis=0)
    pkg_all = _allgather_rows(pkg, "ag_small_grads").reshape(N_DEV, 2 * SMALL_ROWS, d)
    dmod_all = pkg_all[:, :3 * b].reshape(N_DEV * b, 3 * d)
    dmod_cols = lax.dynamic_slice(dmod_all, (0, me * na), (N_DEV * b, na))
    c_rows = c_all.reshape(N_DEV, SMALL_ROWS, d)[:, :b].reshape(N_DEV * b, d)
    g_w_ada, d_w_ada, nm_w_ada, nv_w_ada = _ada_bwd_adam(c_rows, dmod_cols, w_ada[0], m_w_ada[0], v_w_ada[0])

    def pack(b_ada_like, g_norm_like, g_sb_like, g_dil_like, g_final_like):
        return jnp.concatenate([b_ada_like.reshape(3, d), g_norm_like.reshape(1, d), g_final_like.reshape(1, d),
                                jnp.concatenate([g_sb_like, g_dil_like], axis=1).reshape(1, d),
                                jnp.zeros((2, d), F32)], axis=0)

    small = _small_adam(pkg_all, 3 * b, pack(b_ada, g_norm, g_sb, g_dil, g_final),
                        pack(m_b_ada, m_g_norm, m_g_sb, m_g_dil, m_g_final),
                        pack(v_b_ada, v_g_norm, v_g_sb, v_g_dil, v_g_final))

    def unpack(p):
        return (p[0:3].reshape(1, 3 * d), p[3:4], p[5:6, :e], p[5:6, e:], p[4])

    sm_g, sm_d, sm_m, sm_v = (unpack(p) for p in small)

    g_w_in, d_w_in, nm_w_in, nv_w_in = _adam_from_chip_sums(
        gw_in_mine, gw_in_sib, gw_in_recv, w_in[0], m_w_in[0], v_w_in[0], "adam_w_in")
    g_w_out, d_w_out, nm_w_out, nv_w_out = _adam_from_partials(recv_out, w_out[0], m_w_out[0], v_w_out[0], "adam_w_out")

    loss = lax.psum(loss_p[0, 0], ("x", "y", "c"))

    def weights(ada, small_parts, w_in_part, w_out_part):
        b_ada_p, g_norm_p, g_sb_p, g_dil_p, g_final_p = small_parts
        return (ada[None], b_ada_p, g_norm_p, w_in_part[None], g_sb_p, g_dil_p, w_out_part[None], g_final_p)

    return (loss, grad_x,
            *weights(g_w_ada, sm_g, g_w_in, g_w_out),
            *weights(d_w_ada, sm_d, d_w_in, d_w_out),
            *weights(nm_w_ada, sm_m, nm_w_in, nm_w_out),
            *weights(nv_w_ada, sm_v, nv_w_in, nv_w_out))
```

```python
import functools
import math

import jax
import jax.numpy as jnp
from jax import lax
from jax.experimental import pallas as pl
from jax.experimental.pallas import tpu as pltpu

F32 = jnp.float32
BF16 = jnp.bfloat16
MESH = pl.DeviceIdType.MESH

N_DEV = 8
HEAD_DIM = 128
EPS = 1e-6
ALIBI_MAX_BIAS = 8.0
DIL_PAIRS = ((128, 1), (512, 4), (2048, 16))
DIL_STEPS = 128
NEG = -1e30

ADAM_LR = 0.001
ADAM_B1 = 0.9
ADAM_B2 = 0.999
ADAM_EPS = 1e-08
ADAM_WD = 0.01
ADAM_STEP = 10

VMEM_LIMIT_BYTES = 56 * 1024 * 1024
SMALL_ROWS = 8

NT_DIMS = (((1,), (1,)), ((), ()))
TN_DIMS = (((0,), (0,)), ((), ()))


def _params(*semantics):
    return pltpu.CompilerParams(dimension_semantics=semantics, vmem_limit_bytes=VMEM_LIMIT_BYTES)


def _tile(n, want):
    t = min(n, want)
    assert n % t == 0, (n, want)
    return t


def _mesh_pos():
    return lax.axis_index("x"), lax.axis_index("y"), lax.axis_index("c")


def _allgather_rows(x_shard, name):
    m_per, n = x_shard.shape

    def body(x_ref, out_ref, send_sems, recv_sems, local_sem):
        x, y, c = _mesh_pos()
        me, sibling = (x, y, c), (x, y, 1 - c)
        chips = [(1 - x, y), (x, 1 - y), (1 - x, 1 - y)]

        def rows(px, py, pc):
            return out_ref.at[pl.ds((4 * px + 2 * py + pc) * m_per, m_per), :]

        def copy(k, block, to, src=None):
            return pltpu.make_async_remote_copy(
                src_ref=rows(*block) if src is None else src, dst_ref=rows(*block),
                send_sem=send_sems.at[k], recv_sem=recv_sems.at[k], device_id=to, device_id_type=MESH)

        mine = pltpu.make_async_copy(x_ref, rows(*me), local_sem)
        mine.start()
        first = [copy(0, me, sibling, src=x_ref)]
        first += [copy(1 + j, me, (*chip, c), src=x_ref) for j, chip in enumerate(chips)]
        for cp in first:
            cp.start()
        passed = [copy(4 + j, (*chip, c), sibling) for j, chip in enumerate(chips)]
        for j, chip in enumerate(chips):
            copy(1 + j, (*chip, c), me).wait_recv()
            passed[j].start()
        copy(0, sibling, me).wait_recv()
        for j, chip in enumerate(chips):
            copy(4 + j, (*chip, 1 - c), me).wait_recv()
        for cp in first + passed:
            cp.wait_send()
        mine.wait()

    return pl.pallas_call(
        body, name=name,
        out_shape=jax.ShapeDtypeStruct((N_DEV * m_per, n), x_shard.dtype),
        in_specs=[pl.BlockSpec(memory_space=pltpu.VMEM)],
        out_specs=pl.BlockSpec(memory_space=pltpu.VMEM),
        scratch_shapes=[pltpu.SemaphoreType.DMA((7,)), pltpu.SemaphoreType.DMA((7,)), pltpu.SemaphoreType.DMA],
    )(x_shard)


def _two_level_gather(w_ref, out_ref, send_sems, recv_sems, local_sem):
    x, y, c = _mesh_pos()
    me, sibling = (x, y, c), (x, y, 1 - c)
    chips = [(1 - x, y), (x, 1 - y), (1 - x, 1 - y)]

    def copy(k, block, to, src=None):
        dst = out_ref.at[4 * block[0] + 2 * block[1] + block[2]]
        return pltpu.make_async_remote_copy(
            src_ref=dst if src is None else src, dst_ref=dst,
            send_sem=send_sems.at[k], recv_sem=recv_sems.at[k], device_id=to, device_id_type=MESH)

    mine = pltpu.make_async_copy(w_ref, out_ref.at[4 * x + 2 * y + c], local_sem)
    to_sibling = copy(0, me, sibling, src=w_ref)
    to_chips = [copy(1 + j, me, (*chip, c), src=w_ref) for j, chip in enumerate(chips)]
    passed = [copy(4 + j, (*chip, c), sibling) for j, chip in enumerate(chips)]

    def early():
        mine.start()
        to_sibling.start()

    def ici():
        for cp in to_chips:
            cp.start()

    def finish():
        for j, chip in enumerate(chips):
            copy(1 + j, (*chip, c), me).wait_recv()
            passed[j].start()
        copy(0, sibling, me).wait_recv()
        for j, chip in enumerate(chips):
            copy(4 + j, (*chip, 1 - c), me).wait_recv()
        for cp in [to_sibling] + to_chips + passed:
            cp.wait_send()
        mine.wait()

    return early, ici, finish


PROJ_HALVES = 2


def _unit_schedule():
    assert PROJ_HALVES == 2
    sched = [("own", None, hf) for hf in range(PROJ_HALVES)] + [("sib", None, hf) for hf in range(PROJ_HALVES)]
    for rnd in ([(0, 0), (1, 1)], [(2, 0), (2, 1)], [(0, 1), (1, 0)]):
        sched += [("direct", j, hf) for j, hf in rnd] + [("fwd", j, hf) for j, hf in rnd]
    return sched


def _unit_ids():
    x, y, c = _mesh_pos()
    chips = _other_chips()
    ids = []
    for kind, j, hf in _unit_schedule():
        px, py = (x, y) if j is None else chips[j]
        pc = c if kind in ("own", "direct") else 1 - c
        ids.append(PROJ_HALVES * (4 * px + 2 * py + pc) + hf)
    return ids


def _proj_with_allgather(h, w_own, order, w_later):
    t, d = h.shape
    nh = PROJ_HALVES
    u = w_own.shape[1] // nh
    sched = _unit_schedule()
    n_units = len(sched)
    pos = {entry: p for p, entry in enumerate(sched)}
    tm = _tile(t, MM_TM)
    m_tiles = t // tm
    prep_m = max(m_tiles - 2, 0)
    grid = (n_units, m_tiles)

    later_ici_pos = pos[("direct", 2, 0)]

    def body(order_ref, h_ref, wown_ref, later_ref, proj_ref, w3_ref, later3_ref, bbuf, bsems, send_sems, recv_sems,
             local_sems, later_send_sems, later_recv_sems, later_local_sem):
        n, m = pl.program_id(0), pl.program_id(1)
        x, y, c = _mesh_pos()
        chips = _other_chips()
        sibling = (x, y, 1 - c)
        later_early, later_ici, later_finish = _two_level_gather(
            later_ref, later3_ref, later_send_sems, later_recv_sems, later_local_sem)

        def unit(p):
            return w3_ref.at[order_ref[p]]

        def arrival(p):
            return pltpu.make_async_remote_copy(
                src_ref=unit(p), dst_ref=unit(p), send_sem=send_sems.at[0], recv_sem=recv_sems.at[p - nh],
                device_id=sibling, device_id_type=MESH)

        def send(k, src, p_here, p_there, to):
            return pltpu.make_async_remote_copy(
                src_ref=src, dst_ref=unit(p_here), send_sem=send_sems.at[k], recv_sem=recv_sems.at[p_there - nh],
                device_id=to, device_id_type=MESH)

        copies = []

        def out(src, p_here, p_there, to):
            copies.append(send(len(copies), src, p_here, p_there, to))
            return copies[-1]

        def own_unit(hf):
            return wown_ref.at[:, pl.ds(hf * u, u)]

        def own(hf, p_there, to):
            return out(own_unit(hf), pos[("own", None, hf)], p_there, to)

        x_nbr, y_nbr = (*chips[0], c), (*chips[1], c)
        first = [own(hf, pos[("sib", None, hf)], sibling) for hf in range(nh)]
        first += [own(0, pos[("direct", 0, 0)], x_nbr), own(1, pos[("direct", 1, 1)], y_nbr)]
        after = {}
        for kind, j, hf in sched:
            if kind == "direct":
                p = pos[(kind, j, hf)]
                after[p] = [out(unit(p), p, pos[("fwd", j, hf)], sibling)]
        p = pos[("direct", 0, 0)]
        after[p] += [out(unit(p), p, pos[("direct", 2, 0)], y_nbr), own(0, pos[("direct", 1, 0)], y_nbr)]
        p = pos[("direct", 1, 1)]
        after[p] += [out(unit(p), p, pos[("direct", 2, 1)], x_nbr), own(1, pos[("direct", 0, 1)], x_nbr)]
        locals_ = [pltpu.make_async_copy(own_unit(hf), unit(pos[("own", None, hf)]), local_sems.at[hf])
                   for hf in range(nh)]

        def fetch(p):
            src = own_unit(sched[p][2]) if sched[p][0] == "own" else unit(p)
            return pltpu.make_async_copy(src, bbuf.at[p % 2], bsems.at[p % 2])

        @pl.when((n == 0) & (m == 0))
        def _():
            for cp in locals_ + first:
                cp.start()
            fetch(0).start()
            later_early()

        for p in range(n_units):
            @pl.when((n == p) & (m == 0))
            def _(p=p):
                fetch(p).wait()

            if p + 1 < n_units:
                @pl.when((n == p) & (m == prep_m))
                def _(p=p):
                    if sched[p + 1][0] != "own":
                        arrival(p + 1).wait_recv()
                    for cp in after.get(p + 1, []):
                        cp.start()
                    fetch(p + 1).start()
                    if p + 1 == later_ici_pos:
                        later_ici()

        proj_ref[...] = jnp.dot(h_ref[...], bbuf[n % 2], preferred_element_type=F32).astype(proj_ref.dtype)

        @pl.when((n == n_units - 1) & (m == m_tiles - 1))
        def _():
            for cp in copies:
                cp.wait_send()
            for cp in locals_:
                cp.wait()
            later_finish()

    n_out = len(sched) - nh
    return pl.pallas_call(
        body, name="mm_proj_allgather",
        grid_spec=pltpu.PrefetchScalarGridSpec(
            num_scalar_prefetch=1, grid=grid,
            in_specs=[pl.BlockSpec((tm, d), lambda n, m, order: (m, 0)), ANY_SPEC, ANY_SPEC],
            out_specs=(pl.BlockSpec((tm, u), lambda n, m, order: (m, order[n])), ANY_SPEC, ANY_SPEC),
            scratch_shapes=[pltpu.VMEM((2, d, u), h.dtype), pltpu.SemaphoreType.DMA((2,)),
                            pltpu.SemaphoreType.DMA((n_out,)), pltpu.SemaphoreType.DMA((n_out,)),
                            pltpu.SemaphoreType.DMA((nh,)),
                            pltpu.SemaphoreType.DMA((N_DEV - 1,)), pltpu.SemaphoreType.DMA((N_DEV - 1,)),
                            pltpu.SemaphoreType.DMA]),
        out_shape=(jax.ShapeDtypeStruct((t, n_units * u), h.dtype), jax.ShapeDtypeStruct((n_units, d, u), h.dtype),
                   jax.ShapeDtypeStruct((N_DEV,) + w_later.shape, w_later.dtype)),
        compiler_params=_params("arbitrary", "arbitrary"),
    )(order, h, w_own, w_later)


ANY_SPEC = pl.BlockSpec(memory_space=pl.ANY)


def _grid_first_last(grid):
    ids = [pl.program_id(a) for a in range(len(grid))]
    first = functools.reduce(lambda p, q: p & q, [i == 0 for i in ids])
    last = functools.reduce(lambda p, q: p & q, [i == n - 1 for i, n in zip(ids, grid)])
    return first, last


def _all_to_all_copies(src, dst, send_sems, recv_sems, local_sem):
    x, y, c = _mesh_pos()
    my = 4 * x + 2 * y + c
    copies = [pltpu.make_async_copy(src.at[my], dst.at[my], local_sem)]
    for d in range(1, N_DEV):
        px = 1 - x if d & 4 else x
        py = 1 - y if d & 2 else y
        pc = 1 - c if d & 1 else c
        copies.append(pltpu.make_async_remote_copy(
            src_ref=src.at[4 * px + 2 * py + pc], dst_ref=dst.at[my],
            send_sem=send_sems.at[d - 1], recv_sem=recv_sems.at[d - 1],
            device_id=(px, py, pc), device_id_type=MESH))
    return copies


def _other_chips():
    x, y, _ = _mesh_pos()
    return [(1 - x, y), (x, 1 - y), (1 - x, 1 - y)]


def _same_core_copies(src, dst, send_sems, recv_sems):
    c = lax.axis_index("c")
    return [pltpu.make_async_remote_copy(
        src_ref=src.at[j], dst_ref=dst.at[j], send_sem=send_sems.at[j], recv_sem=recv_sems.at[j],
        device_id=(*chip, c), device_id_type=MESH) for j, chip in enumerate(_other_chips())]


def _sibling_copies(src, dst, send_sems, recv_sems):
    x, y, c = _mesh_pos()
    return [pltpu.make_async_remote_copy(
        src_ref=src.at[j], dst_ref=dst.at[j], send_sem=send_sems.at[j], recv_sem=recv_sems.at[j],
        device_id=(x, y, 1 - c), device_id_type=MESH) for j in range(src.shape[0])]


def _chip_presum(mine, from_sibling):
    _, r, cdim = mine.shape
    tr = _tile(r, 256)

    def body(p_ref, s_ref, o_ref):
        o_ref[...] = (p_ref[...].astype(F32) + s_ref[...].astype(F32)).astype(o_ref.dtype)

    blk = pl.BlockSpec((None, tr, cdim), lambda j, i: (1 + j, i, 0))
    return pl.pallas_call(
        body, name="chip_presum", grid=(3, r // tr), in_specs=[blk, blk],
        out_specs=pl.BlockSpec((None, tr, cdim), lambda j, i: (j, i, 0)),
        out_shape=jax.ShapeDtypeStruct((3, r, cdim), mine.dtype),
        compiler_params=_params("parallel", "parallel"),
    )(mine, from_sibling)


def _mm_call(a, b, dims, nk, grid, a_spec, b_spec, o_spec, out_shape, acc_shape, name):
    def body(a_ref, b_ref, o_ref, acc_ref):
        k = pl.program_id(2)

        @pl.when(k == 0)
        def _():
            acc_ref[...] = jnp.zeros_like(acc_ref)

        acc_ref[...] += lax.dot_general(a_ref[...], b_ref[...], dims, preferred_element_type=F32)

        @pl.when(k == nk - 1)
        def _():
            o_ref[...] = acc_ref[...].astype(o_ref.dtype)

    return pl.pallas_call(
        body, name=name, grid=grid, in_specs=[a_spec, b_spec], out_specs=o_spec, out_shape=out_shape,
        scratch_shapes=[pltpu.VMEM(acc_shape, F32)],
        compiler_params=_params("parallel", "parallel", "arbitrary"),
    )(a, b)


MM_TM, MM_TN, MM_TK = 1024, 2048, 1024


def _mm_nn(a, b3, out_dtype, name):
    m, kk = a.shape
    g, _, nb = b3.shape
    tm, tn, tk = _tile(m, MM_TM), _tile(nb, MM_TN), _tile(kk, MM_TK)
    npb = nb // tn
    return _mm_call(
        a, b3, (((1,), (0,)), ((), ())), kk // tk, (m // tm, g * npb, kk // tk),
        pl.BlockSpec((tm, tk), lambda i, j, k: (i, k)),
        pl.BlockSpec((None, tk, tn), lambda i, j, k: (j // npb, k, j % npb)),
        pl.BlockSpec((tm, tn), lambda i, j, k: (i, j)),
        jax.ShapeDtypeStruct((m, g * nb), out_dtype), (tm, tn), name)


def _mm_nt(a, b3, out_dtype, name):
    m, kk = a.shape
    g, n, kb = b3.shape
    tm, tn, tk = _tile(m, MM_TM), _tile(n, MM_TN), _tile(kb, MM_TK)
    kpb = kb // tk
    return _mm_call(
        a, b3, NT_DIMS, kk // tk, (m // tm, n // tn, kk // tk),
        pl.BlockSpec((tm, tk), lambda i, j, k: (i, k)),
        pl.BlockSpec((None, tn, tk), lambda i, j, k: (k // kpb, j, k % kpb)),
        pl.BlockSpec((tm, tn), lambda i, j, k: (i, j)),
        jax.ShapeDtypeStruct((m, n), out_dtype), (tm, tn), name)


def _mm_nt_with_chip_exchange(a, b3, out_dtype, send3, name):
    m, kk = a.shape
    g, n, kb = b3.shape
    tm, tn, tk = _tile(m, MM_TM), _tile(n, MM_TN), _tile(kb, MM_TK)
    kpb = kb // tk
    grid = (m // tm, n // tn, kk // tk)

    def body(a_ref, b_ref, s_ref, o_ref, r_ref, acc_ref, send_sems, recv_sems):
        first, last = _grid_first_last(grid)
        k = pl.program_id(2)

        @pl.when(first)
        def _():
            for cp in _same_core_copies(s_ref, r_ref, send_sems, recv_sems):
                cp.start()

        @pl.when(k == 0)
        def _():
            acc_ref[...] = jnp.zeros_like(acc_ref)

        acc_ref[...] += lax.dot_general(a_ref[...], b_ref[...], NT_DIMS, preferred_element_type=F32)

        @pl.when(k == grid[2] - 1)
        def _():
            o_ref[...] = acc_ref[...].astype(o_ref.dtype)

        @pl.when(last)
        def _():
            for cp in _same_core_copies(s_ref, r_ref, send_sems, recv_sems):
                cp.wait()

    return pl.pallas_call(
        body, name=name, grid=grid,
        in_specs=[pl.BlockSpec((tm, tk), lambda i, j, k: (i, k)),
                  pl.BlockSpec((None, tn, tk), lambda i, j, k: (k // kpb, j, k % kpb)), ANY_SPEC],
        out_specs=(pl.BlockSpec((tm, tn), lambda i, j, k: (i, j)), ANY_SPEC),
        out_shape=(jax.ShapeDtypeStruct((m, n), out_dtype), jax.ShapeDtypeStruct(send3.shape, send3.dtype)),
        scratch_shapes=[pltpu.VMEM((tm, tn), F32), pltpu.SemaphoreType.DMA((3,)), pltpu.SemaphoreType.DMA((3,))],
        compiler_params=_params("arbitrary", "arbitrary", "arbitrary"),
    )(a, b3, send3)


def _mm_tn_groups(groups, a, b, n_groups, name, to_sibling=None):
    t, m = a.shape
    nb = b.shape[1] // n_groups
    ng = groups.shape[0]
    tm, tk = _tile(m, MM_TM), _tile(t, MM_TK)
    grid = (m // tm, ng, t // tk)
    carry = to_sibling is not None

    def body(groups_ref, a_ref, b_ref, *rest):
        if carry:
            s_ref, o_ref, r_ref, acc_ref, send_sems, recv_sems = rest
        else:
            o_ref, acc_ref = rest
        k = pl.program_id(2)
        first, last = _grid_first_last(grid)

        if carry:
            @pl.when(first)
            def _():
                for cp in _sibling_copies(s_ref, r_ref, send_sems, recv_sems):
                    cp.start()

        @pl.when(k == 0)
        def _():
            acc_ref[...] = jnp.zeros_like(acc_ref)

        acc_ref[...] += lax.dot_general(a_ref[...], b_ref[...], TN_DIMS, preferred_element_type=F32)

        @pl.when(k == grid[2] - 1)
        def _():
            o_ref[...] = acc_ref[...].astype(o_ref.dtype)

        if carry:
            @pl.when(last)
            def _():
                for cp in _sibling_copies(s_ref, r_ref, send_sems, recv_sems):
                    cp.wait()

    shp = jax.ShapeDtypeStruct((ng, m, nb), b.dtype)
    o_spec = pl.BlockSpec((None, tm, nb), lambda i, j, k, grp: (j, i, 0))
    return pl.pallas_call(
        body, name=name,
        grid_spec=pltpu.PrefetchScalarGridSpec(
            num_scalar_prefetch=1, grid=grid,
            in_specs=[pl.BlockSpec((tk, tm), lambda i, j, k, grp: (k, i)),
                      pl.BlockSpec((tk, nb), lambda i, j, k, grp: (k, grp[j]))] + ([ANY_SPEC] if carry else []),
            out_specs=(o_spec, ANY_SPEC) if carry else o_spec,
            scratch_shapes=[pltpu.VMEM((tm, nb), F32)] + (
                [pltpu.SemaphoreType.DMA((ng,)), pltpu.SemaphoreType.DMA((ng,))] if carry else [])),
        out_shape=(shp, shp) if carry else shp,
        compiler_params=_params("arbitrary", "arbitrary", "arbitrary"),
    )(*((groups, a, b, to_sibling) if carry else (groups, a, b)))


def _mm_tn(a, b, g, out_dtype, name):
    t, m = a.shape
    nb = b.shape[1] // g
    tm, tn, tk = _tile(m, MM_TM), _tile(nb, MM_TN), _tile(t, MM_TK)
    npb = nb // tn
    return _mm_call(
        a, b, TN_DIMS, t // tk, (m // tm, g * npb, t // tk),
        pl.BlockSpec((tk, tm), lambda i, j, k: (k, i)),
        pl.BlockSpec((tk, tn), lambda i, j, k: (k, j)),
        pl.BlockSpec((None, tm, tn), lambda i, j, k: (j // npb, i, j % npb)),
        jax.ShapeDtypeStruct((g, m, nb), out_dtype), (tm, tn), name)


def _silu(z):
    return z * jax.nn.sigmoid(z)


def _ada_fwd(c_all, w_shard, b_own):
    r, d = c_all.shape
    na = w_shard.shape[1]
    tk = _tile(d, 512)
    nk = d // tk

    def body(c_ref, w_ref, b_ref, o_ref):
        k = pl.program_id(0)

        @pl.when(k == 0)
        def _():
            o_ref[...] = jnp.zeros_like(o_ref) + b_ref[...]

        cs = _silu(c_ref[...]).astype(BF16)
        o_ref[...] += jnp.dot(cs, w_ref[...].astype(BF16), preferred_element_type=F32)

    return pl.pallas_call(
        body, name="ada_fwd", grid=(nk,),
        in_specs=[pl.BlockSpec((r, tk), lambda k: (0, k)), pl.BlockSpec((tk, na), lambda k: (k, 0)),
                  pl.BlockSpec((1, na), lambda k: (0, 0))],
        out_specs=pl.BlockSpec((r, na), lambda k: (0, 0)),
        out_shape=jax.ShapeDtypeStruct((r, na), F32),
        compiler_params=_params("arbitrary"),
    )(c_all, w_shard, b_own)


def _adam(w, g, m, v):
    nm = ADAM_B1 * m + (1.0 - ADAM_B1) * g
    nv = ADAM_B2 * v + (1.0 - ADAM_B2) * (g * g)
    m_hat = nm / (1.0 - ADAM_B1 ** ADAM_STEP)
    v_hat = nv / (1.0 - ADAM_B2 ** ADAM_STEP)
    delta = -ADAM_LR * (m_hat / (jnp.sqrt(v_hat) + ADAM_EPS) + ADAM_WD * w)
    return delta, nm, nv


def _ada_bwd_adam(c_rows, dmod_cols, w, m, v):
    bg, d = c_rows.shape
    na = w.shape[1]
    tr = _tile(d, 256)

    def body(c_ref, dm_ref, w_ref, m_ref, v_ref, g_ref, d_ref, nm_ref, nv_ref):
        cs = _silu(c_ref[...]).astype(BF16)
        g = lax.dot_general(cs, dm_ref[...].astype(BF16), TN_DIMS, preferred_element_type=F32)
        delta, nm, nv = _adam(w_ref[...], g, m_ref[...], v_ref[...])
        g_ref[...] = g
        d_ref[...] = delta
        nm_ref[...] = nm
        nv_ref[...] = nv

    blk = pl.BlockSpec((tr, na), lambda i: (i, 0))
    shp = jax.ShapeDtypeStruct((d, na), F32)
    return pl.pallas_call(
        body, name="ada_bwd_adam", grid=(d // tr,),
        in_specs=[pl.BlockSpec((bg, tr), lambda i: (0, i)), pl.BlockSpec((bg, na), lambda i: (0, 0)), blk, blk, blk],
        out_specs=(blk, blk, blk, blk), out_shape=(shp, shp, shp, shp),
        compiler_params=_params("parallel"),
    )(c_rows, dmod_cols, w, m, v)


def _small_adam(pkg_all, n_batch_rows, w, m, v):
    d = w.shape[1]

    def body(p_ref, w_ref, m_ref, v_ref, g_ref, d_ref, nm_ref, nv_ref):
        for part in range(3):
            acc = jnp.zeros((1, d), F32)
            for dev in range(N_DEV):
                for b in range(n_batch_rows // 3):
                    acc = acc + p_ref[dev, 3 * b + part:3 * b + part + 1, :]
            g_ref[part:part + 1, :] = acc
        for rrow in range(3):
            acc = jnp.zeros((1, d), F32)
            for dev in range(N_DEV):
                acc = acc + p_ref[dev, n_batch_rows + rrow:n_batch_rows + rrow + 1, :]
            g_ref[3 + rrow:4 + rrow, :] = acc
        g_ref[6:8, :] = jnp.zeros((2, d), F32)
        g = g_ref[...]
        delta, nm, nv = _adam(w_ref[...], g, m_ref[...], v_ref[...])
        d_ref[...] = delta
        nm_ref[...] = nm
        nv_ref[...] = nv

    vm = pl.BlockSpec(memory_space=pltpu.VMEM)
    shp = jax.ShapeDtypeStruct((SMALL_ROWS, d), F32)
    return pl.pallas_call(
        body, name="small_adam", in_specs=[vm, vm, vm, vm], out_specs=(vm, vm, vm, vm),
        out_shape=(shp, shp, shp, shp),
    )(pkg_all, w, m, v)


def _adam_from_chip_sums(mine, from_sibling, from_chips, w, m, v, name):
    _, r, c = mine.shape
    tr = _tile(r, 128)

    def body(p_ref, s_ref, f_ref, w_ref, m_ref, v_ref, g_ref, d_ref, nm_ref, nv_ref):
        g = p_ref[...].astype(F32) + s_ref[...].astype(F32)
        for j in range(3):
            g = g + f_ref[j].astype(F32)
        delta, nm, nv = _adam(w_ref[...], g, m_ref[...], v_ref[...])
        g_ref[...] = g
        d_ref[...] = delta
        nm_ref[...] = nm
        nv_ref[...] = nv

    blk = pl.BlockSpec((tr, c), lambda i: (i, 0))
    slot0 = pl.BlockSpec((None, tr, c), lambda i: (0, i, 0))
    shp = jax.ShapeDtypeStruct((r, c), F32)
    return pl.pallas_call(
        body, name=name, grid=(r // tr,),
        in_specs=[slot0, slot0, pl.BlockSpec((3, tr, c), lambda i: (0, i, 0)), blk, blk, blk],
        out_specs=(blk, blk, blk, blk), out_shape=(shp, shp, shp, shp),
        compiler_params=_params("parallel"),
    )(mine, from_sibling, from_chips, w, m, v)


def _adam_from_partials(recv, w, m, v, name):
    _, r, c = recv.shape
    tr = _tile(r, 128)

    def body(p_ref, w_ref, m_ref, v_ref, g_ref, d_ref, nm_ref, nv_ref):
        g = p_ref[0].astype(F32)
        for dev in range(1, N_DEV):
            g = g + p_ref[dev].astype(F32)
        delta, nm, nv = _adam(w_ref[...], g, m_ref[...], v_ref[...])
        g_ref[...] = g
        d_ref[...] = delta
        nm_ref[...] = nm
        nv_ref[...] = nv

    blk = pl.BlockSpec((tr, c), lambda i: (i, 0))
    shp = jax.ShapeDtypeStruct((r, c), F32)
    return pl.pallas_call(
        body, name=name, grid=(r // tr,),
        in_specs=[pl.BlockSpec((N_DEV, tr, c), lambda i: (0, i, 0)), blk, blk, blk],
        out_specs=(blk, blk, blk, blk), out_shape=(shp, shp, shp, shp),
        compiler_params=_params("parallel"),
    )(recv, w, m, v)


def _norm_mod(x, g_norm, scale, shift):
    b, s, d = x.shape
    ts = _tile(s, 256)

    def body(x_ref, g_ref, sc_ref, sh_ref, h_ref):
        xv = x_ref[...]
        r = lax.rsqrt(jnp.mean(xv * xv, axis=-1, keepdims=True) + EPS)
        xn = (xv * r) * g_ref[...]
        h_ref[...] = (xn * (1.0 + sc_ref[...]) + sh_ref[...]).astype(BF16)

    tok = pl.BlockSpec((None, ts, d), lambda i, j: (i, j, 0))
    per_b = pl.BlockSpec((None, 1, d), lambda i, j: (i, 0, 0))
    return pl.pallas_call(
        body, name="norm_mod", grid=(b, s // ts),
        in_specs=[tok, pl.BlockSpec((1, d), lambda i, j: (0, 0)), per_b, per_b],
        out_specs=tok, out_shape=jax.ShapeDtypeStruct((b, s, d), BF16),
        compiler_params=_params("parallel", "parallel"),
    )(x, g_norm, scale, shift)


def _final_fwd_bwd(x, out, gate, g_final, target):
    b, s, d = x.shape
    ts = _tile(s, 256)

    def body(x_ref, o_ref, gt_ref, g_ref, t_ref, loss_ref, dx2_ref, dout_ref, dgate_ref, gg_ref):
        i, j = pl.program_id(0), pl.program_id(1)

        @pl.when((i == 0) & (j == 0))
        def _():
            loss_ref[...] = jnp.zeros_like(loss_ref)
            gg_ref[...] = jnp.zeros_like(gg_ref)

        @pl.when(j == 0)
        def _():
            dgate_ref[...] = jnp.zeros_like(dgate_ref)

        ov = o_ref[...]
        gt = gt_ref[...]
        x2 = x_ref[...] + gt * ov
        r = lax.rsqrt(jnp.mean(x2 * x2, axis=-1, keepdims=True) + EPS)
        xh = x2 * r
        err = xh * g_ref[...] - t_ref[...]
        loss_ref[...] += 0.5 * jnp.sum(jnp.mean(err * err, axis=-1, keepdims=True), axis=0, keepdims=True)
        dfin = err * (1.0 / d)
        gg_ref[...] += jnp.sum(dfin * xh, axis=0, keepdims=True)
        dxh = dfin * g_ref[...]
        dx2 = r * (dxh - xh * jnp.mean(dxh * xh, axis=-1, keepdims=True))
        dx2_ref[...] = dx2
        dout_ref[...] = (gt * dx2).astype(BF16)
        dgate_ref[...] += jnp.sum(dx2 * ov, axis=0, keepdims=True)

    tok = pl.BlockSpec((None, ts, d), lambda i, j: (i, j, 0))
    per_b = pl.BlockSpec((None, 1, d), lambda i, j: (i, 0, 0))
    vec = pl.BlockSpec((1, d), lambda i, j: (0, 0))
    return pl.pallas_call(
        body, name="final_fwd_bwd", grid=(b, s // ts),
        in_specs=[tok, tok, per_b, vec, tok],
        out_specs=(pl.BlockSpec((8, 128), lambda i, j: (0, 0)), tok, tok, per_b, vec),
        out_shape=(jax.ShapeDtypeStruct((8, 128), F32), jax.ShapeDtypeStruct((b, s, d), F32),
                   jax.ShapeDtypeStruct((b, s, d), BF16), jax.ShapeDtypeStruct((b, 1, d), F32),
                   jax.ShapeDtypeStruct((1, d), F32)),
        compiler_params=_params("arbitrary", "arbitrary"),
    )(x, out, gate, g_final, target)


def _norm_bwd(x, dh, dx2, scale, g_norm):
    b, s, d = x.shape
    ts = _tile(s, 256)

    def body(x_ref, dh_ref, dx2_ref, sc_ref, g_ref, gx_ref, dsh_ref, dsc_ref, gg_ref):
        i, j = pl.program_id(0), pl.program_id(1)

        @pl.when((i == 0) & (j == 0))
        def _():
            gg_ref[...] = jnp.zeros_like(gg_ref)

        @pl.when(j == 0)
        def _():
            dsh_ref[...] = jnp.zeros_like(dsh_ref)
            dsc_ref[...] = jnp.zeros_like(dsc_ref)

        xv = x_ref[...]
        dhv = dh_ref[...]
        r = lax.rsqrt(jnp.mean(xv * xv, axis=-1, keepdims=True) + EPS)
        xh = xv * r
        xn = xh * g_ref[...]
        dsh_ref[...] += jnp.sum(dhv, axis=0, keepdims=True)
        dsc_ref[...] += jnp.sum(dhv * xn, axis=0, keepdims=True)
        dxn = dhv * (1.0 + sc_ref[...])
        gg_ref[...] += jnp.sum(dxn * xh, axis=0, keepdims=True)
        dxh = dxn * g_ref[...]
        gx_ref[...] = dx2_ref[...] + r * (dxh - xh * jnp.mean(dxh * xh, axis=-1, keepdims=True))

    tok = pl.BlockSpec((None, ts, d), lambda i, j: (i, j, 0))
    per_b = pl.BlockSpec((None, 1, d), lambda i, j: (i, 0, 0))
    vec = pl.BlockSpec((1, d), lambda i, j: (0, 0))
    return pl.pallas_call(
        body, name="norm_bwd", grid=(b, s // ts),
        in_specs=[tok, tok, tok, per_b, vec],
        out_specs=(tok, per_b, per_b, vec),
        out_shape=(jax.ShapeDtypeStruct((b, s, d), F32), jax.ShapeDtypeStruct((b, 1, d), F32),
                   jax.ShapeDtypeStruct((b, 1, d), F32), jax.ShapeDtypeStruct((1, d), F32)),
        compiler_params=_params("arbitrary", "arbitrary"),
    )(x, dh, dx2, scale, g_norm)


def _gate_fwd(y_sb, y_dl, proj, g_sb, g_dl):
    t, e = y_sb.shape
    n_heads = e // HEAD_DIM
    tt = _tile(t, 256)

    def body(ys_ref, yd_ref, zs_ref, zd_ref, gs_ref, gd_ref, o_ref):
        for grp, (y_ref, z_ref, g_ref) in enumerate(((ys_ref, zs_ref, gs_ref), (yd_ref, zd_ref, gd_ref))):
            for h in range(n_heads):
                sl = slice(h * HEAD_DIM, (h + 1) * HEAD_DIM)
                y = y_ref[:, sl]
                r = lax.rsqrt(jnp.mean(y * y, axis=-1, keepdims=True) + EPS)
                yn = (y * r) * g_ref[:, sl]
                z = z_ref[:, sl].astype(F32)
                o_ref[:, grp * e + h * HEAD_DIM:grp * e + (h + 1) * HEAD_DIM] = (yn * _silu(z)).astype(BF16)

    yblk = pl.BlockSpec((tt, e), lambda i: (i, 0))
    gblk = pl.BlockSpec((1, e), lambda i: (0, 0))
    return pl.pallas_call(
        body, name="gate_fwd", grid=(t // tt,),
        in_specs=[yblk, yblk, pl.BlockSpec((tt, e), lambda i: (i, 3)), pl.BlockSpec((tt, e), lambda i: (i, 7)),
                  gblk, gblk],
        out_specs=pl.BlockSpec((tt, 2 * e), lambda i: (i, 0)),
        out_shape=jax.ShapeDtypeStruct((t, 2 * e), BF16),
        compiler_params=_params("parallel"),
    )(y_sb, y_dl, proj, proj, g_sb, g_dl)


def _gate_bwd(dyg, y_sb, y_dl, proj, g_sb, g_dl):
    t, e = y_sb.shape
    n_heads = e // HEAD_DIM
    tt = _tile(t, 256)

    def body(dg_ref, ys_ref, yd_ref, zs_ref, zd_ref, gs_ref, gd_ref,
             dys_ref, dyd_ref, dzs_ref, dzd_ref, ggs_ref, ggd_ref):
        @pl.when(pl.program_id(0) == 0)
        def _():
            ggs_ref[...] = jnp.zeros_like(ggs_ref)
            ggd_ref[...] = jnp.zeros_like(ggd_ref)

        groups = ((ys_ref, zs_ref, gs_ref, dys_ref, dzs_ref, ggs_ref), (yd_ref, zd_ref, gd_ref, dyd_ref, dzd_ref, ggd_ref))
        for grp, (y_ref, z_ref, g_ref, dy_ref, dz_ref, gg_ref) in enumerate(groups):
            for h in range(n_heads):
                sl = slice(h * HEAD_DIM, (h + 1) * HEAD_DIM)
                dg = dg_ref[:, grp * e + h * HEAD_DIM:grp * e + (h + 1) * HEAD_DIM].astype(F32)
                y = y_ref[:, sl]
                z = z_ref[:, sl].astype(F32)
                g = g_ref[:, sl]
                r = lax.rsqrt(jnp.mean(y * y, axis=-1, keepdims=True) + EPS)
                yh = y * r
                sig = jax.nn.sigmoid(z)
                dyn = dg * (z * sig)
                dz_ref[:, sl] = (dg * (yh * g) * (sig * (1.0 + z * (1.0 - sig)))).astype(BF16)
                gg_ref[:, sl] += jnp.sum(dyn * yh, axis=0, keepdims=True)
                dyh = dyn * g
                dy_ref[:, sl] = (r * (dyh - yh * jnp.mean(dyh * yh, axis=-1, keepdims=True))).astype(BF16)

    yblk = pl.BlockSpec((tt, e), lambda i: (i, 0))
    gblk = pl.BlockSpec((1, e), lambda i: (0, 0))
    act = jax.ShapeDtypeStruct((t, e), BF16)
    vec = jax.ShapeDtypeStruct((1, e), F32)
    return pl.pallas_call(
        body, name="gate_bwd", grid=(t // tt,),
        in_specs=[pl.BlockSpec((tt, 2 * e), lambda i: (i, 0)), yblk, yblk,
                  pl.BlockSpec((tt, e), lambda i: (i, 3)), pl.BlockSpec((tt, e), lambda i: (i, 7)), gblk, gblk],
        out_specs=(yblk, yblk, yblk, yblk, gblk, gblk),
        out_shape=(act, act, act, act, vec, vec),
        compiler_params=_params("arbitrary"),
    )(dyg, y_sb, y_dl, proj, proj, g_sb, g_dl)


ATT_TQ = 256
HEADS_PER_STEP = 4
SOFTPLUS_CLAMP = 30.0
ATT_STRIP = 32


def _split2_dot(x, u):
    hi = x.astype(BF16)
    lo = (x - hi.astype(F32)).astype(BF16)
    return jnp.dot(hi, u, preferred_element_type=F32) + jnp.dot(lo, u, preferred_element_type=F32)


def _iota2(n):
    return lax.broadcasted_iota(jnp.int32, (n, n), 0), lax.broadcasted_iota(jnp.int32, (n, n), 1)


def _head_slices():
    return [slice(hh * HEAD_DIM, (hh + 1) * HEAD_DIM) for hh in range(HEADS_PER_STEP)]


def _att_specs(s, e, tq, col0):
    n_heads = e // HEAD_DIM
    hp = HEADS_PER_STEP
    assert n_heads % hp == 0 and col0 % hp == 0
    w = hp * HEAD_DIM
    q_spec = pl.BlockSpec((None, tq, w), lambda i, h, j: (i, j, col0 // hp + h))
    k_spec = pl.BlockSpec((None, s, w), lambda i, h, j: (i, 0, (col0 + n_heads) // hp + h))
    v_spec = pl.BlockSpec((None, s, w), lambda i, h, j: (i, 0, (col0 + 2 * n_heads) // hp + h))
    return q_spec, k_spec, v_spec


def _sb_fwd(proj3, e):
    b, s, _ = proj3.shape
    n_heads = e // HEAD_DIM
    hp = HEADS_PER_STEP
    tq = _tile(s, ATT_TQ)
    nq = s // tq
    inv = 1.0 / math.sqrt(HEAD_DIM)

    def body(q_ref, k_ref, v_ref, y_ref, tot_ref, acc_ref, car_ref):
        i = pl.program_id(2)
        row, col = _iota2(tq)
        before = row > col
        u_after = before.astype(BF16)
        acc_ref[...] = jnp.zeros_like(acc_ref)
        car_ref[...] = jnp.zeros_like(car_ref)

        def block(j, diagonal):
            keys = pl.ds(pl.multiple_of(j * tq, tq), tq)
            heads = list(enumerate(_head_slices()))
            zs = [lax.dot_general(q_ref[:, hs], k_ref[keys, hs], NT_DIMS, preferred_element_type=F32) * inv
                  for _, hs in heads]
            sps = [jnp.maximum(jnp.log(1.0 + jnp.exp(jnp.minimum(z, SOFTPLUS_CLAMP))), z) for z in zs]
            loms = [jnp.where(before, -sp, 0.0) if diagonal else -sp for sp in sps]
            sufs = [_split2_dot(loms[hh], u_after) + car_ref[hh] for hh, _ in heads]
            avs = [jnp.exp((zs[hh] - sps[hh]) + sufs[hh]) for hh, _ in heads]
            if diagonal:
                avs = [jnp.where(before, a, 0.0) for a in avs]
            pvs = [jnp.dot(avs[hh].astype(BF16), v_ref[keys, hs], preferred_element_type=F32) for hh, hs in heads]
            for hh, _ in heads:
                acc_ref[hh] += pvs[hh]
                car_ref[hh] += jnp.sum(loms[hh], axis=1, keepdims=True)

        block(i, True)

        def step(it, carry):
            block(i - it, False)
            return carry

        lax.fori_loop(1, i + 1, step, 0)
        for hh, hs in enumerate(_head_slices()):
            y_ref[:, hs] = acc_ref[hh]
            tot_ref[:, hs] = jnp.broadcast_to(car_ref[hh], (tq, HEAD_DIM))

    q_spec, k_spec, v_spec = _att_specs(s, e, tq, 0)
    blk_q = pl.BlockSpec((None, tq, hp * HEAD_DIM), lambda i, h, j: (i, j, h))
    shp = jax.ShapeDtypeStruct((b, s, e), F32)
    return pl.pallas_call(
        body, name="sb_fwd", grid=(b, n_heads // hp, nq),
        in_specs=[q_spec, k_spec, v_spec],
        out_specs=(blk_q, blk_q), out_shape=(shp, shp),
        scratch_shapes=[pltpu.VMEM((hp, tq, HEAD_DIM), F32), pltpu.VMEM((hp, tq, 1), F32)],
        compiler_params=_params("parallel", "parallel", "arbitrary"),
    )(proj3, proj3, proj3)


def _sb_bwd(proj3, lom_total, dy, partials):
    b, s, e = dy.shape
    n_heads = e // HEAD_DIM
    hp = HEADS_PER_STEP
    tq = _tile(s, ATT_TQ)
    nq = s // tq
    inv = 1.0 / math.sqrt(HEAD_DIM)
    grid = (b, n_heads // hp, nq)

    def body(q_ref, k_ref, v_ref, tot_ref, dy_ref, p_ref, dq_ref, dk_ref, dv_ref, r_ref,
             dqa, dka, dva, car, car2, send_sems, recv_sems, local_sem):
        i = pl.program_id(2)
        first, last = _grid_first_last(grid)

        @pl.when(first)
        def _():
            for cp in _all_to_all_copies(p_ref, r_ref, send_sems, recv_sems, local_sem):
                cp.start()

        @pl.when(i == 0)
        def _():
            dka[...] = jnp.zeros_like(dka)
            dva[...] = jnp.zeros_like(dva)

        row, col = _iota2(tq)
        before = row > col
        u_upto = (row <= col).astype(BF16)
        u_before = (row < col).astype(BF16)
        dqa[...] = jnp.zeros_like(dqa)
        car[...] = jnp.zeros_like(car)
        car2[...] = jnp.zeros_like(car2)

        def block(j, diagonal):
            keys = pl.ds(pl.multiple_of(j * tq, tq), tq)
            heads = list(enumerate(_head_slices()))
            zs = [lax.dot_general(q_ref[:, hs], k_ref[keys, hs], NT_DIMS, preferred_element_type=F32) * inv
                  for _, hs in heads]
            das = [lax.dot_general(dy_ref[:, hs], v_ref[keys, hs], NT_DIMS, preferred_element_type=F32) for _, hs in heads]
            ezs = [jnp.exp(jnp.minimum(z, SOFTPLUS_CLAMP)) for z in zs]
            sps = [jnp.maximum(jnp.log(1.0 + ezs[hh]), zs[hh]) for hh, _ in heads]
            loms = [jnp.where(before, -sp, 0.0) if diagonal else -sp for sp in sps]
            sufs = [tot_ref[:, hh * HEAD_DIM:hh * HEAD_DIM + 1] - (_split2_dot(loms[hh], u_upto) + car[hh])
                    for hh, _ in heads]
            avs = [jnp.exp((zs[hh] - sps[hh]) + sufs[hh]) for hh, _ in heads]
            if diagonal:
                avs = [jnp.where(before, a, 0.0) for a in avs]
            dls = [avs[hh] * das[hh] for hh, _ in heads]
            prefixes = [_split2_dot(dls[hh], u_before) + car2[hh] for hh, _ in heads]
            dzs = []
            for hh, _ in heads:
                one_minus_beta = jnp.exp(-sps[hh])
                dz = (dls[hh] * one_minus_beta - prefixes[hh] * (1.0 - one_minus_beta)) * inv
                if diagonal:
                    dz = jnp.where(before, dz, 0.0)
                dzs.append(dz.astype(BF16))
            dqs = [jnp.dot(dzs[hh], k_ref[keys, hs], preferred_element_type=F32) for hh, hs in heads]
            dks = [lax.dot_general(dzs[hh], q_ref[:, hs], TN_DIMS, preferred_element_type=F32) for hh, hs in heads]
            dvs = [lax.dot_general(avs[hh].astype(BF16), dy_ref[:, hs], TN_DIMS, preferred_element_type=F32)
                   for hh, hs in heads]
            for hh, hs in heads:
                dqa[hh] += dqs[hh]
                dka[keys, hs] += dks[hh]
                dva[keys, hs] += dvs[hh]
                car[hh] += jnp.sum(loms[hh], axis=1, keepdims=True)
                car2[hh] += jnp.sum(dls[hh], axis=1, keepdims=True)

        def step(j, carry):
            block(j, False)
            return carry

        lax.fori_loop(0, i, step, 0)
        block(i, True)
        for hh, hs in enumerate(_head_slices()):
            dq_ref[:, hs] = dqa[hh].astype(BF16)

        @pl.when(i == nq - 1)
        def _():
            dk_ref[...] = dka[...].astype(BF16)
            dv_ref[...] = dva[...].astype(BF16)

        @pl.when(last)
        def _():
            for cp in _all_to_all_copies(p_ref, r_ref, send_sems, recv_sems, local_sem):
                cp.wait()

    q_spec, k_spec, v_spec = _att_specs(s, e, tq, 0)
    w = hp * HEAD_DIM
    blk_q = pl.BlockSpec((None, tq, w), lambda i, h, j: (i, j, h))
    blk_kv = pl.BlockSpec((None, s, w), lambda i, h, j: (i, 0, h))
    shp = jax.ShapeDtypeStruct((b, s, e), BF16)
    return pl.pallas_call(
        body, name="sb_bwd", grid=grid,
        in_specs=[q_spec, k_spec, v_spec, blk_q, blk_q, ANY_SPEC],
        out_specs=(blk_q, blk_kv, blk_kv, ANY_SPEC),
        out_shape=(shp, shp, shp, jax.ShapeDtypeStruct(partials.shape, partials.dtype)),
        scratch_shapes=[pltpu.VMEM((hp, tq, HEAD_DIM), F32), pltpu.VMEM((s, w), F32), pltpu.VMEM((s, w), F32),
                        pltpu.VMEM((hp, tq, 1), F32), pltpu.VMEM((hp, tq, 1), F32),
                        pltpu.SemaphoreType.DMA((N_DEV - 1,)), pltpu.SemaphoreType.DMA((N_DEV - 1,)),
                        pltpu.SemaphoreType.DMA],
        compiler_params=_params("arbitrary", "arbitrary", "arbitrary"),
    )(proj3, proj3, proj3, lom_total, dy, partials)


def _dil_near_tiles(tq):
    return (DIL_PAIRS[1][0] + tq - 1) // tq + 1


def _dil_fill_bias(bias_ref, sl_ref, tq):
    row, col = _iota2(tq)
    for hh in range(HEADS_PER_STEP):
        slope = sl_ref[hh, 0:1, 0:1]
        for d in range(_dil_near_tiles(tq) + 1):
            dist = d * tq + row - col
            cnt = jnp.zeros(dist.shape, jnp.int32)
            for window, dilation in DIL_PAIRS:
                cnt = cnt + (((dist & (dilation - 1)) == 0) & (dist <= window)).astype(jnp.int32)
            bias = jnp.where(cnt == 3, math.log(3.0), jnp.where(cnt == 2, math.log(2.0), 0.0))
            bias_ref[hh, d] = jnp.where((dist >= 0) & (cnt > 0), bias - slope * dist.astype(F32), NEG)


def _dil_fwd(proj3, e, slopes):
    b, s, _ = proj3.shape
    n_heads = e // HEAD_DIM
    hp = HEADS_PER_STEP
    tq = _tile(s, ATT_TQ)
    nq = s // tq
    inv = 1.0 / math.sqrt(HEAD_DIM)
    assert s <= DIL_PAIRS[2][0]

    def body(q_ref, k_ref, v_ref, sl_ref, y_ref, lse_ref, acc_ref, m_ref, l_ref, bias_ref, p_ref):
        i = pl.program_id(2)

        @pl.when(i == 0)
        def _():
            _dil_fill_bias(bias_ref, sl_ref, tq)

        acc_ref[...] = jnp.zeros_like(acc_ref)
        m_ref[...] = jnp.full_like(m_ref, NEG)
        l_ref[...] = jnp.zeros_like(l_ref)

        def step(it, carry):
            keys = pl.ds(pl.multiple_of((i - it) * tq, tq), tq)
            heads = list(enumerate(_head_slices()))
            near = _dil_near_tiles(tq)
            tile = jnp.minimum(it, near)
            beyond = jnp.maximum(it - near, 0).astype(F32) * float(tq)
            raws = [lax.dot_general(q_ref[:, hs], k_ref[keys, hs], NT_DIMS, preferred_element_type=F32)
                    for _, hs in heads]
            for hh, _ in heads:
                shift = sl_ref[hh, 0:1, 0:1] * beyond
                for r0 in range(0, tq, ATT_STRIP):
                    rows = slice(r0, r0 + ATT_STRIP)
                    sc = (raws[hh][rows] * inv + bias_ref[hh, tile, rows, :]) - shift
                    m_old = m_ref[hh, rows]
                    m_new = jnp.maximum(m_old, jnp.max(sc, axis=1, keepdims=True))
                    p = jnp.exp(sc - m_new)
                    alpha = jnp.exp(m_old - m_new)
                    l_ref[hh, rows] = alpha * l_ref[hh, rows] + (p[:, :tq // 2] + p[:, tq // 2:])
                    acc_ref[hh, rows] = alpha * acc_ref[hh, rows]
                    p_ref[hh, rows] = p.astype(BF16)
                    m_ref[hh, rows] = m_new
            pvs = [jnp.dot(p_ref[hh], v_ref[keys, hs], preferred_element_type=F32) for hh, hs in heads]
            for hh, _ in heads:
                acc_ref[hh] += pvs[hh]
            return carry

        lax.fori_loop(0, i + 1, step, 0)
        for hh, hs in enumerate(_head_slices()):
            l = jnp.sum(l_ref[hh], axis=1, keepdims=True)
            y_ref[:, hs] = acc_ref[hh] / l
            lse_ref[:, hs] = jnp.broadcast_to(m_ref[hh] + jnp.log(l), (tq, HEAD_DIM))

    q_spec, k_spec, v_spec = _att_specs(s, e, tq, 4 * n_heads)
    blk_q = pl.BlockSpec((None, tq, hp * HEAD_DIM), lambda i, h, j: (i, j, h))
    shp = jax.ShapeDtypeStruct((b, s, e), F32)
    return pl.pallas_call(
        body, name="dil_fwd", grid=(b, n_heads // hp, nq),
        in_specs=[q_spec, k_spec, v_spec, pl.BlockSpec((hp, 8, HEAD_DIM), lambda i, h, j: (h, 0, 0))],
        out_specs=(blk_q, blk_q), out_shape=(shp, shp),
        scratch_shapes=[pltpu.VMEM((hp, tq, HEAD_DIM), F32), pltpu.VMEM((hp, tq, 1), F32), pltpu.VMEM((hp, tq, tq // 2), F32),
                        pltpu.VMEM((hp, _dil_near_tiles(tq) + 1, tq, tq), F32), pltpu.VMEM((hp, tq, tq), BF16)],
        compiler_params=_params("parallel", "parallel", "arbitrary"),
    )(proj3, proj3, proj3, slopes)


def _dil_bwd(proj3, y, lse, dy, slopes):
    b, s, e = y.shape
    n_heads = e // HEAD_DIM
    hp = HEADS_PER_STEP
    tq = _tile(s, ATT_TQ)
    nq = s // tq
    inv = 1.0 / math.sqrt(HEAD_DIM)

    def body(q_ref, k_ref, v_ref, sl_ref, y_ref, lse_ref, dy_ref, dq_ref, dk_ref, dv_ref, dqa, dka, dva, bias_ref,
             p_ref, ds_ref):
        i = pl.program_id(2)

        @pl.when(i == 0)
        def _():
            dka[...] = jnp.zeros_like(dka)
            dva[...] = jnp.zeros_like(dva)
            _dil_fill_bias(bias_ref, sl_ref, tq)

        delta = [jnp.sum(dy_ref[:, hs].astype(F32) * y_ref[:, hs], axis=1, keepdims=True) for hs in _head_slices()]
        dqa[...] = jnp.zeros_like(dqa)

        def step(it, carry):
            keys = pl.ds(pl.multiple_of((i - it) * tq, tq), tq)
            heads = list(enumerate(_head_slices()))
            near = _dil_near_tiles(tq)
            tile = jnp.minimum(it, near)
            beyond = jnp.maximum(it - near, 0).astype(F32) * float(tq)
            raws = [lax.dot_general(q_ref[:, hs], k_ref[keys, hs], NT_DIMS, preferred_element_type=F32)
                    for _, hs in heads]
            dps = [lax.dot_general(dy_ref[:, hs], v_ref[keys, hs], NT_DIMS, preferred_element_type=F32) for _, hs in heads]
            for hh, _ in heads:
                shift = sl_ref[hh, 0:1, 0:1] * beyond
                for r0 in range(0, tq, ATT_STRIP):
                    rows = slice(r0, r0 + ATT_STRIP)
                    sc = (raws[hh][rows] * inv + bias_ref[hh, tile, rows, :]) - shift
                    p = jnp.exp(sc - lse_ref[rows, hh * HEAD_DIM:hh * HEAD_DIM + 1])
                    p_ref[hh, rows] = p.astype(BF16)
                    ds_ref[hh, rows] = ((p * (dps[hh][rows] - delta[hh][rows])) * inv).astype(BF16)
            dqs = [jnp.dot(ds_ref[hh], k_ref[keys, hs], preferred_element_type=F32) for hh, hs in heads]
            dks = [lax.dot_general(ds_ref[hh], q_ref[:, hs], TN_DIMS, preferred_element_type=F32) for hh, hs in heads]
            dvs = [lax.dot_general(p_ref[hh], dy_ref[:, hs], TN_DIMS, preferred_element_type=F32) for hh, hs in heads]
            for hh, hs in heads:
                dqa[hh] += dqs[hh]
                dka[keys, hs] += dks[hh]
                dva[keys, hs] += dvs[hh]
            return carry

        lax.fori_loop(0, i + 1, step, 0)
        for hh, hs in enumerate(_head_slices()):
            dq_ref[:, hs] = dqa[hh].astype(BF16)

        @pl.when(i == nq - 1)
        def _():
            dk_ref[...] = dka[...].astype(BF16)
            dv_ref[...] = dva[...].astype(BF16)

    q_spec, k_spec, v_spec = _att_specs(s, e, tq, 4 * n_heads)
    w = hp * HEAD_DIM
    blk_q = pl.BlockSpec((None, tq, w), lambda i, h, j: (i, j, h))
    blk_kv = pl.BlockSpec((None, s, w), lambda i, h, j: (i, 0, h))
    shp = jax.ShapeDtypeStruct((b, s, e), BF16)
    return pl.pallas_call(
        body, name="dil_bwd", grid=(b, n_heads // hp, nq),
        in_specs=[q_spec, k_spec, v_spec, pl.BlockSpec((hp, 8, HEAD_DIM), lambda i, h, j: (h, 0, 0)),
                  blk_q, blk_q, blk_q],
        out_specs=(blk_q, blk_kv, blk_kv), out_shape=(shp, shp, shp),
        scratch_shapes=[pltpu.VMEM((hp, tq, HEAD_DIM), F32), pltpu.VMEM((s, w), F32), pltpu.VMEM((s, w), F32),
                        pltpu.VMEM((hp, _dil_near_tiles(tq) + 1, tq, tq), F32),
                        pltpu.VMEM((hp, tq, tq), BF16), pltpu.VMEM((hp, tq, tq), BF16)],
        compiler_params=_params("parallel", "parallel", "arbitrary"),
    )(proj3, proj3, proj3, slopes, y, lse, dy)


def kernel(x, c, w_ada, b_ada, g_norm, w_in, g_sb, g_dil, w_out, g_final, loss_target, m_w_ada, m_b_ada, m_g_norm, m_w_in, m_g_sb, m_g_dil, m_w_out, m_g_final, v_w_ada, v_b_ada, v_g_norm, v_w_in, v_g_sb, v_g_dil, v_w_out, v_g_final):
    b, s, d = x.shape
    t = b * s
    e = w_in.shape[2]
    n_heads = e // HEAD_DIM
    na = w_ada.shape[2]
    r_out = w_out.shape[1]
    assert g_sb.shape[1] == e and g_dil.shape[1] == e and N_DEV * r_out == 2 * e and N_DEV * na == 3 * d
    assert b <= SMALL_ROWS and 3 * b + 3 <= 2 * SMALL_ROWS
    ix, iy, ic = _mesh_pos()
    me = 4 * ix + 2 * iy + ic

    c_all = _allgather_rows(jnp.pad(c, ((0, SMALL_ROWS - b), (0, 0))), "ag_c")
    b_own = lax.dynamic_slice(b_ada, (0, me * na), (1, na))
    mod_cols = _ada_fwd(c_all, w_ada[0], b_own)
    mod_all = _allgather_rows(mod_cols, "ag_mod").reshape(N_DEV, N_DEV, SMALL_ROWS, na)
    mod_own = lax.dynamic_slice(mod_all, (0, me, 0, 0), (N_DEV, 1, b, na))[:, 0]
    mod = mod_own.transpose(1, 0, 2).reshape(b, 1, 3 * d)
    shift, scale, gate = mod[:, :, :d], mod[:, :, d:2 * d], mod[:, :, 2 * d:]

    w_own = w_in[0].astype(BF16)

    h = _norm_mod(x, g_norm, scale, shift).reshape(t, d)
    proj, w_in3, w_out3 = _proj_with_allgather(
        h, w_own, jnp.stack(_unit_ids()).astype(jnp.int32), w_out[0].astype(BF16))
    w_out1 = w_out3.reshape(1, N_DEV * r_out, d)
    proj3 = proj.reshape(b, s, N_DEV * e)
    slopes = jnp.exp2(-ALIBI_MAX_BIAS * jnp.arange(1, n_heads + 1, dtype=F32) / n_heads)
    slopes = jnp.broadcast_to(slopes[:, None, None], (n_heads, 8, HEAD_DIM))
    y_sb, lom_total = _sb_fwd(proj3, e)
    y_dl, lse = _dil_fwd(proj3, e, slopes)
    yg = _gate_fwd(y_sb.reshape(t, e), y_dl.reshape(t, e), proj, g_sb, g_dil)
    out = _mm_nn(yg, w_out1, F32, "mm_out").reshape(b, s, d)
    loss_p, dx2, d_out, dgate, gg_final = _final_fwd_bwd(x, out, gate, g_final.reshape(1, d), loss_target)

    d_out2 = d_out.reshape(t, d)
    dyg = _mm_nt(d_out2, w_out1, BF16, "mm_dy")
    gw_out_p = _mm_tn(yg, d_out2, 1, BF16, "mm_gw_out").reshape(N_DEV, r_out, d)
    dy_sb, dy_dl, dz_sb, dz_dl, gg_sb, gg_dl = _gate_bwd(dyg, y_sb.reshape(t, e), y_dl.reshape(t, e), proj, g_sb, g_dil)
    dq_sb, dk_sb, dv_sb, recv_out = _sb_bwd(proj3, lom_total, dy_sb.reshape(b, s, e), gw_out_p)
    dq_dl, dk_dl, dv_dl = _dil_bwd(proj3, y_dl, lse, dy_dl.reshape(b, s, e), slopes)
    dproj = jnp.concatenate(
        [a.reshape(t, e) for a in (dq_sb, dk_sb, dv_sb, dz_sb, dq_dl, dk_dl, dv_dl, dz_dl)], axis=1)
    chips4 = [(ix, iy)] + _other_chips()
    to_sibling_core = jnp.stack([4 * px + 2 * py + (1 - ic) for px, py in chips4]).astype(jnp.int32)
    to_my_core = jnp.stack([4 * px + 2 * py + ic for px, py in chips4]).astype(jnp.int32)
    gw_in_sibs = _mm_tn_groups(to_sibling_core, h, dproj, N_DEV, "mm_gw_in_sibling")
    gw_in_mine, gw_in_sib = _mm_tn_groups(to_my_core, h, dproj, N_DEV, "mm_gw_in_mine", to_sibling=gw_in_sibs)
    gw_in_send = _chip_presum(gw_in_mine, gw_in_sib)
    dh, gw_in_recv = _mm_nt_with_chip_exchange(dproj, w_in3, F32, gw_in_send, "mm_dh")
    dh = dh.reshape(b, s, d)
    grad_x, dshift, dscale, gg_norm = _norm_bwd(x, dh, dx2, scale, g_norm)

    dmod = jnp.concatenate([dshift, dscale, dgate], axis=1).reshape(3 * b, d)
    pkg = jnp.concatenate([dmod, gg_norm, gg_final, jnp.concatenate([gg_sb, gg_dl], axis=1),
                           jnp.zeros((2 * SMALL_ROWS - 3 * b - 3, d), F32)], axis=0)
    pkg_all = _allgather_rows(pkg, "ag_small_grads").reshape(N_DEV, 2 * SMALL_ROWS, d)
    dmod_all = pkg_all[:, :3 * b].reshape(N_DEV * b, 3 * d)
    dmod_cols = lax.dynamic_slice(dmod_all, (0, me * na), (N_DEV * b, na))
    c_rows = c_all.reshape(N_DEV, SMALL_ROWS, d)[:, :b].reshape(N_DEV * b, d)
    g_w_ada, d_w_ada, nm_w_ada, nv_w_ada = _ada_bwd_adam(c_rows, dmod_cols, w_ada[0], m_w_ada[0], v_w_ada[0])

    def pack(b_ada_like, g_norm_like, g_sb_like, g_dil_like, g_final_like):
        return jnp.concatenate([b_ada_like.reshape(3, d), g_norm_like.reshape(1, d), g_final_like.reshape(1, d),
                                jnp.concatenate([g_sb_like, g_dil_like], axis=1).reshape(1, d),
                                jnp.zeros((2, d), F32)], axis=0)

    small = _small_adam(pkg_all, 3 * b, pack(b_ada, g_norm, g_sb, g_dil, g_final),
                        pack(m_b_ada, m_g_norm, m_g_sb, m_g_dil, m_g_final),
                        pack(v_b_ada, v_g_norm, v_g_sb, v_g_dil, v_g_final))

    def unpack(p):
        return (p[0:3].reshape(1, 3 * d), p[3:4], p[5:6, :e], p[5:6, e:], p[4])

    sm_g, sm_d, sm_m, sm_v = (unpack(p) for p in small)

    g_w_in, d_w_in, nm_w_in, nv_w_in = _adam_from_chip_sums(
        gw_in_mine, gw_in_sib, gw_in_recv, w_in[0], m_w_in[0], v_w_in[0], "adam_w_in")
    g_w_out, d_w_out, nm_w_out, nv_w_out = _adam_from_partials(recv_out, w_out[0], m_w_out[0], v_w_out[0], "adam_w_out")

    loss = lax.psum(loss_p[0, 0], ("x", "y", "c"))

    def weights(ada, small_parts, w_in_part, w_out_part):
        b_ada_p, g_norm_p, g_sb_p, g_dil_p, g_final_p = small_parts
        return (ada[None], b_ada_p, g_norm_p, w_in_part[None], g_sb_p, g_dil_p, w_out_part[None], g_final_p)

    return (loss, grad_x,
            *weights(g_w_ada, sm_g, g_w_in, g_w_out),
            *weights(d_w_ada, sm_d, d_w_in, d_w_out),
            *weights(nm_w_ada, sm_m, nm_w_in, nm_w_out),
            *weights(nv_w_ada, sm_v, nv_w_in, nv_w_out))
```

```python
import functools
import math

import jax
import jax.numpy as jnp
from jax import lax
from jax.experimental import pallas as pl
from jax.experimental.pallas import tpu as pltpu

F32 = jnp.float32
BF16 = jnp.bfloat16
MESH = pl.DeviceIdType.MESH

N_DEV = 8
HEAD_DIM = 128
EPS = 1e-6
ALIBI_MAX_BIAS = 8.0
DIL_PAIRS = ((128, 1), (512, 4), (2048, 16))
DIL_STEPS = 128
NEG = -1e30

ADAM_LR = 0.001
ADAM_B1 = 0.9
ADAM_B2 = 0.999
ADAM_EPS = 1e-08
ADAM_WD = 0.01
ADAM_STEP = 10

VMEM_LIMIT_BYTES = 56 * 1024 * 1024
SMALL_ROWS = 8

NT_DIMS = (((1,), (1,)), ((), ()))
TN_DIMS = (((0,), (0,)), ((), ()))


def _params(*semantics):
    return pltpu.CompilerParams(dimension_semantics=semantics, vmem_limit_bytes=VMEM_LIMIT_BYTES)


def _tile(n, want):
    t = min(n, want)
    assert n % t == 0, (n, want)
    return t


def _mesh_pos():
    return lax.axis_index("x"), lax.axis_index("y"), lax.axis_index("c")


def _allgather_rows(x_shard, name):
    m_per, n = x_shard.shape

    def body(x_ref, out_ref, send_sems, recv_sems, local_sem):
        x, y, c = _mesh_pos()
        me, sibling = (x, y, c), (x, y, 1 - c)
        chips = [(1 - x, y), (x, 1 - y), (1 - x, 1 - y)]

        def rows(px, py, pc):
            return out_ref.at[pl.ds((4 * px + 2 * py + pc) * m_per, m_per), :]

        def copy(k, block, to, src=None):
            return pltpu.make_async_remote_copy(
                src_ref=rows(*block) if src is None else src, dst_ref=rows(*block),
                send_sem=send_sems.at[k], recv_sem=recv_sems.at[k], device_id=to, device_id_type=MESH)

        mine = pltpu.make_async_copy(x_ref, rows(*me), local_sem)
        mine.start()
        first = [copy(0, me, sibling, src=x_ref)]
        first += [copy(1 + j, me, (*chip, c), src=x_ref) for j, chip in enumerate(chips)]
        for cp in first:
            cp.start()
        passed = [copy(4 + j, (*chip, c), sibling) for j, chip in enumerate(chips)]
        for j, chip in enumerate(chips):
            copy(1 + j, (*chip, c), me).wait_recv()
            passed[j].start()
        copy(0, sibling, me).wait_recv()
        for j, chip in enumerate(chips):
            copy(4 + j, (*chip, 1 - c), me).wait_recv()
        for cp in first + passed:
            cp.wait_send()
        mine.wait()

    return pl.pallas_call(
        body, name=name,
        out_shape=jax.ShapeDtypeStruct((N_DEV * m_per, n), x_shard.dtype),
        in_specs=[pl.BlockSpec(memory_space=pltpu.VMEM)],
        out_specs=pl.BlockSpec(memory_space=pltpu.VMEM),
        scratch_shapes=[pltpu.SemaphoreType.DMA((7,)), pltpu.SemaphoreType.DMA((7,)), pltpu.SemaphoreType.DMA],
    )(x_shard)


def _two_level_gather(w_ref, out_ref, send_sems, recv_sems, local_sem):
    x, y, c = _mesh_pos()
    me, sibling = (x, y, c), (x, y, 1 - c)
    chips = [(1 - x, y), (x, 1 - y), (1 - x, 1 - y)]

    def copy(k, block, to, src=None):
        dst = out_ref.at[4 * block[0] + 2 * block[1] + block[2]]
        return pltpu.make_async_remote_copy(
            src_ref=dst if src is None else src, dst_ref=dst,
            send_sem=send_sems.at[k], recv_sem=recv_sems.at[k], device_id=to, device_id_type=MESH)

    mine = pltpu.make_async_copy(w_ref, out_ref.at[4 * x + 2 * y + c], local_sem)
    to_sibling = copy(0, me, sibling, src=w_ref)
    to_chips = [copy(1 + j, me, (*chip, c), src=w_ref) for j, chip in enumerate(chips)]
    passed = [copy(4 + j, (*chip, c), sibling) for j, chip in enumerate(chips)]

    def early():
        mine.start()
        to_sibling.start()

    def ici():
        for cp in to_chips:
            cp.start()

    def finish():
        for j, chip in enumerate(chips):
            copy(1 + j, (*chip, c), me).wait_recv()
            passed[j].start()
        copy(0, sibling, me).wait_recv()
        for j, chip in enumerate(chips):
            copy(4 + j, (*chip, 1 - c), me).wait_recv()
        for cp in [to_sibling] + to_chips + passed:
            cp.wait_send()
        mine.wait()

    return early, ici, finish


PROJ_HALVES = 2


def _unit_schedule():
    assert PROJ_HALVES == 2
    sched = [("own", None, hf) for hf in range(PROJ_HALVES)] + [("sib", None, hf) for hf in range(PROJ_HALVES)]
    for rnd in ([(0, 0), (1, 1)], [(2, 0), (2, 1)], [(0, 1), (1, 0)]):
        sched += [("direct", j, hf) for j, hf in rnd] + [("fwd", j, hf) for j, hf in rnd]
    return sched


def _unit_ids():
    x, y, c = _mesh_pos()
    chips = _other_chips()
    ids = []
    for kind, j, hf in _unit_schedule():
        px, py = (x, y) if j is None else chips[j]
        pc = c if kind in ("own", "direct") else 1 - c
        ids.append(PROJ_HALVES * (4 * px + 2 * py + pc) + hf)
    return ids


def _proj_with_allgather(h, w_own, order, w_later):
    t, d = h.shape
    nh = PROJ_HALVES
    u = w_own.shape[1] // nh
    sched = _unit_schedule()
    n_units = len(sched)
    pos = {entry: p for p, entry in enumerate(sched)}
    tm = _tile(t, MM_TM)
    m_tiles = t // tm
    prep_m = max(m_tiles - 2, 0)
    grid = (n_units, m_tiles)

    later_ici_pos = pos[("direct", 2, 0)]

    def body(order_ref, h_ref, wown_ref, later_ref, proj_ref, w3_ref, later3_ref, bbuf, bsems, send_sems, recv_sems,
             local_sems, later_send_sems, later_recv_sems, later_local_sem):
        n, m = pl.program_id(0), pl.program_id(1)
        x, y, c = _mesh_pos()
        chips = _other_chips()
        sibling = (x, y, 1 - c)
        later_early, later_ici, later_finish = _two_level_gather(
            later_ref, later3_ref, later_send_sems, later_recv_sems, later_local_sem)

        def unit(p):
            return w3_ref.at[order_ref[p]]

        def arrival(p):
            return pltpu.make_async_remote_copy(
                src_ref=unit(p), dst_ref=unit(p), send_sem=send_sems.at[0], recv_sem=recv_sems.at[p - nh],
                device_id=sibling, device_id_type=MESH)

        def send(k, src, p_here, p_there, to):
            return pltpu.make_async_remote_copy(
                src_ref=src, dst_ref=unit(p_here), send_sem=send_sems.at[k], recv_sem=recv_sems.at[p_there - nh],
                device_id=to, device_id_type=MESH)

        copies = []

        def out(src, p_here, p_there, to):
            copies.append(send(len(copies), src, p_here, p_there, to))
            return copies[-1]

        def own_unit(hf):
            return wown_ref.at[:, pl.ds(hf * u, u)]

        def own(hf, p_there, to):
            return out(own_unit(hf), pos[("own", None, hf)], p_there, to)

        x_nbr, y_nbr = (*chips[0], c), (*chips[1], c)
        first = [own(hf, pos[("sib", None, hf)], sibling) for hf in range(nh)]
        first += [own(0, pos[("direct", 0, 0)], x_nbr), own(1, pos[("direct", 1, 1)], y_nbr)]
        after = {}
        for kind, j, hf in sched:
            if kind == "direct":
                p = pos[(kind, j, hf)]
                after[p] = [out(unit(p), p, pos[("fwd", j, hf)], sibling)]
        p = pos[("direct", 0, 0)]
        after[p] += [out(unit(p), p, pos[("direct", 2, 0)], y_nbr), own(0, pos[("direct", 1, 0)], y_nbr)]
        p = pos[("direct", 1, 1)]
        after[p] += [out(unit(p), p, pos[("direct", 2, 1)], x_nbr), own(1, pos[("direct", 0, 1)], x_nbr)]
        locals_ = [pltpu.make_async_copy(own_unit(hf), unit(pos[("own", None, hf)]), local_sems.at[hf])
                   for hf in range(nh)]

        def fetch(p):
            src = own_unit(sched[p][2]) if sched[p][0] == "own" else unit(p)
            return pltpu.make_async_copy(src, bbuf.at[p % 2], bsems.at[p % 2])

        @pl.when((n == 0) & (m == 0))
        def _():
            for cp in locals_ + first:
                cp.start()
            fetch(0).start()
            later_early()

        for p in range(n_units):
            @pl.when((n == p) & (m == 0))
            def _(p=p):
                fetch(p).wait()

            if p + 1 < n_units:
                @pl.when((n == p) & (m == prep_m))
                def _(p=p):
                    if sched[p + 1][0] != "own":
                        arrival(p + 1).wait_recv()
                    for cp in after.get(p + 1, []):
                        cp.start()
                    fetch(p + 1).start()
                    if p + 1 == later_ici_pos:
                        later_ici()

        proj_ref[...] = jnp.dot(h_ref[...], bbuf[n % 2], preferred_element_type=F32).astype(proj_ref.dtype)

        @pl.when((n == n_units - 1) & (m == m_tiles - 1))
        def _():
            for cp in copies:
                cp.wait_send()
            for cp in locals_:
                cp.wait()
            later_finish()

    n_out = len(sched) - nh
    return pl.pallas_call(
        body, name="mm_proj_allgather",
        grid_spec=pltpu.PrefetchScalarGridSpec(
            num_scalar_prefetch=1, grid=grid,
            in_specs=[pl.BlockSpec((tm, d), lambda n, m, order: (m, 0)), ANY_SPEC, ANY_SPEC],
            out_specs=(pl.BlockSpec((tm, u), lambda n, m, order: (m, order[n])), ANY_SPEC, ANY_SPEC),
            scratch_shapes=[pltpu.VMEM((2, d, u), h.dtype), pltpu.SemaphoreType.DMA((2,)),
                            pltpu.SemaphoreType.DMA((n_out,)), pltpu.SemaphoreType.DMA((n_out,)),
                            pltpu.SemaphoreType.DMA((nh,)),
                            pltpu.SemaphoreType.DMA((N_DEV - 1,)), pltpu.SemaphoreType.DMA((N_DEV - 1,)),
                            pltpu.SemaphoreType.DMA]),
        out_shape=(jax.ShapeDtypeStruct((t, n_units * u), h.dtype), jax.ShapeDtypeStruct((n_units, d, u), h.dtype),
                   jax.ShapeDtypeStruct((N_DEV,) + w_later.shape, w_later.dtype)),
        compiler_params=_params("arbitrary", "arbitrary"),
    )(order, h, w_own, w_later)


ANY_SPEC = pl.BlockSpec(memory_space=pl.ANY)


def _grid_first_last(grid):
    ids = [pl.program_id(a) for a in range(len(grid))]
    first = functools.reduce(lambda p, q: p & q, [i == 0 for i in ids])
    last = functools.reduce(lambda p, q: p & q, [i == n - 1 for i, n in zip(ids, grid)])
    return first, last


def _all_to_all_copies(src, dst, send_sems, recv_sems, local_sem):
    x, y, c = _mesh_pos()
    my = 4 * x + 2 * y + c
    copies = [pltpu.make_async_copy(src.at[my], dst.at[my], local_sem)]
    for d in range(1, N_DEV):
        px = 1 - x if d & 4 else x
        py = 1 - y if d & 2 else y
        pc = 1 - c if d & 1 else c
        copies.append(pltpu.make_async_remote_copy(
            src_ref=src.at[4 * px + 2 * py + pc], dst_ref=dst.at[my],
            send_sem=send_sems.at[d - 1], recv_sem=recv_sems.at[d - 1],
            device_id=(px, py, pc), device_id_type=MESH))
    return copies


def _other_chips():
    x, y, _ = _mesh_pos()
    return [(1 - x, y), (x, 1 - y), (1 - x, 1 - y)]


def _same_core_copies(src, dst, send_sems, recv_sems):
    c = lax.axis_index("c")
    return [pltpu.make_async_remote_copy(
        src_ref=src.at[j], dst_ref=dst.at[j], send_sem=send_sems.at[j], recv_sem=recv_sems.at[j],
        device_id=(*chip, c), device_id_type=MESH) for j, chip in enumerate(_other_chips())]


def _sibling_copies(src, dst, send_sems, recv_sems):
    x, y, c = _mesh_pos()
    return [pltpu.make_async_remote_copy(
        src_ref=src.at[j], dst_ref=dst.at[j], send_sem=send_sems.at[j], recv_sem=recv_sems.at[j],
        device_id=(x, y, 1 - c), device_id_type=MESH) for j in range(src.shape[0])]


def _chip_presum(mine, from_sibling):
    _, r, cdim = mine.shape
    tr = _tile(r, 256)

    def body(p_ref, s_ref, o_ref):
        o_ref[...] = (p_ref[...].astype(F32) + s_ref[...].astype(F32)).astype(o_ref.dtype)

    blk = pl.BlockSpec((None, tr, cdim), lambda j, i: (1 + j, i, 0))
    return pl.pallas_call(
        body, name="chip_presum", grid=(3, r // tr), in_specs=[blk, blk],
        out_specs=pl.BlockSpec((None, tr, cdim), lambda j, i: (j, i, 0)),
        out_shape=jax.ShapeDtypeStruct((3, r, cdim), mine.dtype),
        compiler_params=_params("parallel", "parallel"),
    )(mine, from_sibling)


def _mm_call(a, b, dims, nk, grid, a_spec, b_spec, o_spec, out_shape, acc_shape, name):
    def body(a_ref, b_ref, o_ref, acc_ref):
        k = pl.program_id(2)

        @pl.when(k == 0)
        def _():
            acc_ref[...] = jnp.zeros_like(acc_ref)

        acc_ref[...] += lax.dot_general(a_ref[...], b_ref[...], dims, preferred_element_type=F32)

        @pl.when(k == nk - 1)
        def _():
            o_ref[...] = acc_ref[...].astype(o_ref.dtype)

    return pl.pallas_call(
        body, name=name, grid=grid, in_specs=[a_spec, b_spec], out_specs=o_spec, out_shape=out_shape,
        scratch_shapes=[pltpu.VMEM(acc_shape, F32)],
        compiler_params=_params("parallel", "parallel", "arbitrary"),
    )(a, b)


MM_TM, MM_TN, MM_TK = 1024, 2048, 1024


def _mm_nn(a, b3, out_dtype, name):
    m, kk = a.shape
    g, _, nb = b3.shape
    tm, tn, tk = _tile(m, MM_TM), _tile(nb, MM_TN), _tile(kk, MM_TK)
    npb = nb // tn
    return _mm_call(
        a, b3, (((1,), (0,)), ((), ())), kk // tk, (m // tm, g * npb, kk // tk),
        pl.BlockSpec((tm, tk), lambda i, j, k: (i, k)),
        pl.BlockSpec((None, tk, tn), lambda i, j, k: (j // npb, k, j % npb)),
        pl.BlockSpec((tm, tn), lambda i, j, k: (i, j)),
        jax.ShapeDtypeStruct((m, g * nb), out_dtype), (tm, tn), name)


def _mm_nt(a, b3, out_dtype, name):
    m, kk = a.shape
    g, n, kb = b3.shape
    tm, tn, tk = _tile(m, MM_TM), _tile(n, MM_TN), _tile(kb, MM_TK)
    kpb = kb // tk
    return _mm_call(
        a, b3, NT_DIMS, kk // tk, (m // tm, n // tn, kk // tk),
        pl.BlockSpec((tm, tk), lambda i, j, k: (i, k)),
        pl.BlockSpec((None, tn, tk), lambda i, j, k: (k // kpb, j, k % kpb)),
        pl.BlockSpec((tm, tn), lambda i, j, k: (i, j)),
        jax.ShapeDtypeStruct((m, n), out_dtype), (tm, tn), name)


def _mm_nt_with_chip_exchange(a, b3, out_dtype, send3, name):
    m, kk = a.shape
    g, n, kb = b3.shape
    tm, tn, tk = _tile(m, MM_TM), _tile(n, MM_TN), _tile(kb, MM_TK)
    kpb = kb // tk
    grid = (m // tm, n // tn, kk // tk)

    def body(a_ref, b_ref, s_ref, o_ref, r_ref, acc_ref, send_sems, recv_sems):
        first, last = _grid_first_last(grid)
        k = pl.program_id(2)

        @pl.when(first)
        def _():
            for cp in _same_core_copies(s_ref, r_ref, send_sems, recv_sems):
                cp.start()

        @pl.when(k == 0)
        def _():
            acc_ref[...] = jnp.zeros_like(acc_ref)

        acc_ref[...] += lax.dot_general(a_ref[...], b_ref[...], NT_DIMS, preferred_element_type=F32)

        @pl.when(k == grid[2] - 1)
        def _():
            o_ref[...] = acc_ref[...].astype(o_ref.dtype)

        @pl.when(last)
        def _():
            for cp in _same_core_copies(s_ref, r_ref, send_sems, recv_sems):
                cp.wait()

    return pl.pallas_call(
        body, name=name, grid=grid,
        in_specs=[pl.BlockSpec((tm, tk), lambda i, j, k: (i, k)),
                  pl.BlockSpec((None, tn, tk), lambda i, j, k: (k // kpb, j, k % kpb)), ANY_SPEC],
        out_specs=(pl.BlockSpec((tm, tn), lambda i, j, k: (i, j)), ANY_SPEC),
        out_shape=(jax.ShapeDtypeStruct((m, n), out_dtype), jax.ShapeDtypeStruct(send3.shape, send3.dtype)),
        scratch_shapes=[pltpu.VMEM((tm, tn), F32), pltpu.SemaphoreType.DMA((3,)), pltpu.SemaphoreType.DMA((3,))],
        compiler_params=_params("arbitrary", "arbitrary", "arbitrary"),
    )(a, b3, send3)


def _mm_tn_groups(groups, a, b, n_groups, name, to_sibling=None):
    t, m = a.shape
    nb = b.shape[1] // n_groups
    ng = groups.shape[0]
    tm, tk = _tile(m, MM_TM), _tile(t, MM_TK)
    grid = (m // tm, ng, t // tk)
    carry = to_sibling is not None

    def body(groups_ref, a_ref, b_ref, *rest):
        if carry:
            s_ref, o_ref, r_ref, acc_ref, send_sems, recv_sems = rest
        else:
            o_ref, acc_ref = rest
        k = pl.program_id(2)
        first, last = _grid_first_last(grid)

        if carry:
            @pl.when(first)
            def _():
                for cp in _sibling_copies(s_ref, r_ref, send_sems, recv_sems):
                    cp.start()

        @pl.when(k == 0)
        def _():
            acc_ref[...] = jnp.zeros_like(acc_ref)

        acc_ref[...] += lax.dot_general(a_ref[...], b_ref[...], TN_DIMS, preferred_element_type=F32)

        @pl.when(k == grid[2] - 1)
        def _():
            o_ref[...] = acc_ref[...].astype(o_ref.dtype)

        if carry:
            @pl.when(last)
            def _():
                for cp in _sibling_copies(s_ref, r_ref, send_sems, recv_sems):
                    cp.wait()

    shp = jax.ShapeDtypeStruct((ng, m, nb), b.dtype)
    o_spec = pl.BlockSpec((None, tm, nb), lambda i, j, k, grp: (j, i, 0))
    return pl.pallas_call(
        body, name=name,
        grid_spec=pltpu.PrefetchScalarGridSpec(
            num_scalar_prefetch=1, grid=grid,
            in_specs=[pl.BlockSpec((tk, tm), lambda i, j, k, grp: (k, i)),
                      pl.BlockSpec((tk, nb), lambda i, j, k, grp: (k, grp[j]))] + ([ANY_SPEC] if carry else []),
            out_specs=(o_spec, ANY_SPEC) if carry else o_spec,
            scratch_shapes=[pltpu.VMEM((tm, nb), F32)] + (
                [pltpu.SemaphoreType.DMA((ng,)), pltpu.SemaphoreType.DMA((ng,))] if carry else [])),
        out_shape=(shp, shp) if carry else shp,
        compiler_params=_params("arbitrary", "arbitrary", "arbitrary"),
    )(*((groups, a, b, to_sibling) if carry else (groups, a, b)))


def _mm_tn(a, b, g, out_dtype, name):
    t, m = a.shape
    nb = b.shape[1] // g
    tm, tn, tk = _tile(m, MM_TM), _tile(nb, MM_TN), _tile(t, MM_TK)
    npb = nb // tn
    return _mm_call(
        a, b, TN_DIMS, t // tk, (m // tm, g * npb, t // tk),
        pl.BlockSpec((tk, tm), lambda i, j, k: (k, i)),
        pl.BlockSpec((tk, tn), lambda i, j, k: (k, j)),
        pl.BlockSpec((None, tm, tn), lambda i, j, k: (j // npb, i, j % npb)),
        jax.ShapeDtypeStruct((g, m, nb), out_dtype), (tm, tn), name)


def _silu(z):
    return z * jax.nn.sigmoid(z)


def _ada_fwd(c_all, w_shard, b_own):
    r, d = c_all.shape
    na = w_shard.shape[1]
    tk = _tile(d, 512)
    nk = d // tk

    def body(c_ref, w_ref, b_ref, o_ref):
        k = pl.program_id(0)

        @pl.when(k == 0)
        def _():
            o_ref[...] = jnp.zeros_like(o_ref) + b_ref[...]

        cs = _silu(c_ref[...]).astype(BF16)
        o_ref[...] += jnp.dot(cs, w_ref[...].astype(BF16), preferred_element_type=F32)

    return pl.pallas_call(
        body, name="ada_fwd", grid=(nk,),
        in_specs=[pl.BlockSpec((r, tk), lambda k: (0, k)), pl.BlockSpec((tk, na), lambda k: (k, 0)),
                  pl.BlockSpec((1, na), lambda k: (0, 0))],
        out_specs=pl.BlockSpec((r, na), lambda k: (0, 0)),
        out_shape=jax.ShapeDtypeStruct((r, na), F32),
        compiler_params=_params("arbitrary"),
    )(c_all, w_shard, b_own)


def _adam(w, g, m, v):
    nm = ADAM_B1 * m + (1.0 - ADAM_B1) * g
    nv = ADAM_B2 * v + (1.0 - ADAM_B2) * (g * g)
    m_hat = nm / (1.0 - ADAM_B1 ** ADAM_STEP)
    v_hat = nv / (1.0 - ADAM_B2 ** ADAM_STEP)
    delta = -ADAM_LR * (m_hat / (jnp.sqrt(v_hat) + ADAM_EPS) + ADAM_WD * w)
    return delta, nm, nv


def _ada_bwd_adam(c_rows, dmod_cols, w, m, v):
    bg, d = c_rows.shape
    na = w.shape[1]
    tr = _tile(d, 256)

    def body(c_ref, dm_ref, w_ref, m_ref, v_ref, g_ref, d_ref, nm_ref, nv_ref):
        cs = _silu(c_ref[...]).astype(BF16)
        g = lax.dot_general(cs, dm_ref[...].astype(BF16), TN_DIMS, preferred_element_type=F32)
        delta, nm, nv = _adam(w_ref[...], g, m_ref[...], v_ref[...])
        g_ref[...] = g
        d_ref[...] = delta
        nm_ref[...] = nm
        nv_ref[...] = nv

    blk = pl.BlockSpec((tr, na), lambda i: (i, 0))
    shp = jax.ShapeDtypeStruct((d, na), F32)
    return pl.pallas_call(
        body, name="ada_bwd_adam", grid=(d // tr,),
        in_specs=[pl.BlockSpec((bg, tr), lambda i: (0, i)), pl.BlockSpec((bg, na), lambda i: (0, 0)), blk, blk, blk],
        out_specs=(blk, blk, blk, blk), out_shape=(shp, shp, shp, shp),
        compiler_params=_params("parallel"),
    )(c_rows, dmod_cols, w, m, v)


def _small_adam(pkg_all, n_batch_rows, w, m, v):
    d = w.shape[1]

    def body(p_ref, w_ref, m_ref, v_ref, g_ref, d_ref, nm_ref, nv_ref):
        for part in range(3):
            acc = jnp.zeros((1, d), F32)
            for dev in range(N_DEV):
                for b in range(n_batch_rows // 3):
                    acc = acc + p_ref[dev, 3 * b + part:3 * b + part + 1, :]
            g_ref[part:part + 1, :] = acc
        for rrow in range(3):
            acc = jnp.zeros((1, d), F32)
            for dev in range(N_DEV):
                acc = acc + p_ref[dev, n_batch_rows + rrow:n_batch_rows + rrow + 1, :]
            g_ref[3 + rrow:4 + rrow, :] = acc
        g_ref[6:8, :] = jnp.zeros((2, d), F32)
        g = g_ref[...]
        delta, nm, nv = _adam(w_ref[...], g, m_ref[...], v_ref[...])
        d_ref[...] = delta
        nm_ref[...] = nm
        nv_ref[...] = nv

    vm = pl.BlockSpec(memory_space=pltpu.VMEM)
    shp = jax.ShapeDtypeStruct((SMALL_ROWS, d), F32)
    return pl.pallas_call(
        body, name="small_adam", in_specs=[vm, vm, vm, vm], out_specs=(vm, vm, vm, vm),
        out_shape=(shp, shp, shp, shp),
    )(pkg_all, w, m, v)


def _adam_from_chip_sums(mine, from_sibling, from_chips, w, m, v, name):
    _, r, c = mine.shape
    tr = _tile(r, 128)

    def body(p_ref, s_ref, f_ref, w_ref, m_ref, v_ref, g_ref, d_ref, nm_ref, nv_ref):
        g = p_ref[...].astype(F32) + s_ref[...].astype(F32)
        for j in range(3):
            g = g + f_ref[j].astype(F32)
        delta, nm, nv = _adam(w_ref[...], g, m_ref[...], v_ref[...])
        g_ref[...] = g
        d_ref[...] = delta
        nm_ref[...] = nm
        nv_ref[...] = nv

    blk = pl.BlockSpec((tr, c), lambda i: (i, 0))
    slot0 = pl.BlockSpec((None, tr, c), lambda i: (0, i, 0))
    shp = jax.ShapeDtypeStruct((r, c), F32)
    return pl.pallas_call(
        body, name=name, grid=(r // tr,),
        in_specs=[slot0, slot0, pl.BlockSpec((3, tr, c), lambda i: (0, i, 0)), blk, blk, blk],
        out_specs=(blk, blk, blk, blk), out_shape=(shp, shp, shp, shp),
        compiler_params=_params("parallel"),
    )(mine, from_sibling, from_chips, w, m, v)


def _adam_from_partials(recv, w, m, v, name):
    _, r, c = recv.shape
    tr = _tile(r, 128)

    def body(p_ref, w_ref, m_ref, v_ref, g_ref, d_ref, nm_ref, nv_ref):
        g = p_ref[0].astype(F32)
        for dev in range(1, N_DEV):
            g = g + p_ref[dev].astype(F32)
        delta, nm, nv = _adam(w_ref[...], g, m_ref[...], v_ref[...])
        g_ref[...] = g
        d_ref[...] = delta
        nm_ref[...] = nm
        nv_ref[...] = nv

    blk = pl.BlockSpec((tr, c), lambda i: (i, 0))
    shp = jax.ShapeDtypeStruct((r, c), F32)
    return pl.pallas_call(
        body, name=name, grid=(r // tr,),
        in_specs=[pl.BlockSpec((N_DEV, tr, c), lambda i: (0, i, 0)), blk, blk, blk],
        out_specs=(blk, blk, blk, blk), out_shape=(shp, shp, shp, shp),
        compiler_params=_params("parallel"),
    )(recv, w, m, v)


def _norm_mod(x, g_norm, scale, shift):
    b, s, d = x.shape
    ts = _tile(s, 256)

    def body(x_ref, g_ref, sc_ref, sh_ref, h_ref):
        xv = x_ref[...]
        r = lax.rsqrt(jnp.mean(xv * xv, axis=-1, keepdims=True) + EPS)
        xn = (xv * r) * g_ref[...]
        h_ref[...] = (xn * (1.0 + sc_ref[...]) + sh_ref[...]).astype(BF16)

    tok = pl.BlockSpec((None, ts, d), lambda i, j: (i, j, 0))
    per_b = pl.BlockSpec((None, 1, d), lambda i, j: (i, 0, 0))
    return pl.pallas_call(
        body, name="norm_mod", grid=(b, s // ts),
        in_specs=[tok, pl.BlockSpec((1, d), lambda i, j: (0, 0)), per_b, per_b],
        out_specs=tok, out_shape=jax.ShapeDtypeStruct((b, s, d), BF16),
        compiler_params=_params("parallel", "parallel"),
    )(x, g_norm, scale, shift)


def _final_fwd_bwd(x, out, gate, g_final, target):
    b, s, d = x.shape
    ts = _tile(s, 256)

    def body(x_ref, o_ref, gt_ref, g_ref, t_ref, loss_ref, dx2_ref, dout_ref, dgate_ref, gg_ref):
        i, j = pl.program_id(0), pl.program_id(1)

        @pl.when((i == 0) & (j == 0))
        def _():
            loss_ref[...] = jnp.zeros_like(loss_ref)
            gg_ref[...] = jnp.zeros_like(gg_ref)

        @pl.when(j == 0)
        def _():
            dgate_ref[...] = jnp.zeros_like(dgate_ref)

        ov = o_ref[...]
        gt = gt_ref[...]
        x2 = x_ref[...] + gt * ov
        r = lax.rsqrt(jnp.mean(x2 * x2, axis=-1, keepdims=True) + EPS)
        xh = x2 * r
        err = xh * g_ref[...] - t_ref[...]
        loss_ref[...] += 0.5 * jnp.sum(jnp.mean(err * err, axis=-1, keepdims=True), axis=0, keepdims=True)
        dfin = err * (1.0 / d)
        gg_ref[...] += jnp.sum(dfin * xh, axis=0, keepdims=True)
        dxh = dfin * g_ref[...]
        dx2 = r * (dxh - xh * jnp.mean(dxh * xh, axis=-1, keepdims=True))
        dx2_ref[...] = dx2
        dout_ref[...] = (gt * dx2).astype(BF16)
        dgate_ref[...] += jnp.sum(dx2 * ov, axis=0, keepdims=True)

    tok = pl.BlockSpec((None, ts, d), lambda i, j: (i, j, 0))
    per_b = pl.BlockSpec((None, 1, d), lambda i, j: (i, 0, 0))
    vec = pl.BlockSpec((1, d), lambda i, j: (0, 0))
    return pl.pallas_call(
        body, name="final_fwd_bwd", grid=(b, s // ts),
        in_specs=[tok, tok, per_b, vec, tok],
        out_specs=(pl.BlockSpec((8, 128), lambda i, j: (0, 0)), tok, tok, per_b, vec),
        out_shape=(jax.ShapeDtypeStruct((8, 128), F32), jax.ShapeDtypeStruct((b, s, d), F32),
                   jax.ShapeDtypeStruct((b, s, d), BF16), jax.ShapeDtypeStruct((b, 1, d), F32),
                   jax.ShapeDtypeStruct((1, d), F32)),
        compiler_params=_params("arbitrary", "arbitrary"),
    )(x, out, gate, g_final, target)


def _norm_bwd(x, dh, dx2, scale, g_norm):
    b, s, d = x.shape
    ts = _tile(s, 256)

    def body(x_ref, dh_ref, dx2_ref, sc_ref, g_ref, gx_ref, dsh_ref, dsc_ref, gg_ref):
        i, j = pl.program_id(0), pl.program_id(1)

        @pl.when((i == 0) & (j == 0))
        def _():
            gg_ref[...] = jnp.zeros_like(gg_ref)

        @pl.when(j == 0)
        def _():
            dsh_ref[...] = jnp.zeros_like(dsh_ref)
            dsc_ref[...] = jnp.zeros_like(dsc_ref)

        xv = x_ref[...]
        dhv = dh_ref[...]
        r = lax.rsqrt(jnp.mean(xv * xv, axis=-1, keepdims=True) + EPS)
        xh = xv * r
        xn = xh * g_ref[...]
        dsh_ref[...] += jnp.sum(dhv, axis=0, keepdims=True)
        dsc_ref[...] += jnp.sum(dhv * xn, axis=0, keepdims=True)
        dxn = dhv * (1.0 + sc_ref[...])
        gg_ref[...] += jnp.sum(dxn * xh, axis=0, keepdims=True)
        dxh = dxn * g_ref[...]
        gx_ref[...] = dx2_ref[...] + r * (dxh - xh * jnp.mean(dxh * xh, axis=-1, keepdims=True))

    tok = pl.BlockSpec((None, ts, d), lambda i, j: (i, j, 0))
    per_b = pl.BlockSpec((None, 1, d), lambda i, j: (i, 0, 0))
    vec = pl.BlockSpec((1, d), lambda i, j: (0, 0))
    return pl.pallas_call(
        body, name="norm_bwd", grid=(b, s // ts),
        in_specs=[tok, tok, tok, per_b, vec],
        out_specs=(tok, per_b, per_b, vec),
        out_shape=(jax.ShapeDtypeStruct((b, s, d), F32), jax.ShapeDtypeStruct((b, 1, d), F32),
                   jax.ShapeDtypeStruct((b, 1, d), F32), jax.ShapeDtypeStruct((1, d), F32)),
        compiler_params=_params("arbitrary", "arbitrary"),
    )(x, dh, dx2, scale, g_norm)


def _gate_fwd(y_sb, y_dl, proj, g_sb, g_dl):
    t, e = y_sb.shape
    n_heads = e // HEAD_DIM
    tt = _tile(t, 256)

    def body(ys_ref, yd_ref, zs_ref, zd_ref, gs_ref, gd_ref, o_ref):
        for grp, (y_ref, z_ref, g_ref) in enumerate(((ys_ref, zs_ref, gs_ref), (yd_ref, zd_ref, gd_ref))):
            for h in range(n_heads):
                sl = slice(h * HEAD_DIM, (h + 1) * HEAD_DIM)
                y = y_ref[:, sl]
                r = lax.rsqrt(jnp.mean(y * y, axis=-1, keepdims=True) + EPS)
                yn = (y * r) * g_ref[:, sl]
                z = z_ref[:, sl].astype(F32)
                o_ref[:, grp * e + h * HEAD_DIM:grp * e + (h + 1) * HEAD_DIM] = (yn * _silu(z)).astype(BF16)

    yblk = pl.BlockSpec((tt, e), lambda i: (i, 0))
    gblk = pl.BlockSpec((1, e), lambda i: (0, 0))
    return pl.pallas_call(
        body, name="gate_fwd", grid=(t // tt,),
        in_specs=[yblk, yblk, pl.BlockSpec((tt, e), lambda i: (i, 3)), pl.BlockSpec((tt, e), lambda i: (i, 7)),
                  gblk, gblk],
        out_specs=pl.BlockSpec((tt, 2 * e), lambda i: (i, 0)),
        out_shape=jax.ShapeDtypeStruct((t, 2 * e), BF16),
        compiler_params=_params("parallel"),
    )(y_sb, y_dl, proj, proj, g_sb, g_dl)


def _gate_bwd(dyg, y_sb, y_dl, proj, g_sb, g_dl):
    t, e = y_sb.shape
    n_heads = e // HEAD_DIM
    tt = _tile(t, 256)

    def body(dg_ref, ys_ref, yd_ref, zs_ref, zd_ref, gs_ref, gd_ref,
             dys_ref, dyd_ref, dzs_ref, dzd_ref, ggs_ref, ggd_ref):
        @pl.when(pl.program_id(0) == 0)
        def _():
            ggs_ref[...] = jnp.zeros_like(ggs_ref)
            ggd_ref[...] = jnp.zeros_like(ggd_ref)

        groups = ((ys_ref, zs_ref, gs_ref, dys_ref, dzs_ref, ggs_ref), (yd_ref, zd_ref, gd_ref, dyd_ref, dzd_ref, ggd_ref))
        for grp, (y_ref, z_ref, g_ref, dy_ref, dz_ref, gg_ref) in enumerate(groups):
            for h in range(n_heads):
                sl = slice(h * HEAD_DIM, (h + 1) * HEAD_DIM)
                dg = dg_ref[:, grp * e + h * HEAD_DIM:grp * e + (h + 1) * HEAD_DIM].astype(F32)
                y = y_ref[:, sl]
                z = z_ref[:, sl].astype(F32)
                g = g_ref[:, sl]
                r = lax.rsqrt(jnp.mean(y * y, axis=-1, keepdims=True) + EPS)
                yh = y * r
                sig = jax.nn.sigmoid(z)
                dyn = dg * (z * sig)
                dz_ref[:, sl] = (dg * (yh * g) * (sig * (1.0 + z * (1.0 - sig)))).astype(BF16)
                gg_ref[:, sl] += jnp.sum(dyn * yh, axis=0, keepdims=True)
                dyh = dyn * g
                dy_ref[:, sl] = (r * (dyh - yh * jnp.mean(dyh * yh, axis=-1, keepdims=True))).astype(BF16)

    yblk = pl.BlockSpec((tt, e), lambda i: (i, 0))
    gblk = pl.BlockSpec((1, e), lambda i: (0, 0))
    act = jax.ShapeDtypeStruct((t, e), BF16)
    vec = jax.ShapeDtypeStruct((1, e), F32)
    return pl.pallas_call(
        body, name="gate_bwd", grid=(t // tt,),
        in_specs=[pl.BlockSpec((tt, 2 * e), lambda i: (i, 0)), yblk, yblk,
                  pl.BlockSpec((tt, e), lambda i: (i, 3)), pl.BlockSpec((tt, e), lambda i: (i, 7)), gblk, gblk],
        out_specs=(yblk, yblk, yblk, yblk, gblk, gblk),
        out_shape=(act, act, act, act, vec, vec),
        compiler_params=_params("arbitrary"),
    )(dyg, y_sb, y_dl, proj, proj, g_sb, g_dl)


ATT_TQ = 256
HEADS_PER_STEP = 4
SOFTPLUS_CLAMP = 30.0
ATT_STRIP = 32


def _split2_dot(x, u):
    hi = x.astype(BF16)
    lo = (x - hi.astype(F32)).astype(BF16)
    return jnp.dot(hi, u, preferred_element_type=F32) + jnp.dot(lo, u, preferred_element_type=F32)


def _iota2(n):
    return lax.broadcasted_iota(jnp.int32, (n, n), 0), lax.broadcasted_iota(jnp.int32, (n, n), 1)


def _head_slices():
    return [slice(hh * HEAD_DIM, (hh + 1) * HEAD_DIM) for hh in range(HEADS_PER_STEP)]


def _att_specs(s, e, tq, col0):
    n_heads = e // HEAD_DIM
    hp = HEADS_PER_STEP
    assert n_heads % hp == 0 and col0 % hp == 0
    w = hp * HEAD_DIM
    q_spec = pl.BlockSpec((None, tq, w), lambda i, h, j: (i, j, col0 // hp + h))
    k_spec = pl.BlockSpec((None, s, w), lambda i, h, j: (i, 0, (col0 + n_heads) // hp + h))
    v_spec = pl.BlockSpec((None, s, w), lambda i, h, j: (i, 0, (col0 + 2 * n_heads) // hp + h))
    return q_spec, k_spec, v_spec


def _sb_fwd(proj3, e):
    b, s, _ = proj3.shape
    n_heads = e // HEAD_DIM
    hp = HEADS_PER_STEP
    tq = _tile(s, ATT_TQ)
    nq = s // tq
    inv = 1.0 / math.sqrt(HEAD_DIM)

    def body(q_ref, k_ref, v_ref, y_ref, tot_ref, acc_ref, car_ref):
        i = pl.program_id(2)
        row, col = _iota2(tq)
        before = row > col
        u_after = before.astype(BF16)
        acc_ref[...] = jnp.zeros_like(acc_ref)
        car_ref[...] = jnp.zeros_like(car_ref)

        def block(j, diagonal):
            keys = pl.ds(pl.multiple_of(j * tq, tq), tq)
            heads = list(enumerate(_head_slices()))
            zs = [lax.dot_general(q_ref[:, hs], k_ref[keys, hs], NT_DIMS, preferred_element_type=F32) * inv
                  for _, hs in heads]
            sps = [jnp.maximum(jnp.log(1.0 + jnp.exp(jnp.minimum(z, SOFTPLUS_CLAMP))), z) for z in zs]
            loms = [jnp.where(before, -sp, 0.0) if diagonal else -sp for sp in sps]
            sufs = [_split2_dot(loms[hh], u_after) + car_ref[hh] for hh, _ in heads]
            avs = [jnp.exp((zs[hh] - sps[hh]) + sufs[hh]) for hh, _ in heads]
            if diagonal:
                avs = [jnp.where(before, a, 0.0) for a in avs]
            pvs = [jnp.dot(avs[hh].astype(BF16), v_ref[keys, hs], preferred_element_type=F32) for hh, hs in heads]
            for hh, _ in heads:
                acc_ref[hh] += pvs[hh]
                car_ref[hh] += jnp.sum(loms[hh], axis=1, keepdims=True)

        block(i, True)

        def step(it, carry):
            block(i - it, False)
            return carry

        lax.fori_loop(1, i + 1, step, 0)
        for hh, hs in enumerate(_head_slices()):
            y_ref[:, hs] = acc_ref[hh]
            tot_ref[:, hs] = jnp.broadcast_to(car_ref[hh], (tq, HEAD_DIM))

    q_spec, k_spec, v_spec = _att_specs(s, e, tq, 0)
    blk_q = pl.BlockSpec((None, tq, hp * HEAD_DIM), lambda i, h, j: (i, j, h))
    shp = jax.ShapeDtypeStruct((b, s, e), F32)
    return pl.pallas_call(
        body, name="sb_fwd", grid=(b, n_heads // hp, nq),
        in_specs=[q_spec, k_spec, v_spec],
        out_specs=(blk_q, blk_q), out_shape=(shp, shp),
        scratch_shapes=[pltpu.VMEM((hp, tq, HEAD_DIM), F32), pltpu.VMEM((hp, tq, 1), F32)],
        compiler_params=_params("parallel", "parallel", "arbitrary"),
    )(proj3, proj3, proj3)


def _sb_bwd(proj3, lom_total, dy, partials):
    b, s, e = dy.shape
    n_heads = e // HEAD_DIM
    hp = HEADS_PER_STEP
    tq = _tile(s, ATT_TQ)
    nq = s // tq
    inv = 1.0 / math.sqrt(HEAD_DIM)
    grid = (b, n_heads // hp, nq)

    def body(q_ref, k_ref, v_ref, tot_ref, dy_ref, p_ref, dq_ref, dk_ref, dv_ref, r_ref,
             dqa, dka, dva, car, car2, send_sems, recv_sems, local_sem):
        i = pl.program_id(2)
        first, last = _grid_first_last(grid)

        @pl.when(first)
        def _():
            for cp in _all_to_all_copies(p_ref, r_ref, send_sems, recv_sems, local_sem):
                cp.start()

        @pl.when(i == 0)
        def _():
            dka[...] = jnp.zeros_like(dka)
            dva[...] = jnp.zeros_like(dva)

        row, col = _iota2(tq)
        before = row > col
        u_upto = (row <= col).astype(BF16)
        u_before = (row < col).astype(BF16)
        dqa[...] = jnp.zeros_like(dqa)
        car[...] = jnp.zeros_like(car)
        car2[...] = jnp.zeros_like(car2)

        def block(j, diagonal):
            keys = pl.ds(pl.multiple_of(j * tq, tq), tq)
            heads = list(enumerate(_head_slices()))
            zs = [lax.dot_general(q_ref[:, hs], k_ref[keys, hs], NT_DIMS, preferred_element_type=F32) * inv
                  for _, hs in heads]
            das = [lax.dot_general(dy_ref[:, hs], v_ref[keys, hs], NT_DIMS, preferred_element_type=F32) for _, hs in heads]
            ezs = [jnp.exp(jnp.minimum(z, SOFTPLUS_CLAMP)) for z in zs]
            sps = [jnp.maximum(jnp.log(1.0 + ezs[hh]), zs[hh]) for hh, _ in heads]
            loms = [jnp.where(before, -sp, 0.0) if diagonal else -sp for sp in sps]
            sufs = [tot_ref[:, hh * HEAD_DIM:hh * HEAD_DIM + 1] - (_split2_dot(loms[hh], u_upto) + car[hh])
                    for hh, _ in heads]
            avs = [jnp.exp((zs[hh] - sps[hh]) + sufs[hh]) for hh, _ in heads]
            if diagonal:
                avs = [jnp.where(before, a, 0.0) for a in avs]
            dls = [avs[hh] * das[hh] for hh, _ in heads]
            prefixes = [_split2_dot(dls[hh], u_before) + car2[hh] for hh, _ in heads]
            dzs = []
            for hh, _ in heads:
                one_minus_beta = jnp.exp(-sps[hh])
                dz = (dls[hh] * one_minus_beta - prefixes[hh] * (1.0 - one_minus_beta)) * inv
                if diagonal:
                    dz = jnp.where(before, dz, 0.0)
                dzs.append(dz.astype(BF16))
            dqs = [jnp.dot(dzs[hh], k_ref[keys, hs], preferred_element_type=F32) for hh, hs in heads]
            dks = [lax.dot_general(dzs[hh], q_ref[:, hs], TN_DIMS, preferred_element_type=F32) for hh, hs in heads]
            dvs = [lax.dot_general(avs[hh].astype(BF16), dy_ref[:, hs], TN_DIMS, preferred_element_type=F32)
                   for hh, hs in heads]
            for hh, hs in heads:
                dqa[hh] += dqs[hh]
                dka[keys, hs] += dks[hh]
                dva[keys, hs] += dvs[hh]
                car[hh] += jnp.sum(loms[hh], axis=1, keepdims=True)
                car2[hh] += jnp.sum(dls[hh], axis=1, keepdims=True)

        def step(j, carry):
            block(j, False)
            return carry

        lax.fori_loop(0, i, step, 0)
        block(i, True)
        for hh, hs in enumerate(_head_slices()):
            dq_ref[:, hs] = dqa[hh].astype(BF16)

        @pl.when(i == nq - 1)
        def _():
            dk_ref[...] = dka[...].astype(BF16)
            dv_ref[...] = dva[...].astype(BF16)

        @pl.when(last)
        def _():
            for cp in _all_to_all_copies(p_ref, r_ref, send_sems, recv_sems, local_sem):
                cp.wait()

    q_spec, k_spec, v_spec = _att_specs(s, e, tq, 0)
    w = hp * HEAD_DIM
    blk_q = pl.BlockSpec((None, tq, w), lambda i, h, j: (i, j, h))
    blk_kv = pl.BlockSpec((None, s, w), lambda i, h, j: (i, 0, h))
    shp = jax.ShapeDtypeStruct((b, s, e), BF16)
    return pl.pallas_call(
        body, name="sb_bwd", grid=grid,
        in_specs=[q_spec, k_spec, v_spec, blk_q, blk_q, ANY_SPEC],
        out_specs=(blk_q, blk_kv, blk_kv, ANY_SPEC),
        out_shape=(shp, shp, shp, jax.ShapeDtypeStruct(partials.shape, partials.dtype)),
        scratch_shapes=[pltpu.VMEM((hp, tq, HEAD_DIM), F32), pltpu.VMEM((s, w), F32), pltpu.VMEM((s, w), F32),
                        pltpu.VMEM((hp, tq, 1), F32), pltpu.VMEM((hp, tq, 1), F32),
                        pltpu.SemaphoreType.DMA((N_DEV - 1,)), pltpu.SemaphoreType.DMA((N_DEV - 1,)),
                        pltpu.SemaphoreType.DMA],
        compiler_params=_params("arbitrary", "arbitrary", "arbitrary"),
    )(proj3, proj3, proj3, lom_total, dy, partials)


def _dil_near_tiles(tq):
    return (DIL_PAIRS[1][0] + tq - 1) // tq + 1


def _dil_fill_bias(bias_ref, sl_ref, tq):
    row, col = _iota2(tq)
    for hh in range(HEADS_PER_STEP):
        slope = sl_ref[hh, 0:1, 0:1]
        for d in range(_dil_near_tiles(tq) + 1):
            dist = d * tq + row - col
            cnt = jnp.zeros(dist.shape, jnp.int32)
            for window, dilation in DIL_PAIRS:
                cnt = cnt + (((dist & (dilation - 1)) == 0) & (dist <= window)).astype(jnp.int32)
            bias = jnp.where(cnt == 3, math.log(3.0), jnp.where(cnt == 2, math.log(2.0), 0.0))
            bias_ref[hh, d] = jnp.where((dist >= 0) & (cnt > 0), bias - slope * dist.astype(F32), NEG)


def _dil_fwd(proj3, e, slopes):
    b, s, _ = proj3.shape
    n_heads = e // HEAD_DIM
    hp = HEADS_PER_STEP
    tq = _tile(s, ATT_TQ)
    nq = s // tq
    inv = 1.0 / math.sqrt(HEAD_DIM)
    assert s <= DIL_PAIRS[2][0]

    def body(q_ref, k_ref, v_ref, sl_ref, y_ref, lse_ref, acc_ref, m_ref, l_ref, bias_ref, p_ref):
        i = pl.program_id(2)

        @pl.when(i == 0)
        def _():
            _dil_fill_bias(bias_ref, sl_ref, tq)

        acc_ref[...] = jnp.zeros_like(acc_ref)
        m_ref[...] = jnp.full_like(m_ref, NEG)
        l_ref[...] = jnp.zeros_like(l_ref)

        def step(it, carry):
            keys = pl.ds(pl.multiple_of((i - it) * tq, tq), tq)
            heads = list(enumerate(_head_slices()))
            near = _dil_near_tiles(tq)
            tile = jnp.minimum(it, near)
            beyond = jnp.maximum(it - near, 0).astype(F32) * float(tq)
            halves = [slice(0, tq // 2), slice(tq // 2, tq)]
            raws = [[lax.dot_general(q_ref[hv, hs], k_ref[keys, hs], NT_DIMS, preferred_element_type=F32)
                     for hv in halves] for _, hs in heads]
            for hh, _ in heads:
                shift = sl_ref[hh, 0:1, 0:1] * beyond
                for hi, hv in enumerate(halves):
                    for r0 in range(0, tq // 2, ATT_STRIP):
                        rows = slice(hv.start + r0, hv.start + r0 + ATT_STRIP)
                        sc = (raws[hh][hi][r0:r0 + ATT_STRIP] * inv + bias_ref[hh, tile, rows, :]) - shift
                        m_old = m_ref[hh, rows]
                        m_new = jnp.maximum(m_old, jnp.max(sc, axis=1, keepdims=True))
                        p = jnp.exp(sc - m_new)
                        alpha = jnp.exp(m_old - m_new)
                        l_ref[hh, rows] = alpha * l_ref[hh, rows] + (p[:, :tq // 2] + p[:, tq // 2:])
                        acc_ref[hh, rows] = alpha * acc_ref[hh, rows]
                        p_ref[hh, rows] = p.astype(BF16)
                        m_ref[hh, rows] = m_new
            pvs = [[jnp.dot(p_ref[hh, hv], v_ref[keys, hs], preferred_element_type=F32) for hv in halves]
                   for hh, hs in heads]
            for hh, _ in heads:
                for hi, hv in enumerate(halves):
                    acc_ref[hh, hv] += pvs[hh][hi]
            return carry

        lax.fori_loop(0, i + 1, step, 0)
        for hh, hs in enumerate(_head_slices()):
            l = jnp.sum(l_ref[hh], axis=1, keepdims=True)
            y_ref[:, hs] = acc_ref[hh] / l
            lse_ref[:, hs] = jnp.broadcast_to(m_ref[hh] + jnp.log(l), (tq, HEAD_DIM))

    q_spec, k_spec, v_spec = _att_specs(s, e, tq, 4 * n_heads)
    blk_q = pl.BlockSpec((None, tq, hp * HEAD_DIM), lambda i, h, j: (i, j, h))
    shp = jax.ShapeDtypeStruct((b, s, e), F32)
    return pl.pallas_call(
        body, name="dil_fwd", grid=(b, n_heads // hp, nq),
        in_specs=[q_spec, k_spec, v_spec, pl.BlockSpec((hp, 8, HEAD_DIM), lambda i, h, j: (h, 0, 0))],
        out_specs=(blk_q, blk_q), out_shape=(shp, shp),
        scratch_shapes=[pltpu.VMEM((hp, tq, HEAD_DIM), F32), pltpu.VMEM((hp, tq, 1), F32), pltpu.VMEM((hp, tq, tq // 2), F32),
                        pltpu.VMEM((hp, _dil_near_tiles(tq) + 1, tq, tq), F32), pltpu.VMEM((hp, tq, tq), BF16)],
        compiler_params=_params("parallel", "parallel", "arbitrary"),
    )(proj3, proj3, proj3, slopes)


def _dil_bwd(proj3, y, lse, dy, slopes):
    b, s, e = y.shape
    n_heads = e // HEAD_DIM
    hp = HEADS_PER_STEP
    tq = _tile(s, ATT_TQ)
    nq = s // tq
    inv = 1.0 / math.sqrt(HEAD_DIM)

    def body(q_ref, k_ref, v_ref, sl_ref, y_ref, lse_ref, dy_ref, dq_ref, dk_ref, dv_ref, dqa, dka, dva, bias_ref,
             p_ref, ds_ref):
        i = pl.program_id(2)

        @pl.when(i == 0)
        def _():
            dka[...] = jnp.zeros_like(dka)
            dva[...] = jnp.zeros_like(dva)
            _dil_fill_bias(bias_ref, sl_ref, tq)

        delta = [jnp.sum(dy_ref[:, hs].astype(F32) * y_ref[:, hs], axis=1, keepdims=True) for hs in _head_slices()]
        dqa[...] = jnp.zeros_like(dqa)

        def step(it, carry):
            keys = pl.ds(pl.multiple_of((i - it) * tq, tq), tq)
            heads = list(enumerate(_head_slices()))
            near = _dil_near_tiles(tq)
            tile = jnp.minimum(it, near)
            beyond = jnp.maximum(it - near, 0).astype(F32) * float(tq)
            raws = [lax.dot_general(q_ref[:, hs], k_ref[keys, hs], NT_DIMS, preferred_element_type=F32)
                    for _, hs in heads]
            dps = [lax.dot_general(dy_ref[:, hs], v_ref[keys, hs], NT_DIMS, preferred_element_type=F32) for _, hs in heads]
            for hh, _ in heads:
                shift = sl_ref[hh, 0:1, 0:1] * beyond
                for r0 in range(0, tq, ATT_STRIP):
                    rows = slice(r0, r0 + ATT_STRIP)
                    sc = (raws[hh][rows] * inv + bias_ref[hh, tile, rows, :]) - shift
                    p = jnp.exp(sc - lse_ref[rows, hh * HEAD_DIM:hh * HEAD_DIM + 1])
                    p_ref[hh, rows] = p.astype(BF16)
                    ds_ref[hh, rows] = ((p * (dps[hh][rows] - delta[hh][rows])) * inv).astype(BF16)
            dqs = [jnp.dot(ds_ref[hh], k_ref[keys, hs], preferred_element_type=F32) for hh, hs in heads]
            dks = [lax.dot_general(ds_ref[hh], q_ref[:, hs], TN_DIMS, preferred_element_type=F32) for hh, hs in heads]
            dvs = [lax.dot_general(p_ref[hh], dy_ref[:, hs], TN_DIMS, preferred_element_type=F32) for hh, hs in heads]
            for hh, hs in heads:
                dqa[hh] += dqs[hh]
                dka[keys, hs] += dks[hh]
                dva[keys, hs] += dvs[hh]
            return carry

        lax.fori_loop(0, i + 1, step, 0)
        for hh, hs in enumerate(_head_slices()):
            dq_ref[:, hs] = dqa[hh].astype(BF16)

        @pl.when(i == nq - 1)
        def _():
            dk_ref[...] = dka[...].astype(BF16)
            dv_ref[...] = dva[...].astype(BF16)

    q_spec, k_spec, v_spec = _att_specs(s, e, tq, 4 * n_heads)
    w = hp * HEAD_DIM
    blk_q = pl.BlockSpec((None, tq, w), lambda i, h, j: (i, j, h))
    blk_kv = pl.BlockSpec((None, s, w), lambda i, h, j: (i, 0, h))
    shp = jax.ShapeDtypeStruct((b, s, e), BF16)
    return pl.pallas_call(
        body, name="dil_bwd", grid=(b, n_heads // hp, nq),
        in_specs=[q_spec, k_spec, v_spec, pl.BlockSpec((hp, 8, HEAD_DIM), lambda i, h, j: (h, 0, 0)),
                  blk_q, blk_q, blk_q],
        out_specs=(blk_q, blk_kv, blk_kv), out_shape=(shp, shp, shp),
        scratch_shapes=[pltpu.VMEM((hp, tq, HEAD_DIM), F32), pltpu.VMEM((s, w), F32), pltpu.VMEM((s, w), F32),
                        pltpu.VMEM((hp, _dil_near_tiles(tq) + 1, tq, tq), F32),
                        pltpu.VMEM((hp, tq, tq), BF16), pltpu.VMEM((hp, tq, tq), BF16)],
        compiler_params=_params("parallel", "parallel", "arbitrary"),
    )(proj3, proj3, proj3, slopes, y, lse, dy)


def kernel(x, c, w_ada, b_ada, g_norm, w_in, g_sb, g_dil, w_out, g_final, loss_target, m_w_ada, m_b_ada, m_g_norm, m_w_in, m_g_sb, m_g_dil, m_w_out, m_g_final, v_w_ada, v_b_ada, v_g_norm, v_w_in, v_g_sb, v_g_dil, v_w_out, v_g_final):
    b, s, d = x.shape
    t = b * s
    e = w_in.shape[2]
    n_heads = e // HEAD_DIM
    na = w_ada.shape[2]
    r_out = w_out.shape[1]
    assert g_sb.shape[1] == e and g_dil.shape[1] == e and N_DEV * r_out == 2 * e and N_DEV * na == 3 * d
    assert b <= SMALL_ROWS and 3 * b + 3 <= 2 * SMALL_ROWS
    ix, iy, ic = _mesh_pos()
    me = 4 * ix + 2 * iy + ic

    c_all = _allgather_rows(jnp.pad(c, ((0, SMALL_ROWS - b), (0, 0))), "ag_c")
    b_own = lax.dynamic_slice(b_ada, (0, me * na), (1, na))
    mod_cols = _ada_fwd(c_all, w_ada[0], b_own)
    mod_all = _allgather_rows(mod_cols, "ag_mod").reshape(N_DEV, N_DEV, SMALL_ROWS, na)
    mod_own = lax.dynamic_slice(mod_all, (0, me, 0, 0), (N_DEV, 1, b, na))[:, 0]
    mod = mod_own.transpose(1, 0, 2).reshape(b, 1, 3 * d)
    shift, scale, gate = mod[:, :, :d], mod[:, :, d:2 * d], mod[:, :, 2 * d:]

    w_own = w_in[0].astype(BF16)

    h = _norm_mod(x, g_norm, scale, shift).reshape(t, d)
    proj, w_in3, w_out3 = _proj_with_allgather(
        h, w_own, jnp.stack(_unit_ids()).astype(jnp.int32), w_out[0].astype(BF16))
    w_out1 = w_out3.reshape(1, N_DEV * r_out, d)
    proj3 = proj.reshape(b, s, N_DEV * e)
    slopes = jnp.exp2(-ALIBI_MAX_BIAS * jnp.arange(1, n_heads + 1, dtype=F32) / n_heads)
    slopes = jnp.broadcast_to(slopes[:, None, None], (n_heads, 8, HEAD_DIM))
    y_sb, lom_total = _sb_fwd(proj3, e)
    y_dl, lse = _dil_fwd(proj3, e, slopes)
    yg = _gate_fwd(y_sb.reshape(t, e), y_dl.reshape(t, e), proj, g_sb, g_dil)
    out = _mm_nn(yg, w_out1, F32, "mm_out").reshape(b, s, d)
    loss_p, dx2, d_out, dgate, gg_final = _final_fwd_bwd(x, out, gate, g_final.reshape(1, d), loss_target)

    d_out2 = d_out.reshape(t, d)
    dyg = _mm_nt(d_out2, w_out1, BF16, "mm_dy")
    gw_out_p = _mm_tn(yg, d_out2, 1, BF16, "mm_gw_out").reshape(N_DEV, r_out, d)
    dy_sb, dy_dl, dz_sb, dz_dl, gg_sb, gg_dl = _gate_bwd(dyg, y_sb.reshape(t, e), y_dl.reshape(t, e), proj, g_sb, g_dil)
    dq_sb, dk_sb, dv_sb, recv_out = _sb_bwd(proj3, lom_total, dy_sb.reshape(b, s, e), gw_out_p)
    dq_dl, dk_dl, dv_dl = _dil_bwd(proj3, y_dl, lse, dy_dl.reshape(b, s, e), slopes)
    dproj = jnp.concatenate(
        [a.reshape(t, e) for a in (dq_sb, dk_sb, dv_sb, dz_sb, dq_dl, dk_dl, dv_dl, dz_dl)], axis=1)
    chips4 = [(ix, iy)] + _other_chips()
    to_sibling_core = jnp.stack([4 * px + 2 * py + (1 - ic) for px, py in chips4]).astype(jnp.int32)
    to_my_core = jnp.stack([4 * px + 2 * py + ic for px, py in chips4]).astype(jnp.int32)
    gw_in_sibs = _mm_tn_groups(to_sibling_core, h, dproj, N_DEV, "mm_gw_in_sibling")
    gw_in_mine, gw_in_sib = _mm_tn_groups(to_my_core, h, dproj, N_DEV, "mm_gw_in_mine", to_sibling=gw_in_sibs)
    gw_in_send = _chip_presum(gw_in_mine, gw_in_sib)
    dh, gw_in_recv = _mm_nt_with_chip_exchange(dproj, w_in3, F32, gw_in_send, "mm_dh")
    dh = dh.reshape(b, s, d)
    grad_x, dshift, dscale, gg_norm = _norm_bwd(x, dh, dx2, scale, g_norm)

    dmod = jnp.concatenate([dshift, dscale, dgate], axis=1).reshape(3 * b, d)
    pkg = jnp.concatenate([dmod, gg_norm, gg_final, jnp.concatenate([gg_sb, gg_dl], axis=1),
                           jnp.zeros((2 * SMALL_ROWS - 3 * b - 3, d), F32)], axis=0)
    pkg_all = _allgather_rows(pkg, "ag_small_grads").reshape(N_DEV, 2 * SMALL_ROWS, d)
    dmod_all = pkg_all[:, :3 * b].reshape(N_DEV * b, 3 * d)
    dmod_cols = lax.dynamic_slice(dmod_all, (0, me * na), (N_DEV * b, na))
    c_rows = c_all.reshape(N_DEV, SMALL_ROWS, d)[:, :b].reshape(N_DEV * b, d)
    g_w_ada, d_w_ada, nm_w_ada, nv_w_ada = _ada_bwd_adam(c_rows, dmod_cols, w_ada[0], m_w_ada[0], v_w_ada[0])

    def pack(b_ada_like, g_norm_like, g_sb_like, g_dil_like, g_final_like):
        return jnp.concatenate([b_ada_like.reshape(3, d), g_norm_like.reshape(1, d), g_final_like.reshape(1, d),
                                jnp.concatenate([g_sb_like, g_dil_like], axis=1).reshape(1, d),
                                jnp.zeros((2, d), F32)], axis=0)

    small = _small_adam(pkg_all, 3 * b, pack(b_ada, g_norm, g_sb, g_dil, g_final),
                        pack(m_b_ada, m_g_norm, m_g_sb, m_g_dil, m_g_final),
                        pack(v_b_ada, v_g_norm, v_g_sb, v_g_dil, v_g_final))

    def unpack(p):
        return (p[0:3].reshape(1, 3 * d), p[3:4], p[5:6, :e], p[5:6, e:], p[4])

    sm_g, sm_d, sm_m, sm_v = (unpack(p) for p in small)

    g_w_in, d_w_in, nm_w_in, nv_w_in = _adam_from_chip_sums(
        gw_in_mine, gw_in_sib, gw_in_recv, w_in[0], m_w_in[0], v_w_in[0], "adam_w_in")
    g_w_out, d_w_out, nm_w_out, nv_w_out = _adam_from_partials(recv_out, w_out[0], m_w_out[0], v_w_out[0], "adam_w_out")

    loss = lax.psum(loss_p[0, 0], ("x", "y", "c"))

    def weights(ada, small_parts, w_in_part, w_out_part):
        b_ada_p, g_norm_p, g_sb_p, g_dil_p, g_final_p = small_parts
        return (ada[None], b_ada_p, g_norm_p, w_in_part[None], g_sb_p, g_dil_p, w_out_part[None], g_final_p)

    return (loss, grad_x,
            *weights(g_w_ada, sm_g, g_w_in, g_w_out),
            *weights(d_w_ada, sm_d, d_w_in, d_w_out),
            *weights(nm_w_ada, sm_m, nm_w_in, nm_w_out),
            *weights(nv_w_ada, sm_v, nv_w_in, nv_w_out))
```
